```python
import jax, jax.numpy as jnp
from jax import lax
import numpy as np

D_MODEL = 1024
BATCH = 8
SEQ = 8192
DEPTH = 1

NSA_HEADS = 8
NSA_KV_GROUPS = 2
NSA_REP = NSA_HEADS // NSA_KV_GROUPS
HEAD_DIM = 64
NSA_WIDTH = NSA_HEADS * HEAD_DIM
KV_WIDTH = NSA_KV_GROUPS * HEAD_DIM
CMP_BLOCK = 32
CMP_STRIDE = 16
CMP_HIDDEN = 2 * HEAD_DIM
SEL_BLOCK = 64
SEL_TOPK = 16
WINDOW = 512
Q_BLOCK = 64
ROPE_THETA = 10000.0
SGU_GROUPS = 8
SGU_WIDTH = D_MODEL // 2
SGU_GROUP_DIM = SGU_WIDTH // SGU_GROUPS
SGU_CHUNK = 128
IN_COLS = NSA_WIDTH + 6 * KV_WIDTH + 3 * NSA_HEADS + 2 * SGU_WIDTH
PEER_HEADS = 8
PEER_NKEYS = 128
PEER_EXPERTS = PEER_NKEYS * PEER_NKEYS
PEER_QDIM = 256
PEER_HALF = PEER_QDIM // 2
PEER_TOPK = 16
PEER_TOK_BLOCK = 128
DN_ALPHA = (2.0 * DEPTH) ** 0.25
DN_BETA = (8.0 * DEPTH) ** -0.25
LN_EPS = 1e-5
NEG_INF = -1e30
FORCE_SCORE = 1e9

kernel_name = 'hybrid_nsa_sgu_peer_block'


def _layernorm(x, g=None, b=None):
    xf = x.astype(jnp.float32)
    mu = jnp.mean(xf, axis=-1, keepdims=True)
    var = jnp.mean(jnp.square(xf - mu), axis=-1, keepdims=True)
    y = (xf - mu) * lax.rsqrt(var + LN_EPS)
    if g is not None:
        y = y * g.astype(jnp.float32) + b.astype(jnp.float32)
    return y.astype(x.dtype)


def _rope(t):
    half = HEAD_DIM // 2
    pos = jnp.arange(t.shape[1], dtype=jnp.float32)
    inv_freq = ROPE_THETA ** (-jnp.arange(half, dtype=jnp.float32) / half)
    ang = pos[:, None] * inv_freq[None, :]
    cos = jnp.cos(ang)[:, None, :]
    sin = jnp.sin(ang)[:, None, :]
    tf = t.astype(jnp.float32)
    t1, t2 = tf[..., :half], tf[..., half:]
    return jnp.concatenate([t1 * cos - t2 * sin, t1 * sin + t2 * cos], axis=-1).astype(t.dtype)


def _masked_softmax(s, mask):
    p = jax.nn.softmax(jnp.where(mask, s, NEG_INF), axis=-1)
    return p * mask


def _compress(t, pos_emb, w1, b1, w2, b2):
    B, S, G, _ = t.shape
    n_cmp = (S - CMP_BLOCK) // CMP_STRIDE + 1
    idx = np.arange(n_cmp)[:, None] * CMP_STRIDE + np.arange(CMP_BLOCK)[None, :]
    blk = t[:, idx] + pos_emb[:, None, :]
    blk = jnp.moveaxis(blk, 3, 2).reshape(B, n_cmp, G, CMP_BLOCK * HEAD_DIM)
    hid = jax.nn.gelu(blk @ w1 + b1, approximate=False)
    return hid @ w2 + b2


def _sel_aggregation(n_cmp, n_sel):
    c0 = np.arange(n_cmp)[:, None] * CMP_STRIDE
    s0 = np.arange(n_sel)[None, :] * SEL_BLOCK
    ov = np.clip(np.minimum(c0 + CMP_BLOCK, s0 + SEL_BLOCK) - np.maximum(c0, s0), 0, None)
    return (ov / CMP_BLOCK).astype(np.float32)


def _nsa(q, kc, vc, ks, vs, kw, vw, gate_logits):
    B, S = q.shape[0], q.shape[1]
    G, R, HD = NSA_KV_GROUPS, NSA_REP, HEAD_DIM
    n_cmp = kc.shape[1]
    n_sel = S // SEL_BLOCK
    k_top = min(SEL_TOPK, n_sel)
    n_qb = S // Q_BLOCK
    scale = HD ** -0.5
    cmp_end = jnp.arange(n_cmp) * CMP_STRIDE + CMP_BLOCK - 1
    agg = jnp.asarray(_sel_aggregation(n_cmp, n_sel))
    blk_id = jnp.arange(n_sel)
    ks_b = ks.reshape(B, n_sel, SEL_BLOCK, G, HD).transpose(0, 3, 1, 2, 4)
    vs_b = vs.reshape(B, n_sel, SEL_BLOCK, G, HD).transpose(0, 3, 1, 2, 4)
    pad = ((0, 0), (WINDOW, 0), (0, 0), (0, 0))
    kw_p = jnp.pad(kw, pad)
    vw_p = jnp.pad(vw, pad)
    span = Q_BLOCK + WINDOW
    gather = jax.vmap(jax.vmap(lambda tab, idx: tab[idx]))

    def block(args):
        qb, q_blk, g_blk = args
        t = qb * Q_BLOCK + jnp.arange(Q_BLOCK)
        qg = q_blk.reshape(B, Q_BLOCK, G, R, HD)
        s_c = jnp.einsum('bqgrd,bngd->bgrqn', qg, kc).astype(jnp.float32) * scale
        p_c = _masked_softmax(s_c, cmp_end[None, :] <= t[:, None])
        o_c = jnp.einsum('bgrqn,bngd->bqgrd', p_c.astype(vc.dtype), vc)
        imp = jnp.einsum('bgrqn,nj->bgqj', p_c, agg)
        cur = (t // SEL_BLOCK)[:, None]
        forced = (blk_id == 0) | (blk_id == cur) | (blk_id == cur - 1)
        imp = jnp.where(forced, FORCE_SCORE, jnp.where(blk_id <= cur, imp, NEG_INF))
        _, sel = lax.top_k(imp, k_top)
        k_g = gather(ks_b, sel)
        v_g = gather(vs_b, sel)
        kpos = sel[..., None] * SEL_BLOCK + jnp.arange(SEL_BLOCK)
        valid_s = (kpos <= t[:, None, None]).reshape(B, G, 1, Q_BLOCK, k_top * SEL_BLOCK)
        s_s = jnp.einsum('bqgrd,bgqnkd->bgrqnk', qg, k_g).astype(jnp.float32) * scale
        p_s = _masked_softmax(s_s.reshape(B, G, R, Q_BLOCK, k_top * SEL_BLOCK), valid_s)
        o_s = jnp.einsum('bgrqm,bgqmd->bqgrd', p_s.astype(vs.dtype),
                         v_g.reshape(B, G, Q_BLOCK, k_top * SEL_BLOCK, HD))
        start = qb * Q_BLOCK
        k_w = lax.dynamic_slice_in_dim(kw_p, start, span, axis=1)
        v_w = lax.dynamic_slice_in_dim(vw_p, start, span, axis=1)
        kpos_w = start - WINDOW + jnp.arange(span)
        dlt = t[:, None] - kpos_w[None, :]
        valid_w = (dlt >= 0) & (dlt < WINDOW) & (kpos_w[None, :] >= 0)
        s_w = jnp.einsum('bqgrd,bkgd->bgrqk', qg, k_w).astype(jnp.float32) * scale
        p_w = _masked_softmax(s_w, valid_w)
        o_w = jnp.einsum('bgrqk,bkgd->bqgrd', p_w.astype(vw.dtype), v_w)
        g = jax.nn.sigmoid(g_blk.astype(jnp.float32)).reshape(B, Q_BLOCK, G, R, 3).astype(q.dtype)
        o = g[..., 0:1] * o_c + g[..., 1:2] * o_s + g[..., 2:3] * o_w
        return o.reshape(B, Q_BLOCK, NSA_WIDTH)

    qs = q.reshape(B, n_qb, Q_BLOCK, NSA_HEADS, HD).swapaxes(0, 1)
    gs = gate_logits.reshape(B, n_qb, Q_BLOCK, NSA_HEADS, 3).swapaxes(0, 1)
    out = lax.map(block, (jnp.arange(n_qb), qs, gs))
    return out.swapaxes(0, 1).reshape(B, S, NSA_WIDTH)


def _sgu(z, ln_g, ln_b, w_s, b_s):
    B, S, _ = z.shape
    z = jax.nn.gelu(z, approximate=False)
    u, v = z[..., :SGU_WIDTH], z[..., SGU_WIDTH:]
    v = _layernorm(v, ln_g, ln_b)
    v = v.reshape(B, S // SGU_CHUNK, SGU_CHUNK, SGU_GROUPS, SGU_GROUP_DIM)
    mixed = jnp.einsum('gij,bcjgd->bcigd', jnp.tril(w_s), v) + b_s.T[None, None, :, :, None]
    return u * mixed.reshape(B, S, SGU_WIDTH)


def _peer(h, w_q, sub_keys, expert_u, expert_v):
    B, S, D = h.shape
    tokens = h.reshape((B * S) // PEER_TOK_BLOCK, PEER_TOK_BLOCK, D)

    def step(xt):
        qh = (xt @ w_q).reshape(PEER_TOK_BLOCK, PEER_HEADS, 2, PEER_HALF)
        s = jnp.einsum('thcd,hcnd->thcn', qh, sub_keys).astype(jnp.float32)
        s1, i1 = lax.top_k(s[:, :, 0], PEER_TOPK)
        s2, i2 = lax.top_k(s[:, :, 1], PEER_TOPK)
        cand = (s1[..., :, None] + s2[..., None, :]).reshape(PEER_TOK_BLOCK, PEER_HEADS, PEER_TOPK * PEER_TOPK)
        cidx = (i1[..., :, None] * PEER_NKEYS + i2[..., None, :]).reshape(PEER_TOK_BLOCK, PEER_HEADS, PEER_TOPK * PEER_TOPK)
        top_s, pos = lax.top_k(cand, PEER_TOPK)
        eidx = jnp.take_along_axis(cidx, pos, axis=-1)
        gate = jax.nn.softmax(top_s, axis=-1)
        u = expert_u[eidx]
        act = jax.nn.gelu(jnp.einsum('thkd,td->thk', u, xt).astype(jnp.float32), approximate=False)
        w = (gate * act).astype(xt.dtype)
        return jnp.einsum('thk,thkd->td', w, expert_v[eidx])

    return lax.map(step, tokens).reshape(B, S, D)


def _token_mixers(h, w_in, cmp_pos, cmp_w1, cmp_b1, cmp_w2, cmp_b2, sgu_ln_g, sgu_ln_b,
                  sgu_w, sgu_b, w_branch, w_merge, b_merge, w_out):
    B, S, _ = h.shape
    G = NSA_KV_GROUPS
    proj = h @ w_in
    splits = np.cumsum([NSA_WIDTH] + [KV_WIDTH] * 6 + [3 * NSA_HEADS]).tolist()
    q, kc, vc, ks, vs, kw, vw, g_nsa, z = jnp.split(proj, splits, axis=-1)
    q = _rope(q.reshape(B, S, NSA_HEADS, HEAD_DIM))
    kc = _compress(_rope(kc.reshape(B, S, G, HEAD_DIM)), cmp_pos[0], cmp_w1[0], cmp_b1[0], cmp_w2[0], cmp_b2[0])
    vc = _compress(vc.reshape(B, S, G, HEAD_DIM), cmp_pos[1], cmp_w1[1], cmp_b1[1], cmp_w2[1], cmp_b2[1])
    ks = _rope(ks.reshape(B, S, G, HEAD_DIM))
    vs = vs.reshape(B, S, G, HEAD_DIM)
    kw = _rope(kw.reshape(B, S, G, HEAD_DIM))
    vw = vw.reshape(B, S, G, HEAD_DIM)
    o_nsa = _nsa(q, kc, vc, ks, vs, kw, vw, g_nsa.reshape(B, S, NSA_HEADS, 3))
    o_sgu = _sgu(z, sgu_ln_g, sgu_ln_b, sgu_w, sgu_b)
    gate_a, gate_b = jnp.split(jax.nn.sigmoid(h @ w_merge + b_merge), 2, axis=-1)
    merged = gate_a * (o_nsa @ w_branch[0]) + gate_b * (o_sgu @ w_branch[1])
    return merged @ w_out


def setup_inputs(seed: int = 0) -> dict:
    key = jax.random.key(seed)
    ks = jax.random.split(key, 26)
    L, D = DEPTH, D_MODEL
    f32 = jnp.float32

    def nrm(k, shape, scale):
        return scale * jax.random.normal(k, shape, f32)

    return {
        'x': nrm(ks[0], (BATCH, SEQ, D), 1.0),
        'c': nrm(ks[1], (BATCH, D), 1.0),
        'w_ada': nrm(ks[2], (L, D, 6 * D), 0.5 * D ** -0.5),
        'b_ada': nrm(ks[3], (L, 6 * D), 0.02),
        'w_in': nrm(ks[4], (L, D, IN_COLS), D ** -0.5),
        'cmp_pos': nrm(ks[5], (L, 2, CMP_BLOCK, HEAD_DIM), 0.1),
        'cmp_w1': nrm(ks[6], (L, 2, CMP_BLOCK * HEAD_DIM, CMP_HIDDEN), (CMP_BLOCK * HEAD_DIM) ** -0.5),
        'cmp_b1': nrm(ks[7], (L, 2, CMP_HIDDEN), 0.02),
        'cmp_w2': nrm(ks[8], (L, 2, CMP_HIDDEN, HEAD_DIM), CMP_HIDDEN ** -0.5),
        'cmp_b2': nrm(ks[9], (L, 2, HEAD_DIM), 0.02),
        'sgu_ln_g': 1.0 + nrm(ks[10], (L, SGU_WIDTH), 0.02),
        'sgu_ln_b': nrm(ks[11], (L, SGU_WIDTH), 0.02),
        'sgu_w': nrm(ks[12], (L, SGU_GROUPS, SGU_CHUNK, SGU_CHUNK), SGU_CHUNK ** -0.5),
        'sgu_b': 1.0 + nrm(ks[13], (L, SGU_GROUPS, SGU_CHUNK), 0.1),
        'w_branch': nrm(ks[14], (L, 2, NSA_WIDTH, D), NSA_WIDTH ** -0.5),
        'w_merge': nrm(ks[15], (L, D, 2 * D), D ** -0.5),
        'b_merge': nrm(ks[16], (L, 2 * D), 0.02),
        'w_out': nrm(ks[17], (L, D, D), DN_BETA * D ** -0.5),
        'ln1_g': 1.0 + nrm(ks[18], (L, D), 0.02),
        'ln1_b': nrm(ks[19], (L, D), 0.02),
        'peer_wq': nrm(ks[20], (L, D, PEER_HEADS * PEER_QDIM), D ** -0.5),
        'peer_keys': nrm(ks[21], (L, PEER_HEADS, 2, PEER_NKEYS, PEER_HALF), PEER_HALF ** -0.5),
        'peer_u': nrm(ks[22], (L, PEER_EXPERTS, D), D ** -0.5),
        'peer_v': nrm(ks[23], (L, PEER_EXPERTS, D), DN_BETA),
        'ln2_g': 1.0 + nrm(ks[24], (L, D), 0.02),
        'ln2_b': nrm(ks[25], (L, D), 0.02),
    }


def reference(x, c, w_ada, b_ada, w_in, cmp_pos, cmp_w1, cmp_b1, cmp_w2, cmp_b2,
              sgu_ln_g, sgu_ln_b, sgu_w, sgu_b, w_branch, w_merge, b_merge, w_out,
              ln1_g, ln1_b, peer_wq, peer_keys, peer_u, peer_v, ln2_g, ln2_b):
    for l in range(DEPTH):
        mod = jax.nn.silu(c) @ w_ada[l] + b_ada[l]
        sh1, sc1, gt1, sh2, sc2, gt2 = [m[:, None, :] for m in jnp.split(mod, 6, axis=-1)]
        h = _layernorm(x) * (1.0 + sc1) + sh1
        mix = _token_mixers(h, w_in[l], cmp_pos[l], cmp_w1[l], cmp_b1[l], cmp_w2[l], cmp_b2[l],
                            sgu_ln_g[l], sgu_ln_b[l], sgu_w[l], sgu_b[l], w_branch[l],
                            w_merge[l], b_merge[l], w_out[l])
        x = _layernorm(DN_ALPHA * x + gt1 * mix, ln1_g[l], ln1_b[l])
        h = _layernorm(x) * (1.0 + sc2) + sh2
        ffn = _peer(h, peer_wq[l], peer_keys[l], peer_u[l], peer_v[l])
        x = _layernorm(DN_ALPHA * x + gt2 * ffn, ln2_g[l], ln2_b[l])
    return x
```

```python
import functools

import jax
import jax.numpy as jnp
from jax import lax
from jax.experimental import pallas as pl
from jax.experimental.pallas import tpu as pltpu

D_MODEL = 1024
NSA_HEADS = 8
NSA_KV_GROUPS = 2
NSA_REP = NSA_HEADS // NSA_KV_GROUPS
HEAD_DIM = 64
NSA_WIDTH = NSA_HEADS * HEAD_DIM
KV_WIDTH = NSA_KV_GROUPS * HEAD_DIM
CMP_BLOCK = 32
CMP_STRIDE = 16
CMP_HIDDEN = 2 * HEAD_DIM
SEL_BLOCK = 64
SEL_TOPK = 16
WINDOW = 512
ROPE_THETA = 10000.0
SGU_GROUPS = 8
SGU_WIDTH = D_MODEL // 2
SGU_GROUP_DIM = SGU_WIDTH // SGU_GROUPS
SGU_CHUNK = 128
PEER_HEADS = 8
PEER_NKEYS = 128
PEER_EXPERTS = PEER_NKEYS * PEER_NKEYS
PEER_QDIM = 256
PEER_HALF = PEER_QDIM // 2
PEER_TOPK = 16
PEER_PAIRS = PEER_HEADS * PEER_TOPK
LN_EPS = 1e-5
NEG_INF = -1e30
FORCE_SCORE = 1e9

LANES = 128
MIB = 1024 * 1024
BF16 = jnp.bfloat16
F32 = jnp.float32

TM_PROJ = 256
TM_MERGE = 256
TM_LN = 512
TQ = 128
TK = 512
TC = 128
TB = 32

_OQ, _OQS = 0, 512
_OKC, _OKCS, _OKS, _OKSS, _OKW, _OKWS = 1024, 1152, 1280, 1408, 1536, 1664
_OVC, _OVS, _OVW = 1792, 1920, 2048
_OG = 2176
_OZ = 2432
_OM = 3456
_WCOLS = 5504


def _cparams(n_axes, vmem_mib):
    return pltpu.CompilerParams(
        dimension_semantics=("parallel",) * n_axes,
        vmem_limit_bytes=vmem_mib * MIB)


def _ln(x):
    mu = jnp.mean(x, axis=-1, keepdims=True)
    xc = x - mu
    var = jnp.mean(xc * xc, axis=-1, keepdims=True)
    return xc * lax.rsqrt(var + LN_EPS)


def _gelu(x):
    return 0.5 * x * (1.0 + lax.erf(x * (2.0 ** -0.5)))


def _dot(a, b):
    return jnp.dot(a, b, preferred_element_type=F32)


def _dot_nt(a, b):
    return lax.dot_general(a, b, (((1,), (1,)), ((), ())), preferred_element_type=F32)


def _ada_body(c_ref, w_ref, b_ref, o_ref):
    cv = c_ref[...]
    a = cv * jax.nn.sigmoid(cv)
    o_ref[...] = _dot(a.astype(BF16), w_ref[...].astype(BF16)) + b_ref[...]


def _ada(c, w, b):
    bsz, d = c.shape
    n = w.shape[1]
    tn = 1024
    return pl.pallas_call(
        _ada_body,
        grid=(n // tn,),
        in_specs=[pl.BlockSpec((bsz, d), lambda j: (0, 0)),
                  pl.BlockSpec((d, tn), lambda j: (0, j)),
                  pl.BlockSpec((1, tn), lambda j: (0, j))],
        out_specs=pl.BlockSpec((bsz, tn), lambda j: (0, j)),
        out_shape=jax.ShapeDtypeStruct((bsz, n), F32),
        compiler_params=_cparams(1, 32),
    )(c, w, b.reshape(1, n))


def _proj_body(x_ref, sc_ref, sh_ref, w_ref, bm_ref, cos_ref, sin_ref, sg_ref, sb_ref,
               q_ref, kc_ref, vc_ref, ks_ref, vs_ref, kw_ref, vw_ref, gn_ref, u_ref, v_ref, gate_ref):
    h = _ln(x_ref[...]) * (1.0 + sc_ref[0]) + sh_ref[0]
    hb = h.astype(BF16)

    def proj(off, width):
        return _dot(hb, w_ref[:, off:off + width])

    cosk = cos_ref[...]
    sink = sin_ref[...]
    cosq = jnp.concatenate([cosk] * (NSA_WIDTH // LANES), axis=1)
    sinq = jnp.concatenate([sink] * (NSA_WIDTH // LANES), axis=1)
    q_ref[...] = proj(_OQ, NSA_WIDTH) * cosq + proj(_OQS, NSA_WIDTH) * sinq
    kc_ref[...] = proj(_OKC, KV_WIDTH) * cosk + proj(_OKCS, KV_WIDTH) * sink
    ks_ref[...] = (proj(_OKS, KV_WIDTH) * cosk + proj(_OKSS, KV_WIDTH) * sink).astype(BF16)
    kw_ref[...] = (proj(_OKW, KV_WIDTH) * cosk + proj(_OKWS, KV_WIDTH) * sink).astype(BF16)
    vc_ref[...] = proj(_OVC, KV_WIDTH)
    vs_ref[...] = proj(_OVS, KV_WIDTH).astype(BF16)
    vw_ref[...] = proj(_OVW, KV_WIDTH).astype(BF16)
    gn_ref[...] = jax.nn.sigmoid(proj(_OG, 2 * LANES))
    u_ref[...] = _gelu(proj(_OZ, SGU_WIDTH))
    v_ref[...] = _ln(_gelu(proj(_OZ + SGU_WIDTH, SGU_WIDTH))) * sg_ref[...] + sb_ref[...]
    gate_ref[...] = jax.nn.sigmoid(proj(_OM, 2 * D_MODEL) + bm_ref[...])


def _proj(x2, sc, sh, wall, bm, cosk, sink, sg, sb, seq):
    n, d = x2.shape
    tm = TM_PROJ
    nst = seq // tm
    row = lambda w: pl.BlockSpec((tm, w), lambda i: (i, 0))
    per_batch = pl.BlockSpec((1, 1, d), lambda i: (i // nst, 0, 0))
    const = lambda shp: pl.BlockSpec(shp, lambda i: (0,) * len(shp))
    pos = pl.BlockSpec((tm, LANES), lambda i: (i % nst, 0))
    outs = [(NSA_WIDTH, F32), (KV_WIDTH, F32), (KV_WIDTH, F32), (KV_WIDTH, BF16), (KV_WIDTH, BF16),
            (KV_WIDTH, BF16), (KV_WIDTH, BF16), (2 * LANES, F32), (SGU_WIDTH, F32), (SGU_WIDTH, F32),
            (2 * D_MODEL, F32)]
    return pl.pallas_call(
        _proj_body,
        grid=(n // tm,),
        in_specs=[row(d), per_batch, per_batch, const((d, _WCOLS)), const((1, 2 * D_MODEL)),
                  pos, pos, const((1, SGU_WIDTH)), const((1, SGU_WIDTH))],
        out_specs=[row(w) for w, _ in outs],
        out_shape=[jax.ShapeDtypeStruct((n, w), dt) for w, dt in outs],
        compiler_params=_cparams(1, 56),
    )(x2, sc, sh, wall, bm, cosk, sink, sg, sb)


def _cmp_body(t_ref, pos_ref, w1_ref, b1_ref, w2_ref, b2_ref, o_ref):
    half = (CMP_BLOCK // 2) * HEAD_DIM
    t = t_ref[0, 0, 0]
    pos = pos_ref[0]
    ta = (t + pos[:, :half]).astype(BF16)
    tb = (t + pos[:, half:]).astype(BF16)
    a = _dot(ta, w1_ref[0, :half, :].astype(BF16))
    b = _dot(tb, w1_ref[0, half:, :].astype(BF16))
    nrow = t.shape[0]
    b_next = pltpu.roll(b, nrow - 1, axis=0)
    hid = _gelu(a + b_next + b1_ref[0])
    out = _dot(hid.astype(BF16), w2_ref[0].astype(BF16)) + b2_ref[0]
    o_ref[0, 0, 0] = out.astype(o_ref.dtype)


def _compress(t16, pos, w1, b1, w2, b2):
    _, bsz, g, nr, dd = t16.shape
    per_kind = lambda shp: pl.BlockSpec((1,) + shp, lambda k, b, gg: (k,) + (0,) * len(shp))
    blk = lambda w: pl.BlockSpec((1, 1, 1, nr, w), lambda k, b, gg: (k, b, gg, 0, 0))
    return pl.pallas_call(
        _cmp_body,
        grid=(2, bsz, g),
        in_specs=[blk(dd), per_kind((1, CMP_BLOCK * HEAD_DIM)), per_kind((CMP_BLOCK * HEAD_DIM, CMP_HIDDEN)),
                  per_kind((1, CMP_HIDDEN)), per_kind((CMP_HIDDEN, HEAD_DIM)), per_kind((1, HEAD_DIM))],
        out_specs=blk(HEAD_DIM),
        out_shape=jax.ShapeDtypeStruct((2, bsz, g, nr, HEAD_DIM), BF16),
        compiler_params=_cparams(3, 48),
    )(t16, pos, w1, b1, w2, b2)


def _masked_softmax_rows(s, mask):
    sm = jnp.where(mask, s, NEG_INF)
    m = jnp.max(sm, axis=-1, keepdims=True)
    e = jnp.where(mask, jnp.exp(sm - m), 0.0)
    l = jnp.sum(e, axis=-1, keepdims=True)
    return e / jnp.where(l > 0.0, l, 1.0)


def _nsa_body(q_ref, kc_ref, vc_ref, ks_ref, vs_ref, kw_ref, vw_ref, gn_ref, agg_ref, o_ref, *, n_cmp):
    qi = pl.program_id(2)
    q0 = qi * TQ
    t = q0 + lax.broadcasted_iota(jnp.int32, (TQ, 1), 0)
    qh = [q_ref[0][:, r * HEAD_DIM:(r + 1) * HEAD_DIM].astype(BF16) for r in range(NSA_REP)]

    kcm = kc_ref[0, 0]
    vcm = vc_ref[0, 0]
    ncp = kcm.shape[0]
    nidx = lax.broadcasted_iota(jnp.int32, (1, ncp), 1)
    mask_c = (nidx * CMP_STRIDE + (CMP_BLOCK - 1) <= t) & (nidx < n_cmp)
    o_c = []
    psum = jnp.zeros((TQ, ncp), F32)
    for r in range(NSA_REP):
        p = _masked_softmax_rows(_dot_nt(qh[r], kcm), mask_c)
        o_c.append(_dot(p.astype(BF16), vcm))
        psum = psum + p

    p_hi = psum.astype(BF16)
    p_lo = (psum - p_hi.astype(F32)).astype(BF16)
    agg = agg_ref[...]
    imp = _dot(p_hi, agg) + _dot(p_lo, agg)
    n_sel = imp.shape[1]
    blk = lax.broadcasted_iota(jnp.int32, (1, n_sel), 1)
    blk_f = blk.astype(F32)
    cur = t // SEL_BLOCK
    forced = (blk == 0) | (blk == cur) | (blk == cur - 1)
    score = jnp.where(forced, FORCE_SCORE, jnp.where(blk <= cur, imp, NEG_INF))
    sel = jnp.zeros((TQ, n_sel), jnp.bool_)
    for _ in range(min(SEL_TOPK, n_sel)):
        m = jnp.max(score, axis=-1, keepdims=True)
        first = jnp.min(jnp.where(score == m, blk_f, float(n_sel)), axis=-1, keepdims=True)
        hit = blk_f == first
        sel = sel | hit
        score = jnp.where(hit, -jnp.inf, score)
    sel_b = jnp.where(sel, 1.0, 0.0).astype(BF16)

    blk_row = lax.broadcasted_iota(jnp.int32, (n_sel, 1), 0)
    kcol = lax.broadcasted_iota(jnp.int32, (1, TK), 1)

    def kv_step(j, carry):
        ms, ls, accs = carry
        k0 = pl.multiple_of(j * TK, TK)
        kt = ks_ref[0, 0, pl.ds(k0, TK), :]
        vt = vs_ref[0, 0, pl.ds(k0, TK), :]
        kpos = k0 + kcol
        expand = jnp.where(blk_row == kpos // SEL_BLOCK, 1.0, 0.0).astype(BF16)
        mask = (_dot(sel_b, expand) > 0.5) & (kpos <= t)
        new_m, new_l, new_acc = [], [], []
        for r in range(NSA_REP):
            s = _dot_nt(qh[r], kt)
            m_new = jnp.maximum(ms[r], jnp.max(jnp.where(mask, s, NEG_INF), axis=-1, keepdims=True))
            p = jnp.where(mask, jnp.exp(s - m_new), 0.0)
            alpha = jnp.exp(ms[r] - m_new)
            new_m.append(m_new)
            new_l.append(alpha * ls[r] + jnp.sum(p, axis=-1, keepdims=True))
            new_acc.append(alpha * accs[r] + _dot(p.astype(BF16), vt))
        return tuple(new_m), tuple(new_l), tuple(new_acc)

    init = (tuple(jnp.full((TQ, 1), NEG_INF, F32) for _ in range(NSA_REP)),
            tuple(jnp.zeros((TQ, 1), F32) for _ in range(NSA_REP)),
            tuple(jnp.zeros((TQ, HEAD_DIM), F32) for _ in range(NSA_REP)))
    n_kv = (q0 + TQ - 1) // TK + 1
    _, ls, accs = lax.fori_loop(0, n_kv, kv_step, init)
    o_s = [accs[r] / ls[r] for r in range(NSA_REP)]

    span = TQ + WINDOW
    w0 = pl.multiple_of(q0, TQ)
    kwt = kw_ref[0, 0, pl.ds(w0, span), :]
    vwt = vw_ref[0, 0, pl.ds(w0, span), :]
    kpos_w = q0 - WINDOW + lax.broadcasted_iota(jnp.int32, (1, span), 1)
    dlt = t - kpos_w
    mask_w = (dlt >= 0) & (dlt < WINDOW) & (kpos_w >= 0)
    o_w = [_dot(_masked_softmax_rows(_dot_nt(qh[r], kwt), mask_w).astype(BF16), vwt) for r in range(NSA_REP)]

    gn = gn_ref[0]
    outs = []
    for r in range(NSA_REP):
        g = [gn[:, 3 * r + c:3 * r + c + 1] for c in range(3)]
        outs.append(g[0] * o_c[r] + g[1] * o_s[r] + g[2] * o_w[r])
    o_ref[0] = jnp.concatenate(outs, axis=1)


def _nsa(q, kc, vc, ks, vs, kwp, vwp, gn, agg, n_cmp):
    bsz, seq, _ = q.shape
    g = NSA_KV_GROUPS
    gw = NSA_REP * HEAD_DIM
    per_bg = lambda a: pl.BlockSpec((1, 1) + a.shape[2:], lambda b, gg, i: (b, gg, 0, 0))
    return pl.pallas_call(
        functools.partial(_nsa_body, n_cmp=n_cmp),
        grid=(bsz, g, seq // TQ),
        in_specs=[pl.BlockSpec((1, TQ, gw), lambda b, gg, i: (b, i, gg)),
                  per_bg(kc), per_bg(vc), per_bg(ks), per_bg(vs), per_bg(kwp), per_bg(vwp),
                  pl.BlockSpec((1, TQ, LANES), lambda b, gg, i: (b, i, gg)),
                  pl.BlockSpec(agg.shape, lambda b, gg, i: (0, 0))],
        out_specs=pl.BlockSpec((1, TQ, gw), lambda b, gg, i: (b, i, gg)),
        out_shape=jax.ShapeDtypeStruct((bsz, seq, NSA_WIDTH), F32),
        compiler_params=_cparams(3, 56),
    )(q, kc, vc, ks, vs, kwp, vwp, gn, agg)


def _merge_body(on_ref, u_ref, v_ref, gate_ref, x_ref, gt_ref, sc_ref, sh_ref, ws_ref, bs_ref,
                wb_ref, wo_ref, g1_ref, b1_ref, wq_ref, x1_ref, h2_ref, qp_ref, *, alpha):
    tm = x_ref.shape[0]
    row = lax.broadcasted_iota(jnp.int32, (SGU_CHUNK, SGU_CHUNK), 0)
    col = lax.broadcasted_iota(jnp.int32, (SGU_CHUNK, SGU_CHUNK), 1)
    lane_group = lax.broadcasted_iota(jnp.int32, (1, SGU_WIDTH), 1) // SGU_GROUP_DIM
    mixed = []
    for c in range(tm // SGU_CHUNK):
        v = v_ref[c * SGU_CHUNK:(c + 1) * SGU_CHUNK, :]
        acc = jnp.zeros((SGU_CHUNK, SGU_WIDTH), F32)
        for g in range(SGU_GROUPS):
            wg = jnp.where(col <= row, ws_ref[g], 0.0).astype(BF16)
            vg = jnp.where(lane_group == g, v, 0.0).astype(BF16)
            acc = acc + _dot(wg, vg)
        mixed.append(acc + bs_ref[...])
    o_sgu = u_ref[...] * jnp.concatenate(mixed, axis=0)
    gate = gate_ref[...]
    merged = (gate[:, :D_MODEL] * _dot(on_ref[...].astype(BF16), wb_ref[0])
              + gate[:, D_MODEL:] * _dot(o_sgu.astype(BF16), wb_ref[1]))
    mix = _dot(merged.astype(BF16), wo_ref[...])
    x1 = _ln(alpha * x_ref[...] + gt_ref[0] * mix) * g1_ref[...] + b1_ref[...]
    x1_ref[...] = x1
    h2 = _ln(x1) * (1.0 + sc_ref[0]) + sh_ref[0]
    h2_ref[...] = h2
    qp_ref[...] = _dot(h2.astype(BF16), wq_ref[...])


def _merge(o_nsa, u, v, gate, x2, gt1, sc2, sh2, ws, bs, wb, wo, g1, b1, wq, seq, alpha):
    n, d = x2.shape
    tm = TM_MERGE
    nst = seq // tm
    row = lambda w: pl.BlockSpec((tm, w), lambda i: (i, 0))
    per_batch = pl.BlockSpec((1, 1, d), lambda i: (i // nst, 0, 0))
    const = lambda a: pl.BlockSpec(a.shape, lambda i: (0,) * a.ndim)
    nq = wq.shape[1]
    return pl.pallas_call(
        functools.partial(_merge_body, alpha=alpha),
        grid=(n // tm,),
        in_specs=[row(NSA_WIDTH), row(SGU_WIDTH), row(SGU_WIDTH), row(2 * d), row(d),
                  per_batch, per_batch, per_batch, const(ws), const(bs), const(wb), const(wo),
                  const(g1), const(b1), const(wq)],
        out_specs=[row(d), row(d), row(nq)],
        out_shape=[jax.ShapeDtypeStruct((n, d), F32), jax.ShapeDtypeStruct((n, d), F32),
                   jax.ShapeDtypeStruct((n, nq), F32)],
        compiler_params=_cparams(1, 56),
    )(o_nsa, u, v, gate, x2, gt1, sc2, sh2, ws, bs, wb, wo, g1, b1, wq)


def _topk_rows(s, k):
    nrow = s.shape[0]
    rid = lax.broadcasted_iota(jnp.int32, s.shape, 0).astype(F32)
    vals, rows = [], []
    for _ in range(k):
        m = jnp.max(s, axis=0, keepdims=True)
        first = jnp.min(jnp.where(s == m, rid, float(nrow)), axis=0, keepdims=True)
        s = jnp.where(rid == first, -jnp.inf, s)
        vals.append(m)
        rows.append(first)
    return vals, rows


def _retrieve_body(qp_ref, keys_ref, idx_ref, gate_ref):
    k = PEER_TOPK
    for h in range(PEER_HEADS):
        tops = []
        for c in range(2):
            off = (h * 2 + c) * PEER_HALF
            qh = qp_ref[:, off:off + PEER_HALF].astype(BF16)
            s = _dot_nt(keys_ref[h, c].astype(BF16), qh)
            tops.append(_topk_rows(s, k))
        (v1, i1), (v2, i2) = tops
        v2m = jnp.concatenate(v2, axis=0)
        i2m = jnp.concatenate(i2, axis=0)
        cand = jnp.concatenate([v1[a] + v2m for a in range(k)], axis=0)
        cidx = jnp.concatenate([i1[a] * float(PEER_NKEYS) + i2m for a in range(k)], axis=0)
        rid = lax.broadcasted_iota(jnp.int32, cand.shape, 0).astype(F32)
        top_s, top_e = [], []
        for _ in range(k):
            m = jnp.max(cand, axis=0, keepdims=True)
            first = jnp.min(jnp.where(cand == m, rid, float(k * k)), axis=0, keepdims=True)
            hit = rid == first
            top_e.append(jnp.sum(jnp.where(hit, cidx, 0.0), axis=0, keepdims=True))
            cand = jnp.where(hit, -jnp.inf, cand)
            top_s.append(m)
        ts = jnp.concatenate(top_s, axis=0)
        e = jnp.exp(ts - ts[0:1])
        gate_ref[0, h * k:(h + 1) * k, :] = e / jnp.sum(e, axis=0, keepdims=True)
        idx_ref[0, h * k:(h + 1) * k, :] = jnp.concatenate(top_e, axis=0).astype(jnp.int32)


def _retrieve(qp, keys):
    n = qp.shape[0]
    nt = n // TC
    out = pl.BlockSpec((1, PEER_PAIRS, TC), lambda i: (i, 0, 0))
    return pl.pallas_call(
        _retrieve_body,
        grid=(nt,),
        in_specs=[pl.BlockSpec((TC, qp.shape[1]), lambda i: (i, 0)),
                  pl.BlockSpec(keys.shape, lambda i: (0, 0, 0, 0))],
        out_specs=[out, out],
        out_shape=[jax.ShapeDtypeStruct((nt, PEER_PAIRS, TC), jnp.int32),
                   jax.ShapeDtypeStruct((nt, PEER_PAIRS, TC), F32)],
        compiler_params=_cparams(1, 48),
    )(qp, keys)


def _unpack_row(row):
    lo = lax.bitcast_convert_type(lax.shift_left(row, jnp.uint32(16)), F32)
    hi = lax.bitcast_convert_type(row & jnp.uint32(0xFFFF0000), F32)
    return lo, hi


def _expert_act_body(idx_ref, h_ref, gate_ref, tab_ref, w_ref, slot_ref, act_ref):
    half = D_MODEL // 2
    ones = jnp.ones((8, LANES), F32)

    def token(t, carry):
        x = h_ref[t]
        xlo, xhi = x[:, :half], x[:, half:]
        base = t * PEER_PAIRS
        for p in range(PEER_PAIRS):
            lo, hi = _unpack_row(tab_ref[idx_ref[base + p]])
            prod = lo * xlo + hi * xhi
            part = (prod[:, 0:LANES] + prod[:, LANES:2 * LANES]) + (prod[:, 2 * LANES:3 * LANES] + prod[:, 3 * LANES:])
            slot_ref[p:p + 1, :] = part
        sums = lax.dot_general(ones, slot_ref[...], (((1,), (1,)), ((), ())),
                               precision=lax.Precision.HIGHEST, preferred_element_type=F32)
        act_ref[pl.ds(t, 1), :] = sums[0:1, :]
        return carry

    lax.fori_loop(0, TB, token, 0)
    w_ref[...] = gate_ref[...] * _gelu(act_ref[...])


def _expert_act(idx_flat, h3, gate, tab):
    n = h3.shape[0]
    return pl.pallas_call(
        _expert_act_body,
        grid=(n // TB,),
        in_specs=[pl.BlockSpec((TB * PEER_PAIRS,), lambda i: (i,), memory_space=pltpu.SMEM),
                  pl.BlockSpec((TB, 1, D_MODEL), lambda i: (i, 0, 0)),
                  pl.BlockSpec((TB, PEER_PAIRS), lambda i: (i, 0)),
                  pl.BlockSpec(tab.shape, lambda i: (0, 0, 0), pipeline_mode=pl.Buffered(1))],
        out_specs=pl.BlockSpec((TB, PEER_PAIRS), lambda i: (i, 0)),
        out_shape=jax.ShapeDtypeStruct((n, PEER_PAIRS), F32),
        scratch_shapes=[pltpu.VMEM((PEER_PAIRS, LANES), F32), pltpu.VMEM((TB, PEER_PAIRS), F32)],
        compiler_params=_cparams(1, 48),
    )(idx_flat, h3, gate, tab)


def _expert_out_body(idx_ref, w_ref, tab_ref, o_ref):
    half = D_MODEL // 2
    n_acc = 4

    def token(t, carry):
        base = t * PEER_PAIRS
        acc_lo = [jnp.zeros((1, half), F32) for _ in range(n_acc)]
        acc_hi = [jnp.zeros((1, half), F32) for _ in range(n_acc)]
        for p in range(PEER_PAIRS):
            lo, hi = _unpack_row(tab_ref[idx_ref[base + p]])
            w = w_ref[base + p]
            acc_lo[p % n_acc] = acc_lo[p % n_acc] + w * lo
            acc_hi[p % n_acc] = acc_hi[p % n_acc] + w * hi
        lo = (acc_lo[0] + acc_lo[1]) + (acc_lo[2] + acc_lo[3])
        hi = (acc_hi[0] + acc_hi[1]) + (acc_hi[2] + acc_hi[3])
        o_ref[t] = jnp.concatenate([lo, hi], axis=1)
        return carry

    lax.fori_loop(0, TB, token, 0)


def _expert_out(idx_flat, w_flat, tab, n):
    smem = pl.BlockSpec((TB * PEER_PAIRS,), lambda i: (i,), memory_space=pltpu.SMEM)
    return pl.pallas_call(
        _expert_out_body,
        grid=(n // TB,),
        in_specs=[smem, smem,
                  pl.BlockSpec(tab.shape, lambda i: (0, 0, 0), pipeline_mode=pl.Buffered(1))],
        out_specs=pl.BlockSpec((TB, 1, D_MODEL), lambda i: (i, 0, 0)),
        out_shape=jax.ShapeDtypeStruct((n, 1, D_MODEL), F32),
        compiler_params=_cparams(1, 48),
    )(idx_flat, w_flat, tab)


def _final_body(x1_ref, f_ref, gt_ref, g_ref, b_ref, o_ref, *, alpha):
    o_ref[...] = _ln(alpha * x1_ref[...] + gt_ref[0] * f_ref[...]) * g_ref[...] + b_ref[...]


def _final(x1, ffn, gt2, g2, b2, seq, alpha):
    n, d = x1.shape
    tm = TM_LN
    nst = seq // tm
    row = pl.BlockSpec((tm, d), lambda i: (i, 0))
    const = pl.BlockSpec((1, d), lambda i: (0, 0))
    return pl.pallas_call(
        functools.partial(_final_body, alpha=alpha),
        grid=(n // tm,),
        in_specs=[row, row, pl.BlockSpec((1, 1, d), lambda i: (i // nst, 0, 0)), const, const],
        out_specs=row,
        out_shape=jax.ShapeDtypeStruct((n, d), F32),
        compiler_params=_cparams(1, 32),
    )(x1, ffn, gt2, g2, b2)


def _swap_halves(w, heads):
    d = w.shape[0]
    w4 = w.reshape(d, heads, 2, HEAD_DIM // 2)
    return jnp.flip(w4, axis=2).reshape(d, heads * HEAD_DIM)


def _fused_in_weight(w_in, w_merge):
    d = w_in.shape[0]
    o = 0
    parts = {}
    for name, width in (("q", NSA_WIDTH), ("kc", KV_WIDTH), ("vc", KV_WIDTH), ("ks", KV_WIDTH),
                        ("vs", KV_WIDTH), ("kw", KV_WIDTH), ("vw", KV_WIDTH), ("g", 3 * NSA_HEADS),
                        ("z", 2 * SGU_WIDTH)):
        parts[name] = w_in[:, o:o + width]
        o += width
    wq = parts["q"] * (HEAD_DIM ** -0.5)
    gcols = 3 * NSA_REP
    wg = jnp.zeros((d, 2 * LANES), w_in.dtype)
    wg = wg.at[:, :gcols].set(parts["g"][:, :gcols]).at[:, LANES:LANES + gcols].set(parts["g"][:, gcols:])
    g = NSA_KV_GROUPS
    cols = [wq, _swap_halves(wq, NSA_HEADS),
            parts["kc"], _swap_halves(parts["kc"], g), parts["ks"], _swap_halves(parts["ks"], g),
            parts["kw"], _swap_halves(parts["kw"], g), parts["vc"], parts["vs"], parts["vw"],
            wg, parts["z"], w_merge]
    return jnp.concatenate(cols, axis=1).astype(BF16)


def _rope_tables(seq):
    half = HEAD_DIM // 2
    pos = jnp.arange(seq, dtype=F32)
    inv_freq = ROPE_THETA ** (-jnp.arange(half, dtype=F32) / half)
    ang = pos[:, None] * inv_freq[None, :]
    cos, sin = jnp.cos(ang), jnp.sin(ang)
    reps = LANES // HEAD_DIM
    cosk = jnp.tile(jnp.concatenate([cos, cos], axis=1), (1, reps))
    sink = jnp.tile(jnp.concatenate([-sin, sin], axis=1), (1, reps))
    return cosk, sink


def _sel_aggregation(n_cmp_pad, n_sel):
    c0 = jnp.arange(n_cmp_pad)[:, None] * CMP_STRIDE
    s0 = jnp.arange(n_sel)[None, :] * SEL_BLOCK
    ov = jnp.clip(jnp.minimum(c0 + CMP_BLOCK, s0 + SEL_BLOCK) - jnp.maximum(c0, s0), 0, None)
    return (ov / CMP_BLOCK).astype(BF16)


def _pack_table(tab):
    half = tab.shape[1] // 2
    bits = lax.bitcast_convert_type(tab.astype(BF16), jnp.uint16).astype(jnp.uint32)
    packed = (bits[:, half:] << 16) | bits[:, :half]
    return packed.reshape(tab.shape[0], 1, half)


def _split_groups(a, bsz, seq):
    return a.reshape(bsz, seq, NSA_KV_GROUPS, HEAD_DIM).transpose(0, 2, 1, 3)


def kernel(x, c, w_ada, b_ada, w_in, cmp_pos, cmp_w1, cmp_b1, cmp_w2, cmp_b2, sgu_ln_g, sgu_ln_b, sgu_w, sgu_b, w_branch, w_merge, b_merge, w_out, ln1_g, ln1_b, peer_wq, peer_keys, peer_u, peer_v, ln2_g, ln2_b):
    bsz, seq, d = x.shape
    n = bsz * seq
    depth = w_ada.shape[0]
    alpha = (2.0 * depth) ** 0.25
    n_cmp = (seq - CMP_BLOCK) // CMP_STRIDE + 1
    n_half = seq // CMP_STRIDE
    n_sel = seq // SEL_BLOCK
    cosk, sink = _rope_tables(seq)
    agg = _sel_aggregation(n_half, n_sel)
    x2 = x.reshape(n, d)
    for l in range(depth):
        mod = _ada(c, w_ada[l], b_ada[l])
        sh1, sc1, gt1, sh2, sc2, gt2 = [m.reshape(bsz, 1, d) for m in jnp.split(mod, 6, axis=-1)]

        wall = _fused_in_weight(w_in[l], w_merge[l])
        (q, kc_r, vc_r, ks, vs, kw, vw, gn, u, v, gate) = _proj(
            x2, sc1, sh1, wall, b_merge[l].reshape(1, -1), cosk, sink,
            sgu_ln_g[l].reshape(1, -1), sgu_ln_b[l].reshape(1, -1), seq)

        t16 = jnp.stack([_split_groups(kc_r, bsz, seq), _split_groups(vc_r, bsz, seq)])
        t16 = t16.reshape(2, bsz, NSA_KV_GROUPS, n_half, CMP_STRIDE * HEAD_DIM)
        cmp_kv = _compress(t16, cmp_pos[l].reshape(2, 1, CMP_BLOCK * HEAD_DIM), cmp_w1[l],
                           cmp_b1[l].reshape(2, 1, CMP_HIDDEN), cmp_w2[l], cmp_b2[l].reshape(2, 1, HEAD_DIM))

        front = ((0, 0), (0, 0), (WINDOW, 0), (0, 0))
        o_nsa = _nsa(q.reshape(bsz, seq, NSA_WIDTH), cmp_kv[0], cmp_kv[1],
                     _split_groups(ks, bsz, seq), _split_groups(vs, bsz, seq),
                     jnp.pad(_split_groups(kw, bsz, seq), front), jnp.pad(_split_groups(vw, bsz, seq), front),
                     gn.reshape(bsz, seq, 2 * LANES), agg, n_cmp)

        bs = jnp.repeat(sgu_b[l].T, SGU_GROUP_DIM, axis=1)
        x1, h2, qp = _merge(o_nsa.reshape(n, NSA_WIDTH), u, v, gate, x2, gt1, sc2, sh2,
                            sgu_w[l], bs, w_branch[l].astype(BF16), w_out[l].astype(BF16),
                            ln1_g[l].reshape(1, d), ln1_b[l].reshape(1, d), peer_wq[l].astype(BF16), seq, alpha)

        idx_t, gate_t = _retrieve(qp, peer_keys[l])
        idx_flat = idx_t.transpose(0, 2, 1).reshape(n * PEER_PAIRS)
        gate_tok = gate_t.transpose(0, 2, 1).reshape(n, PEER_PAIRS)
        w_tok = _expert_act(idx_flat, h2.reshape(n, 1, d), gate_tok, _pack_table(peer_u[l]))
        ffn = _expert_out(idx_flat, w_tok.reshape(n * PEER_PAIRS), _pack_table(peer_v[l]), n)
        x2 = _final(x1, ffn.reshape(n, d), gt2, ln2_g[l].reshape(1, d), ln2_b[l].reshape(1, d), seq, alpha)
    return x2.reshape(bsz, seq, d)
```

```python
import functools

import jax
import jax.numpy as jnp
from jax import lax
from jax.experimental import pallas as pl
from jax.experimental.pallas import tpu as pltpu

D_MODEL = 1024
NSA_HEADS = 8
NSA_KV_GROUPS = 2
NSA_REP = NSA_HEADS // NSA_KV_GROUPS
HEAD_DIM = 64
NSA_WIDTH = NSA_HEADS * HEAD_DIM
KV_WIDTH = NSA_KV_GROUPS * HEAD_DIM
CMP_BLOCK = 32
CMP_STRIDE = 16
CMP_HIDDEN = 2 * HEAD_DIM
SEL_BLOCK = 64
SEL_TOPK = 16
WINDOW = 512
ROPE_THETA = 10000.0
SGU_GROUPS = 8
SGU_WIDTH = D_MODEL // 2
SGU_GROUP_DIM = SGU_WIDTH // SGU_GROUPS
SGU_CHUNK = 128
PEER_HEADS = 8
PEER_NKEYS = 128
PEER_EXPERTS = PEER_NKEYS * PEER_NKEYS
PEER_QDIM = 256
PEER_HALF = PEER_QDIM // 2
PEER_TOPK = 16
PEER_PAIRS = PEER_HEADS * PEER_TOPK
LN_EPS = 1e-5
NEG_INF = -1e30
FORCE_SCORE = 1e9

LANES = 128
MIB = 1024 * 1024
BF16 = jnp.bfloat16
F32 = jnp.float32

TM_PROJ = 256
TM_MERGE = 256
TM_LN = 512
TQ = 128
TK = 512
TC = 128
TB = 64

_OQ, _OQS = 0, 512
_OKC, _OKCS, _OKS, _OKSS, _OKW, _OKWS = 1024, 1152, 1280, 1408, 1536, 1664
_OVC, _OVS, _OVW = 1792, 1920, 2048
_OG = 2176
_OZ = 2432
_OM = 3456
_WCOLS = 5504


def _cparams(n_axes, vmem_mib):
    return pltpu.CompilerParams(
        dimension_semantics=("parallel",) * n_axes,
        vmem_limit_bytes=vmem_mib * MIB)


def _ln(x):
    mu = jnp.mean(x, axis=-1, keepdims=True)
    xc = x - mu
    var = jnp.mean(xc * xc, axis=-1, keepdims=True)
    return xc * lax.rsqrt(var + LN_EPS)


def _gelu(x):
    return 0.5 * x * (1.0 + lax.erf(x * (2.0 ** -0.5)))


def _dot(a, b):
    return jnp.dot(a, b, preferred_element_type=F32)


def _dot_nt(a, b):
    return lax.dot_general(a, b, (((1,), (1,)), ((), ())), preferred_element_type=F32)


def _ada_body(c_ref, w_ref, b_ref, o_ref):
    cv = c_ref[...]
    a = cv * jax.nn.sigmoid(cv)
    o_ref[...] = _dot(a.astype(BF16), w_ref[...].astype(BF16)) + b_ref[...]


def _ada(c, w, b):
    bsz, d = c.shape
    n = w.shape[1]
    tn = 1024
    return pl.pallas_call(
        _ada_body,
        grid=(n // tn,),
        in_specs=[pl.BlockSpec((bsz, d), lambda j: (0, 0)),
                  pl.BlockSpec((d, tn), lambda j: (0, j)),
                  pl.BlockSpec((1, tn), lambda j: (0, j))],
        out_specs=pl.BlockSpec((bsz, tn), lambda j: (0, j)),
        out_shape=jax.ShapeDtypeStruct((bsz, n), F32),
        compiler_params=_cparams(1, 32),
    )(c, w, b.reshape(1, n))


def _proj_body(x_ref, sc_ref, sh_ref, w_ref, bm_ref, cos_ref, sin_ref, sg_ref, sb_ref,
               q_ref, kc_ref, vc_ref, ks_ref, vs_ref, kw_ref, vw_ref, gn_ref, u_ref, v_ref, gate_ref):
    h = _ln(x_ref[...]) * (1.0 + sc_ref[0]) + sh_ref[0]
    hb = h.astype(BF16)

    def proj(off, width):
        return _dot(hb, w_ref[:, off:off + width])

    cosk = cos_ref[...]
    sink = sin_ref[...]
    cosq = jnp.concatenate([cosk] * (NSA_WIDTH // LANES), axis=1)
    sinq = jnp.concatenate([sink] * (NSA_WIDTH // LANES), axis=1)
    q_ref[...] = proj(_OQ, NSA_WIDTH) * cosq + proj(_OQS, NSA_WIDTH) * sinq
    kc_ref[...] = proj(_OKC, KV_WIDTH) * cosk + proj(_OKCS, KV_WIDTH) * sink
    ks_ref[...] = (proj(_OKS, KV_WIDTH) * cosk + proj(_OKSS, KV_WIDTH) * sink).astype(BF16)
    kw_ref[...] = (proj(_OKW, KV_WIDTH) * cosk + proj(_OKWS, KV_WIDTH) * sink).astype(BF16)
    vc_ref[...] = proj(_OVC, KV_WIDTH)
    vs_ref[...] = proj(_OVS, KV_WIDTH).astype(BF16)
    vw_ref[...] = proj(_OVW, KV_WIDTH).astype(BF16)
    gn_ref[...] = jax.nn.sigmoid(proj(_OG, 2 * LANES))
    u_ref[...] = _gelu(proj(_OZ, SGU_WIDTH))
    v_ref[...] = _ln(_gelu(proj(_OZ + SGU_WIDTH, SGU_WIDTH))) * sg_ref[...] + sb_ref[...]
    gate_ref[...] = jax.nn.sigmoid(proj(_OM, 2 * D_MODEL) + bm_ref[...])


def _proj(x2, sc, sh, wall, bm, cosk, sink, sg, sb, seq):
    n, d = x2.shape
    tm = TM_PROJ
    nst = seq // tm
    row = lambda w: pl.BlockSpec((tm, w), lambda i: (i, 0))
    per_batch = pl.BlockSpec((1, 1, d), lambda i: (i // nst, 0, 0))
    const = lambda shp: pl.BlockSpec(shp, lambda i: (0,) * len(shp))
    pos = pl.BlockSpec((tm, LANES), lambda i: (i % nst, 0))
    outs = [(NSA_WIDTH, F32), (KV_WIDTH, F32), (KV_WIDTH, F32), (KV_WIDTH, BF16), (KV_WIDTH, BF16),
            (KV_WIDTH, BF16), (KV_WIDTH, BF16), (2 * LANES, F32), (SGU_WIDTH, F32), (SGU_WIDTH, F32),
            (2 * D_MODEL, F32)]
    return pl.pallas_call(
        _proj_body,
        grid=(n // tm,),
        in_specs=[row(d), per_batch, per_batch, const((d, _WCOLS)), const((1, 2 * D_MODEL)),
                  pos, pos, const((1, SGU_WIDTH)), const((1, SGU_WIDTH))],
        out_specs=[row(w) for w, _ in outs],
        out_shape=[jax.ShapeDtypeStruct((n, w), dt) for w, dt in outs],
        compiler_params=_cparams(1, 56),
    )(x2, sc, sh, wall, bm, cosk, sink, sg, sb)


def _cmp_body(t_ref, pos_ref, w1_ref, b1_ref, w2_ref, b2_ref, o_ref):
    half = (CMP_BLOCK // 2) * HEAD_DIM
    t = t_ref[0, 0, 0]
    pos = pos_ref[0]
    ta = (t + pos[:, :half]).astype(BF16)
    tb = (t + pos[:, half:]).astype(BF16)
    a = _dot(ta, w1_ref[0, :half, :].astype(BF16))
    b = _dot(tb, w1_ref[0, half:, :].astype(BF16))
    nrow = t.shape[0]
    b_next = pltpu.roll(b, nrow - 1, axis=0)
    hid = _gelu(a + b_next + b1_ref[0])
    out = _dot(hid.astype(BF16), w2_ref[0].astype(BF16)) + b2_ref[0]
    o_ref[0, 0, 0] = out.astype(o_ref.dtype)


def _compress(t16, pos, w1, b1, w2, b2):
    _, bsz, g, nr, dd = t16.shape
    per_kind = lambda shp: pl.BlockSpec((1,) + shp, lambda k, b, gg: (k,) + (0,) * len(shp))
    blk = lambda w: pl.BlockSpec((1, 1, 1, nr, w), lambda k, b, gg: (k, b, gg, 0, 0))
    return pl.pallas_call(
        _cmp_body,
        grid=(2, bsz, g),
        in_specs=[blk(dd), per_kind((1, CMP_BLOCK * HEAD_DIM)), per_kind((CMP_BLOCK * HEAD_DIM, CMP_HIDDEN)),
                  per_kind((1, CMP_HIDDEN)), per_kind((CMP_HIDDEN, HEAD_DIM)), per_kind((1, HEAD_DIM))],
        out_specs=blk(HEAD_DIM),
        out_shape=jax.ShapeDtypeStruct((2, bsz, g, nr, HEAD_DIM), BF16),
        compiler_params=_cparams(3, 48),
    )(t16, pos, w1, b1, w2, b2)


def _masked_softmax_rows(s, mask):
    sm = jnp.where(mask, s, NEG_INF)
    m = jnp.max(sm, axis=-1, keepdims=True)
    e = jnp.where(mask, jnp.exp(sm - m), 0.0)
    l = jnp.sum(e, axis=-1, keepdims=True)
    return e / jnp.where(l > 0.0, l, 1.0)


def _nsa_body(q_ref, kc_ref, vc_ref, ks_ref, vs_ref, kw_ref, vw_ref, gn_ref, agg_ref, o_ref, *, n_cmp):
    qi = pl.program_id(2)
    q0 = qi * TQ
    t = q0 + lax.broadcasted_iota(jnp.int32, (TQ, 1), 0)
    qh = [q_ref[0][:, r * HEAD_DIM:(r + 1) * HEAD_DIM].astype(BF16) for r in range(NSA_REP)]

    kcm = kc_ref[0, 0]
    vcm = vc_ref[0, 0]
    ncp = kcm.shape[0]
    nidx = lax.broadcasted_iota(jnp.int32, (1, ncp), 1)
    mask_c = (nidx * CMP_STRIDE + (CMP_BLOCK - 1) <= t) & (nidx < n_cmp)
    o_c = []
    psum = jnp.zeros((TQ, ncp), F32)
    for r in range(NSA_REP):
        p = _masked_softmax_rows(_dot_nt(qh[r], kcm), mask_c)
        o_c.append(_dot(p.astype(BF16), vcm))
        psum = psum + p

    p_hi = psum.astype(BF16)
    p_lo = (psum - p_hi.astype(F32)).astype(BF16)
    agg = agg_ref[...]
    imp = _dot(p_hi, agg) + _dot(p_lo, agg)
    n_sel = imp.shape[1]
    blk = lax.broadcasted_iota(jnp.int32, (1, n_sel), 1)
    blk_f = blk.astype(F32)
    cur = t // SEL_BLOCK
    forced = (blk == 0) | (blk == cur) | (blk == cur - 1)
    score = jnp.where(forced, FORCE_SCORE, jnp.where(blk <= cur, imp, NEG_INF))
    sel = jnp.zeros((TQ, n_sel), jnp.bool_)
    for _ in range(min(SEL_TOPK, n_sel)):
        m = jnp.max(score, axis=-1, keepdims=True)
        first = jnp.min(jnp.where(score == m, blk_f, float(n_sel)), axis=-1, keepdims=True)
        hit = blk_f == first
        sel = sel | hit
        score = jnp.where(hit, -jnp.inf, score)
    sel_b = jnp.where(sel, 1.0, 0.0).astype(BF16)

    blk_row = lax.broadcasted_iota(jnp.int32, (n_sel, 1), 0)
    kcol = lax.broadcasted_iota(jnp.int32, (1, TK), 1)

    def kv_step(j, carry):
        ms, ls, accs = carry
        k0 = pl.multiple_of(j * TK, TK)
        kt = ks_ref[0, 0, pl.ds(k0, TK), :]
        vt = vs_ref[0, 0, pl.ds(k0, TK), :]
        kpos = k0 + kcol
        expand = jnp.where(blk_row == kpos // SEL_BLOCK, 1.0, 0.0).astype(BF16)
        mask = (_dot(sel_b, expand) > 0.5) & (kpos <= t)
        new_m, new_l, new_acc = [], [], []
        for r in range(NSA_REP):
            s = _dot_nt(qh[r], kt)
            m_new = jnp.maximum(ms[r], jnp.max(jnp.where(mask, s, NEG_INF), axis=-1, keepdims=True))
            p = jnp.where(mask, jnp.exp(s - m_new), 0.0)
            alpha = jnp.exp(ms[r] - m_new)
            new_m.append(m_new)
            new_l.append(alpha * ls[r] + jnp.sum(p, axis=-1, keepdims=True))
            new_acc.append(alpha * accs[r] + _dot(p.astype(BF16), vt))
        return tuple(new_m), tuple(new_l), tuple(new_acc)

    init = (tuple(jnp.full((TQ, 1), NEG_INF, F32) for _ in range(NSA_REP)),
            tuple(jnp.zeros((TQ, 1), F32) for _ in range(NSA_REP)),
            tuple(jnp.zeros((TQ, HEAD_DIM), F32) for _ in range(NSA_REP)))
    n_kv = (q0 + TQ - 1) // TK + 1
    _, ls, accs = lax.fori_loop(0, n_kv, kv_step, init)
    o_s = [accs[r] / ls[r] for r in range(NSA_REP)]

    span = TQ + WINDOW
    w0 = pl.multiple_of(q0, TQ)
    kwt = kw_ref[0, 0, pl.ds(w0, span), :]
    vwt = vw_ref[0, 0, pl.ds(w0, span), :]
    kpos_w = q0 - WINDOW + lax.broadcasted_iota(jnp.int32, (1, span), 1)
    dlt = t - kpos_w
    mask_w = (dlt >= 0) & (dlt < WINDOW) & (kpos_w >= 0)
    o_w = [_dot(_masked_softmax_rows(_dot_nt(qh[r], kwt), mask_w).astype(BF16), vwt) for r in range(NSA_REP)]

    gn = gn_ref[0]
    outs = []
    for r in range(NSA_REP):
        g = [gn[:, 3 * r + c:3 * r + c + 1] for c in range(3)]
        outs.append(g[0] * o_c[r] + g[1] * o_s[r] + g[2] * o_w[r])
    o_ref[0] = jnp.concatenate(outs, axis=1)


def _nsa(q, kc, vc, ks, vs, kwp, vwp, gn, agg, n_cmp):
    bsz, seq, _ = q.shape
    g = NSA_KV_GROUPS
    gw = NSA_REP * HEAD_DIM
    per_bg = lambda a: pl.BlockSpec((1, 1) + a.shape[2:], lambda b, gg, i: (b, gg, 0, 0))
    return pl.pallas_call(
        functools.partial(_nsa_body, n_cmp=n_cmp),
        grid=(bsz, g, seq // TQ),
        in_specs=[pl.BlockSpec((1, TQ, gw), lambda b, gg, i: (b, i, gg)),
                  per_bg(kc), per_bg(vc), per_bg(ks), per_bg(vs), per_bg(kwp), per_bg(vwp),
                  pl.BlockSpec((1, TQ, LANES), lambda b, gg, i: (b, i, gg)),
                  pl.BlockSpec(agg.shape, lambda b, gg, i: (0, 0))],
        out_specs=pl.BlockSpec((1, TQ, gw), lambda b, gg, i: (b, i, gg)),
        out_shape=jax.ShapeDtypeStruct((bsz, seq, NSA_WIDTH), F32),
        compiler_params=_cparams(3, 56),
    )(q, kc, vc, ks, vs, kwp, vwp, gn, agg)


def _merge_body(on_ref, u_ref, v_ref, gate_ref, x_ref, gt_ref, sc_ref, sh_ref, ws_ref, bs_ref,
                wb_ref, wo_ref, g1_ref, b1_ref, wq_ref, x1_ref, h2_ref, qp_ref, *, alpha):
    tm = x_ref.shape[0]
    row = lax.broadcasted_iota(jnp.int32, (SGU_CHUNK, SGU_CHUNK), 0)
    col = lax.broadcasted_iota(jnp.int32, (SGU_CHUNK, SGU_CHUNK), 1)
    lane_group = lax.broadcasted_iota(jnp.int32, (1, SGU_WIDTH), 1) // SGU_GROUP_DIM
    mixed = []
    for c in range(tm // SGU_CHUNK):
        v = v_ref[c * SGU_CHUNK:(c + 1) * SGU_CHUNK, :]
        acc = jnp.zeros((SGU_CHUNK, SGU_WIDTH), F32)
        for g in range(SGU_GROUPS):
            wg = jnp.where(col <= row, ws_ref[g], 0.0).astype(BF16)
            vg = jnp.where(lane_group == g, v, 0.0).astype(BF16)
            acc = acc + _dot(wg, vg)
        mixed.append(acc + bs_ref[...])
    o_sgu = u_ref[...] * jnp.concatenate(mixed, axis=0)
    gate = gate_ref[...]
    merged = (gate[:, :D_MODEL] * _dot(on_ref[...].astype(BF16), wb_ref[0])
              + gate[:, D_MODEL:] * _dot(o_sgu.astype(BF16), wb_ref[1]))
    mix = _dot(merged.astype(BF16), wo_ref[...])
    x1 = _ln(alpha * x_ref[...] + gt_ref[0] * mix) * g1_ref[...] + b1_ref[...]
    x1_ref[...] = x1
    h2 = _ln(x1) * (1.0 + sc_ref[0]) + sh_ref[0]
    h2_ref[...] = h2
    qp_ref[...] = _dot(h2.astype(BF16), wq_ref[...])


def _merge(o_nsa, u, v, gate, x2, gt1, sc2, sh2, ws, bs, wb, wo, g1, b1, wq, seq, alpha):
    n, d = x2.shape
    tm = TM_MERGE
    nst = seq // tm
    row = lambda w: pl.BlockSpec((tm, w), lambda i: (i, 0))
    per_batch = pl.BlockSpec((1, 1, d), lambda i: (i // nst, 0, 0))
    const = lambda a: pl.BlockSpec(a.shape, lambda i: (0,) * a.ndim)
    nq = wq.shape[1]
    return pl.pallas_call(
        functools.partial(_merge_body, alpha=alpha),
        grid=(n // tm,),
        in_specs=[row(NSA_WIDTH), row(SGU_WIDTH), row(SGU_WIDTH), row(2 * d), row(d),
                  per_batch, per_batch, per_batch, const(ws), const(bs), const(wb), const(wo),
                  const(g1), const(b1), const(wq)],
        out_specs=[row(d), row(d), row(nq)],
        out_shape=[jax.ShapeDtypeStruct((n, d), F32), jax.ShapeDtypeStruct((n, d), F32),
                   jax.ShapeDtypeStruct((n, nq), F32)],
        compiler_params=_cparams(1, 56),
    )(o_nsa, u, v, gate, x2, gt1, sc2, sh2, ws, bs, wb, wo, g1, b1, wq)


def _topk_rows(s, k):
    nrow = s.shape[0]
    rid = lax.broadcasted_iota(jnp.int32, s.shape, 0).astype(F32)
    vals, rows = [], []
    for _ in range(k):
        m = jnp.max(s, axis=0, keepdims=True)
        first = jnp.min(jnp.where(s == m, rid, float(nrow)), axis=0, keepdims=True)
        s = jnp.where(rid == first, -jnp.inf, s)
        vals.append(m)
        rows.append(first)
    return vals, rows


def _retrieve_body(qp_ref, keys_ref, idx_ref, gate_ref):
    k = PEER_TOPK
    for h in range(PEER_HEADS):
        tops = []
        for c in range(2):
            off = (h * 2 + c) * PEER_HALF
            qh = qp_ref[:, off:off + PEER_HALF].astype(BF16)
            s = _dot_nt(keys_ref[h, c].astype(BF16), qh)
            tops.append(_topk_rows(s, k))
        (v1, i1), (v2, i2) = tops
        v2m = jnp.concatenate(v2, axis=0)
        i2m = jnp.concatenate(i2, axis=0)
        cand = jnp.concatenate([v1[a] + v2m for a in range(k)], axis=0)
        cidx = jnp.concatenate([i1[a] * float(PEER_NKEYS) + i2m for a in range(k)], axis=0)
        rid = lax.broadcasted_iota(jnp.int32, cand.shape, 0).astype(F32)
        top_s, top_e = [], []
        for _ in range(k):
            m = jnp.max(cand, axis=0, keepdims=True)
            first = jnp.min(jnp.where(cand == m, rid, float(k * k)), axis=0, keepdims=True)
            hit = rid == first
            top_e.append(jnp.sum(jnp.where(hit, cidx, 0.0), axis=0, keepdims=True))
            cand = jnp.where(hit, -jnp.inf, cand)
            top_s.append(m)
        ts = jnp.concatenate(top_s, axis=0)
        e = jnp.exp(ts - ts[0:1])
        gate_ref[0, h * k:(h + 1) * k, :] = e / jnp.sum(e, axis=0, keepdims=True)
        idx_ref[0, h * k:(h + 1) * k, :] = jnp.concatenate(top_e, axis=0).astype(jnp.int32)


def _retrieve(qp, keys):
    n = qp.shape[0]
    nt = n // TC
    out = pl.BlockSpec((1, PEER_PAIRS, TC), lambda i: (i, 0, 0))
    return pl.pallas_call(
        _retrieve_body,
        grid=(nt,),
        in_specs=[pl.BlockSpec((TC, qp.shape[1]), lambda i: (i, 0)),
                  pl.BlockSpec(keys.shape, lambda i: (0, 0, 0, 0))],
        out_specs=[out, out],
        out_shape=[jax.ShapeDtypeStruct((nt, PEER_PAIRS, TC), jnp.int32),
                   jax.ShapeDtypeStruct((nt, PEER_PAIRS, TC), F32)],
        compiler_params=_cparams(1, 48),
    )(qp, keys)


ROW_SUB = D_MODEL // 2 // LANES
ACT_UNROLL = 8


def _load_row(tab_ref, scaled_idx):
    row = tab_ref[pl.ds(pl.multiple_of(scaled_idx, ROW_SUB), ROW_SUB), :]
    lo = lax.bitcast_convert_type(lax.shift_left(row, jnp.uint32(16)), F32)
    hi = lax.bitcast_convert_type(row & jnp.uint32(0xFFFF0000), F32)
    return lo, hi


def _expert_act_body(idx_ref, h_ref, gate_ref, tab_ref, w_ref, slot_ref, part_ref):
    def token(t, carry):
        x = h_ref[t]
        xlo, xhi = x[:ROW_SUB], x[ROW_SUB:]
        base = t * PEER_PAIRS
        for p in range(PEER_PAIRS):
            lo, hi = _load_row(tab_ref, idx_ref[base + p])
            slot_ref[p * ROW_SUB:(p + 1) * ROW_SUB, :] = lo * xlo + hi * xhi
        part = slot_ref[pl.ds(0, PEER_PAIRS, stride=ROW_SUB), :]
        for s in range(1, ROW_SUB):
            part = part + slot_ref[pl.ds(s, PEER_PAIRS, stride=ROW_SUB), :]
        part_ref[pl.ds(pl.multiple_of(base, PEER_PAIRS), PEER_PAIRS), :] = part
        return carry

    lax.fori_loop(0, TC, token, 0)

    tok_lane = lax.broadcasted_iota(jnp.int32, (PEER_PAIRS, TC), 1)

    def lane_sums(i, act):
        for k in range(ACT_UNROLL):
            t = i * ACT_UNROLL + k
            part = part_ref[pl.ds(pl.multiple_of(t * PEER_PAIRS, PEER_PAIRS), PEER_PAIRS), :]
            act = jnp.where(tok_lane == t, jnp.sum(part, axis=1, keepdims=True), act)
        return act

    act = lax.fori_loop(0, TC // ACT_UNROLL, lane_sums, jnp.zeros((PEER_PAIRS, TC), F32))
    w_ref[0] = gate_ref[0] * _gelu(act)


def _expert_act(idx_flat, h3, gate_t, tab):
    nt = gate_t.shape[0]
    tile = pl.BlockSpec((1, PEER_PAIRS, TC), lambda i: (i, 0, 0))
    return pl.pallas_call(
        _expert_act_body,
        grid=(nt,),
        in_specs=[pl.BlockSpec((TC * PEER_PAIRS,), lambda i: (i,), memory_space=pltpu.SMEM),
                  pl.BlockSpec((TC,) + h3.shape[1:], lambda i: (i, 0, 0)),
                  tile,
                  pl.BlockSpec(tab.shape, lambda i: (0, 0), pipeline_mode=pl.Buffered(1))],
        out_specs=tile,
        out_shape=jax.ShapeDtypeStruct(gate_t.shape, F32),
        scratch_shapes=[pltpu.VMEM((PEER_PAIRS * ROW_SUB, LANES), F32),
                        pltpu.VMEM((TC * PEER_PAIRS, LANES), F32)],
        compiler_params=_cparams(1, 52),
    )(idx_flat, h3, gate_t, tab)


def _expert_out_body(idx_ref, w_ref, tab_ref, o_ref):
    n_acc = 4

    def token(t, carry):
        base = t * PEER_PAIRS
        acc_lo = [jnp.zeros((ROW_SUB, LANES), F32) for _ in range(n_acc)]
        acc_hi = [jnp.zeros((ROW_SUB, LANES), F32) for _ in range(n_acc)]
        for p in range(PEER_PAIRS):
            lo, hi = _load_row(tab_ref, idx_ref[base + p])
            w = w_ref[base + p]
            acc_lo[p % n_acc] = acc_lo[p % n_acc] + w * lo
            acc_hi[p % n_acc] = acc_hi[p % n_acc] + w * hi
        lo = (acc_lo[0] + acc_lo[1]) + (acc_lo[2] + acc_lo[3])
        hi = (acc_hi[0] + acc_hi[1]) + (acc_hi[2] + acc_hi[3])
        o_ref[t] = jnp.concatenate([lo, hi], axis=0)
        return carry

    lax.fori_loop(0, TB, token, 0)


def _expert_out(idx_flat, w_flat, tab, n):
    smem = pl.BlockSpec((TB * PEER_PAIRS,), lambda i: (i,), memory_space=pltpu.SMEM)
    rows = D_MODEL // LANES
    return pl.pallas_call(
        _expert_out_body,
        grid=(n // TB,),
        in_specs=[smem, smem,
                  pl.BlockSpec(tab.shape, lambda i: (0, 0), pipeline_mode=pl.Buffered(1))],
        out_specs=pl.BlockSpec((TB, rows, LANES), lambda i: (i, 0, 0)),
        out_shape=jax.ShapeDtypeStruct((n, rows, LANES), F32),
        compiler_params=_cparams(1, 48),
    )(idx_flat, w_flat, tab)


def _final_body(x1_ref, f_ref, gt_ref, g_ref, b_ref, o_ref, *, alpha):
    o_ref[...] = _ln(alpha * x1_ref[...] + gt_ref[0] * f_ref[...]) * g_ref[...] + b_ref[...]


def _final(x1, ffn, gt2, g2, b2, seq, alpha):
    n, d = x1.shape
    tm = TM_LN
    nst = seq // tm
    row = pl.BlockSpec((tm, d), lambda i: (i, 0))
    const = pl.BlockSpec((1, d), lambda i: (0, 0))
    return pl.pallas_call(
        functools.partial(_final_body, alpha=alpha),
        grid=(n // tm,),
        in_specs=[row, row, pl.BlockSpec((1, 1, d), lambda i: (i // nst, 0, 0)), const, const],
        out_specs=row,
        out_shape=jax.ShapeDtypeStruct((n, d), F32),
        compiler_params=_cparams(1, 32),
    )(x1, ffn, gt2, g2, b2)


def _swap_halves(w, heads):
    d = w.shape[0]
    w4 = w.reshape(d, heads, 2, HEAD_DIM // 2)
    return jnp.flip(w4, axis=2).reshape(d, heads * HEAD_DIM)


def _fused_in_weight(w_in, w_merge):
    d = w_in.shape[0]
    o = 0
    parts = {}
    for name, width in (("q", NSA_WIDTH), ("kc", KV_WIDTH), ("vc", KV_WIDTH), ("ks", KV_WIDTH),
                        ("vs", KV_WIDTH), ("kw", KV_WIDTH), ("vw", KV_WIDTH), ("g", 3 * NSA_HEADS),
                        ("z", 2 * SGU_WIDTH)):
        parts[name] = w_in[:, o:o + width]
        o += width
    wq = parts["q"] * (HEAD_DIM ** -0.5)
    gcols = 3 * NSA_REP
    wg = jnp.zeros((d, 2 * LANES), w_in.dtype)
    wg = wg.at[:, :gcols].set(parts["g"][:, :gcols]).at[:, LANES:LANES + gcols].set(parts["g"][:, gcols:])
    g = NSA_KV_GROUPS
    cols = [wq, _swap_halves(wq, NSA_HEADS),
            parts["kc"], _swap_halves(parts["kc"], g), parts["ks"], _swap_halves(parts["ks"], g),
            parts["kw"], _swap_halves(parts["kw"], g), parts["vc"], parts["vs"], parts["vw"],
            wg, parts["z"], w_merge]
    return jnp.concatenate(cols, axis=1).astype(BF16)


def _rope_tables(seq):
    half = HEAD_DIM // 2
    pos = jnp.arange(seq, dtype=F32)
    inv_freq = ROPE_THETA ** (-jnp.arange(half, dtype=F32) / half)
    ang = pos[:, None] * inv_freq[None, :]
    cos, sin = jnp.cos(ang), jnp.sin(ang)
    reps = LANES // HEAD_DIM
    cosk = jnp.tile(jnp.concatenate([cos, cos], axis=1), (1, reps))
    sink = jnp.tile(jnp.concatenate([-sin, sin], axis=1), (1, reps))
    return cosk, sink


def _sel_aggregation(n_cmp_pad, n_sel):
    c0 = jnp.arange(n_cmp_pad)[:, None] * CMP_STRIDE
    s0 = jnp.arange(n_sel)[None, :] * SEL_BLOCK
    ov = jnp.clip(jnp.minimum(c0 + CMP_BLOCK, s0 + SEL_BLOCK) - jnp.maximum(c0, s0), 0, None)
    return (ov / CMP_BLOCK).astype(BF16)


def _pack_table(tab):
    half = tab.shape[1] // 2
    bits = lax.bitcast_convert_type(tab.astype(BF16), jnp.uint16).astype(jnp.uint32)
    packed = (bits[:, half:] << 16) | bits[:, :half]
    return packed.reshape(tab.shape[0] * ROW_SUB, LANES)


def _split_groups(a, bsz, seq):
    return a.reshape(bsz, seq, NSA_KV_GROUPS, HEAD_DIM).transpose(0, 2, 1, 3)


def kernel(x, c, w_ada, b_ada, w_in, cmp_pos, cmp_w1, cmp_b1, cmp_w2, cmp_b2, sgu_ln_g, sgu_ln_b, sgu_w, sgu_b, w_branch, w_merge, b_merge, w_out, ln1_g, ln1_b, peer_wq, peer_keys, peer_u, peer_v, ln2_g, ln2_b):
    bsz, seq, d = x.shape
    n = bsz * seq
    depth = w_ada.shape[0]
    alpha = (2.0 * depth) ** 0.25
    n_cmp = (seq - CMP_BLOCK) // CMP_STRIDE + 1
    n_half = seq // CMP_STRIDE
    n_sel = seq // SEL_BLOCK
    cosk, sink = _rope_tables(seq)
    agg = _sel_aggregation(n_half, n_sel)
    x2 = x.reshape(n, d)
    for l in range(depth):
        mod = _ada(c, w_ada[l], b_ada[l])
        sh1, sc1, gt1, sh2, sc2, gt2 = [m.reshape(bsz, 1, d) for m in jnp.split(mod, 6, axis=-1)]

        wall = _fused_in_weight(w_in[l], w_merge[l])
        (q, kc_r, vc_r, ks, vs, kw, vw, gn, u, v, gate) = _proj(
            x2, sc1, sh1, wall, b_merge[l].reshape(1, -1), cosk, sink,
            sgu_ln_g[l].reshape(1, -1), sgu_ln_b[l].reshape(1, -1), seq)

        t16 = jnp.stack([_split_groups(kc_r, bsz, seq), _split_groups(vc_r, bsz, seq)])
        t16 = t16.reshape(2, bsz, NSA_KV_GROUPS, n_half, CMP_STRIDE * HEAD_DIM)
        cmp_kv = _compress(t16, cmp_pos[l].reshape(2, 1, CMP_BLOCK * HEAD_DIM), cmp_w1[l],
                           cmp_b1[l].reshape(2, 1, CMP_HIDDEN), cmp_w2[l], cmp_b2[l].reshape(2, 1, HEAD_DIM))

        front = ((0, 0), (0, 0), (WINDOW, 0), (0, 0))
        o_nsa = _nsa(q.reshape(bsz, seq, NSA_WIDTH), cmp_kv[0], cmp_kv[1],
                     _split_groups(ks, bsz, seq), _split_groups(vs, bsz, seq),
                     jnp.pad(_split_groups(kw, bsz, seq), front), jnp.pad(_split_groups(vw, bsz, seq), front),
                     gn.reshape(bsz, seq, 2 * LANES), agg, n_cmp)

        bs = jnp.repeat(sgu_b[l].T, SGU_GROUP_DIM, axis=1)
        x1, h2, qp = _merge(o_nsa.reshape(n, NSA_WIDTH), u, v, gate, x2, gt1, sc2, sh2,
                            sgu_w[l], bs, w_branch[l].astype(BF16), w_out[l].astype(BF16),
                            ln1_g[l].reshape(1, d), ln1_b[l].reshape(1, d), peer_wq[l].astype(BF16), seq, alpha)

        idx_t, gate_t = _retrieve(qp, peer_keys[l])
        idx_flat = idx_t.transpose(0, 2, 1).reshape(n * PEER_PAIRS) * ROW_SUB
        w_t = _expert_act(idx_flat, h2.reshape(n, d // LANES, LANES), gate_t, _pack_table(peer_u[l]))
        w_flat = w_t.transpose(0, 2, 1).reshape(n * PEER_PAIRS)
        ffn = _expert_out(idx_flat, w_flat, _pack_table(peer_v[l]), n)
        x2 = _final(x1, ffn.reshape(n, d), gt2, ln2_g[l].reshape(1, d), ln2_b[l].reshape(1, d), seq, alpha)
    return x2.reshape(bsz, seq, d)
```

```python
import functools

import jax
import jax.numpy as jnp
from jax import lax
from jax.experimental import pallas as pl
from jax.experimental.pallas import tpu as pltpu

D_MODEL = 1024
NSA_HEADS = 8
NSA_KV_GROUPS = 2
NSA_REP = NSA_HEADS // NSA_KV_GROUPS
HEAD_DIM = 64
NSA_WIDTH = NSA_HEADS * HEAD_DIM
KV_WIDTH = NSA_KV_GROUPS * HEAD_DIM
CMP_BLOCK = 32
CMP_STRIDE = 16
CMP_HIDDEN = 2 * HEAD_DIM
SEL_BLOCK = 64
SEL_TOPK = 16
WINDOW = 512
ROPE_THETA = 10000.0
SGU_GROUPS = 8
SGU_WIDTH = D_MODEL // 2
SGU_GROUP_DIM = SGU_WIDTH // SGU_GROUPS
SGU_CHUNK = 128
PEER_HEADS = 8
PEER_NKEYS = 128
PEER_EXPERTS = PEER_NKEYS * PEER_NKEYS
PEER_QDIM = 256
PEER_HALF = PEER_QDIM // 2
PEER_TOPK = 16
PEER_PAIRS = PEER_HEADS * PEER_TOPK
LN_EPS = 1e-5
NEG_INF = -1e30
FORCE_SCORE = 1e9

LANES = 128
MIB = 1024 * 1024
BF16 = jnp.bfloat16
F32 = jnp.float32

TM_PROJ = 256
TM_MERGE = 256
TM_LN = 512
TQ = 128
TK = 256
KV_GROUP = 4
TC = 128
TB = 64

_OQ, _OQS = 0, 512
_OKC, _OKCS, _OKS, _OKSS, _OKW, _OKWS = 1024, 1152, 1280, 1408, 1536, 1664
_OVC, _OVS, _OVW = 1792, 1920, 2048
_OG = 2176
_OZ = 2432
_OM = 3456
_WCOLS = 5504


def _cparams(n_axes, vmem_mib):
    return pltpu.CompilerParams(
        dimension_semantics=("parallel",) * n_axes,
        vmem_limit_bytes=vmem_mib * MIB)


def _ln(x):
    mu = jnp.mean(x, axis=-1, keepdims=True)
    xc = x - mu
    var = jnp.mean(xc * xc, axis=-1, keepdims=True)
    return xc * lax.rsqrt(var + LN_EPS)


def _gelu(x):
    return 0.5 * x * (1.0 + lax.erf(x * (2.0 ** -0.5)))


def _dot(a, b):
    return jnp.dot(a, b, preferred_element_type=F32)


def _dot_nt(a, b):
    return lax.dot_general(a, b, (((1,), (1,)), ((), ())), preferred_element_type=F32)


def _ada_body(c_ref, w_ref, b_ref, o_ref):
    cv = c_ref[...]
    a = cv * jax.nn.sigmoid(cv)
    o_ref[...] = _dot(a.astype(BF16), w_ref[...].astype(BF16)) + b_ref[...]


def _ada(c, w, b):
    bsz, d = c.shape
    n = w.shape[1]
    tn = 1024
    return pl.pallas_call(
        _ada_body,
        grid=(n // tn,),
        in_specs=[pl.BlockSpec((bsz, d), lambda j: (0, 0)),
                  pl.BlockSpec((d, tn), lambda j: (0, j)),
                  pl.BlockSpec((1, tn), lambda j: (0, j))],
        out_specs=pl.BlockSpec((bsz, tn), lambda j: (0, j)),
        out_shape=jax.ShapeDtypeStruct((bsz, n), F32),
        compiler_params=_cparams(1, 32),
    )(c, w, b.reshape(1, n))


def _proj_body(x_ref, sc_ref, sh_ref, w_ref, bm_ref, cos_ref, sin_ref, sg_ref, sb_ref,
               q_ref, kc_ref, vc_ref, ks_ref, vs_ref, kw_ref, vw_ref, gn_ref, u_ref, v_ref, gate_ref):
    h = _ln(x_ref[...]) * (1.0 + sc_ref[0]) + sh_ref[0]
    hb = h.astype(BF16)

    def proj(off, width):
        return _dot(hb, w_ref[:, off:off + width])

    cosk = cos_ref[...]
    sink = sin_ref[...]
    cosq = jnp.concatenate([cosk] * (NSA_WIDTH // LANES), axis=1)
    sinq = jnp.concatenate([sink] * (NSA_WIDTH // LANES), axis=1)
    q_ref[...] = proj(_OQ, NSA_WIDTH) * cosq + proj(_OQS, NSA_WIDTH) * sinq
    kc_ref[...] = proj(_OKC, KV_WIDTH) * cosk + proj(_OKCS, KV_WIDTH) * sink
    ks_ref[...] = (proj(_OKS, KV_WIDTH) * cosk + proj(_OKSS, KV_WIDTH) * sink).astype(BF16)
    kw_ref[...] = (proj(_OKW, KV_WIDTH) * cosk + proj(_OKWS, KV_WIDTH) * sink).astype(BF16)
    vc_ref[...] = proj(_OVC, KV_WIDTH)
    vs_ref[...] = proj(_OVS, KV_WIDTH).astype(BF16)
    vw_ref[...] = proj(_OVW, KV_WIDTH).astype(BF16)
    gn_ref[...] = jax.nn.sigmoid(proj(_OG, 2 * LANES))
    u_ref[...] = _gelu(proj(_OZ, SGU_WIDTH))
    v_ref[...] = _ln(_gelu(proj(_OZ + SGU_WIDTH, SGU_WIDTH))) * sg_ref[...] + sb_ref[...]
    gate_ref[...] = jax.nn.sigmoid(proj(_OM, 2 * D_MODEL) + bm_ref[...])


def _proj(x2, sc, sh, wall, bm, cosk, sink, sg, sb, seq):
    n, d = x2.shape
    tm = TM_PROJ
    nst = seq // tm
    row = lambda w: pl.BlockSpec((tm, w), lambda i: (i, 0))
    per_batch = pl.BlockSpec((1, 1, d), lambda i: (i // nst, 0, 0))
    const = lambda shp: pl.BlockSpec(shp, lambda i: (0,) * len(shp))
    pos = pl.BlockSpec((tm, LANES), lambda i: (i % nst, 0))
    outs = [(NSA_WIDTH, F32), (KV_WIDTH, F32), (KV_WIDTH, F32), (KV_WIDTH, BF16), (KV_WIDTH, BF16),
            (KV_WIDTH, BF16), (KV_WIDTH, BF16), (2 * LANES, F32), (SGU_WIDTH, F32), (SGU_WIDTH, F32),
            (2 * D_MODEL, F32)]
    return pl.pallas_call(
        _proj_body,
        grid=(n // tm,),
        in_specs=[row(d), per_batch, per_batch, const((d, _WCOLS)), const((1, 2 * D_MODEL)),
                  pos, pos, const((1, SGU_WIDTH)), const((1, SGU_WIDTH))],
        out_specs=[row(w) for w, _ in outs],
        out_shape=[jax.ShapeDtypeStruct((n, w), dt) for w, dt in outs],
        compiler_params=_cparams(1, 56),
    )(x2, sc, sh, wall, bm, cosk, sink, sg, sb)


def _cmp_body(t_ref, pos_ref, w1_ref, b1_ref, w2_ref, b2_ref, o_ref):
    half = (CMP_BLOCK // 2) * HEAD_DIM
    t = t_ref[0, 0, 0]
    pos = pos_ref[0]
    ta = (t + pos[:, :half]).astype(BF16)
    tb = (t + pos[:, half:]).astype(BF16)
    a = _dot(ta, w1_ref[0, :half, :].astype(BF16))
    b = _dot(tb, w1_ref[0, half:, :].astype(BF16))
    nrow = t.shape[0]
    b_next = pltpu.roll(b, nrow - 1, axis=0)
    hid = _gelu(a + b_next + b1_ref[0])
    out = _dot(hid.astype(BF16), w2_ref[0].astype(BF16)) + b2_ref[0]
    o_ref[0, 0, 0] = out.astype(o_ref.dtype)


def _compress(t16, pos, w1, b1, w2, b2):
    _, bsz, g, nr, dd = t16.shape
    per_kind = lambda shp: pl.BlockSpec((1,) + shp, lambda k, b, gg: (k,) + (0,) * len(shp))
    blk = lambda w: pl.BlockSpec((1, 1, 1, nr, w), lambda k, b, gg: (k, b, gg, 0, 0))
    return pl.pallas_call(
        _cmp_body,
        grid=(2, bsz, g),
        in_specs=[blk(dd), per_kind((1, CMP_BLOCK * HEAD_DIM)), per_kind((CMP_BLOCK * HEAD_DIM, CMP_HIDDEN)),
                  per_kind((1, CMP_HIDDEN)), per_kind((CMP_HIDDEN, HEAD_DIM)), per_kind((1, HEAD_DIM))],
        out_specs=blk(HEAD_DIM),
        out_shape=jax.ShapeDtypeStruct((2, bsz, g, nr, HEAD_DIM), BF16),
        compiler_params=_cparams(3, 48),
    )(t16, pos, w1, b1, w2, b2)


def _attend(state, q_all, kt, vt_t, bias=None):
    m, l, acc = state
    s = _dot(kt, q_all)
    if bias is not None:
        s = s + bias
    m_new = jnp.maximum(m, jnp.max(s, axis=0, keepdims=True))
    p = jnp.exp(s - m_new)
    alpha = jnp.exp(m - m_new)
    return (m_new, alpha * l + jnp.sum(p, axis=0, keepdims=True), alpha * acc + _dot(vt_t, p.astype(BF16)))


def _attend_init():
    w = NSA_REP * TQ
    return (jnp.full((1, w), NEG_INF, F32), jnp.zeros((1, w), F32), jnp.zeros((HEAD_DIM, w), F32))


def _lane_tile(a):
    return jnp.concatenate([a] * NSA_REP, axis=1)


def _nsa_body(q_ref, kc_ref, vc_ref, ks_ref, vs_ref, kw_ref, vw_ref, gn_ref, agg_ref, o_ref, *, n_cmp):
    qi = pl.program_id(2)
    q0 = qi * TQ
    t1 = q0 + lax.broadcasted_iota(jnp.int32, (1, TQ), 1)
    t = _lane_tile(t1)
    q_all = jnp.concatenate([q_ref[0, r * HEAD_DIM:(r + 1) * HEAD_DIM, :] for r in range(NSA_REP)],
                            axis=1).astype(BF16)

    kcm = kc_ref[0, 0]
    ncp = kcm.shape[0]
    nidx = lax.broadcasted_iota(jnp.int32, (ncp, 1), 0)
    mask_c = (nidx * CMP_STRIDE + (CMP_BLOCK - 1) <= t) & (nidx < n_cmp)
    sm = jnp.where(mask_c, _dot(kcm, q_all), NEG_INF)
    e = jnp.where(mask_c, jnp.exp(sm - jnp.max(sm, axis=0, keepdims=True)), 0.0)
    l = jnp.sum(e, axis=0, keepdims=True)
    p = e * (1.0 / jnp.where(l > 0.0, l, 1.0))
    o_c = _dot(vc_ref[0, 0], p.astype(BF16))
    psum = p[:, :TQ]
    for r in range(1, NSA_REP):
        psum = psum + p[:, r * TQ:(r + 1) * TQ]

    p_hi = psum.astype(BF16)
    p_lo = (psum - p_hi.astype(F32)).astype(BF16)
    agg_t = agg_ref[...]
    imp = _dot(agg_t, p_hi) + _dot(agg_t, p_lo)
    n_sel = imp.shape[0]
    blk = lax.broadcasted_iota(jnp.int32, (n_sel, 1), 0)
    blk_f = blk.astype(F32)
    cur = t1 // SEL_BLOCK
    forced = (blk == 0) | (blk == cur) | (blk == cur - 1)
    score = jnp.where(forced, FORCE_SCORE, jnp.where(blk <= cur, imp, NEG_INF))
    sel = jnp.zeros((n_sel, TQ), jnp.bool_)
    for _ in range(min(SEL_TOPK, n_sel)):
        m = jnp.max(score, axis=0, keepdims=True)
        first = jnp.min(jnp.where(score == m, blk_f, float(n_sel)), axis=0, keepdims=True)
        hit = blk_f == first
        sel = sel | hit
        score = jnp.where(hit, -jnp.inf, score)
    sel_neg = _lane_tile(jnp.where(sel, 0.0, NEG_INF).astype(BF16))
    q_ext = jnp.concatenate([q_all, sel_neg], axis=0)

    krow_w = lax.broadcasted_iota(jnp.int32, (TQ, 1), 0)
    state = _attend_init()
    for c in range(WINDOW // TQ + 1):
        kpos = q0 - WINDOW + c * TQ + krow_w
        dlt = t - kpos
        bias = jnp.where((dlt >= 0) & (dlt < WINDOW) & (kpos >= 0), 0.0, NEG_INF)
        kt = kw_ref[0, 0, pl.ds(pl.multiple_of(q0 + c * TQ, TQ), TQ), :]
        state = _attend(state, q_all, kt, vw_ref[0, 0, qi + c], bias)
    o_w = state[2] / state[1]

    krow = lax.broadcasted_iota(jnp.int32, (TK, 1), 0)

    def kv_tile(j, state, causal):
        k0 = pl.multiple_of(j * TK, TK)
        bias = jnp.where(k0 + krow <= t, 0.0, NEG_INF) if causal else None
        return _attend(state, q_ext, ks_ref[0, 0, pl.ds(k0, TK), :], vs_ref[0, 0, j], bias)

    def kv_group(jg, state):
        for sub in range(KV_GROUP):
            state = kv_tile(jg * KV_GROUP + sub, state, False)
        return state

    n_full = q0 // (TK * KV_GROUP)
    n_kv = (q0 + TQ - 1) // TK + 1
    state = lax.fori_loop(0, n_full, kv_group, _attend_init())
    state = lax.fori_loop(n_full * KV_GROUP, n_kv, lambda j, st: kv_tile(j, st, True), state)
    o_s = state[2] / state[1]

    gn = gn_ref[0, 0]
    gate = [jnp.concatenate([gn[3 * r + c:3 * r + c + 1, :] for r in range(NSA_REP)], axis=1) for c in range(3)]
    o = gate[0] * o_c + gate[1] * o_s + gate[2] * o_w
    for r in range(NSA_REP):
        o_ref[0, r * HEAD_DIM:(r + 1) * HEAD_DIM, :] = o[:, r * TQ:(r + 1) * TQ]


def _nsa(q_t, kc, vc_t, ks, vs_t, kwp, vwp_t, gn_t, agg_t, n_cmp):
    bsz, _, seq = q_t.shape
    g = NSA_KV_GROUPS
    gw = NSA_REP * HEAD_DIM
    per_bg = lambda a: pl.BlockSpec((1, 1) + a.shape[2:], lambda b, gg, i: (b, gg) + (0,) * (a.ndim - 2))
    return pl.pallas_call(
        functools.partial(_nsa_body, n_cmp=n_cmp),
        grid=(bsz, g, seq // TQ),
        in_specs=[pl.BlockSpec((1, gw, TQ), lambda b, gg, i: (b, gg, i)),
                  per_bg(kc), per_bg(vc_t), per_bg(ks), per_bg(vs_t), per_bg(kwp), per_bg(vwp_t),
                  pl.BlockSpec((1, 1, gn_t.shape[2], TQ), lambda b, gg, i: (b, gg, 0, i)),
                  pl.BlockSpec(agg_t.shape, lambda b, gg, i: (0, 0))],
        out_specs=pl.BlockSpec((1, gw, TQ), lambda b, gg, i: (b, gg, i)),
        out_shape=jax.ShapeDtypeStruct((bsz, NSA_WIDTH, seq), F32),
        compiler_params=_cparams(3, 56),
    )(q_t, kc, vc_t, ks, vs_t, kwp, vwp_t, gn_t, agg_t)


def _merge_body(on_ref, u_ref, v_ref, gate_ref, x_ref, gt_ref, sc_ref, sh_ref, ws_ref, bs_ref,
                wb_ref, wo_ref, g1_ref, b1_ref, wq_ref, x1_ref, h2_ref, qp_ref, *, alpha):
    tm = x_ref.shape[0]
    row = lax.broadcasted_iota(jnp.int32, (SGU_CHUNK, SGU_CHUNK), 0)
    col = lax.broadcasted_iota(jnp.int32, (SGU_CHUNK, SGU_CHUNK), 1)
    lane_group = lax.broadcasted_iota(jnp.int32, (1, SGU_WIDTH), 1) // SGU_GROUP_DIM
    mixed = []
    for c in range(tm // SGU_CHUNK):
        v = v_ref[c * SGU_CHUNK:(c + 1) * SGU_CHUNK, :]
        acc = jnp.zeros((SGU_CHUNK, SGU_WIDTH), F32)
        for g in range(SGU_GROUPS):
            wg = jnp.where(col <= row, ws_ref[g], 0.0).astype(BF16)
            vg = jnp.where(lane_group == g, v, 0.0).astype(BF16)
            acc = acc + _dot(wg, vg)
        mixed.append(acc + bs_ref[...])
    o_sgu = u_ref[...] * jnp.concatenate(mixed, axis=0)
    gate = gate_ref[...]
    merged = (gate[:, :D_MODEL] * _dot(on_ref[...].astype(BF16), wb_ref[0])
              + gate[:, D_MODEL:] * _dot(o_sgu.astype(BF16), wb_ref[1]))
    mix = _dot(merged.astype(BF16), wo_ref[...])
    x1 = _ln(alpha * x_ref[...] + gt_ref[0] * mix) * g1_ref[...] + b1_ref[...]
    x1_ref[...] = x1
    h2 = _ln(x1) * (1.0 + sc_ref[0]) + sh_ref[0]
    h2_ref[...] = h2
    qp_ref[...] = _dot(h2.astype(BF16), wq_ref[...])


def _merge(o_nsa, u, v, gate, x2, gt1, sc2, sh2, ws, bs, wb, wo, g1, b1, wq, seq, alpha):
    n, d = x2.shape
    tm = TM_MERGE
    nst = seq // tm
    row = lambda w: pl.BlockSpec((tm, w), lambda i: (i, 0))
    per_batch = pl.BlockSpec((1, 1, d), lambda i: (i // nst, 0, 0))
    const = lambda a: pl.BlockSpec(a.shape, lambda i: (0,) * a.ndim)
    nq = wq.shape[1]
    return pl.pallas_call(
        functools.partial(_merge_body, alpha=alpha),
        grid=(n // tm,),
        in_specs=[row(NSA_WIDTH), row(SGU_WIDTH), row(SGU_WIDTH), row(2 * d), row(d),
                  per_batch, per_batch, per_batch, const(ws), const(bs), const(wb), const(wo),
                  const(g1), const(b1), const(wq)],
        out_specs=[row(d), row(d), row(nq)],
        out_shape=[jax.ShapeDtypeStruct((n, d), F32), jax.ShapeDtypeStruct((n, d), F32),
                   jax.ShapeDtypeStruct((n, nq), F32)],
        compiler_params=_cparams(1, 56),
    )(o_nsa, u, v, gate, x2, gt1, sc2, sh2, ws, bs, wb, wo, g1, b1, wq)


def _topk_rows(s, k):
    nrow = s.shape[0]
    rid = lax.broadcasted_iota(jnp.int32, s.shape, 0).astype(F32)
    vals, rows = [], []
    for _ in range(k):
        m = jnp.max(s, axis=0, keepdims=True)
        first = jnp.min(jnp.where(s == m, rid, float(nrow)), axis=0, keepdims=True)
        s = jnp.where(rid == first, -jnp.inf, s)
        vals.append(m)
        rows.append(first)
    return vals, rows


def _retrieve_body(qp_ref, keys_ref, idx_ref, gate_ref):
    k = PEER_TOPK
    for h in range(PEER_HEADS):
        tops = []
        for c in range(2):
            off = (h * 2 + c) * PEER_HALF
            qh = qp_ref[:, off:off + PEER_HALF].astype(BF16)
            s = _dot_nt(keys_ref[h, c].astype(BF16), qh)
            tops.append(_topk_rows(s, k))
        (v1, i1), (v2, i2) = tops
        v2m = jnp.concatenate(v2, axis=0)
        i2m = jnp.concatenate(i2, axis=0)
        cand = jnp.concatenate([v1[a] + v2m for a in range(k)], axis=0)
        cidx = jnp.concatenate([i1[a] * float(PEER_NKEYS) + i2m for a in range(k)], axis=0)
        rid = lax.broadcasted_iota(jnp.int32, cand.shape, 0).astype(F32)
        top_s, top_e = [], []
        for _ in range(k):
            m = jnp.max(cand, axis=0, keepdims=True)
            first = jnp.min(jnp.where(cand == m, rid, float(k * k)), axis=0, keepdims=True)
            hit = rid == first
            top_e.append(jnp.sum(jnp.where(hit, cidx, 0.0), axis=0, keepdims=True))
            cand = jnp.where(hit, -jnp.inf, cand)
            top_s.append(m)
        ts = jnp.concatenate(top_s, axis=0)
        e = jnp.exp(ts - ts[0:1])
        gate_ref[0, h * k:(h + 1) * k, :] = e / jnp.sum(e, axis=0, keepdims=True)
        idx_ref[0, h * k:(h + 1) * k, :] = jnp.concatenate(top_e, axis=0).astype(jnp.int32)


def _retrieve(qp, keys):
    n = qp.shape[0]
    nt = n // TC
    out = pl.BlockSpec((1, PEER_PAIRS, TC), lambda i: (i, 0, 0))
    return pl.pallas_call(
        _retrieve_body,
        grid=(nt,),
        in_specs=[pl.BlockSpec((TC, qp.shape[1]), lambda i: (i, 0)),
                  pl.BlockSpec(keys.shape, lambda i: (0, 0, 0, 0))],
        out_specs=[out, out],
        out_shape=[jax.ShapeDtypeStruct((nt, PEER_PAIRS, TC), jnp.int32),
                   jax.ShapeDtypeStruct((nt, PEER_PAIRS, TC), F32)],
        compiler_params=_cparams(1, 48),
    )(qp, keys)


ROW_SUB = D_MODEL // 2 // LANES
ACT_UNROLL = 8


def _load_row(tab_ref, scaled_idx):
    row = tab_ref[pl.ds(pl.multiple_of(scaled_idx, ROW_SUB), ROW_SUB), :]
    lo = lax.bitcast_convert_type(lax.shift_left(row, jnp.uint32(16)), F32)
    hi = lax.bitcast_convert_type(row & jnp.uint32(0xFFFF0000), F32)
    return lo, hi


def _expert_act_body(idx_ref, h_ref, gate_ref, tab_ref, w_ref, slot_ref, part_ref):
    def token(t, carry):
        x = h_ref[t]
        xlo, xhi = x[:ROW_SUB], x[ROW_SUB:]
        base = t * PEER_PAIRS
        for p in range(PEER_PAIRS):
            lo, hi = _load_row(tab_ref, idx_ref[base + p])
            slot_ref[p * ROW_SUB:(p + 1) * ROW_SUB, :] = lo * xlo + hi * xhi
        part = slot_ref[pl.ds(0, PEER_PAIRS, stride=ROW_SUB), :]
        for s in range(1, ROW_SUB):
            part = part + slot_ref[pl.ds(s, PEER_PAIRS, stride=ROW_SUB), :]
        part_ref[pl.ds(pl.multiple_of(base, PEER_PAIRS), PEER_PAIRS), :] = part
        return carry

    lax.fori_loop(0, TC, token, 0)

    tok_lane = lax.broadcasted_iota(jnp.int32, (PEER_PAIRS, TC), 1)

    def lane_sums(i, act):
        for k in range(ACT_UNROLL):
            t = i * ACT_UNROLL + k
            part = part_ref[pl.ds(pl.multiple_of(t * PEER_PAIRS, PEER_PAIRS), PEER_PAIRS), :]
            act = jnp.where(tok_lane == t, jnp.sum(part, axis=1, keepdims=True), act)
        return act

    act = lax.fori_loop(0, TC // ACT_UNROLL, lane_sums, jnp.zeros((PEER_PAIRS, TC), F32))
    w_ref[0] = gate_ref[0] * _gelu(act)


def _expert_act(idx_flat, h3, gate_t, tab):
    nt = gate_t.shape[0]
    tile = pl.BlockSpec((1, PEER_PAIRS, TC), lambda i: (i, 0, 0))
    return pl.pallas_call(
        _expert_act_body,
        grid=(nt,),
        in_specs=[pl.BlockSpec((TC * PEER_PAIRS,), lambda i: (i,), memory_space=pltpu.SMEM),
                  pl.BlockSpec((TC,) + h3.shape[1:], lambda i: (i, 0, 0)),
                  tile,
                  pl.BlockSpec(tab.shape, lambda i: (0, 0), pipeline_mode=pl.Buffered(1))],
        out_specs=tile,
        out_shape=jax.ShapeDtypeStruct(gate_t.shape, F32),
        scratch_shapes=[pltpu.VMEM((PEER_PAIRS * ROW_SUB, LANES), F32),
                        pltpu.VMEM((TC * PEER_PAIRS, LANES), F32)],
        compiler_params=_cparams(1, 52),
    )(idx_flat, h3, gate_t, tab)


def _expert_out_body(idx_ref, w_ref, tab_ref, o_ref):
    n_acc = 4

    def token(t, carry):
        base = t * PEER_PAIRS
        acc_lo = [jnp.zeros((ROW_SUB, LANES), F32) for _ in range(n_acc)]
        acc_hi = [jnp.zeros((ROW_SUB, LANES), F32) for _ in range(n_acc)]
        for p in range(PEER_PAIRS):
            lo, hi = _load_row(tab_ref, idx_ref[base + p])
            w = w_ref[base + p]
            acc_lo[p % n_acc] = acc_lo[p % n_acc] + w * lo
            acc_hi[p % n_acc] = acc_hi[p % n_acc] + w * hi
        lo = (acc_lo[0] + acc_lo[1]) + (acc_lo[2] + acc_lo[3])
        hi = (acc_hi[0] + acc_hi[1]) + (acc_hi[2] + acc_hi[3])
        o_ref[t] = jnp.concatenate([lo, hi], axis=0)
        return carry

    lax.fori_loop(0, TB, token, 0)


def _expert_out(idx_flat, w_flat, tab, n):
    smem = pl.BlockSpec((TB * PEER_PAIRS,), lambda i: (i,), memory_space=pltpu.SMEM)
    rows = D_MODEL // LANES
    return pl.pallas_call(
        _expert_out_body,
        grid=(n // TB,),
        in_specs=[smem, smem,
                  pl.BlockSpec(tab.shape, lambda i: (0, 0), pipeline_mode=pl.Buffered(1))],
        out_specs=pl.BlockSpec((TB, rows, LANES), lambda i: (i, 0, 0)),
        out_shape=jax.ShapeDtypeStruct((n, rows, LANES), F32),
        compiler_params=_cparams(1, 48),
    )(idx_flat, w_flat, tab)


def _final_body(x1_ref, f_ref, gt_ref, g_ref, b_ref, o_ref, *, alpha):
    o_ref[...] = _ln(alpha * x1_ref[...] + gt_ref[0] * f_ref[...]) * g_ref[...] + b_ref[...]


def _final(x1, ffn, gt2, g2, b2, seq, alpha):
    n, d = x1.shape
    tm = TM_LN
    nst = seq // tm
    row = pl.BlockSpec((tm, d), lambda i: (i, 0))
    const = pl.BlockSpec((1, d), lambda i: (0, 0))
    return pl.pallas_call(
        functools.partial(_final_body, alpha=alpha),
        grid=(n // tm,),
        in_specs=[row, row, pl.BlockSpec((1, 1, d), lambda i: (i // nst, 0, 0)), const, const],
        out_specs=row,
        out_shape=jax.ShapeDtypeStruct((n, d), F32),
        compiler_params=_cparams(1, 32),
    )(x1, ffn, gt2, g2, b2)


def _swap_halves(w, heads):
    d = w.shape[0]
    w4 = w.reshape(d, heads, 2, HEAD_DIM // 2)
    return jnp.flip(w4, axis=2).reshape(d, heads * HEAD_DIM)


def _fused_in_weight(w_in, w_merge):
    d = w_in.shape[0]
    o = 0
    parts = {}
    for name, width in (("q", NSA_WIDTH), ("kc", KV_WIDTH), ("vc", KV_WIDTH), ("ks", KV_WIDTH),
                        ("vs", KV_WIDTH), ("kw", KV_WIDTH), ("vw", KV_WIDTH), ("g", 3 * NSA_HEADS),
                        ("z", 2 * SGU_WIDTH)):
        parts[name] = w_in[:, o:o + width]
        o += width
    wq = parts["q"] * (HEAD_DIM ** -0.5)
    gcols = 3 * NSA_REP
    wg = jnp.zeros((d, 2 * LANES), w_in.dtype)
    wg = wg.at[:, :gcols].set(parts["g"][:, :gcols]).at[:, LANES:LANES + gcols].set(parts["g"][:, gcols:])
    g = NSA_KV_GROUPS
    cols = [wq, _swap_halves(wq, NSA_HEADS),
            parts["kc"], _swap_halves(parts["kc"], g), parts["ks"], _swap_halves(parts["ks"], g),
            parts["kw"], _swap_halves(parts["kw"], g), parts["vc"], parts["vs"], parts["vw"],
            wg, parts["z"], w_merge]
    return jnp.concatenate(cols, axis=1).astype(BF16)


def _rope_tables(seq):
    half = HEAD_DIM // 2
    pos = jnp.arange(seq, dtype=F32)
    inv_freq = ROPE_THETA ** (-jnp.arange(half, dtype=F32) / half)
    ang = pos[:, None] * inv_freq[None, :]
    cos, sin = jnp.cos(ang), jnp.sin(ang)
    reps = LANES // HEAD_DIM
    cosk = jnp.tile(jnp.concatenate([cos, cos], axis=1), (1, reps))
    sink = jnp.tile(jnp.concatenate([-sin, sin], axis=1), (1, reps))
    return cosk, sink


def _sel_aggregation(n_cmp_pad, n_sel):
    c0 = jnp.arange(n_cmp_pad)[:, None] * CMP_STRIDE
    s0 = jnp.arange(n_sel)[None, :] * SEL_BLOCK
    ov = jnp.clip(jnp.minimum(c0 + CMP_BLOCK, s0 + SEL_BLOCK) - jnp.maximum(c0, s0), 0, None)
    return (ov / CMP_BLOCK).astype(BF16)


def _pack_table(tab):
    half = tab.shape[1] // 2
    bits = lax.bitcast_convert_type(tab.astype(BF16), jnp.uint16).astype(jnp.uint32)
    packed = (bits[:, half:] << 16) | bits[:, :half]
    return packed.reshape(tab.shape[0] * ROW_SUB, LANES)


def _split_groups(a, bsz, seq):
    return a.reshape(bsz, seq, NSA_KV_GROUPS, HEAD_DIM).transpose(0, 2, 1, 3)


def kernel(x, c, w_ada, b_ada, w_in, cmp_pos, cmp_w1, cmp_b1, cmp_w2, cmp_b2, sgu_ln_g, sgu_ln_b, sgu_w, sgu_b, w_branch, w_merge, b_merge, w_out, ln1_g, ln1_b, peer_wq, peer_keys, peer_u, peer_v, ln2_g, ln2_b):
    bsz, seq, d = x.shape
    n = bsz * seq
    depth = w_ada.shape[0]
    alpha = (2.0 * depth) ** 0.25
    n_cmp = (seq - CMP_BLOCK) // CMP_STRIDE + 1
    n_half = seq // CMP_STRIDE
    n_sel = seq // SEL_BLOCK
    cosk, sink = _rope_tables(seq)
    agg = _sel_aggregation(n_half, n_sel)
    x2 = x.reshape(n, d)
    for l in range(depth):
        mod = _ada(c, w_ada[l], b_ada[l])
        sh1, sc1, gt1, sh2, sc2, gt2 = [m.reshape(bsz, 1, d) for m in jnp.split(mod, 6, axis=-1)]

        wall = _fused_in_weight(w_in[l], w_merge[l])
        (q, kc_r, vc_r, ks, vs, kw, vw, gn, u, v, gate) = _proj(
            x2, sc1, sh1, wall, b_merge[l].reshape(1, -1), cosk, sink,
            sgu_ln_g[l].reshape(1, -1), sgu_ln_b[l].reshape(1, -1), seq)

        t16 = jnp.stack([_split_groups(kc_r, bsz, seq), _split_groups(vc_r, bsz, seq)])
        t16 = t16.reshape(2, bsz, NSA_KV_GROUPS, n_half, CMP_STRIDE * HEAD_DIM)
        cmp_kv = _compress(t16, cmp_pos[l].reshape(2, 1, CMP_BLOCK * HEAD_DIM), cmp_w1[l],
                           cmp_b1[l].reshape(2, 1, CMP_HIDDEN), cmp_w2[l], cmp_b2[l].reshape(2, 1, HEAD_DIM))

        front = ((0, 0), (0, 0), (WINDOW, 0), (0, 0))
        key_tiles = lambda a, tk: a.reshape(a.shape[:2] + (a.shape[2] // tk, tk, HEAD_DIM)).transpose(0, 1, 2, 4, 3)
        gn_t = gn.reshape(bsz, seq, NSA_KV_GROUPS, LANES)[..., :16].transpose(0, 2, 3, 1)
        ks4 = _split_groups(ks, bsz, seq)
        block_onehot = (jnp.arange(seq)[:, None] // SEL_BLOCK == jnp.arange(n_sel)[None, :]).astype(BF16)
        ks_ext = jnp.concatenate([ks4, jnp.broadcast_to(block_onehot, ks4.shape[:3] + (n_sel,))], axis=-1)
        o_nsa_t = _nsa(q.reshape(bsz, seq, NSA_WIDTH).transpose(0, 2, 1), cmp_kv[0], cmp_kv[1].transpose(0, 1, 3, 2),
                       ks_ext, key_tiles(_split_groups(vs, bsz, seq), TK),
                       jnp.pad(_split_groups(kw, bsz, seq), front),
                       key_tiles(jnp.pad(_split_groups(vw, bsz, seq), front), TQ),
                       gn_t, agg.T, n_cmp)
        o_nsa = o_nsa_t.transpose(0, 2, 1)

        bs = jnp.repeat(sgu_b[l].T, SGU_GROUP_DIM, axis=1)
        x1, h2, qp = _merge(o_nsa.reshape(n, NSA_WIDTH), u, v, gate, x2, gt1, sc2, sh2,
                            sgu_w[l], bs, w_branch[l].astype(BF16), w_out[l].astype(BF16),
                            ln1_g[l].reshape(1, d), ln1_b[l].reshape(1, d), peer_wq[l].astype(BF16), seq, alpha)

        idx_t, gate_t = _retrieve(qp, peer_keys[l])
        idx_flat = idx_t.transpose(0, 2, 1).reshape(n * PEER_PAIRS) * ROW_SUB
        w_t = _expert_act(idx_flat, h2.reshape(n, d // LANES, LANES), gate_t, _pack_table(peer_u[l]))
        w_flat = w_t.transpose(0, 2, 1).reshape(n * PEER_PAIRS)
        ffn = _expert_out(idx_flat, w_flat, _pack_table(peer_v[l]), n)
        x2 = _final(x1, ffn.reshape(n, d), gt2, ln2_g[l].reshape(1, d), ln2_b[l].reshape(1, d), seq, alpha)
    return x2.reshape(bsz, seq, d)
```

```python
import functools

import jax
import jax.numpy as jnp
from jax import lax
from jax.experimental import pallas as pl
from jax.experimental.pallas import tpu as pltpu

D_MODEL = 1024
NSA_HEADS = 8
NSA_KV_GROUPS = 2
NSA_REP = NSA_HEADS // NSA_KV_GROUPS
HEAD_DIM = 64
NSA_WIDTH = NSA_HEADS * HEAD_DIM
KV_WIDTH = NSA_KV_GROUPS * HEAD_DIM
CMP_BLOCK = 32
CMP_STRIDE = 16
CMP_HIDDEN = 2 * HEAD_DIM
SEL_BLOCK = 64
SEL_TOPK = 16
WINDOW = 512
ROPE_THETA = 10000.0
SGU_GROUPS = 8
SGU_WIDTH = D_MODEL // 2
SGU_GROUP_DIM = SGU_WIDTH // SGU_GROUPS
SGU_CHUNK = 128
PEER_HEADS = 8
PEER_NKEYS = 128
PEER_EXPERTS = PEER_NKEYS * PEER_NKEYS
PEER_QDIM = 256
PEER_HALF = PEER_QDIM // 2
PEER_TOPK = 16
PEER_PAIRS = PEER_HEADS * PEER_TOPK
LN_EPS = 1e-5
NEG_INF = -1e30
FORCE_SCORE = 1e9

LANES = 128
MIB = 1024 * 1024
BF16 = jnp.bfloat16
F32 = jnp.float32

TM_PROJ = 256
TM_MERGE = 256
TM_LN = 512
TQ = 128
TK = 256
KV_GROUP = 4
TC = 128
TB = 64

_OQ, _OQS = 0, 512
_OKC, _OKCS, _OKS, _OKSS, _OKW, _OKWS = 1024, 1152, 1280, 1408, 1536, 1664
_OVC, _OVS, _OVW = 1792, 1920, 2048
_OG = 2176
_OZ = 2432
_OM = 3456
_WCOLS = 5504


def _cparams(n_axes, vmem_mib):
    return pltpu.CompilerParams(
        dimension_semantics=("parallel",) * n_axes,
        vmem_limit_bytes=vmem_mib * MIB)


def _ln(x):
    mu = jnp.mean(x, axis=-1, keepdims=True)
    xc = x - mu
    var = jnp.mean(xc * xc, axis=-1, keepdims=True)
    return xc * lax.rsqrt(var + LN_EPS)


def _gelu(x):
    return 0.5 * x * (1.0 + lax.erf(x * (2.0 ** -0.5)))


def _dot(a, b):
    return jnp.dot(a, b, preferred_element_type=F32)


def _dot_nt(a, b):
    return lax.dot_general(a, b, (((1,), (1,)), ((), ())), preferred_element_type=F32)


def _ada_body(c_ref, w_ref, b_ref, o_ref):
    cv = c_ref[...]
    a = cv * jax.nn.sigmoid(cv)
    o_ref[...] = _dot(a.astype(BF16), w_ref[...].astype(BF16)) + b_ref[...]


def _ada(c, w, b):
    bsz, d = c.shape
    n = w.shape[1]
    tn = 1024
    return pl.pallas_call(
        _ada_body,
        grid=(n // tn,),
        in_specs=[pl.BlockSpec((bsz, d), lambda j: (0, 0)),
                  pl.BlockSpec((d, tn), lambda j: (0, j)),
                  pl.BlockSpec((1, tn), lambda j: (0, j))],
        out_specs=pl.BlockSpec((bsz, tn), lambda j: (0, j)),
        out_shape=jax.ShapeDtypeStruct((bsz, n), F32),
        compiler_params=_cparams(1, 32),
    )(c, w, b.reshape(1, n))


def _proj_body(x_ref, sc_ref, sh_ref, w_ref, bm_ref, cos_ref, sin_ref, sg_ref, sb_ref,
               q_ref, kc_ref, vc_ref, ks_ref, vs_ref, kw_ref, vw_ref, gn_ref, u_ref, v_ref, gate_ref):
    h = _ln(x_ref[...]) * (1.0 + sc_ref[0]) + sh_ref[0]
    hb = h.astype(BF16)

    def proj(off, width):
        return _dot(hb, w_ref[:, off:off + width])

    cosk = cos_ref[...]
    sink = sin_ref[...]
    cosq = jnp.concatenate([cosk] * (NSA_WIDTH // LANES), axis=1)
    sinq = jnp.concatenate([sink] * (NSA_WIDTH // LANES), axis=1)
    q_ref[...] = proj(_OQ, NSA_WIDTH) * cosq + proj(_OQS, NSA_WIDTH) * sinq
    kc_ref[...] = proj(_OKC, KV_WIDTH) * cosk + proj(_OKCS, KV_WIDTH) * sink
    ks_ref[...] = (proj(_OKS, KV_WIDTH) * cosk + proj(_OKSS, KV_WIDTH) * sink).astype(BF16)
    kw_ref[...] = (proj(_OKW, KV_WIDTH) * cosk + proj(_OKWS, KV_WIDTH) * sink).astype(BF16)
    vc_ref[...] = proj(_OVC, KV_WIDTH)
    vs_ref[...] = proj(_OVS, KV_WIDTH).astype(BF16)
    vw_ref[...] = proj(_OVW, KV_WIDTH).astype(BF16)
    gn_ref[...] = jax.nn.sigmoid(proj(_OG, 2 * LANES))
    u_ref[...] = _gelu(proj(_OZ, SGU_WIDTH))
    v_ref[...] = _ln(_gelu(proj(_OZ + SGU_WIDTH, SGU_WIDTH))) * sg_ref[...] + sb_ref[...]
    gate_ref[...] = jax.nn.sigmoid(proj(_OM, 2 * D_MODEL) + bm_ref[...])


def _proj(x2, sc, sh, wall, bm, cosk, sink, sg, sb, seq):
    n, d = x2.shape
    tm = TM_PROJ
    nst = seq // tm
    row = lambda w: pl.BlockSpec((tm, w), lambda i: (i, 0))
    per_batch = pl.BlockSpec((1, 1, d), lambda i: (i // nst, 0, 0))
    const = lambda shp: pl.BlockSpec(shp, lambda i: (0,) * len(shp))
    pos = pl.BlockSpec((tm, LANES), lambda i: (i % nst, 0))
    outs = [(NSA_WIDTH, F32), (KV_WIDTH, F32), (KV_WIDTH, F32), (KV_WIDTH, BF16), (KV_WIDTH, BF16),
            (KV_WIDTH, BF16), (KV_WIDTH, BF16), (2 * LANES, F32), (SGU_WIDTH, F32), (SGU_WIDTH, F32),
            (2 * D_MODEL, F32)]
    return pl.pallas_call(
        _proj_body,
        grid=(n // tm,),
        in_specs=[row(d), per_batch, per_batch, const((d, _WCOLS)), const((1, 2 * D_MODEL)),
                  pos, pos, const((1, SGU_WIDTH)), const((1, SGU_WIDTH))],
        out_specs=[row(w) for w, _ in outs],
        out_shape=[jax.ShapeDtypeStruct((n, w), dt) for w, dt in outs],
        compiler_params=_cparams(1, 56),
    )(x2, sc, sh, wall, bm, cosk, sink, sg, sb)


def _cmp_body(t_ref, pos_ref, w1_ref, b1_ref, w2_ref, b2_ref, o_ref):
    half = (CMP_BLOCK // 2) * HEAD_DIM
    t = t_ref[0, 0, 0]
    pos = pos_ref[0]
    ta = (t + pos[:, :half]).astype(BF16)
    tb = (t + pos[:, half:]).astype(BF16)
    a = _dot(ta, w1_ref[0, :half, :].astype(BF16))
    b = _dot(tb, w1_ref[0, half:, :].astype(BF16))
    nrow = t.shape[0]
    b_next = pltpu.roll(b, nrow - 1, axis=0)
    hid = _gelu(a + b_next + b1_ref[0])
    out = _dot(hid.astype(BF16), w2_ref[0].astype(BF16)) + b2_ref[0]
    o_ref[0, 0, 0] = out.astype(o_ref.dtype)


def _compress(t16, pos, w1, b1, w2, b2):
    _, bsz, g, nr, dd = t16.shape
    per_kind = lambda shp: pl.BlockSpec((1,) + shp, lambda k, b, gg: (k,) + (0,) * len(shp))
    blk = lambda w: pl.BlockSpec((1, 1, 1, nr, w), lambda k, b, gg: (k, b, gg, 0, 0))
    return pl.pallas_call(
        _cmp_body,
        grid=(2, bsz, g),
        in_specs=[blk(dd), per_kind((1, CMP_BLOCK * HEAD_DIM)), per_kind((CMP_BLOCK * HEAD_DIM, CMP_HIDDEN)),
                  per_kind((1, CMP_HIDDEN)), per_kind((CMP_HIDDEN, HEAD_DIM)), per_kind((1, HEAD_DIM))],
        out_specs=blk(HEAD_DIM),
        out_shape=jax.ShapeDtypeStruct((2, bsz, g, nr, HEAD_DIM), BF16),
        compiler_params=_cparams(3, 48),
    )(t16, pos, w1, b1, w2, b2)


def _scores(q_all, kt, bias):
    s = _dot(kt, q_all)
    return s if bias is None else s + bias


def _row_max(m, q_all, kt, bias=None):
    return jnp.maximum(m, jnp.max(_scores(q_all, kt, bias), axis=0, keepdims=True))


def _accumulate(state, m, q_all, kt, vt_t, bias=None):
    l, acc = state
    p = jnp.exp(_scores(q_all, kt, bias) - m)
    return l + jnp.sum(p, axis=0, keepdims=True), acc + _dot(vt_t, p.astype(BF16))


def _max_init():
    return jnp.full((1, NSA_REP * TQ), NEG_INF, F32)


def _acc_init():
    w = NSA_REP * TQ
    return jnp.zeros((1, w), F32), jnp.zeros((HEAD_DIM, w), F32)


def _lane_tile(a):
    return jnp.concatenate([a] * NSA_REP, axis=1)


def _nsa_body(q_ref, kc_ref, vc_ref, ks_ref, vs_ref, kw_ref, vw_ref, gn_ref, agg_ref, o_ref, *, n_cmp):
    qi = pl.program_id(2)
    q0 = qi * TQ
    t1 = q0 + lax.broadcasted_iota(jnp.int32, (1, TQ), 1)
    t = _lane_tile(t1)
    q_all = jnp.concatenate([q_ref[0, r * HEAD_DIM:(r + 1) * HEAD_DIM, :] for r in range(NSA_REP)],
                            axis=1).astype(BF16)

    kcm = kc_ref[0, 0]
    ncp = kcm.shape[0]
    nidx = lax.broadcasted_iota(jnp.int32, (ncp, 1), 0)
    mask_c = (nidx * CMP_STRIDE + (CMP_BLOCK - 1) <= t) & (nidx < n_cmp)
    sm = jnp.where(mask_c, _dot(kcm, q_all), NEG_INF)
    e = jnp.where(mask_c, jnp.exp(sm - jnp.max(sm, axis=0, keepdims=True)), 0.0)
    l = jnp.sum(e, axis=0, keepdims=True)
    p = e * (1.0 / jnp.where(l > 0.0, l, 1.0))
    o_c = _dot(vc_ref[0, 0], p.astype(BF16))
    psum = p[:, :TQ]
    for r in range(1, NSA_REP):
        psum = psum + p[:, r * TQ:(r + 1) * TQ]

    p_hi = psum.astype(BF16)
    p_lo = (psum - p_hi.astype(F32)).astype(BF16)
    agg_t = agg_ref[...]
    imp = _dot(agg_t, p_hi) + _dot(agg_t, p_lo)
    n_sel = imp.shape[0]
    blk = lax.broadcasted_iota(jnp.int32, (n_sel, 1), 0)
    blk_f = blk.astype(F32)
    cur = t1 // SEL_BLOCK
    forced = (blk == 0) | (blk == cur) | (blk == cur - 1)
    score = jnp.where(forced, FORCE_SCORE, jnp.where(blk <= cur, imp, NEG_INF))
    sel = jnp.zeros((n_sel, TQ), jnp.bool_)
    for _ in range(min(SEL_TOPK, n_sel)):
        m = jnp.max(score, axis=0, keepdims=True)
        first = jnp.min(jnp.where(score == m, blk_f, float(n_sel)), axis=0, keepdims=True)
        hit = blk_f == first
        sel = sel | hit
        score = jnp.where(hit, -jnp.inf, score)
    sel_neg = _lane_tile(jnp.where(sel, 0.0, NEG_INF).astype(BF16))
    q_ext = jnp.concatenate([q_all, sel_neg], axis=0)

    krow_w = lax.broadcasted_iota(jnp.int32, (TQ, 1), 0)

    def win_chunk(c):
        kpos = q0 - WINDOW + c * TQ + krow_w
        dlt = t - kpos
        bias = jnp.where((dlt >= 0) & (dlt < WINDOW) & (kpos >= 0), 0.0, NEG_INF)
        return kw_ref[0, 0, pl.ds(pl.multiple_of(q0 + c * TQ, TQ), TQ), :], bias

    n_win = WINDOW // TQ + 1
    m_w = _max_init()
    for c in range(n_win):
        kt, bias = win_chunk(c)
        m_w = _row_max(m_w, q_all, kt, bias)
    state = _acc_init()
    for c in range(n_win):
        kt, bias = win_chunk(c)
        state = _accumulate(state, m_w, q_all, kt, vw_ref[0, 0, qi + c], bias)
    o_w = state[1] / state[0]

    krow = lax.broadcasted_iota(jnp.int32, (TK, 1), 0)

    def group_keys(jg):
        kt = ks_ref[0, 0, pl.ds(pl.multiple_of(jg * (TK * KV_GROUP), TK * KV_GROUP), TK * KV_GROUP), :]
        vt_t = jnp.concatenate([vs_ref[0, 0, jg * KV_GROUP + sub] for sub in range(KV_GROUP)], axis=1)
        return kt, vt_t, None

    def tile_keys(j):
        k0 = pl.multiple_of(j * TK, TK)
        return ks_ref[0, 0, pl.ds(k0, TK), :], vs_ref[0, 0, j], jnp.where(k0 + krow <= t, 0.0, NEG_INF)

    def sweep(step, init):
        carry = lax.fori_loop(0, n_full, lambda jg, c: step(c, *group_keys(jg)), init)
        return lax.fori_loop(n_full * KV_GROUP, n_kv, lambda j, c: step(c, *tile_keys(j)), carry)

    n_full = q0 // (TK * KV_GROUP)
    n_kv = (q0 + TQ - 1) // TK + 1
    m_s = sweep(lambda m, kt, vt_t, bias: _row_max(m, q_ext, kt, bias), _max_init())
    state = sweep(lambda st, kt, vt_t, bias: _accumulate(st, m_s, q_ext, kt, vt_t, bias), _acc_init())
    o_s = state[1] / state[0]

    gn = gn_ref[0, 0]
    gate = [jnp.concatenate([gn[3 * r + c:3 * r + c + 1, :] for r in range(NSA_REP)], axis=1) for c in range(3)]
    o = gate[0] * o_c + gate[1] * o_s + gate[2] * o_w
    for r in range(NSA_REP):
        o_ref[0, r * HEAD_DIM:(r + 1) * HEAD_DIM, :] = o[:, r * TQ:(r + 1) * TQ]


def _nsa(q_t, kc, vc_t, ks, vs_t, kwp, vwp_t, gn_t, agg_t, n_cmp):
    bsz, _, seq = q_t.shape
    g = NSA_KV_GROUPS
    gw = NSA_REP * HEAD_DIM
    per_bg = lambda a: pl.BlockSpec((1, 1) + a.shape[2:], lambda b, gg, i: (b, gg) + (0,) * (a.ndim - 2))
    return pl.pallas_call(
        functools.partial(_nsa_body, n_cmp=n_cmp),
        grid=(bsz, g, seq // TQ),
        in_specs=[pl.BlockSpec((1, gw, TQ), lambda b, gg, i: (b, gg, i)),
                  per_bg(kc), per_bg(vc_t), per_bg(ks), per_bg(vs_t), per_bg(kwp), per_bg(vwp_t),
                  pl.BlockSpec((1, 1, gn_t.shape[2], TQ), lambda b, gg, i: (b, gg, 0, i)),
                  pl.BlockSpec(agg_t.shape, lambda b, gg, i: (0, 0))],
        out_specs=pl.BlockSpec((1, gw, TQ), lambda b, gg, i: (b, gg, i)),
        out_shape=jax.ShapeDtypeStruct((bsz, NSA_WIDTH, seq), F32),
        compiler_params=_cparams(3, 56),
    )(q_t, kc, vc_t, ks, vs_t, kwp, vwp_t, gn_t, agg_t)


def _merge_body(on_ref, u_ref, v_ref, gate_ref, x_ref, gt_ref, sc_ref, sh_ref, ws_ref, bs_ref,
                wb_ref, wo_ref, g1_ref, b1_ref, wq_ref, x1_ref, h2_ref, qp_ref, *, alpha):
    tm = x_ref.shape[0]
    row = lax.broadcasted_iota(jnp.int32, (SGU_CHUNK, SGU_CHUNK), 0)
    col = lax.broadcasted_iota(jnp.int32, (SGU_CHUNK, SGU_CHUNK), 1)
    lane_group = lax.broadcasted_iota(jnp.int32, (1, SGU_WIDTH), 1) // SGU_GROUP_DIM
    mixed = []
    for c in range(tm // SGU_CHUNK):
        v = v_ref[c * SGU_CHUNK:(c + 1) * SGU_CHUNK, :]
        acc = jnp.zeros((SGU_CHUNK, SGU_WIDTH), F32)
        for g in range(SGU_GROUPS):
            wg = jnp.where(col <= row, ws_ref[g], 0.0).astype(BF16)
            vg = jnp.where(lane_group == g, v, 0.0).astype(BF16)
            acc = acc + _dot(wg, vg)
        mixed.append(acc + bs_ref[...])
    o_sgu = u_ref[...] * jnp.concatenate(mixed, axis=0)
    gate = gate_ref[...]
    merged = (gate[:, :D_MODEL] * _dot(on_ref[...].astype(BF16), wb_ref[0])
              + gate[:, D_MODEL:] * _dot(o_sgu.astype(BF16), wb_ref[1]))
    mix = _dot(merged.astype(BF16), wo_ref[...])
    x1 = _ln(alpha * x_ref[...] + gt_ref[0] * mix) * g1_ref[...] + b1_ref[...]
    x1_ref[...] = x1
    h2 = _ln(x1) * (1.0 + sc_ref[0]) + sh_ref[0]
    h2_ref[...] = h2
    qp_ref[...] = _dot(h2.astype(BF16), wq_ref[...])


def _merge(o_nsa, u, v, gate, x2, gt1, sc2, sh2, ws, bs, wb, wo, g1, b1, wq, seq, alpha):
    n, d = x2.shape
    tm = TM_MERGE
    nst = seq // tm
    row = lambda w: pl.BlockSpec((tm, w), lambda i: (i, 0))
    per_batch = pl.BlockSpec((1, 1, d), lambda i: (i // nst, 0, 0))
    const = lambda a: pl.BlockSpec(a.shape, lambda i: (0,) * a.ndim)
    nq = wq.shape[1]
    return pl.pallas_call(
        functools.partial(_merge_body, alpha=alpha),
        grid=(n // tm,),
        in_specs=[row(NSA_WIDTH), row(SGU_WIDTH), row(SGU_WIDTH), row(2 * d), row(d),
                  per_batch, per_batch, per_batch, const(ws), const(bs), const(wb), const(wo),
                  const(g1), const(b1), const(wq)],
        out_specs=[row(d), row(d), row(nq)],
        out_shape=[jax.ShapeDtypeStruct((n, d), F32), jax.ShapeDtypeStruct((n, d), F32),
                   jax.ShapeDtypeStruct((n, nq), F32)],
        compiler_params=_cparams(1, 56),
    )(o_nsa, u, v, gate, x2, gt1, sc2, sh2, ws, bs, wb, wo, g1, b1, wq)


def _topk_rows(s, k):
    nrow = s.shape[0]
    rid = lax.broadcasted_iota(jnp.int32, s.shape, 0).astype(F32)
    vals, rows = [], []
    for _ in range(k):
        m = jnp.max(s, axis=0, keepdims=True)
        first = jnp.min(jnp.where(s == m, rid, float(nrow)), axis=0, keepdims=True)
        s = jnp.where(rid == first, -jnp.inf, s)
        vals.append(m)
        rows.append(first)
    return vals, rows


def _candidate_blocks(k):
    todo = {(a, b) for a in range(k) for b in range(k) if (a + 1) * (b + 1) <= k}
    options = [(axis, fixed, start) for axis in (0, 1) for fixed in range(k) for start in range(0, k, 8)]
    cells = lambda axis, fixed, start: [((fixed, start + r) if axis == 0 else (start + r, fixed)) for r in range(8)]
    blocks = []
    while todo:
        best = max(options, key=lambda o: sum(c in todo for c in cells(*o)))
        owned = tuple(c in todo for c in cells(*best))
        todo -= set(cells(*best))
        blocks.append(best + (owned,))
    return tuple(blocks)


_CAND_BLOCKS = _candidate_blocks(PEER_TOPK)


def _retrieve_body(qp_ref, keys_ref, idx_ref, gate_ref):
    k = PEER_TOPK
    for h in range(PEER_HEADS):
        tops = []
        for c in range(2):
            off = (h * 2 + c) * PEER_HALF
            qh = qp_ref[:, off:off + PEER_HALF].astype(BF16)
            s = _dot_nt(keys_ref[h, c].astype(BF16), qh)
            tops.append(_topk_rows(s, k))
        (v1, i1), (v2, i2) = tops
        vm = (jnp.concatenate(v1, axis=0), jnp.concatenate(v2, axis=0))
        im = (jnp.concatenate(i1, axis=0), jnp.concatenate(i2, axis=0))
        sub = lax.broadcasted_iota(jnp.int32, (8, 1), 0)
        vals, eids, poss = [], [], []
        for axis, fixed, start, owned in _CAND_BLOCKS:
            if axis == 0:
                v = v1[fixed] + vm[1][start:start + 8]
                e = i1[fixed] * float(PEER_NKEYS) + im[1][start:start + 8]
                pos = fixed * k + start + sub
            else:
                v = vm[0][start:start + 8] + v2[fixed]
                e = im[0][start:start + 8] * float(PEER_NKEYS) + i2[fixed]
                pos = (start + sub) * k + fixed
            if not all(owned):
                keep = functools.reduce(jnp.logical_or, [sub == r for r in range(8) if owned[r]])
                v = jnp.where(keep, v, -jnp.inf)
                pos = jnp.where(keep, pos, k * k)
            vals.append(v)
            eids.append(e)
            poss.append(pos.astype(F32))
        cand = jnp.concatenate(vals, axis=0)
        cidx = jnp.concatenate(eids, axis=0)
        rid = jnp.concatenate(poss, axis=0)
        top_s, top_e = [], []
        for _ in range(k):
            m = jnp.max(cand, axis=0, keepdims=True)
            first = jnp.min(jnp.where(cand == m, rid, float(k * k)), axis=0, keepdims=True)
            hit = rid == first
            top_e.append(jnp.sum(jnp.where(hit, cidx, 0.0), axis=0, keepdims=True))
            cand = jnp.where(hit, -jnp.inf, cand)
            top_s.append(m)
        ts = jnp.concatenate(top_s, axis=0)
        e = jnp.exp(ts - ts[0:1])
        gate_ref[0, h * k:(h + 1) * k, :] = e / jnp.sum(e, axis=0, keepdims=True)
        idx_ref[0, h * k:(h + 1) * k, :] = jnp.concatenate(top_e, axis=0).astype(jnp.int32)


def _retrieve(qp, keys):
    n = qp.shape[0]
    nt = n // TC
    out = pl.BlockSpec((1, PEER_PAIRS, TC), lambda i: (i, 0, 0))
    return pl.pallas_call(
        _retrieve_body,
        grid=(nt,),
        in_specs=[pl.BlockSpec((TC, qp.shape[1]), lambda i: (i, 0)),
                  pl.BlockSpec(keys.shape, lambda i: (0, 0, 0, 0))],
        out_specs=[out, out],
        out_shape=[jax.ShapeDtypeStruct((nt, PEER_PAIRS, TC), jnp.int32),
                   jax.ShapeDtypeStruct((nt, PEER_PAIRS, TC), F32)],
        compiler_params=_cparams(1, 48),
    )(qp, keys)


ROW_SUB = D_MODEL // 2 // LANES
ACT_UNROLL = 8


def _load_row(tab_ref, scaled_idx):
    row = tab_ref[pl.ds(pl.multiple_of(scaled_idx, ROW_SUB), ROW_SUB), :]
    lo = lax.bitcast_convert_type(lax.shift_left(row, jnp.uint32(16)), F32)
    hi = lax.bitcast_convert_type(row & jnp.uint32(0xFFFF0000), F32)
    return lo, hi


def _expert_act_body(idx_ref, h_ref, gate_ref, tab_ref, w_ref, slot_ref, part_ref):
    def token(t, carry):
        x = h_ref[t]
        xlo, xhi = x[:ROW_SUB], x[ROW_SUB:]
        base = t * PEER_PAIRS
        for p in range(PEER_PAIRS):
            lo, hi = _load_row(tab_ref, idx_ref[base + p])
            slot_ref[p * ROW_SUB:(p + 1) * ROW_SUB, :] = lo * xlo + hi * xhi
        part = slot_ref[pl.ds(0, PEER_PAIRS, stride=ROW_SUB), :]
        for s in range(1, ROW_SUB):
            part = part + slot_ref[pl.ds(s, PEER_PAIRS, stride=ROW_SUB), :]
        part_ref[pl.ds(pl.multiple_of(base, PEER_PAIRS), PEER_PAIRS), :] = part
        return carry

    lax.fori_loop(0, TC, token, 0)

    tok_lane = lax.broadcasted_iota(jnp.int32, (PEER_PAIRS, TC), 1)

    def lane_sums(i, act):
        for k in range(ACT_UNROLL):
            t = i * ACT_UNROLL + k
            part = part_ref[pl.ds(pl.multiple_of(t * PEER_PAIRS, PEER_PAIRS), PEER_PAIRS), :]
            act = jnp.where(tok_lane == t, jnp.sum(part, axis=1, keepdims=True), act)
        return act

    act = lax.fori_loop(0, TC // ACT_UNROLL, lane_sums, jnp.zeros((PEER_PAIRS, TC), F32))
    w_ref[0] = gate_ref[0] * _gelu(act)


def _expert_act(idx_flat, h3, gate_t, tab):
    nt = gate_t.shape[0]
    tile = pl.BlockSpec((1, PEER_PAIRS, TC), lambda i: (i, 0, 0))
    return pl.pallas_call(
        _expert_act_body,
        grid=(nt,),
        in_specs=[pl.BlockSpec((TC * PEER_PAIRS,), lambda i: (i,), memory_space=pltpu.SMEM),
                  pl.BlockSpec((TC,) + h3.shape[1:], lambda i: (i, 0, 0)),
                  tile,
                  pl.BlockSpec(tab.shape, lambda i: (0, 0), pipeline_mode=pl.Buffered(1))],
        out_specs=tile,
        out_shape=jax.ShapeDtypeStruct(gate_t.shape, F32),
        scratch_shapes=[pltpu.VMEM((PEER_PAIRS * ROW_SUB, LANES), F32),
                        pltpu.VMEM((TC * PEER_PAIRS, LANES), F32)],
        compiler_params=_cparams(1, 52),
    )(idx_flat, h3, gate_t, tab)


def _expert_out_body(idx_ref, w_ref, tab_ref, o_ref):
    n_acc = 4

    def token(t, carry):
        base = pl.multiple_of(t * PEER_PAIRS, PEER_PAIRS)
        idx_tok = idx_ref.at[pl.ds(base, PEER_PAIRS)]
        w_tok = w_ref.at[pl.ds(base, PEER_PAIRS)]
        acc_lo = [jnp.zeros((ROW_SUB, LANES), F32) for _ in range(n_acc)]
        acc_hi = [jnp.zeros((ROW_SUB, LANES), F32) for _ in range(n_acc)]
        for p in range(PEER_PAIRS):
            lo, hi = _load_row(tab_ref, idx_tok[p])
            w = w_tok[p]
            acc_lo[p % n_acc] = acc_lo[p % n_acc] + w * lo
            acc_hi[p % n_acc] = acc_hi[p % n_acc] + w * hi
        lo = (acc_lo[0] + acc_lo[1]) + (acc_lo[2] + acc_lo[3])
        hi = (acc_hi[0] + acc_hi[1]) + (acc_hi[2] + acc_hi[3])
        o_ref[t] = jnp.concatenate([lo, hi], axis=0)
        return carry

    lax.fori_loop(0, TB, token, 0)


def _expert_out(idx_flat, w_flat, tab, n):
    smem = pl.BlockSpec((TB * PEER_PAIRS,), lambda i: (i,), memory_space=pltpu.SMEM)
    rows = D_MODEL // LANES
    return pl.pallas_call(
        _expert_out_body,
        grid=(n // TB,),
        in_specs=[smem, smem,
                  pl.BlockSpec(tab.shape, lambda i: (0, 0), pipeline_mode=pl.Buffered(1))],
        out_specs=pl.BlockSpec((TB, rows, LANES), lambda i: (i, 0, 0)),
        out_shape=jax.ShapeDtypeStruct((n, rows, LANES), F32),
        compiler_params=_cparams(1, 48),
    )(idx_flat, w_flat, tab)


def _final_body(x1_ref, f_ref, gt_ref, g_ref, b_ref, o_ref, *, alpha):
    o_ref[...] = _ln(alpha * x1_ref[...] + gt_ref[0] * f_ref[...]) * g_ref[...] + b_ref[...]


def _final(x1, ffn, gt2, g2, b2, seq, alpha):
    n, d = x1.shape
    tm = TM_LN
    nst = seq // tm
    row = pl.BlockSpec((tm, d), lambda i: (i, 0))
    const = pl.BlockSpec((1, d), lambda i: (0, 0))
    return pl.pallas_call(
        functools.partial(_final_body, alpha=alpha),
        grid=(n // tm,),
        in_specs=[row, row, pl.BlockSpec((1, 1, d), lambda i: (i // nst, 0, 0)), const, const],
        out_specs=row,
        out_shape=jax.ShapeDtypeStruct((n, d), F32),
        compiler_params=_cparams(1, 32),
    )(x1, ffn, gt2, g2, b2)


def _swap_halves(w, heads):
    d = w.shape[0]
    w4 = w.reshape(d, heads, 2, HEAD_DIM // 2)
    return jnp.flip(w4, axis=2).reshape(d, heads * HEAD_DIM)


def _fused_in_weight(w_in, w_merge):
    d = w_in.shape[0]
    o = 0
    parts = {}
    for name, width in (("q", NSA_WIDTH), ("kc", KV_WIDTH), ("vc", KV_WIDTH), ("ks", KV_WIDTH),
                        ("vs", KV_WIDTH), ("kw", KV_WIDTH), ("vw", KV_WIDTH), ("g", 3 * NSA_HEADS),
                        ("z", 2 * SGU_WIDTH)):
        parts[name] = w_in[:, o:o + width]
        o += width
    wq = parts["q"] * (HEAD_DIM ** -0.5)
    gcols = 3 * NSA_REP
    wg = jnp.zeros((d, 2 * LANES), w_in.dtype)
    wg = wg.at[:, :gcols].set(parts["g"][:, :gcols]).at[:, LANES:LANES + gcols].set(parts["g"][:, gcols:])
    g = NSA_KV_GROUPS
    cols = [wq, _swap_halves(wq, NSA_HEADS),
            parts["kc"], _swap_halves(parts["kc"], g), parts["ks"], _swap_halves(parts["ks"], g),
            parts["kw"], _swap_halves(parts["kw"], g), parts["vc"], parts["vs"], parts["vw"],
            wg, parts["z"], w_merge]
    return jnp.concatenate(cols, axis=1).astype(BF16)


def _rope_tables(seq):
    half = HEAD_DIM // 2
    pos = jnp.arange(seq, dtype=F32)
    inv_freq = ROPE_THETA ** (-jnp.arange(half, dtype=F32) / half)
    ang = pos[:, None] * inv_freq[None, :]
    cos, sin = jnp.cos(ang), jnp.sin(ang)
    reps = LANES // HEAD_DIM
    cosk = jnp.tile(jnp.concatenate([cos, cos], axis=1), (1, reps))
    sink = jnp.tile(jnp.concatenate([-sin, sin], axis=1), (1, reps))
    return cosk, sink


def _sel_aggregation(n_cmp_pad, n_sel):
    c0 = jnp.arange(n_cmp_pad)[:, None] * CMP_STRIDE
    s0 = jnp.arange(n_sel)[None, :] * SEL_BLOCK
    ov = jnp.clip(jnp.minimum(c0 + CMP_BLOCK, s0 + SEL_BLOCK) - jnp.maximum(c0, s0), 0, None)
    return (ov / CMP_BLOCK).astype(BF16)


def _pack_table(tab):
    half = tab.shape[1] // 2
    bits = lax.bitcast_convert_type(tab.astype(BF16), jnp.uint16).astype(jnp.uint32)
    packed = (bits[:, half:] << 16) | bits[:, :half]
    return packed.reshape(tab.shape[0] * ROW_SUB, LANES)


def _split_groups(a, bsz, seq):
    return a.reshape(bsz, seq, NSA_KV_GROUPS, HEAD_DIM).transpose(0, 2, 1, 3)


def kernel(x, c, w_ada, b_ada, w_in, cmp_pos, cmp_w1, cmp_b1, cmp_w2, cmp_b2, sgu_ln_g, sgu_ln_b, sgu_w, sgu_b, w_branch, w_merge, b_merge, w_out, ln1_g, ln1_b, peer_wq, peer_keys, peer_u, peer_v, ln2_g, ln2_b):
    bsz, seq, d = x.shape
    n = bsz * seq
    depth = w_ada.shape[0]
    alpha = (2.0 * depth) ** 0.25
    n_cmp = (seq - CMP_BLOCK) // CMP_STRIDE + 1
    n_half = seq // CMP_STRIDE
    n_sel = seq // SEL_BLOCK
    cosk, sink = _rope_tables(seq)
    agg = _sel_aggregation(n_half, n_sel)
    x2 = x.reshape(n, d)
    for l in range(depth):
        mod = _ada(c, w_ada[l], b_ada[l])
        sh1, sc1, gt1, sh2, sc2, gt2 = [m.reshape(bsz, 1, d) for m in jnp.split(mod, 6, axis=-1)]

        wall = _fused_in_weight(w_in[l], w_merge[l])
        (q, kc_r, vc_r, ks, vs, kw, vw, gn, u, v, gate) = _proj(
            x2, sc1, sh1, wall, b_merge[l].reshape(1, -1), cosk, sink,
            sgu_ln_g[l].reshape(1, -1), sgu_ln_b[l].reshape(1, -1), seq)

        t16 = jnp.stack([_split_groups(kc_r, bsz, seq), _split_groups(vc_r, bsz, seq)])
        t16 = t16.reshape(2, bsz, NSA_KV_GROUPS, n_half, CMP_STRIDE * HEAD_DIM)
        cmp_kv = _compress(t16, cmp_pos[l].reshape(2, 1, CMP_BLOCK * HEAD_DIM), cmp_w1[l],
                           cmp_b1[l].reshape(2, 1, CMP_HIDDEN), cmp_w2[l], cmp_b2[l].reshape(2, 1, HEAD_DIM))

        front = ((0, 0), (0, 0), (WINDOW, 0), (0, 0))
        key_tiles = lambda a, tk: a.reshape(a.shape[:2] + (a.shape[2] // tk, tk, HEAD_DIM)).transpose(0, 1, 2, 4, 3)
        gn_t = gn.reshape(bsz, seq, NSA_KV_GROUPS, LANES)[..., :16].transpose(0, 2, 3, 1)
        ks4 = _split_groups(ks, bsz, seq)
        block_onehot = (jnp.arange(seq)[:, None] // SEL_BLOCK == jnp.arange(n_sel)[None, :]).astype(BF16)
        ks_ext = jnp.concatenate([ks4, jnp.broadcast_to(block_onehot, ks4.shape[:3] + (n_sel,))], axis=-1)
        o_nsa_t = _nsa(q.reshape(bsz, seq, NSA_WIDTH).transpose(0, 2, 1), cmp_kv[0], cmp_kv[1].transpose(0, 1, 3, 2),
                       ks_ext, key_tiles(_split_groups(vs, bsz, seq), TK),
                       jnp.pad(_split_groups(kw, bsz, seq), front),
                       key_tiles(jnp.pad(_split_groups(vw, bsz, seq), front), TQ),
                       gn_t, agg.T, n_cmp)
        o_nsa = o_nsa_t.transpose(0, 2, 1)

        bs = jnp.repeat(sgu_b[l].T, SGU_GROUP_DIM, axis=1)
        x1, h2, qp = _merge(o_nsa.reshape(n, NSA_WIDTH), u, v, gate, x2, gt1, sc2, sh2,
                            sgu_w[l], bs, w_branch[l].astype(BF16), w_out[l].astype(BF16),
                            ln1_g[l].reshape(1, d), ln1_b[l].reshape(1, d), peer_wq[l].astype(BF16), seq, alpha)

        idx_t, gate_t = _retrieve(qp, peer_keys[l])
        idx_flat = idx_t.transpose(0, 2, 1).reshape(n * PEER_PAIRS) * ROW_SUB
        w_t = _expert_act(idx_flat, h2.reshape(n, d // LANES, LANES), gate_t, _pack_table(peer_u[l]))
        w_flat = w_t.transpose(0, 2, 1).reshape(n * PEER_PAIRS)
        ffn = _expert_out(idx_flat, w_flat, _pack_table(peer_v[l]), n)
        x2 = _final(x1, ffn.reshape(n, d), gt2, ln2_g[l].reshape(1, d), ln2_b[l].reshape(1, d), seq, alpha)
    return x2.reshape(bsz, seq, d)
```

```python
import functools

import jax
import jax.numpy as jnp
from jax import lax
from jax.experimental import pallas as pl
from jax.experimental.pallas import tpu as pltpu

D_MODEL = 1024
NSA_HEADS = 8
NSA_KV_GROUPS = 2
NSA_REP = NSA_HEADS // NSA_KV_GROUPS
HEAD_DIM = 64
NSA_WIDTH = NSA_HEADS * HEAD_DIM
KV_WIDTH = NSA_KV_GROUPS * HEAD_DIM
CMP_BLOCK = 32
CMP_STRIDE = 16
CMP_HIDDEN = 2 * HEAD_DIM
SEL_BLOCK = 64
SEL_TOPK = 16
WINDOW = 512
ROPE_THETA = 10000.0
SGU_GROUPS = 8
SGU_WIDTH = D_MODEL // 2
SGU_GROUP_DIM = SGU_WIDTH // SGU_GROUPS
SGU_CHUNK = 128
PEER_HEADS = 8
PEER_NKEYS = 128
PEER_EXPERTS = PEER_NKEYS * PEER_NKEYS
PEER_QDIM = 256
PEER_HALF = PEER_QDIM // 2
PEER_TOPK = 16
PEER_PAIRS = PEER_HEADS * PEER_TOPK
LN_EPS = 1e-5
NEG_INF = -1e30
FORCE_SCORE = 1e9

LANES = 128
MIB = 1024 * 1024
BF16 = jnp.bfloat16
F32 = jnp.float32

TM_PROJ = 256
TM_MERGE = 256
TM_LN = 512
TQ = 128
TK = 256
KV_GROUP = 4
TC = 128
TB = 64

_OQ, _OQS = 0, 512
_OKC, _OKCS, _OKS, _OKSS, _OKW, _OKWS = 1024, 1152, 1280, 1408, 1536, 1664
_OVC, _OVS, _OVW = 1792, 1920, 2048
_OG = 2176
_OZ = 2432
_OM = 3456
_WCOLS = 5504


def _cparams(n_axes, vmem_mib):
    return pltpu.CompilerParams(
        dimension_semantics=("parallel",) * n_axes,
        vmem_limit_bytes=vmem_mib * MIB)


def _ln(x):
    mu = jnp.mean(x, axis=-1, keepdims=True)
    xc = x - mu
    var = jnp.mean(xc * xc, axis=-1, keepdims=True)
    return xc * lax.rsqrt(var + LN_EPS)


def _gelu(x):
    return 0.5 * x * (1.0 + lax.erf(x * (2.0 ** -0.5)))


def _dot(a, b):
    return jnp.dot(a, b, preferred_element_type=F32)


def _dot_nt(a, b):
    return lax.dot_general(a, b, (((1,), (1,)), ((), ())), preferred_element_type=F32)


def _ada_body(c_ref, w_ref, b_ref, o_ref):
    cv = c_ref[...]
    a = cv * jax.nn.sigmoid(cv)
    o_ref[...] = _dot(a.astype(BF16), w_ref[...].astype(BF16)) + b_ref[...]


def _ada(c, w, b):
    bsz, d = c.shape
    n = w.shape[1]
    tn = 1024
    return pl.pallas_call(
        _ada_body,
        grid=(n // tn,),
        in_specs=[pl.BlockSpec((bsz, d), lambda j: (0, 0)),
                  pl.BlockSpec((d, tn), lambda j: (0, j)),
                  pl.BlockSpec((1, tn), lambda j: (0, j))],
        out_specs=pl.BlockSpec((bsz, tn), lambda j: (0, j)),
        out_shape=jax.ShapeDtypeStruct((bsz, n), F32),
        compiler_params=_cparams(1, 32),
    )(c, w, b.reshape(1, n))


def _proj_body(x_ref, sc_ref, sh_ref, w_ref, bm_ref, cos_ref, sin_ref, sg_ref, sb_ref,
               q_ref, kc_ref, vc_ref, ks_ref, vs_ref, kw_ref, vw_ref, gn_ref, u_ref, v_ref, gate_ref):
    h = _ln(x_ref[...]) * (1.0 + sc_ref[0]) + sh_ref[0]
    hb = h.astype(BF16)

    def proj(off, width):
        return _dot(hb, w_ref[:, off:off + width])

    cosk = cos_ref[...]
    sink = sin_ref[...]
    cosq = jnp.concatenate([cosk] * (NSA_WIDTH // LANES), axis=1)
    sinq = jnp.concatenate([sink] * (NSA_WIDTH // LANES), axis=1)
    q_ref[...] = proj(_OQ, NSA_WIDTH) * cosq + proj(_OQS, NSA_WIDTH) * sinq
    kc_ref[...] = proj(_OKC, KV_WIDTH) * cosk + proj(_OKCS, KV_WIDTH) * sink
    ks_ref[...] = (proj(_OKS, KV_WIDTH) * cosk + proj(_OKSS, KV_WIDTH) * sink).astype(BF16)
    kw_ref[...] = (proj(_OKW, KV_WIDTH) * cosk + proj(_OKWS, KV_WIDTH) * sink).astype(BF16)
    vc_ref[...] = proj(_OVC, KV_WIDTH)
    vs_ref[...] = proj(_OVS, KV_WIDTH).astype(BF16)
    vw_ref[...] = proj(_OVW, KV_WIDTH).astype(BF16)
    gn_ref[...] = jax.nn.sigmoid(proj(_OG, 2 * LANES))
    u_ref[...] = _gelu(proj(_OZ, SGU_WIDTH))
    v_ref[...] = _ln(_gelu(proj(_OZ + SGU_WIDTH, SGU_WIDTH))) * sg_ref[...] + sb_ref[...]
    gate_ref[...] = jax.nn.sigmoid(proj(_OM, 2 * D_MODEL) + bm_ref[...])


def _proj(x2, sc, sh, wall, bm, cosk, sink, sg, sb, seq):
    n, d = x2.shape
    tm = TM_PROJ
    nst = seq // tm
    row = lambda w: pl.BlockSpec((tm, w), lambda i: (i, 0))
    per_batch = pl.BlockSpec((1, 1, d), lambda i: (i // nst, 0, 0))
    const = lambda shp: pl.BlockSpec(shp, lambda i: (0,) * len(shp))
    pos = pl.BlockSpec((tm, LANES), lambda i: (i % nst, 0))
    outs = [(NSA_WIDTH, F32), (KV_WIDTH, F32), (KV_WIDTH, F32), (KV_WIDTH, BF16), (KV_WIDTH, BF16),
            (KV_WIDTH, BF16), (KV_WIDTH, BF16), (2 * LANES, F32), (SGU_WIDTH, F32), (SGU_WIDTH, F32),
            (2 * D_MODEL, F32)]
    return pl.pallas_call(
        _proj_body,
        grid=(n // tm,),
        in_specs=[row(d), per_batch, per_batch, const((d, _WCOLS)), const((1, 2 * D_MODEL)),
                  pos, pos, const((1, SGU_WIDTH)), const((1, SGU_WIDTH))],
        out_specs=[row(w) for w, _ in outs],
        out_shape=[jax.ShapeDtypeStruct((n, w), dt) for w, dt in outs],
        compiler_params=_cparams(1, 56),
    )(x2, sc, sh, wall, bm, cosk, sink, sg, sb)


def _cmp_body(t_ref, pos_ref, w1_ref, b1_ref, w2_ref, b2_ref, o_ref):
    half = (CMP_BLOCK // 2) * HEAD_DIM
    t = t_ref[0, 0, 0]
    pos = pos_ref[0]
    ta = (t + pos[:, :half]).astype(BF16)
    tb = (t + pos[:, half:]).astype(BF16)
    a = _dot(ta, w1_ref[0, :half, :].astype(BF16))
    b = _dot(tb, w1_ref[0, half:, :].astype(BF16))
    nrow = t.shape[0]
    b_next = pltpu.roll(b, nrow - 1, axis=0)
    hid = _gelu(a + b_next + b1_ref[0])
    out = _dot(hid.astype(BF16), w2_ref[0].astype(BF16)) + b2_ref[0]
    o_ref[0, 0, 0] = out.astype(o_ref.dtype)


def _compress(t16, pos, w1, b1, w2, b2):
    _, bsz, g, nr, dd = t16.shape
    per_kind = lambda shp: pl.BlockSpec((1,) + shp, lambda k, b, gg: (k,) + (0,) * len(shp))
    blk = lambda w: pl.BlockSpec((1, 1, 1, nr, w), lambda k, b, gg: (k, b, gg, 0, 0))
    return pl.pallas_call(
        _cmp_body,
        grid=(2, bsz, g),
        in_specs=[blk(dd), per_kind((1, CMP_BLOCK * HEAD_DIM)), per_kind((CMP_BLOCK * HEAD_DIM, CMP_HIDDEN)),
                  per_kind((1, CMP_HIDDEN)), per_kind((CMP_HIDDEN, HEAD_DIM)), per_kind((1, HEAD_DIM))],
        out_specs=blk(HEAD_DIM),
        out_shape=jax.ShapeDtypeStruct((2, bsz, g, nr, HEAD_DIM), BF16),
        compiler_params=_cparams(3, 48),
    )(t16, pos, w1, b1, w2, b2)


def _scores(q_all, kt, bias):
    s = _dot(kt, q_all)
    return s if bias is None else s + bias


def _row_max(m, q_all, kt, bias=None):
    return jnp.maximum(m, jnp.max(_scores(q_all, kt, bias), axis=0, keepdims=True))


def _accumulate(state, m, q_all, kt, vt_t, bias=None):
    l, acc = state
    p = jnp.exp(_scores(q_all, kt, bias) - m)
    return l + jnp.sum(p, axis=0, keepdims=True), acc + _dot(vt_t, p.astype(BF16))


def _max_init():
    return jnp.full((1, NSA_REP * TQ), NEG_INF, F32)


def _acc_init():
    w = NSA_REP * TQ
    return jnp.zeros((1, w), F32), jnp.zeros((HEAD_DIM, w), F32)


def _lane_tile(a):
    return jnp.concatenate([a] * NSA_REP, axis=1)


def _nsa_body(q_ref, kc_ref, vc_ref, ks_ref, vs_ref, kw_ref, vw_ref, gn_ref, agg_ref, o_ref, *, n_cmp):
    qi = pl.program_id(2)
    q0 = qi * TQ
    t1 = q0 + lax.broadcasted_iota(jnp.int32, (1, TQ), 1)
    t = _lane_tile(t1)
    q_all = jnp.concatenate([q_ref[0, r * HEAD_DIM:(r + 1) * HEAD_DIM, :] for r in range(NSA_REP)],
                            axis=1).astype(BF16)

    kcm = kc_ref[0, 0]
    ncp = kcm.shape[0]
    nidx = lax.broadcasted_iota(jnp.int32, (ncp, 1), 0)
    mask_c = (nidx * CMP_STRIDE + (CMP_BLOCK - 1) <= t) & (nidx < n_cmp)
    sm = jnp.where(mask_c, _dot(kcm, q_all), NEG_INF)
    e = jnp.where(mask_c, jnp.exp(sm - jnp.max(sm, axis=0, keepdims=True)), 0.0)
    l = jnp.sum(e, axis=0, keepdims=True)
    p = e * (1.0 / jnp.where(l > 0.0, l, 1.0))
    o_c = _dot(vc_ref[0, 0], p.astype(BF16))
    psum = p[:, :TQ]
    for r in range(1, NSA_REP):
        psum = psum + p[:, r * TQ:(r + 1) * TQ]

    p_hi = psum.astype(BF16)
    p_lo = (psum - p_hi.astype(F32)).astype(BF16)
    agg_t = agg_ref[...]
    imp = _dot(agg_t, p_hi) + _dot(agg_t, p_lo)
    n_sel = imp.shape[0]
    blk = lax.broadcasted_iota(jnp.int32, (n_sel, 1), 0)
    blk_f = blk.astype(F32)
    cur = t1 // SEL_BLOCK
    forced = (blk == 0) | (blk == cur) | (blk == cur - 1)
    score = jnp.where(forced, FORCE_SCORE, jnp.where(blk <= cur, imp, NEG_INF))
    sel = jnp.zeros((n_sel, TQ), jnp.bool_)
    for _ in range(min(SEL_TOPK, n_sel)):
        m = jnp.max(score, axis=0, keepdims=True)
        first = jnp.min(jnp.where(score == m, blk_f, float(n_sel)), axis=0, keepdims=True)
        hit = blk_f == first
        sel = sel | hit
        score = jnp.where(hit, -jnp.inf, score)
    sel_neg = _lane_tile(jnp.where(sel, 0.0, NEG_INF).astype(BF16))
    q_ext = jnp.concatenate([q_all, sel_neg], axis=0)

    krow_w = lax.broadcasted_iota(jnp.int32, (TQ, 1), 0)

    def win_chunk(c):
        kpos = q0 - WINDOW + c * TQ + krow_w
        dlt = t - kpos
        bias = jnp.where((dlt >= 0) & (dlt < WINDOW) & (kpos >= 0), 0.0, NEG_INF)
        return kw_ref[0, 0, pl.ds(pl.multiple_of(q0 + c * TQ, TQ), TQ), :], bias

    n_win = WINDOW // TQ + 1
    m_w = _max_init()
    for c in range(n_win):
        kt, bias = win_chunk(c)
        m_w = _row_max(m_w, q_all, kt, bias)
    state = _acc_init()
    for c in range(n_win):
        kt, bias = win_chunk(c)
        state = _accumulate(state, m_w, q_all, kt, vw_ref[0, 0, qi + c], bias)
    o_w = state[1] / state[0]

    krow = lax.broadcasted_iota(jnp.int32, (TK, 1), 0)

    def group_keys(jg):
        kt = ks_ref[0, 0, pl.ds(pl.multiple_of(jg * (TK * KV_GROUP), TK * KV_GROUP), TK * KV_GROUP), :]
        vt_t = jnp.concatenate([vs_ref[0, 0, jg * KV_GROUP + sub] for sub in range(KV_GROUP)], axis=1)
        return kt, vt_t, None

    def tile_keys(j):
        k0 = pl.multiple_of(j * TK, TK)
        return ks_ref[0, 0, pl.ds(k0, TK), :], vs_ref[0, 0, j], jnp.where(k0 + krow <= t, 0.0, NEG_INF)

    def sweep(step, init):
        carry = lax.fori_loop(0, n_full, lambda jg, c: step(c, *group_keys(jg)), init)
        return lax.fori_loop(n_full * KV_GROUP, n_kv, lambda j, c: step(c, *tile_keys(j)), carry)

    n_full = q0 // (TK * KV_GROUP)
    n_kv = (q0 + TQ - 1) // TK + 1
    m_s = sweep(lambda m, kt, vt_t, bias: _row_max(m, q_ext, kt, bias), _max_init())
    state = sweep(lambda st, kt, vt_t, bias: _accumulate(st, m_s, q_ext, kt, vt_t, bias), _acc_init())
    o_s = state[1] / state[0]

    gn = gn_ref[0, 0]
    gate = [jnp.concatenate([gn[3 * r + c:3 * r + c + 1, :] for r in range(NSA_REP)], axis=1) for c in range(3)]
    o = gate[0] * o_c + gate[1] * o_s + gate[2] * o_w
    for r in range(NSA_REP):
        o_ref[0, r * HEAD_DIM:(r + 1) * HEAD_DIM, :] = o[:, r * TQ:(r + 1) * TQ]


def _nsa(q_t, kc, vc_t, ks, vs_t, kwp, vwp_t, gn_t, agg_t, n_cmp):
    bsz, _, seq = q_t.shape
    g = NSA_KV_GROUPS
    gw = NSA_REP * HEAD_DIM
    per_bg = lambda a: pl.BlockSpec((1, 1) + a.shape[2:], lambda b, gg, i: (b, gg) + (0,) * (a.ndim - 2))
    return pl.pallas_call(
        functools.partial(_nsa_body, n_cmp=n_cmp),
        grid=(bsz, g, seq // TQ),
        in_specs=[pl.BlockSpec((1, gw, TQ), lambda b, gg, i: (b, gg, i)),
                  per_bg(kc), per_bg(vc_t), per_bg(ks), per_bg(vs_t), per_bg(kwp), per_bg(vwp_t),
                  pl.BlockSpec((1, 1, gn_t.shape[2], TQ), lambda b, gg, i: (b, gg, 0, i)),
                  pl.BlockSpec(agg_t.shape, lambda b, gg, i: (0, 0))],
        out_specs=pl.BlockSpec((1, gw, TQ), lambda b, gg, i: (b, gg, i)),
        out_shape=jax.ShapeDtypeStruct((bsz, NSA_WIDTH, seq), F32),
        compiler_params=_cparams(3, 56),
    )(q_t, kc, vc_t, ks, vs_t, kwp, vwp_t, gn_t, agg_t)


def _merge_body(on_ref, u_ref, v_ref, gate_ref, x_ref, gt_ref, sc_ref, sh_ref, ws_ref, bs_ref,
                wb_ref, wo_ref, g1_ref, b1_ref, wq_ref, x1_ref, h2_ref, qp_ref, *, alpha):
    tm = x_ref.shape[0]
    row = lax.broadcasted_iota(jnp.int32, (SGU_CHUNK, SGU_CHUNK), 0)
    col = lax.broadcasted_iota(jnp.int32, (SGU_CHUNK, SGU_CHUNK), 1)
    lane_group = lax.broadcasted_iota(jnp.int32, (1, SGU_WIDTH), 1) // SGU_GROUP_DIM
    mixed = []
    for c in range(tm // SGU_CHUNK):
        v = v_ref[c * SGU_CHUNK:(c + 1) * SGU_CHUNK, :]
        acc = jnp.zeros((SGU_CHUNK, SGU_WIDTH), F32)
        for g in range(SGU_GROUPS):
            wg = jnp.where(col <= row, ws_ref[g], 0.0).astype(BF16)
            vg = jnp.where(lane_group == g, v, 0.0).astype(BF16)
            acc = acc + _dot(wg, vg)
        mixed.append(acc + bs_ref[...])
    o_sgu = u_ref[...] * jnp.concatenate(mixed, axis=0)
    gate = gate_ref[...]
    merged = (gate[:, :D_MODEL] * _dot(on_ref[...].astype(BF16), wb_ref[0])
              + gate[:, D_MODEL:] * _dot(o_sgu.astype(BF16), wb_ref[1]))
    mix = _dot(merged.astype(BF16), wo_ref[...])
    x1 = _ln(alpha * x_ref[...] + gt_ref[0] * mix) * g1_ref[...] + b1_ref[...]
    x1_ref[...] = x1
    h2 = _ln(x1) * (1.0 + sc_ref[0]) + sh_ref[0]
    h2_ref[...] = h2
    qp_ref[...] = _dot(h2.astype(BF16), wq_ref[...])


def _merge(o_nsa, u, v, gate, x2, gt1, sc2, sh2, ws, bs, wb, wo, g1, b1, wq, seq, alpha):
    n, d = x2.shape
    tm = TM_MERGE
    nst = seq // tm
    row = lambda w: pl.BlockSpec((tm, w), lambda i: (i, 0))
    per_batch = pl.BlockSpec((1, 1, d), lambda i: (i // nst, 0, 0))
    const = lambda a: pl.BlockSpec(a.shape, lambda i: (0,) * a.ndim)
    nq = wq.shape[1]
    return pl.pallas_call(
        functools.partial(_merge_body, alpha=alpha),
        grid=(n // tm,),
        in_specs=[row(NSA_WIDTH), row(SGU_WIDTH), row(SGU_WIDTH), row(2 * d), row(d),
                  per_batch, per_batch, per_batch, const(ws), const(bs), const(wb), const(wo),
                  const(g1), const(b1), const(wq)],
        out_specs=[row(d), row(d), row(nq)],
        out_shape=[jax.ShapeDtypeStruct((n, d), F32), jax.ShapeDtypeStruct((n, d), F32),
                   jax.ShapeDtypeStruct((n, nq), F32)],
        compiler_params=_cparams(1, 56),
    )(o_nsa, u, v, gate, x2, gt1, sc2, sh2, ws, bs, wb, wo, g1, b1, wq)


def _topk_rows(s, k):
    nrow = s.shape[0]
    rid = lax.broadcasted_iota(jnp.int32, s.shape, 0).astype(F32)
    vals, rows = [], []
    for _ in range(k):
        m = jnp.max(s, axis=0, keepdims=True)
        first = jnp.min(jnp.where(s == m, rid, float(nrow)), axis=0, keepdims=True)
        s = jnp.where(rid == first, -jnp.inf, s)
        vals.append(m)
        rows.append(first)
    return vals, rows


def _candidate_blocks(k):
    todo = {(a, b) for a in range(k) for b in range(k) if (a + 1) * (b + 1) <= k}
    options = [(axis, fixed, start) for axis in (0, 1) for fixed in range(k) for start in range(0, k, 8)]
    cells = lambda axis, fixed, start: [((fixed, start + r) if axis == 0 else (start + r, fixed)) for r in range(8)]
    blocks = []
    while todo:
        best = max(options, key=lambda o: sum(c in todo for c in cells(*o)))
        owned = tuple(c in todo for c in cells(*best))
        todo -= set(cells(*best))
        blocks.append(best + (owned,))
    return tuple(blocks)


_CAND_BLOCKS = _candidate_blocks(PEER_TOPK)


def _retrieve_body(qp_ref, keys_ref, idx_ref, gate_ref):
    k = PEER_TOPK
    for h in range(PEER_HEADS):
        tops = []
        for c in range(2):
            off = (h * 2 + c) * PEER_HALF
            qh = qp_ref[:, off:off + PEER_HALF].astype(BF16)
            s = _dot_nt(keys_ref[h, c].astype(BF16), qh)
            tops.append(_topk_rows(s, k))
        (v1, i1), (v2, i2) = tops
        vm = (jnp.concatenate(v1, axis=0), jnp.concatenate(v2, axis=0))
        im = (jnp.concatenate(i1, axis=0), jnp.concatenate(i2, axis=0))
        sub = lax.broadcasted_iota(jnp.int32, (8, 1), 0)
        vals, eids, poss = [], [], []
        for axis, fixed, start, owned in _CAND_BLOCKS:
            if axis == 0:
                v = v1[fixed] + vm[1][start:start + 8]
                e = i1[fixed] * float(PEER_NKEYS) + im[1][start:start + 8]
                pos = fixed * k + start + sub
            else:
                v = vm[0][start:start + 8] + v2[fixed]
                e = im[0][start:start + 8] * float(PEER_NKEYS) + i2[fixed]
                pos = (start + sub) * k + fixed
            if not all(owned):
                keep = functools.reduce(jnp.logical_or, [sub == r for r in range(8) if owned[r]])
                v = jnp.where(keep, v, -jnp.inf)
                pos = jnp.where(keep, pos, k * k)
            vals.append(v)
            eids.append(e)
            poss.append(pos.astype(F32))
        cand = jnp.concatenate(vals, axis=0)
        cidx = jnp.concatenate(eids, axis=0)
        rid = jnp.concatenate(poss, axis=0)
        top_s, top_e = [], []
        for _ in range(k):
            m = jnp.max(cand, axis=0, keepdims=True)
            first = jnp.min(jnp.where(cand == m, rid, float(k * k)), axis=0, keepdims=True)
            hit = rid == first
            top_e.append(jnp.sum(jnp.where(hit, cidx, 0.0), axis=0, keepdims=True))
            cand = jnp.where(hit, -jnp.inf, cand)
            top_s.append(m)
        ts = jnp.concatenate(top_s, axis=0)
        e = jnp.exp(ts - ts[0:1])
        gate_ref[0, h * k:(h + 1) * k, :] = e / jnp.sum(e, axis=0, keepdims=True)
        idx_ref[0, h * k:(h + 1) * k, :] = jnp.concatenate(top_e, axis=0).astype(jnp.int32)


def _retrieve(qp, keys):
    n = qp.shape[0]
    nt = n // TC
    out = pl.BlockSpec((1, PEER_PAIRS, TC), lambda i: (i, 0, 0))
    return pl.pallas_call(
        _retrieve_body,
        grid=(nt,),
        in_specs=[pl.BlockSpec((TC, qp.shape[1]), lambda i: (i, 0)),
                  pl.BlockSpec(keys.shape, lambda i: (0, 0, 0, 0))],
        out_specs=[out, out],
        out_shape=[jax.ShapeDtypeStruct((nt, PEER_PAIRS, TC), jnp.int32),
                   jax.ShapeDtypeStruct((nt, PEER_PAIRS, TC), F32)],
        compiler_params=_cparams(1, 48),
    )(qp, keys)


ROW_SUB = D_MODEL // 2 // LANES
ACT_UNROLL = 8
OFF_REGS = 8


def _load_row(tab_ref, scaled_idx):
    row = tab_ref[pl.ds(pl.multiple_of(scaled_idx, ROW_SUB), ROW_SUB), :]
    lo = lax.bitcast_convert_type(lax.shift_left(row, jnp.uint32(16)), F32)
    hi = lax.bitcast_convert_type(row & jnp.uint32(0xFFFF0000), F32)
    return lo, hi


def _expert_act_body(off_ref, idx_ref, h_ref, gate_ref, tab_ref, w_ref, slot_ref, part_ref):
    off = [off_ref[j] for j in range(OFF_REGS)]

    def token(t, carry):
        x = h_ref[t]
        xlo, xhi = x[:ROW_SUB], x[ROW_SUB:]
        base = pl.multiple_of(t * PEER_PAIRS, PEER_PAIRS)
        for p in range(PEER_PAIRS):
            if p % OFF_REGS == 0:
                idx_sub = idx_ref.at[pl.ds(base + p, OFF_REGS)]
            lo, hi = _load_row(tab_ref, idx_sub[off[p % OFF_REGS]])
            slot_ref[p * ROW_SUB:(p + 1) * ROW_SUB, :] = lo * xlo + hi * xhi
        part = slot_ref[pl.ds(0, PEER_PAIRS, stride=ROW_SUB), :]
        for s in range(1, ROW_SUB):
            part = part + slot_ref[pl.ds(s, PEER_PAIRS, stride=ROW_SUB), :]
        part_ref[pl.ds(pl.multiple_of(base, PEER_PAIRS), PEER_PAIRS), :] = part
        return carry

    lax.fori_loop(0, TC, token, 0)

    tok_lane = lax.broadcasted_iota(jnp.int32, (PEER_PAIRS, TC), 1)

    def lane_sums(i, act):
        for k in range(ACT_UNROLL):
            t = i * ACT_UNROLL + k
            part = part_ref[pl.ds(pl.multiple_of(t * PEER_PAIRS, PEER_PAIRS), PEER_PAIRS), :]
            act = jnp.where(tok_lane == t, jnp.sum(part, axis=1, keepdims=True), act)
        return act

    act = lax.fori_loop(0, TC // ACT_UNROLL, lane_sums, jnp.zeros((PEER_PAIRS, TC), F32))
    w_ref[0] = gate_ref[0] * _gelu(act)


def _expert_act(idx_flat, h3, gate_t, tab):
    nt = gate_t.shape[0]
    tile = pl.BlockSpec((1, PEER_PAIRS, TC), lambda i: (i, 0, 0))
    return pl.pallas_call(
        _expert_act_body,
        grid=(nt,),
        in_specs=[pl.BlockSpec(memory_space=pltpu.SMEM),
                  pl.BlockSpec((TC * PEER_PAIRS,), lambda i: (i,), memory_space=pltpu.SMEM),
                  pl.BlockSpec((TC,) + h3.shape[1:], lambda i: (i, 0, 0)),
                  tile,
                  pl.BlockSpec(tab.shape, lambda i: (0, 0), pipeline_mode=pl.Buffered(1))],
        out_specs=tile,
        out_shape=jax.ShapeDtypeStruct(gate_t.shape, F32),
        scratch_shapes=[pltpu.VMEM((PEER_PAIRS * ROW_SUB, LANES), F32),
                        pltpu.VMEM((TC * PEER_PAIRS, LANES), F32)],
        compiler_params=_cparams(1, 52),
    )(jnp.arange(OFF_REGS, dtype=jnp.int32), idx_flat, h3, gate_t, tab)


def _expert_out_body(off_ref, idx_ref, w_ref, tab_ref, o_ref):
    n_acc = 4
    off = [off_ref[j] for j in range(OFF_REGS)]

    def token(t, carry):
        base = pl.multiple_of(t * PEER_PAIRS, PEER_PAIRS)
        acc_lo = [jnp.zeros((ROW_SUB, LANES), F32) for _ in range(n_acc)]
        acc_hi = [jnp.zeros((ROW_SUB, LANES), F32) for _ in range(n_acc)]
        for p in range(PEER_PAIRS):
            if p % OFF_REGS == 0:
                idx_sub = idx_ref.at[pl.ds(base + p, OFF_REGS)]
                w_sub = w_ref.at[pl.ds(base + p, OFF_REGS)]
            lo, hi = _load_row(tab_ref, idx_sub[off[p % OFF_REGS]])
            w = w_sub[off[p % OFF_REGS]]
            acc_lo[p % n_acc] = acc_lo[p % n_acc] + w * lo
            acc_hi[p % n_acc] = acc_hi[p % n_acc] + w * hi
        lo = (acc_lo[0] + acc_lo[1]) + (acc_lo[2] + acc_lo[3])
        hi = (acc_hi[0] + acc_hi[1]) + (acc_hi[2] + acc_hi[3])
        o_ref[t] = jnp.concatenate([lo, hi], axis=0)
        return carry

    lax.fori_loop(0, TB, token, 0)


def _expert_out(idx_flat, w_flat, tab, n):
    smem = pl.BlockSpec((TB * PEER_PAIRS,), lambda i: (i,), memory_space=pltpu.SMEM)
    rows = D_MODEL // LANES
    return pl.pallas_call(
        _expert_out_body,
        grid=(n // TB,),
        in_specs=[pl.BlockSpec(memory_space=pltpu.SMEM), smem, smem,
                  pl.BlockSpec(tab.shape, lambda i: (0, 0), pipeline_mode=pl.Buffered(1))],
        out_specs=pl.BlockSpec((TB, rows, LANES), lambda i: (i, 0, 0)),
        out_shape=jax.ShapeDtypeStruct((n, rows, LANES), F32),
        compiler_params=_cparams(1, 48),
    )(jnp.arange(OFF_REGS, dtype=jnp.int32), idx_flat, w_flat, tab)


def _final_body(x1_ref, f_ref, gt_ref, g_ref, b_ref, o_ref, *, alpha):
    o_ref[...] = _ln(alpha * x1_ref[...] + gt_ref[0] * f_ref[...]) * g_ref[...] + b_ref[...]


def _final(x1, ffn, gt2, g2, b2, seq, alpha):
    n, d = x1.shape
    tm = TM_LN
    nst = seq // tm
    row = pl.BlockSpec((tm, d), lambda i: (i, 0))
    const = pl.BlockSpec((1, d), lambda i: (0, 0))
    return pl.pallas_call(
        functools.partial(_final_body, alpha=alpha),
        grid=(n // tm,),
        in_specs=[row, row, pl.BlockSpec((1, 1, d), lambda i: (i // nst, 0, 0)), const, const],
        out_specs=row,
        out_shape=jax.ShapeDtypeStruct((n, d), F32),
        compiler_params=_cparams(1, 32),
    )(x1, ffn, gt2, g2, b2)


def _swap_halves(w, heads):
    d = w.shape[0]
    w4 = w.reshape(d, heads, 2, HEAD_DIM // 2)
    return jnp.flip(w4, axis=2).reshape(d, heads * HEAD_DIM)


def _fused_in_weight(w_in, w_merge):
    d = w_in.shape[0]
    o = 0
    parts = {}
    for name, width in (("q", NSA_WIDTH), ("kc", KV_WIDTH), ("vc", KV_WIDTH), ("ks", KV_WIDTH),
                        ("vs", KV_WIDTH), ("kw", KV_WIDTH), ("vw", KV_WIDTH), ("g", 3 * NSA_HEADS),
                        ("z", 2 * SGU_WIDTH)):
        parts[name] = w_in[:, o:o + width]
        o += width
    wq = parts["q"] * (HEAD_DIM ** -0.5)
    gcols = 3 * NSA_REP
    wg = jnp.zeros((d, 2 * LANES), w_in.dtype)
    wg = wg.at[:, :gcols].set(parts["g"][:, :gcols]).at[:, LANES:LANES + gcols].set(parts["g"][:, gcols:])
    g = NSA_KV_GROUPS
    cols = [wq, _swap_halves(wq, NSA_HEADS),
            parts["kc"], _swap_halves(parts["kc"], g), parts["ks"], _swap_halves(parts["ks"], g),
            parts["kw"], _swap_halves(parts["kw"], g), parts["vc"], parts["vs"], parts["vw"],
            wg, parts["z"], w_merge]
    return jnp.concatenate(cols, axis=1).astype(BF16)


def _rope_tables(seq):
    half = HEAD_DIM // 2
    pos = jnp.arange(seq, dtype=F32)
    inv_freq = ROPE_THETA ** (-jnp.arange(half, dtype=F32) / half)
    ang = pos[:, None] * inv_freq[None, :]
    cos, sin = jnp.cos(ang), jnp.sin(ang)
    reps = LANES // HEAD_DIM
    cosk = jnp.tile(jnp.concatenate([cos, cos], axis=1), (1, reps))
    sink = jnp.tile(jnp.concatenate([-sin, sin], axis=1), (1, reps))
    return cosk, sink


def _sel_aggregation(n_cmp_pad, n_sel):
    c0 = jnp.arange(n_cmp_pad)[:, None] * CMP_STRIDE
    s0 = jnp.arange(n_sel)[None, :] * SEL_BLOCK
    ov = jnp.clip(jnp.minimum(c0 + CMP_BLOCK, s0 + SEL_BLOCK) - jnp.maximum(c0, s0), 0, None)
    return (ov / CMP_BLOCK).astype(BF16)


def _pack_table(tab):
    half = tab.shape[1] // 2
    bits = lax.bitcast_convert_type(tab.astype(BF16), jnp.uint16).astype(jnp.uint32)
    packed = (bits[:, half:] << 16) | bits[:, :half]
    return packed.reshape(tab.shape[0] * ROW_SUB, LANES)


def _split_groups(a, bsz, seq):
    return a.reshape(bsz, seq, NSA_KV_GROUPS, HEAD_DIM).transpose(0, 2, 1, 3)


def kernel(x, c, w_ada, b_ada, w_in, cmp_pos, cmp_w1, cmp_b1, cmp_w2, cmp_b2, sgu_ln_g, sgu_ln_b, sgu_w, sgu_b, w_branch, w_merge, b_merge, w_out, ln1_g, ln1_b, peer_wq, peer_keys, peer_u, peer_v, ln2_g, ln2_b):
    bsz, seq, d = x.shape
    n = bsz * seq
    depth = w_ada.shape[0]
    alpha = (2.0 * depth) ** 0.25
    n_cmp = (seq - CMP_BLOCK) // CMP_STRIDE + 1
    n_half = seq // CMP_STRIDE
    n_sel = seq // SEL_BLOCK
    cosk, sink = _rope_tables(seq)
    agg = _sel_aggregation(n_half, n_sel)
    x2 = x.reshape(n, d)
    for l in range(depth):
        mod = _ada(c, w_ada[l], b_ada[l])
        sh1, sc1, gt1, sh2, sc2, gt2 = [m.reshape(bsz, 1, d) for m in jnp.split(mod, 6, axis=-1)]

        wall = _fused_in_weight(w_in[l], w_merge[l])
        (q, kc_r, vc_r, ks, vs, kw, vw, gn, u, v, gate) = _proj(
            x2, sc1, sh1, wall, b_merge[l].reshape(1, -1), cosk, sink,
            sgu_ln_g[l].reshape(1, -1), sgu_ln_b[l].reshape(1, -1), seq)

        t16 = jnp.stack([_split_groups(kc_r, bsz, seq), _split_groups(vc_r, bsz, seq)])
        t16 = t16.reshape(2, bsz, NSA_KV_GROUPS, n_half, CMP_STRIDE * HEAD_DIM)
        cmp_kv = _compress(t16, cmp_pos[l].reshape(2, 1, CMP_BLOCK * HEAD_DIM), cmp_w1[l],
                           cmp_b1[l].reshape(2, 1, CMP_HIDDEN), cmp_w2[l], cmp_b2[l].reshape(2, 1, HEAD_DIM))

        front = ((0, 0), (0, 0), (WINDOW, 0), (0, 0))
        key_tiles = lambda a, tk: a.reshape(a.shape[:2] + (a.shape[2] // tk, tk, HEAD_DIM)).transpose(0, 1, 2, 4, 3)
        gn_t = gn.reshape(bsz, seq, NSA_KV_GROUPS, LANES)[..., :16].transpose(0, 2, 3, 1)
        ks4 = _split_groups(ks, bsz, seq)
        block_onehot = (jnp.arange(seq)[:, None] // SEL_BLOCK == jnp.arange(n_sel)[None, :]).astype(BF16)
        ks_ext = jnp.concatenate([ks4, jnp.broadcast_to(block_onehot, ks4.shape[:3] + (n_sel,))], axis=-1)
        o_nsa_t = _nsa(q.reshape(bsz, seq, NSA_WIDTH).transpose(0, 2, 1), cmp_kv[0], cmp_kv[1].transpose(0, 1, 3, 2),
                       ks_ext, key_tiles(_split_groups(vs, bsz, seq), TK),
                       jnp.pad(_split_groups(kw, bsz, seq), front),
                       key_tiles(jnp.pad(_split_groups(vw, bsz, seq), front), TQ),
                       gn_t, agg.T, n_cmp)
        o_nsa = o_nsa_t.transpose(0, 2, 1)

        bs = jnp.repeat(sgu_b[l].T, SGU_GROUP_DIM, axis=1)
        x1, h2, qp = _merge(o_nsa.reshape(n, NSA_WIDTH), u, v, gate, x2, gt1, sc2, sh2,
                            sgu_w[l], bs, w_branch[l].astype(BF16), w_out[l].astype(BF16),
                            ln1_g[l].reshape(1, d), ln1_b[l].reshape(1, d), peer_wq[l].astype(BF16), seq, alpha)

        idx_t, gate_t = _retrieve(qp, peer_keys[l])
        idx_flat = idx_t.transpose(0, 2, 1).reshape(n * PEER_PAIRS) * ROW_SUB
        w_t = _expert_act(idx_flat, h2.reshape(n, d // LANES, LANES), gate_t, _pack_table(peer_u[l]))
        w_flat = w_t.transpose(0, 2, 1).reshape(n * PEER_PAIRS)
        ffn = _expert_out(idx_flat, w_flat, _pack_table(peer_v[l]), n)
        x2 = _final(x1, ffn.reshape(n, d), gt2, ln2_g[l].reshape(1, d), ln2_b[l].reshape(1, d), seq, alpha)
    return x2.reshape(bsz, seq, d)
```

```python
import functools

import jax
import jax.numpy as jnp
from jax import lax
from jax.experimental import pallas as pl
from jax.experimental.pallas import tpu as pltpu

D_MODEL = 1024
NSA_HEADS = 8
NSA_KV_GROUPS = 2
NSA_REP = NSA_HEADS // NSA_KV_GROUPS
HEAD_DIM = 64
NSA_WIDTH = NSA_HEADS * HEAD_DIM
KV_WIDTH = NSA_KV_GROUPS * HEAD_DIM
CMP_BLOCK = 32
CMP_STRIDE = 16
CMP_HIDDEN = 2 * HEAD_DIM
SEL_BLOCK = 64
SEL_TOPK = 16
WINDOW = 512
ROPE_THETA = 10000.0
SGU_GROUPS = 8
SGU_WIDTH = D_MODEL // 2
SGU_GROUP_DIM = SGU_WIDTH // SGU_GROUPS
SGU_CHUNK = 128
PEER_HEADS = 8
PEER_NKEYS = 128
PEER_EXPERTS = PEER_NKEYS * PEER_NKEYS
PEER_QDIM = 256
PEER_HALF = PEER_QDIM // 2
PEER_TOPK = 16
PEER_PAIRS = PEER_HEADS * PEER_TOPK
LN_EPS = 1e-5
NEG_INF = -1e30
FORCE_SCORE = 1e9

LANES = 128
MIB = 1024 * 1024
BF16 = jnp.bfloat16
F32 = jnp.float32

TM_PROJ = 256
TM_MERGE = 256
TM_LN = 512
TQ = 128
TK = 256
KV_GROUP = 4
GROUP_BLOCKS = TK * KV_GROUP // SEL_BLOCK
TC = 128
TB = 64

_OQ, _OQS = 0, 512
_OKC, _OKCS, _OKS, _OKSS, _OKW, _OKWS = 1024, 1152, 1280, 1408, 1536, 1664
_OVC, _OVS, _OVW = 1792, 1920, 2048
_OG = 2176
_OZ = 2432
_OM = 3456
_WCOLS = 5504


def _cparams(n_axes, vmem_mib):
    return pltpu.CompilerParams(
        dimension_semantics=("parallel",) * n_axes,
        vmem_limit_bytes=vmem_mib * MIB)


def _ln(x):
    mu = jnp.mean(x, axis=-1, keepdims=True)
    xc = x - mu
    var = jnp.mean(xc * xc, axis=-1, keepdims=True)
    return xc * lax.rsqrt(var + LN_EPS)


def _gelu(x):
    return 0.5 * x * (1.0 + lax.erf(x * (2.0 ** -0.5)))


def _dot(a, b):
    return jnp.dot(a, b, preferred_element_type=F32)


def _dot_nt(a, b):
    return lax.dot_general(a, b, (((1,), (1,)), ((), ())), preferred_element_type=F32)


def _ada_body(c_ref, w_ref, b_ref, o_ref):
    cv = c_ref[...]
    a = cv * jax.nn.sigmoid(cv)
    o_ref[...] = _dot(a.astype(BF16), w_ref[...].astype(BF16)) + b_ref[...]


def _ada(c, w, b):
    bsz, d = c.shape
    n = w.shape[1]
    tn = 1024
    return pl.pallas_call(
        _ada_body,
        grid=(n // tn,),
        in_specs=[pl.BlockSpec((bsz, d), lambda j: (0, 0)),
                  pl.BlockSpec((d, tn), lambda j: (0, j)),
                  pl.BlockSpec((1, tn), lambda j: (0, j))],
        out_specs=pl.BlockSpec((bsz, tn), lambda j: (0, j)),
        out_shape=jax.ShapeDtypeStruct((bsz, n), F32),
        compiler_params=_cparams(1, 32),
    )(c, w, b.reshape(1, n))


def _proj_body(x_ref, sc_ref, sh_ref, w_ref, bm_ref, cos_ref, sin_ref, sg_ref, sb_ref,
               q_ref, kc_ref, vc_ref, ks_ref, vs_ref, kw_ref, vw_ref, gn_ref, u_ref, v_ref, gate_ref):
    h = _ln(x_ref[...]) * (1.0 + sc_ref[0]) + sh_ref[0]
    hb = h.astype(BF16)

    def proj(off, width):
        return _dot(hb, w_ref[:, off:off + width])

    cosk = cos_ref[...]
    sink = sin_ref[...]
    cosq = jnp.concatenate([cosk] * (NSA_WIDTH // LANES), axis=1)
    sinq = jnp.concatenate([sink] * (NSA_WIDTH // LANES), axis=1)
    q_ref[...] = proj(_OQ, NSA_WIDTH) * cosq + proj(_OQS, NSA_WIDTH) * sinq
    kc_ref[...] = proj(_OKC, KV_WIDTH) * cosk + proj(_OKCS, KV_WIDTH) * sink
    ks_ref[...] = (proj(_OKS, KV_WIDTH) * cosk + proj(_OKSS, KV_WIDTH) * sink).astype(BF16)
    kw_ref[...] = (proj(_OKW, KV_WIDTH) * cosk + proj(_OKWS, KV_WIDTH) * sink).astype(BF16)
    vc_ref[...] = proj(_OVC, KV_WIDTH)
    vs_ref[...] = proj(_OVS, KV_WIDTH).astype(BF16)
    vw_ref[...] = proj(_OVW, KV_WIDTH).astype(BF16)
    gn_ref[...] = jax.nn.sigmoid(proj(_OG, 2 * LANES))
    u_ref[...] = _gelu(proj(_OZ, SGU_WIDTH))
    v_ref[...] = _ln(_gelu(proj(_OZ + SGU_WIDTH, SGU_WIDTH))) * sg_ref[...] + sb_ref[...]
    gate_ref[...] = jax.nn.sigmoid(proj(_OM, 2 * D_MODEL) + bm_ref[...])


def _proj(x2, sc, sh, wall, bm, cosk, sink, sg, sb, seq):
    n, d = x2.shape
    tm = TM_PROJ
    nst = seq // tm
    row = lambda w: pl.BlockSpec((tm, w), lambda i: (i, 0))
    per_batch = pl.BlockSpec((1, 1, d), lambda i: (i // nst, 0, 0))
    const = lambda shp: pl.BlockSpec(shp, lambda i: (0,) * len(shp))
    pos = pl.BlockSpec((tm, LANES), lambda i: (i % nst, 0))
    outs = [(NSA_WIDTH, F32), (KV_WIDTH, F32), (KV_WIDTH, F32), (KV_WIDTH, BF16), (KV_WIDTH, BF16),
            (KV_WIDTH, BF16), (KV_WIDTH, BF16), (2 * LANES, F32), (SGU_WIDTH, F32), (SGU_WIDTH, F32),
            (2 * D_MODEL, F32)]
    return pl.pallas_call(
        _proj_body,
        grid=(n // tm,),
        in_specs=[row(d), per_batch, per_batch, const((d, _WCOLS)), const((1, 2 * D_MODEL)),
                  pos, pos, const((1, SGU_WIDTH)), const((1, SGU_WIDTH))],
        out_specs=[row(w) for w, _ in outs],
        out_shape=[jax.ShapeDtypeStruct((n, w), dt) for w, dt in outs],
        compiler_params=_cparams(1, 56),
    )(x2, sc, sh, wall, bm, cosk, sink, sg, sb)


def _cmp_body(t_ref, pos_ref, w1_ref, b1_ref, w2_ref, b2_ref, o_ref):
    half = (CMP_BLOCK // 2) * HEAD_DIM
    t = t_ref[0, 0, 0]
    pos = pos_ref[0]
    ta = (t + pos[:, :half]).astype(BF16)
    tb = (t + pos[:, half:]).astype(BF16)
    a = _dot(ta, w1_ref[0, :half, :].astype(BF16))
    b = _dot(tb, w1_ref[0, half:, :].astype(BF16))
    nrow = t.shape[0]
    b_next = pltpu.roll(b, nrow - 1, axis=0)
    hid = _gelu(a + b_next + b1_ref[0])
    out = _dot(hid.astype(BF16), w2_ref[0].astype(BF16)) + b2_ref[0]
    o_ref[0, 0, 0] = out.astype(o_ref.dtype)


def _compress(t16, pos, w1, b1, w2, b2):
    _, bsz, g, nr, dd = t16.shape
    per_kind = lambda shp: pl.BlockSpec((1,) + shp, lambda k, b, gg: (k,) + (0,) * len(shp))
    blk = lambda w: pl.BlockSpec((1, 1, 1, nr, w), lambda k, b, gg: (k, b, gg, 0, 0))
    return pl.pallas_call(
        _cmp_body,
        grid=(2, bsz, g),
        in_specs=[blk(dd), per_kind((1, CMP_BLOCK * HEAD_DIM)), per_kind((CMP_BLOCK * HEAD_DIM, CMP_HIDDEN)),
                  per_kind((1, CMP_HIDDEN)), per_kind((CMP_HIDDEN, HEAD_DIM)), per_kind((1, HEAD_DIM))],
        out_specs=blk(HEAD_DIM),
        out_shape=jax.ShapeDtypeStruct((2, bsz, g, nr, HEAD_DIM), BF16),
        compiler_params=_cparams(3, 48),
    )(t16, pos, w1, b1, w2, b2)


def _scores(q_all, kt, bias):
    s = _dot(kt, q_all)
    return s if bias is None else s + bias


def _row_max(m, q_all, kt, bias=None):
    return jnp.maximum(m, jnp.max(_scores(q_all, kt, bias), axis=0, keepdims=True))


def _accumulate(state, m, q_all, kt, vt_t, bias=None):
    l, acc = state
    p = jnp.exp(_scores(q_all, kt, bias) - m)
    return l + jnp.sum(p, axis=0, keepdims=True), acc + _dot(vt_t, p.astype(BF16))


def _max_init():
    return jnp.full((1, NSA_REP * TQ), NEG_INF, F32)


def _acc_init():
    w = NSA_REP * TQ
    return jnp.zeros((1, w), F32), jnp.zeros((HEAD_DIM, w), F32)


def _lane_tile(a):
    return jnp.concatenate([a] * NSA_REP, axis=1)


def _nsa_body(q_ref, kc_ref, vc_ref, ks_ref, vs_ref, kw_ref, vw_ref, gn_ref, agg_ref, o_ref, selneg_ref, score_ref, *, n_cmp):
    qi = pl.program_id(2)
    q0 = qi * TQ
    t1 = q0 + lax.broadcasted_iota(jnp.int32, (1, TQ), 1)
    t = _lane_tile(t1)
    q_all = jnp.concatenate([q_ref[0, r * HEAD_DIM:(r + 1) * HEAD_DIM, :] for r in range(NSA_REP)],
                            axis=1).astype(BF16)

    kcm = kc_ref[0, 0]
    ncp = kcm.shape[0]
    nidx = lax.broadcasted_iota(jnp.int32, (ncp, 1), 0)
    mask_c = (nidx * CMP_STRIDE + (CMP_BLOCK - 1) <= t) & (nidx < n_cmp)
    sm = jnp.where(mask_c, _dot(kcm, q_all), NEG_INF)
    e = jnp.where(mask_c, jnp.exp(sm - jnp.max(sm, axis=0, keepdims=True)), 0.0)
    l = jnp.sum(e, axis=0, keepdims=True)
    p = e * (1.0 / jnp.where(l > 0.0, l, 1.0))
    o_c = _dot(vc_ref[0, 0], p.astype(BF16))
    psum = p[:, :TQ]
    for r in range(1, NSA_REP):
        psum = psum + p[:, r * TQ:(r + 1) * TQ]

    p_hi = psum.astype(BF16)
    p_lo = (psum - p_hi.astype(F32)).astype(BF16)
    agg_t = agg_ref[...]
    imp = _dot(agg_t, p_hi) + _dot(agg_t, p_lo)
    n_sel = imp.shape[0]
    blk = lax.broadcasted_iota(jnp.int32, (n_sel, 1), 0)
    blk_f = blk.astype(F32)
    cur = t1 // SEL_BLOCK
    forced = (blk == 0) | (blk == cur) | (blk == cur - 1)
    score = jnp.where(forced, FORCE_SCORE, jnp.where(blk <= cur, imp, NEG_INF))
    sel = jnp.zeros((n_sel, TQ), jnp.bool_)
    for _ in range(min(SEL_TOPK, n_sel)):
        m = jnp.max(score, axis=0, keepdims=True)
        first = jnp.min(jnp.where(score == m, blk_f, float(n_sel)), axis=0, keepdims=True)
        hit = blk_f == first
        sel = sel | hit
        score = jnp.where(hit, -jnp.inf, score)
    selneg_ref[...] = _lane_tile(jnp.where(sel, 0.0, NEG_INF).astype(BF16))

    def q_ext(jg):
        rows = selneg_ref[pl.ds(pl.multiple_of(jg * GROUP_BLOCKS, GROUP_BLOCKS), GROUP_BLOCKS), :]
        return jnp.concatenate([q_all, rows], axis=0)

    krow_w = lax.broadcasted_iota(jnp.int32, (TQ, 1), 0)

    def win_chunk(c):
        kpos = q0 - WINDOW + c * TQ + krow_w
        dlt = t - kpos
        bias = jnp.where((dlt >= 0) & (dlt < WINDOW) & (kpos >= 0), 0.0, NEG_INF)
        return kw_ref[0, 0, pl.ds(pl.multiple_of(q0 + c * TQ, TQ), TQ), :], bias

    n_win = WINDOW // TQ + 1
    m_w = _max_init()
    for c in range(n_win):
        kt, bias = win_chunk(c)
        m_w = _row_max(m_w, q_all, kt, bias)
    state = _acc_init()
    for c in range(n_win):
        kt, bias = win_chunk(c)
        state = _accumulate(state, m_w, q_all, kt, vw_ref[0, 0, qi + c], bias)
    o_w = state[1] / state[0]

    krow = lax.broadcasted_iota(jnp.int32, (TK, 1), 0)
    group_rows = TK * KV_GROUP

    def max_group(jg, m):
        k0 = pl.multiple_of(jg * group_rows, group_rows)
        s = _scores(q_ext(jg), ks_ref[0, 0, pl.ds(k0, group_rows), :], None)
        score_ref[pl.ds(k0, group_rows), :] = s
        return jnp.maximum(m, jnp.max(s, axis=0, keepdims=True))

    def max_tile(j, m):
        k0 = pl.multiple_of(j * TK, TK)
        bias = jnp.where(k0 + krow <= t, 0.0, NEG_INF)
        s = _scores(q_ext(j // KV_GROUP), ks_ref[0, 0, pl.ds(k0, TK), :], bias)
        score_ref[pl.ds(k0, TK), :] = s
        return jnp.maximum(m, jnp.max(s, axis=0, keepdims=True))

    def sum_rows(state, k0, rows, vt_t):
        l, acc = state
        p = jnp.exp(score_ref[pl.ds(k0, rows), :] - m_s)
        return l + jnp.sum(p, axis=0, keepdims=True), acc + _dot(vt_t, p.astype(BF16))

    def sum_group(jg, state):
        vt_t = jnp.concatenate([vs_ref[0, 0, jg * KV_GROUP + sub] for sub in range(KV_GROUP)], axis=1)
        return sum_rows(state, pl.multiple_of(jg * group_rows, group_rows), group_rows, vt_t)

    def sum_tile(j, state):
        return sum_rows(state, pl.multiple_of(j * TK, TK), TK, vs_ref[0, 0, j])

    n_full = q0 // group_rows
    n_kv = (q0 + TQ - 1) // TK + 1
    m_s = lax.fori_loop(0, n_full, max_group, _max_init())
    m_s = lax.fori_loop(n_full * KV_GROUP, n_kv, max_tile, m_s)
    state = lax.fori_loop(0, n_full, sum_group, _acc_init())
    state = lax.fori_loop(n_full * KV_GROUP, n_kv, sum_tile, state)
    o_s = state[1] / state[0]

    gn = gn_ref[0, 0]
    gate = [jnp.concatenate([gn[3 * r + c:3 * r + c + 1, :] for r in range(NSA_REP)], axis=1) for c in range(3)]
    o = gate[0] * o_c + gate[1] * o_s + gate[2] * o_w
    for r in range(NSA_REP):
        o_ref[0, r * HEAD_DIM:(r + 1) * HEAD_DIM, :] = o[:, r * TQ:(r + 1) * TQ]


def _nsa(q_t, kc, vc_t, ks, vs_t, kwp, vwp_t, gn_t, agg_t, n_cmp):
    bsz, _, seq = q_t.shape
    g = NSA_KV_GROUPS
    gw = NSA_REP * HEAD_DIM
    per_bg = lambda a: pl.BlockSpec((1, 1) + a.shape[2:], lambda b, gg, i: (b, gg) + (0,) * (a.ndim - 2))
    return pl.pallas_call(
        functools.partial(_nsa_body, n_cmp=n_cmp),
        grid=(bsz, g, seq // TQ),
        in_specs=[pl.BlockSpec((1, gw, TQ), lambda b, gg, i: (b, gg, i)),
                  per_bg(kc), per_bg(vc_t), per_bg(ks), per_bg(vs_t), per_bg(kwp), per_bg(vwp_t),
                  pl.BlockSpec((1, 1, gn_t.shape[2], TQ), lambda b, gg, i: (b, gg, 0, i)),
                  pl.BlockSpec(agg_t.shape, lambda b, gg, i: (0, 0))],
        out_specs=pl.BlockSpec((1, gw, TQ), lambda b, gg, i: (b, gg, i)),
        out_shape=jax.ShapeDtypeStruct((bsz, NSA_WIDTH, seq), F32),
        scratch_shapes=[pltpu.VMEM((agg_t.shape[0], NSA_REP * TQ), BF16),
                        pltpu.VMEM((seq, NSA_REP * TQ), F32)],
        compiler_params=_cparams(3, 56),
    )(q_t, kc, vc_t, ks, vs_t, kwp, vwp_t, gn_t, agg_t)


def _merge_body(on_ref, u_ref, v_ref, gate_ref, x_ref, gt_ref, sc_ref, sh_ref, ws_ref, bs_ref,
                wb_ref, wo_ref, g1_ref, b1_ref, wq_ref, x1_ref, h2_ref, qp_ref, *, alpha):
    tm = x_ref.shape[0]
    row = lax.broadcasted_iota(jnp.int32, (SGU_CHUNK, SGU_CHUNK), 0)
    col = lax.broadcasted_iota(jnp.int32, (SGU_CHUNK, SGU_CHUNK), 1)
    lane_group = lax.broadcasted_iota(jnp.int32, (1, SGU_WIDTH), 1) // SGU_GROUP_DIM
    mixed = []
    for c in range(tm // SGU_CHUNK):
        v = v_ref[c * SGU_CHUNK:(c + 1) * SGU_CHUNK, :]
        acc = jnp.zeros((SGU_CHUNK, SGU_WIDTH), F32)
        for g in range(SGU_GROUPS):
            wg = jnp.where(col <= row, ws_ref[g], 0.0).astype(BF16)
            vg = jnp.where(lane_group == g, v, 0.0).astype(BF16)
            acc = acc + _dot(wg, vg)
        mixed.append(acc + bs_ref[...])
    o_sgu = u_ref[...] * jnp.concatenate(mixed, axis=0)
    gate = gate_ref[...]
    merged = (gate[:, :D_MODEL] * _dot(on_ref[...].astype(BF16), wb_ref[0])
              + gate[:, D_MODEL:] * _dot(o_sgu.astype(BF16), wb_ref[1]))
    mix = _dot(merged.astype(BF16), wo_ref[...])
    x1 = _ln(alpha * x_ref[...] + gt_ref[0] * mix) * g1_ref[...] + b1_ref[...]
    x1_ref[...] = x1
    h2 = _ln(x1) * (1.0 + sc_ref[0]) + sh_ref[0]
    h2_ref[...] = h2
    qp_ref[...] = _dot(h2.astype(BF16), wq_ref[...])


def _merge(o_nsa, u, v, gate, x2, gt1, sc2, sh2, ws, bs, wb, wo, g1, b1, wq, seq, alpha):
    n, d = x2.shape
    tm = TM_MERGE
    nst = seq // tm
    row = lambda w: pl.BlockSpec((tm, w), lambda i: (i, 0))
    per_batch = pl.BlockSpec((1, 1, d), lambda i: (i // nst, 0, 0))
    const = lambda a: pl.BlockSpec(a.shape, lambda i: (0,) * a.ndim)
    nq = wq.shape[1]
    return pl.pallas_call(
        functools.partial(_merge_body, alpha=alpha),
        grid=(n // tm,),
        in_specs=[row(NSA_WIDTH), row(SGU_WIDTH), row(SGU_WIDTH), row(2 * d), row(d),
                  per_batch, per_batch, per_batch, const(ws), const(bs), const(wb), const(wo),
                  const(g1), const(b1), const(wq)],
        out_specs=[row(d), row(d), row(nq)],
        out_shape=[jax.ShapeDtypeStruct((n, d), F32), jax.ShapeDtypeStruct((n, d), F32),
                   jax.ShapeDtypeStruct((n, nq), F32)],
        compiler_params=_cparams(1, 56),
    )(o_nsa, u, v, gate, x2, gt1, sc2, sh2, ws, bs, wb, wo, g1, b1, wq)


def _topk_rows(s, k):
    nrow = s.shape[0]
    rid = lax.broadcasted_iota(jnp.int32, s.shape, 0).astype(F32)
    vals, rows = [], []
    for _ in range(k):
        m = jnp.max(s, axis=0, keepdims=True)
        first = jnp.min(jnp.where(s == m, rid, float(nrow)), axis=0, keepdims=True)
        s = jnp.where(rid == first, -jnp.inf, s)
        vals.append(m)
        rows.append(first)
    return vals, rows


def _candidate_blocks(k):
    todo = {(a, b) for a in range(k) for b in range(k) if (a + 1) * (b + 1) <= k}
    options = [(axis, fixed, start) for axis in (0, 1) for fixed in range(k) for start in range(0, k, 8)]
    cells = lambda axis, fixed, start: [((fixed, start + r) if axis == 0 else (start + r, fixed)) for r in range(8)]
    blocks = []
    while todo:
        best = max(options, key=lambda o: sum(c in todo for c in cells(*o)))
        owned = tuple(c in todo for c in cells(*best))
        todo -= set(cells(*best))
        blocks.append(best + (owned,))
    return tuple(blocks)


_CAND_BLOCKS = _candidate_blocks(PEER_TOPK)


def _retrieve_body(qp_ref, keys_ref, idx_ref, gate_ref):
    k = PEER_TOPK
    for h in range(PEER_HEADS):
        tops = []
        for c in range(2):
            off = (h * 2 + c) * PEER_HALF
            qh = qp_ref[:, off:off + PEER_HALF].astype(BF16)
            s = _dot_nt(keys_ref[h, c].astype(BF16), qh)
            tops.append(_topk_rows(s, k))
        (v1, i1), (v2, i2) = tops
        vm = (jnp.concatenate(v1, axis=0), jnp.concatenate(v2, axis=0))
        im = (jnp.concatenate(i1, axis=0), jnp.concatenate(i2, axis=0))
        sub = lax.broadcasted_iota(jnp.int32, (8, 1), 0)
        vals, eids, poss = [], [], []
        for axis, fixed, start, owned in _CAND_BLOCKS:
            if axis == 0:
                v = v1[fixed] + vm[1][start:start + 8]
                e = i1[fixed] * float(PEER_NKEYS) + im[1][start:start + 8]
                pos = fixed * k + start + sub
            else:
                v = vm[0][start:start + 8] + v2[fixed]
                e = im[0][start:start + 8] * float(PEER_NKEYS) + i2[fixed]
                pos = (start + sub) * k + fixed
            if not all(owned):
                keep = functools.reduce(jnp.logical_or, [sub == r for r in range(8) if owned[r]])
                v = jnp.where(keep, v, -jnp.inf)
                pos = jnp.where(keep, pos, k * k)
            vals.append(v)
            eids.append(e)
            poss.append(pos.astype(F32))
        cand = jnp.concatenate(vals, axis=0)
        cidx = jnp.concatenate(eids, axis=0)
        rid = jnp.concatenate(poss, axis=0)
        top_s, top_e = [], []
        for _ in range(k):
            m = jnp.max(cand, axis=0, keepdims=True)
            first = jnp.min(jnp.where(cand == m, rid, float(k * k)), axis=0, keepdims=True)
            hit = rid == first
            top_e.append(jnp.sum(jnp.where(hit, cidx, 0.0), axis=0, keepdims=True))
            cand = jnp.where(hit, -jnp.inf, cand)
            top_s.append(m)
        ts = jnp.concatenate(top_s, axis=0)
        e = jnp.exp(ts - ts[0:1])
        gate_ref[0, h * k:(h + 1) * k, :] = e / jnp.sum(e, axis=0, keepdims=True)
        idx_ref[0, h * k:(h + 1) * k, :] = jnp.concatenate(top_e, axis=0).astype(jnp.int32)


def _retrieve(qp, keys):
    n = qp.shape[0]
    nt = n // TC
    out = pl.BlockSpec((1, PEER_PAIRS, TC), lambda i: (i, 0, 0))
    return pl.pallas_call(
        _retrieve_body,
        grid=(nt,),
        in_specs=[pl.BlockSpec((TC, qp.shape[1]), lambda i: (i, 0)),
                  pl.BlockSpec(keys.shape, lambda i: (0, 0, 0, 0))],
        out_specs=[out, out],
        out_shape=[jax.ShapeDtypeStruct((nt, PEER_PAIRS, TC), jnp.int32),
                   jax.ShapeDtypeStruct((nt, PEER_PAIRS, TC), F32)],
        compiler_params=_cparams(1, 48),
    )(qp, keys)


ROW_SUB = D_MODEL // 2 // LANES
ACT_UNROLL = 8
OFF_REGS = 8


def _load_row(tab_ref, scaled_idx):
    row = tab_ref[pl.ds(pl.multiple_of(scaled_idx, ROW_SUB), ROW_SUB), :]
    lo = lax.bitcast_convert_type(lax.shift_left(row, jnp.uint32(16)), F32)
    hi = lax.bitcast_convert_type(row & jnp.uint32(0xFFFF0000), F32)
    return lo, hi


def _expert_act_body(off_ref, idx_ref, h_ref, gate_ref, tab_ref, w_ref, slot_ref, part_ref):
    off = [off_ref[j] for j in range(OFF_REGS)]

    def token(t, carry):
        x = h_ref[t]
        xlo, xhi = x[:ROW_SUB], x[ROW_SUB:]
        base = pl.multiple_of(t * PEER_PAIRS, PEER_PAIRS)
        for p in range(PEER_PAIRS):
            if p % OFF_REGS == 0:
                idx_sub = idx_ref.at[pl.ds(base + p, OFF_REGS)]
            lo, hi = _load_row(tab_ref, idx_sub[off[p % OFF_REGS]])
            slot_ref[p * ROW_SUB:(p + 1) * ROW_SUB, :] = lo * xlo + hi * xhi
        part = slot_ref[pl.ds(0, PEER_PAIRS, stride=ROW_SUB), :]
        for s in range(1, ROW_SUB):
            part = part + slot_ref[pl.ds(s, PEER_PAIRS, stride=ROW_SUB), :]
        part_ref[pl.ds(pl.multiple_of(base, PEER_PAIRS), PEER_PAIRS), :] = part
        return carry

    lax.fori_loop(0, TC, token, 0)

    tok_lane = lax.broadcasted_iota(jnp.int32, (PEER_PAIRS, TC), 1)

    def lane_sums(i, act):
        for k in range(ACT_UNROLL):
            t = i * ACT_UNROLL + k
            part = part_ref[pl.ds(pl.multiple_of(t * PEER_PAIRS, PEER_PAIRS), PEER_PAIRS), :]
            act = jnp.where(tok_lane == t, jnp.sum(part, axis=1, keepdims=True), act)
        return act

    act = lax.fori_loop(0, TC // ACT_UNROLL, lane_sums, jnp.zeros((PEER_PAIRS, TC), F32))
    w_ref[0] = gate_ref[0] * _gelu(act)


def _expert_act(idx_flat, h3, gate_t, tab):
    nt = gate_t.shape[0]
    tile = pl.BlockSpec((1, PEER_PAIRS, TC), lambda i: (i, 0, 0))
    return pl.pallas_call(
        _expert_act_body,
        grid=(nt,),
        in_specs=[pl.BlockSpec(memory_space=pltpu.SMEM),
                  pl.BlockSpec((TC * PEER_PAIRS,), lambda i: (i,), memory_space=pltpu.SMEM),
                  pl.BlockSpec((TC,) + h3.shape[1:], lambda i: (i, 0, 0)),
                  tile,
                  pl.BlockSpec(tab.shape, lambda i: (0, 0), pipeline_mode=pl.Buffered(1))],
        out_specs=tile,
        out_shape=jax.ShapeDtypeStruct(gate_t.shape, F32),
        scratch_shapes=[pltpu.VMEM((PEER_PAIRS * ROW_SUB, LANES), F32),
                        pltpu.VMEM((TC * PEER_PAIRS, LANES), F32)],
        compiler_params=_cparams(1, 52),
    )(jnp.arange(OFF_REGS, dtype=jnp.int32), idx_flat, h3, gate_t, tab)


def _expert_out_body(off_ref, idx_ref, w_ref, tab_ref, o_ref):
    n_acc = 4
    off = [off_ref[j] for j in range(OFF_REGS)]

    def token(t, carry):
        base = pl.multiple_of(t * PEER_PAIRS, PEER_PAIRS)
        acc_lo = [jnp.zeros((ROW_SUB, LANES), F32) for _ in range(n_acc)]
        acc_hi = [jnp.zeros((ROW_SUB, LANES), F32) for _ in range(n_acc)]
        for p in range(PEER_PAIRS):
            if p % OFF_REGS == 0:
                idx_sub = idx_ref.at[pl.ds(base + p, OFF_REGS)]
                w_sub = w_ref.at[pl.ds(base + p, OFF_REGS)]
            lo, hi = _load_row(tab_ref, idx_sub[off[p % OFF_REGS]])
            w = w_sub[off[p % OFF_REGS]]
            acc_lo[p % n_acc] = acc_lo[p % n_acc] + w * lo
            acc_hi[p % n_acc] = acc_hi[p % n_acc] + w * hi
        lo = (acc_lo[0] + acc_lo[1]) + (acc_lo[2] + acc_lo[3])
        hi = (acc_hi[0] + acc_hi[1]) + (acc_hi[2] + acc_hi[3])
        o_ref[t] = jnp.concatenate([lo, hi], axis=0)
        return carry

    lax.fori_loop(0, TB, token, 0)


def _expert_out(idx_flat, w_flat, tab, n):
    smem = pl.BlockSpec((TB * PEER_PAIRS,), lambda i: (i,), memory_space=pltpu.SMEM)
    rows = D_MODEL // LANES
    return pl.pallas_call(
        _expert_out_body,
        grid=(n // TB,),
        in_specs=[pl.BlockSpec(memory_space=pltpu.SMEM), smem, smem,
                  pl.BlockSpec(tab.shape, lambda i: (0, 0), pipeline_mode=pl.Buffered(1))],
        out_specs=pl.BlockSpec((TB, rows, LANES), lambda i: (i, 0, 0)),
        out_shape=jax.ShapeDtypeStruct((n, rows, LANES), F32),
        compiler_params=_cparams(1, 48),
    )(jnp.arange(OFF_REGS, dtype=jnp.int32), idx_flat, w_flat, tab)


def _final_body(x1_ref, f_ref, gt_ref, g_ref, b_ref, o_ref, *, alpha):
    o_ref[...] = _ln(alpha * x1_ref[...] + gt_ref[0] * f_ref[...]) * g_ref[...] + b_ref[...]


def _final(x1, ffn, gt2, g2, b2, seq, alpha):
    n, d = x1.shape
    tm = TM_LN
    nst = seq // tm
    row = pl.BlockSpec((tm, d), lambda i: (i, 0))
    const = pl.BlockSpec((1, d), lambda i: (0, 0))
    return pl.pallas_call(
        functools.partial(_final_body, alpha=alpha),
        grid=(n // tm,),
        in_specs=[row, row, pl.BlockSpec((1, 1, d), lambda i: (i // nst, 0, 0)), const, const],
        out_specs=row,
        out_shape=jax.ShapeDtypeStruct((n, d), F32),
        compiler_params=_cparams(1, 32),
    )(x1, ffn, gt2, g2, b2)


def _swap_halves(w, heads):
    d = w.shape[0]
    w4 = w.reshape(d, heads, 2, HEAD_DIM // 2)
    return jnp.flip(w4, axis=2).reshape(d, heads * HEAD_DIM)


def _fused_in_weight(w_in, w_merge):
    d = w_in.shape[0]
    o = 0
    parts = {}
    for name, width in (("q", NSA_WIDTH), ("kc", KV_WIDTH), ("vc", KV_WIDTH), ("ks", KV_WIDTH),
                        ("vs", KV_WIDTH), ("kw", KV_WIDTH), ("vw", KV_WIDTH), ("g", 3 * NSA_HEADS),
                        ("z", 2 * SGU_WIDTH)):
        parts[name] = w_in[:, o:o + width]
        o += width
    wq = parts["q"] * (HEAD_DIM ** -0.5)
    gcols = 3 * NSA_REP
    wg = jnp.zeros((d, 2 * LANES), w_in.dtype)
    wg = wg.at[:, :gcols].set(parts["g"][:, :gcols]).at[:, LANES:LANES + gcols].set(parts["g"][:, gcols:])
    g = NSA_KV_GROUPS
    cols = [wq, _swap_halves(wq, NSA_HEADS),
            parts["kc"], _swap_halves(parts["kc"], g), parts["ks"], _swap_halves(parts["ks"], g),
            parts["kw"], _swap_halves(parts["kw"], g), parts["vc"], parts["vs"], parts["vw"],
            wg, parts["z"], w_merge]
    return jnp.concatenate(cols, axis=1).astype(BF16)


def _rope_tables(seq):
    half = HEAD_DIM // 2
    pos = jnp.arange(seq, dtype=F32)
    inv_freq = ROPE_THETA ** (-jnp.arange(half, dtype=F32) / half)
    ang = pos[:, None] * inv_freq[None, :]
    cos, sin = jnp.cos(ang), jnp.sin(ang)
    reps = LANES // HEAD_DIM
    cosk = jnp.tile(jnp.concatenate([cos, cos], axis=1), (1, reps))
    sink = jnp.tile(jnp.concatenate([-sin, sin], axis=1), (1, reps))
    return cosk, sink


def _sel_aggregation(n_cmp_pad, n_sel):
    c0 = jnp.arange(n_cmp_pad)[:, None] * CMP_STRIDE
    s0 = jnp.arange(n_sel)[None, :] * SEL_BLOCK
    ov = jnp.clip(jnp.minimum(c0 + CMP_BLOCK, s0 + SEL_BLOCK) - jnp.maximum(c0, s0), 0, None)
    return (ov / CMP_BLOCK).astype(BF16)


def _pack_table(tab):
    half = tab.shape[1] // 2
    bits = lax.bitcast_convert_type(tab.astype(BF16), jnp.uint16).astype(jnp.uint32)
    packed = (bits[:, half:] << 16) | bits[:, :half]
    return packed.reshape(tab.shape[0] * ROW_SUB, LANES)


def _split_groups(a, bsz, seq):
    return a.reshape(bsz, seq, NSA_KV_GROUPS, HEAD_DIM).transpose(0, 2, 1, 3)


def kernel(x, c, w_ada, b_ada, w_in, cmp_pos, cmp_w1, cmp_b1, cmp_w2, cmp_b2, sgu_ln_g, sgu_ln_b, sgu_w, sgu_b, w_branch, w_merge, b_merge, w_out, ln1_g, ln1_b, peer_wq, peer_keys, peer_u, peer_v, ln2_g, ln2_b):
    bsz, seq, d = x.shape
    n = bsz * seq
    depth = w_ada.shape[0]
    alpha = (2.0 * depth) ** 0.25
    n_cmp = (seq - CMP_BLOCK) // CMP_STRIDE + 1
    n_half = seq // CMP_STRIDE
    n_sel = seq // SEL_BLOCK
    cosk, sink = _rope_tables(seq)
    agg = _sel_aggregation(n_half, n_sel)
    x2 = x.reshape(n, d)
    for l in range(depth):
        mod = _ada(c, w_ada[l], b_ada[l])
        sh1, sc1, gt1, sh2, sc2, gt2 = [m.reshape(bsz, 1, d) for m in jnp.split(mod, 6, axis=-1)]

        wall = _fused_in_weight(w_in[l], w_merge[l])
        (q, kc_r, vc_r, ks, vs, kw, vw, gn, u, v, gate) = _proj(
            x2, sc1, sh1, wall, b_merge[l].reshape(1, -1), cosk, sink,
            sgu_ln_g[l].reshape(1, -1), sgu_ln_b[l].reshape(1, -1), seq)

        t16 = jnp.stack([_split_groups(kc_r, bsz, seq), _split_groups(vc_r, bsz, seq)])
        t16 = t16.reshape(2, bsz, NSA_KV_GROUPS, n_half, CMP_STRIDE * HEAD_DIM)
        cmp_kv = _compress(t16, cmp_pos[l].reshape(2, 1, CMP_BLOCK * HEAD_DIM), cmp_w1[l],
                           cmp_b1[l].reshape(2, 1, CMP_HIDDEN), cmp_w2[l], cmp_b2[l].reshape(2, 1, HEAD_DIM))

        front = ((0, 0), (0, 0), (WINDOW, 0), (0, 0))
        key_tiles = lambda a, tk: a.reshape(a.shape[:2] + (a.shape[2] // tk, tk, HEAD_DIM)).transpose(0, 1, 2, 4, 3)
        gn_t = gn.reshape(bsz, seq, NSA_KV_GROUPS, LANES)[..., :16].transpose(0, 2, 3, 1)
        ks4 = _split_groups(ks, bsz, seq)
        block_onehot = ((jnp.arange(seq)[:, None] // SEL_BLOCK) % GROUP_BLOCKS
                        == jnp.arange(GROUP_BLOCKS)[None, :]).astype(BF16)
        ks_ext = jnp.concatenate([ks4, jnp.broadcast_to(block_onehot, ks4.shape[:3] + (GROUP_BLOCKS,))], axis=-1)
        o_nsa_t = _nsa(q.reshape(bsz, seq, NSA_WIDTH).transpose(0, 2, 1), cmp_kv[0], cmp_kv[1].transpose(0, 1, 3, 2),
                       ks_ext, key_tiles(_split_groups(vs, bsz, seq), TK),
                       jnp.pad(_split_groups(kw, bsz, seq), front),
                       key_tiles(jnp.pad(_split_groups(vw, bsz, seq), front), TQ),
                       gn_t, agg.T, n_cmp)
        o_nsa = o_nsa_t.transpose(0, 2, 1)

        bs = jnp.repeat(sgu_b[l].T, SGU_GROUP_DIM, axis=1)
        x1, h2, qp = _merge(o_nsa.reshape(n, NSA_WIDTH), u, v, gate, x2, gt1, sc2, sh2,
                            sgu_w[l], bs, w_branch[l].astype(BF16), w_out[l].astype(BF16),
                            ln1_g[l].reshape(1, d), ln1_b[l].reshape(1, d), peer_wq[l].astype(BF16), seq, alpha)

        idx_t, gate_t = _retrieve(qp, peer_keys[l])
        idx_flat = idx_t.transpose(0, 2, 1).reshape(n * PEER_PAIRS) * ROW_SUB
        w_t = _expert_act(idx_flat, h2.reshape(n, d // LANES, LANES), gate_t, _pack_table(peer_u[l]))
        w_flat = w_t.transpose(0, 2, 1).reshape(n * PEER_PAIRS)
        ffn = _expert_out(idx_flat, w_flat, _pack_table(peer_v[l]), n)
        x2 = _final(x1, ffn.reshape(n, d), gt2, ln2_g[l].reshape(1, d), ln2_b[l].reshape(1, d), seq, alpha)
    return x2.reshape(bsz, seq, d)
```

```python
import functools

import jax
import jax.numpy as jnp
from jax import lax
from jax.experimental import pallas as pl
from jax.experimental.pallas import tpu as pltpu

D_MODEL = 1024
NSA_HEADS = 8
NSA_KV_GROUPS = 2
NSA_REP = NSA_HEADS // NSA_KV_GROUPS
HEAD_DIM = 64
NSA_WIDTH = NSA_HEADS * HEAD_DIM
KV_WIDTH = NSA_KV_GROUPS * HEAD_DIM
CMP_BLOCK = 32
CMP_STRIDE = 16
CMP_HIDDEN = 2 * HEAD_DIM
SEL_BLOCK = 64
SEL_TOPK = 16
WINDOW = 512
ROPE_THETA = 10000.0
SGU_GROUPS = 8
SGU_WIDTH = D_MODEL // 2
SGU_GROUP_DIM = SGU_WIDTH // SGU_GROUPS
SGU_CHUNK = 128
PEER_HEADS = 8
PEER_NKEYS = 128
PEER_EXPERTS = PEER_NKEYS * PEER_NKEYS
PEER_QDIM = 256
PEER_HALF = PEER_QDIM // 2
PEER_TOPK = 16
PEER_PAIRS = PEER_HEADS * PEER_TOPK
LN_EPS = 1e-5
NEG_INF = -1e30
FORCE_SCORE = 1e9

LANES = 128
MIB = 1024 * 1024
BF16 = jnp.bfloat16
F32 = jnp.float32

TM_PROJ = 256
TM_MERGE = 256
TM_LN = 512
TQ = 128
TK = 256
KV_GROUP = 4
GROUP_BLOCKS = TK * KV_GROUP // SEL_BLOCK
TC = 128
TB = 64

_OQ, _OQS = 0, 512
_OKC, _OKCS, _OKS, _OKSS, _OKW, _OKWS = 1024, 1152, 1280, 1408, 1536, 1664
_OVC, _OVS, _OVW = 1792, 1920, 2048
_OG = 2176
_OZ = 2432
_OM = 3456
_WCOLS = 5504


def _cparams(n_axes, vmem_mib):
    return pltpu.CompilerParams(
        dimension_semantics=("parallel",) * n_axes,
        vmem_limit_bytes=vmem_mib * MIB)


def _ln(x):
    mu = jnp.mean(x, axis=-1, keepdims=True)
    xc = x - mu
    var = jnp.mean(xc * xc, axis=-1, keepdims=True)
    return xc * lax.rsqrt(var + LN_EPS)


def _gelu(x):
    return 0.5 * x * (1.0 + lax.erf(x * (2.0 ** -0.5)))


def _dot(a, b):
    return jnp.dot(a, b, preferred_element_type=F32)


def _dot_nt(a, b):
    return lax.dot_general(a, b, (((1,), (1,)), ((), ())), preferred_element_type=F32)


def _ada_body(c_ref, w_ref, b_ref, o_ref):
    cv = c_ref[...]
    a = cv * jax.nn.sigmoid(cv)
    o_ref[...] = _dot(a.astype(BF16), w_ref[...].astype(BF16)) + b_ref[...]


def _ada(c, w, b):
    bsz, d = c.shape
    n = w.shape[1]
    tn = 1024
    return pl.pallas_call(
        _ada_body,
        grid=(n // tn,),
        in_specs=[pl.BlockSpec((bsz, d), lambda j: (0, 0)),
                  pl.BlockSpec((d, tn), lambda j: (0, j)),
                  pl.BlockSpec((1, tn), lambda j: (0, j))],
        out_specs=pl.BlockSpec((bsz, tn), lambda j: (0, j)),
        out_shape=jax.ShapeDtypeStruct((bsz, n), F32),
        compiler_params=_cparams(1, 32),
    )(c, w, b.reshape(1, n))


def _proj_body(x_ref, sc_ref, sh_ref, w_ref, bm_ref, cos_ref, sin_ref, sg_ref, sb_ref,
               q_ref, kc_ref, vc_ref, ks_ref, vs_ref, kw_ref, vw_ref, gn_ref, u_ref, v_ref, gate_ref):
    h = _ln(x_ref[...]) * (1.0 + sc_ref[0]) + sh_ref[0]
    hb = h.astype(BF16)

    def proj(off, width):
        return _dot(hb, w_ref[:, off:off + width])

    cosk = cos_ref[...]
    sink = sin_ref[...]
    cosq = jnp.concatenate([cosk] * (NSA_WIDTH // LANES), axis=1)
    sinq = jnp.concatenate([sink] * (NSA_WIDTH // LANES), axis=1)
    q_ref[0] = (proj(_OQ, NSA_WIDTH) * cosq + proj(_OQS, NSA_WIDTH) * sinq).T
    kc_ref[...] = proj(_OKC, KV_WIDTH) * cosk + proj(_OKCS, KV_WIDTH) * sink
    ks_ref[...] = (proj(_OKS, KV_WIDTH) * cosk + proj(_OKSS, KV_WIDTH) * sink).astype(BF16)
    kw_ref[...] = (proj(_OKW, KV_WIDTH) * cosk + proj(_OKWS, KV_WIDTH) * sink).astype(BF16)
    vc_ref[...] = proj(_OVC, KV_WIDTH)
    vs_ref[...] = proj(_OVS, KV_WIDTH).astype(BF16)
    vw_ref[...] = proj(_OVW, KV_WIDTH).astype(BF16)
    gn_ref[...] = jax.nn.sigmoid(proj(_OG, 2 * LANES))
    u_ref[...] = _gelu(proj(_OZ, SGU_WIDTH))
    v_ref[...] = _ln(_gelu(proj(_OZ + SGU_WIDTH, SGU_WIDTH))) * sg_ref[...] + sb_ref[...]
    gate_ref[...] = jax.nn.sigmoid(proj(_OM, 2 * D_MODEL) + bm_ref[...])


def _proj(x2, sc, sh, wall, bm, cosk, sink, sg, sb, seq):
    n, d = x2.shape
    tm = TM_PROJ
    nst = seq // tm
    row = lambda w: pl.BlockSpec((tm, w), lambda i: (i, 0))
    per_batch = pl.BlockSpec((1, 1, d), lambda i: (i // nst, 0, 0))
    const = lambda shp: pl.BlockSpec(shp, lambda i: (0,) * len(shp))
    pos = pl.BlockSpec((tm, LANES), lambda i: (i % nst, 0))
    outs = [(KV_WIDTH, F32), (KV_WIDTH, F32), (KV_WIDTH, BF16), (KV_WIDTH, BF16),
            (KV_WIDTH, BF16), (KV_WIDTH, BF16), (2 * LANES, F32), (SGU_WIDTH, F32), (SGU_WIDTH, F32),
            (2 * D_MODEL, F32)]
    q_spec = pl.BlockSpec((1, NSA_WIDTH, tm), lambda i: (i // nst, 0, i % nst))
    q_shape = jax.ShapeDtypeStruct((n // seq, NSA_WIDTH, seq), F32)
    return pl.pallas_call(
        _proj_body,
        grid=(n // tm,),
        in_specs=[row(d), per_batch, per_batch, const((d, _WCOLS)), const((1, 2 * D_MODEL)),
                  pos, pos, const((1, SGU_WIDTH)), const((1, SGU_WIDTH))],
        out_specs=[q_spec] + [row(w) for w, _ in outs],
        out_shape=[q_shape] + [jax.ShapeDtypeStruct((n, w), dt) for w, dt in outs],
        compiler_params=_cparams(1, 56),
    )(x2, sc, sh, wall, bm, cosk, sink, sg, sb)


def _cmp_body(t_ref, pos_ref, w1_ref, b1_ref, w2_ref, b2_ref, o_ref):
    half = (CMP_BLOCK // 2) * HEAD_DIM
    t = t_ref[0, 0, 0]
    pos = pos_ref[0]
    ta = (t + pos[:, :half]).astype(BF16)
    tb = (t + pos[:, half:]).astype(BF16)
    a = _dot(ta, w1_ref[0, :half, :].astype(BF16))
    b = _dot(tb, w1_ref[0, half:, :].astype(BF16))
    nrow = t.shape[0]
    b_next = pltpu.roll(b, nrow - 1, axis=0)
    hid = _gelu(a + b_next + b1_ref[0])
    out = _dot(hid.astype(BF16), w2_ref[0].astype(BF16)) + b2_ref[0]
    o_ref[0, 0, 0] = out.astype(o_ref.dtype)


def _compress(t16, pos, w1, b1, w2, b2):
    _, bsz, g, nr, dd = t16.shape
    per_kind = lambda shp: pl.BlockSpec((1,) + shp, lambda k, b, gg: (k,) + (0,) * len(shp))
    blk = lambda w: pl.BlockSpec((1, 1, 1, nr, w), lambda k, b, gg: (k, b, gg, 0, 0))
    return pl.pallas_call(
        _cmp_body,
        grid=(2, bsz, g),
        in_specs=[blk(dd), per_kind((1, CMP_BLOCK * HEAD_DIM)), per_kind((CMP_BLOCK * HEAD_DIM, CMP_HIDDEN)),
                  per_kind((1, CMP_HIDDEN)), per_kind((CMP_HIDDEN, HEAD_DIM)), per_kind((1, HEAD_DIM))],
        out_specs=blk(HEAD_DIM),
        out_shape=jax.ShapeDtypeStruct((2, bsz, g, nr, HEAD_DIM), BF16),
        compiler_params=_cparams(3, 48),
    )(t16, pos, w1, b1, w2, b2)


def _scores(q_all, kt, bias):
    s = _dot(kt, q_all)
    return s if bias is None else s + bias


def _row_max(m, q_all, kt, bias=None):
    return jnp.maximum(m, jnp.max(_scores(q_all, kt, bias), axis=0, keepdims=True))


def _accumulate(state, m, q_all, kt, vt_t, bias=None):
    l, acc = state
    p = jnp.exp(_scores(q_all, kt, bias) - m)
    return l + jnp.sum(p, axis=0, keepdims=True), acc + _dot(vt_t, p.astype(BF16))


def _max_init():
    return jnp.full((1, NSA_REP * TQ), NEG_INF, F32)


def _acc_init():
    w = NSA_REP * TQ
    return jnp.zeros((1, w), F32), jnp.zeros((HEAD_DIM, w), F32)


def _lane_tile(a):
    return jnp.concatenate([a] * NSA_REP, axis=1)


def _nsa_body(q_ref, kc_ref, vc_ref, ks_ref, vs_ref, kw_ref, vw_ref, gn_ref, agg_ref, o_ref, selneg_ref, score_ref, *, n_cmp):
    qi = pl.program_id(2)
    q0 = qi * TQ
    t1 = q0 + lax.broadcasted_iota(jnp.int32, (1, TQ), 1)
    t = _lane_tile(t1)
    q_all = jnp.concatenate([q_ref[0, r * HEAD_DIM:(r + 1) * HEAD_DIM, :] for r in range(NSA_REP)],
                            axis=1).astype(BF16)

    kcm = kc_ref[0, 0]
    ncp = kcm.shape[0]
    nidx = lax.broadcasted_iota(jnp.int32, (ncp, 1), 0)
    mask_c = (nidx * CMP_STRIDE + (CMP_BLOCK - 1) <= t) & (nidx < n_cmp)
    sm = jnp.where(mask_c, _dot(kcm, q_all), NEG_INF)
    e = jnp.where(mask_c, jnp.exp(sm - jnp.max(sm, axis=0, keepdims=True)), 0.0)
    l = jnp.sum(e, axis=0, keepdims=True)
    p = e * (1.0 / jnp.where(l > 0.0, l, 1.0))
    o_c = _dot(vc_ref[0, 0], p.astype(BF16))
    psum = p[:, :TQ]
    for r in range(1, NSA_REP):
        psum = psum + p[:, r * TQ:(r + 1) * TQ]

    p_hi = psum.astype(BF16)
    p_lo = (psum - p_hi.astype(F32)).astype(BF16)
    agg_t = agg_ref[...]
    imp = _dot(agg_t, p_hi) + _dot(agg_t, p_lo)
    n_sel = imp.shape[0]
    blk = lax.broadcasted_iota(jnp.int32, (n_sel, 1), 0)
    blk_f = blk.astype(F32)
    cur = t1 // SEL_BLOCK
    forced = (blk == 0) | (blk == cur) | (blk == cur - 1)
    score = jnp.where(forced, FORCE_SCORE, jnp.where(blk <= cur, imp, NEG_INF))
    sel = jnp.zeros((n_sel, TQ), jnp.bool_)
    for _ in range(min(SEL_TOPK, n_sel)):
        m = jnp.max(score, axis=0, keepdims=True)
        first = jnp.min(jnp.where(score == m, blk_f, float(n_sel)), axis=0, keepdims=True)
        hit = blk_f == first
        sel = sel | hit
        score = jnp.where(hit, -jnp.inf, score)
    selneg_ref[...] = _lane_tile(jnp.where(sel, 0.0, NEG_INF).astype(BF16))

    def q_ext(jg):
        rows = selneg_ref[pl.ds(pl.multiple_of(jg * GROUP_BLOCKS, GROUP_BLOCKS), GROUP_BLOCKS), :]
        return jnp.concatenate([q_all, rows], axis=0)

    krow_w = lax.broadcasted_iota(jnp.int32, (TQ, 1), 0)

    def win_chunk(c):
        kpos = q0 - WINDOW + c * TQ + krow_w
        dlt = t - kpos
        bias = jnp.where((dlt >= 0) & (dlt < WINDOW) & (kpos >= 0), 0.0, NEG_INF)
        return kw_ref[0, 0, pl.ds(pl.multiple_of(q0 + c * TQ, TQ), TQ), :], bias

    n_win = WINDOW // TQ + 1
    m_w = _max_init()
    for c in range(n_win):
        kt, bias = win_chunk(c)
        m_w = _row_max(m_w, q_all, kt, bias)
    state = _acc_init()
    for c in range(n_win):
        kt, bias = win_chunk(c)
        state = _accumulate(state, m_w, q_all, kt, vw_ref[0, 0, qi + c], bias)
    o_w = state[1] / state[0]

    krow = lax.broadcasted_iota(jnp.int32, (TK, 1), 0)
    group_rows = TK * KV_GROUP

    def max_group(jg, m):
        k0 = pl.multiple_of(jg * group_rows, group_rows)
        s = _scores(q_ext(jg), ks_ref[0, 0, pl.ds(k0, group_rows), :], None)
        score_ref[pl.ds(k0, group_rows), :] = s
        return jnp.maximum(m, jnp.max(s, axis=0, keepdims=True))

    def max_tile(j, m):
        k0 = pl.multiple_of(j * TK, TK)
        bias = jnp.where(k0 + krow <= t, 0.0, NEG_INF)
        s = _scores(q_ext(j // KV_GROUP), ks_ref[0, 0, pl.ds(k0, TK), :], bias)
        score_ref[pl.ds(k0, TK), :] = s
        return jnp.maximum(m, jnp.max(s, axis=0, keepdims=True))

    def sum_rows(state, k0, rows, vt_t):
        l, acc = state
        p = jnp.exp(score_ref[pl.ds(k0, rows), :] - m_s)
        return l + jnp.sum(p, axis=0, keepdims=True), acc + _dot(vt_t, p.astype(BF16))

    def sum_group(jg, state):
        vt_t = jnp.concatenate([vs_ref[0, 0, jg * KV_GROUP + sub] for sub in range(KV_GROUP)], axis=1)
        return sum_rows(state, pl.multiple_of(jg * group_rows, group_rows), group_rows, vt_t)

    def sum_tile(j, state):
        return sum_rows(state, pl.multiple_of(j * TK, TK), TK, vs_ref[0, 0, j])

    n_full = q0 // group_rows
    n_kv = (q0 + TQ - 1) // TK + 1
    m_s = lax.fori_loop(0, n_full, max_group, _max_init())
    m_s = lax.fori_loop(n_full * KV_GROUP, n_kv, max_tile, m_s)
    state = lax.fori_loop(0, n_full, sum_group, _acc_init())
    state = lax.fori_loop(n_full * KV_GROUP, n_kv, sum_tile, state)
    o_s = state[1] / state[0]

    gn = gn_ref[0, 0]
    gate = [jnp.concatenate([gn[3 * r + c:3 * r + c + 1, :] for r in range(NSA_REP)], axis=1) for c in range(3)]
    o = gate[0] * o_c + gate[1] * o_s + gate[2] * o_w
    for r in range(NSA_REP):
        o_ref[0, r * HEAD_DIM:(r + 1) * HEAD_DIM, :] = o[:, r * TQ:(r + 1) * TQ]


def _nsa(q_t, kc, vc_t, ks, vs_t, kwp, vwp_t, gn_t, agg_t, n_cmp):
    bsz, _, seq = q_t.shape
    g = NSA_KV_GROUPS
    gw = NSA_REP * HEAD_DIM
    per_bg = lambda a: pl.BlockSpec((1, 1) + a.shape[2:], lambda b, gg, i: (b, gg) + (0,) * (a.ndim - 2))
    return pl.pallas_call(
        functools.partial(_nsa_body, n_cmp=n_cmp),
        grid=(bsz, g, seq // TQ),
        in_specs=[pl.BlockSpec((1, gw, TQ), lambda b, gg, i: (b, gg, i)),
                  per_bg(kc), per_bg(vc_t), per_bg(ks), per_bg(vs_t), per_bg(kwp), per_bg(vwp_t),
                  pl.BlockSpec((1, 1, gn_t.shape[2], TQ), lambda b, gg, i: (b, gg, 0, i)),
                  pl.BlockSpec(agg_t.shape, lambda b, gg, i: (0, 0))],
        out_specs=pl.BlockSpec((1, gw, TQ), lambda b, gg, i: (b, gg, i)),
        out_shape=jax.ShapeDtypeStruct((bsz, NSA_WIDTH, seq), F32),
        scratch_shapes=[pltpu.VMEM((agg_t.shape[0], NSA_REP * TQ), BF16),
                        pltpu.VMEM((seq, NSA_REP * TQ), F32)],
        compiler_params=_cparams(3, 56),
    )(q_t, kc, vc_t, ks, vs_t, kwp, vwp_t, gn_t, agg_t)


def _merge_body(on_ref, u_ref, v_ref, gate_ref, x_ref, gt_ref, sc_ref, sh_ref, ws_ref, bs_ref,
                wb_ref, wo_ref, g1_ref, b1_ref, wq_ref, x1_ref, h2_ref, qp_ref, *, alpha):
    tm = x_ref.shape[0]
    row = lax.broadcasted_iota(jnp.int32, (SGU_CHUNK, SGU_CHUNK), 0)
    col = lax.broadcasted_iota(jnp.int32, (SGU_CHUNK, SGU_CHUNK), 1)
    lane_group = lax.broadcasted_iota(jnp.int32, (1, SGU_WIDTH), 1) // SGU_GROUP_DIM
    mixed = []
    for c in range(tm // SGU_CHUNK):
        v = v_ref[c * SGU_CHUNK:(c + 1) * SGU_CHUNK, :]
        acc = jnp.zeros((SGU_CHUNK, SGU_WIDTH), F32)
        for g in range(SGU_GROUPS):
            wg = jnp.where(col <= row, ws_ref[g], 0.0).astype(BF16)
            vg = jnp.where(lane_group == g, v, 0.0).astype(BF16)
            acc = acc + _dot(wg, vg)
        mixed.append(acc + bs_ref[...])
    o_sgu = u_ref[...] * jnp.concatenate(mixed, axis=0)
    gate = gate_ref[...]
    merged = (gate[:, :D_MODEL] * _dot(on_ref[0].T.astype(BF16), wb_ref[0])
              + gate[:, D_MODEL:] * _dot(o_sgu.astype(BF16), wb_ref[1]))
    mix = _dot(merged.astype(BF16), wo_ref[...])
    x1 = _ln(alpha * x_ref[...] + gt_ref[0] * mix) * g1_ref[...] + b1_ref[...]
    x1_ref[...] = x1
    h2 = _ln(x1) * (1.0 + sc_ref[0]) + sh_ref[0]
    h2_ref[...] = h2
    qp_ref[...] = _dot(h2.astype(BF16), wq_ref[...])


def _merge(o_nsa, u, v, gate, x2, gt1, sc2, sh2, ws, bs, wb, wo, g1, b1, wq, seq, alpha):
    n, d = x2.shape
    tm = TM_MERGE
    nst = seq // tm
    row = lambda w: pl.BlockSpec((tm, w), lambda i: (i, 0))
    per_batch = pl.BlockSpec((1, 1, d), lambda i: (i // nst, 0, 0))
    const = lambda a: pl.BlockSpec(a.shape, lambda i: (0,) * a.ndim)
    nq = wq.shape[1]
    return pl.pallas_call(
        functools.partial(_merge_body, alpha=alpha),
        grid=(n // tm,),
        in_specs=[pl.BlockSpec((1, NSA_WIDTH, tm), lambda i: (i // nst, 0, i % nst)),
                  row(SGU_WIDTH), row(SGU_WIDTH), row(2 * d), row(d),
                  per_batch, per_batch, per_batch, const(ws), const(bs), const(wb), const(wo),
                  const(g1), const(b1), const(wq)],
        out_specs=[row(d), row(d), row(nq)],
        out_shape=[jax.ShapeDtypeStruct((n, d), F32), jax.ShapeDtypeStruct((n, d), F32),
                   jax.ShapeDtypeStruct((n, nq), F32)],
        compiler_params=_cparams(1, 56),
    )(o_nsa, u, v, gate, x2, gt1, sc2, sh2, ws, bs, wb, wo, g1, b1, wq)


def _topk_rows(s, k):
    nrow = s.shape[0]
    rid = lax.broadcasted_iota(jnp.int32, s.shape, 0).astype(F32)
    vals, rows = [], []
    for _ in range(k):
        m = jnp.max(s, axis=0, keepdims=True)
        first = jnp.min(jnp.where(s == m, rid, float(nrow)), axis=0, keepdims=True)
        s = jnp.where(rid == first, -jnp.inf, s)
        vals.append(m)
        rows.append(first)
    return vals, rows


def _candidate_blocks(k):
    todo = {(a, b) for a in range(k) for b in range(k) if (a + 1) * (b + 1) <= k}
    options = [(axis, fixed, start) for axis in (0, 1) for fixed in range(k) for start in range(0, k, 8)]
    cells = lambda axis, fixed, start: [((fixed, start + r) if axis == 0 else (start + r, fixed)) for r in range(8)]
    blocks = []
    while todo:
        best = max(options, key=lambda o: sum(c in todo for c in cells(*o)))
        owned = tuple(c in todo for c in cells(*best))
        todo -= set(cells(*best))
        blocks.append(best + (owned,))
    return tuple(blocks)


_CAND_BLOCKS = _candidate_blocks(PEER_TOPK)


def _retrieve_body(qp_ref, keys_ref, idx_ref, gate_ref, eid_ref):
    k = PEER_TOPK
    for h in range(PEER_HEADS):
        tops = []
        for c in range(2):
            off = (h * 2 + c) * PEER_HALF
            qh = qp_ref[:, off:off + PEER_HALF].astype(BF16)
            s = _dot_nt(keys_ref[h, c].astype(BF16), qh)
            tops.append(_topk_rows(s, k))
        (v1, i1), (v2, i2) = tops
        vm = (jnp.concatenate(v1, axis=0), jnp.concatenate(v2, axis=0))
        im = (jnp.concatenate(i1, axis=0), jnp.concatenate(i2, axis=0))
        sub = lax.broadcasted_iota(jnp.int32, (8, 1), 0)
        vals, eids, poss = [], [], []
        for axis, fixed, start, owned in _CAND_BLOCKS:
            if axis == 0:
                v = v1[fixed] + vm[1][start:start + 8]
                e = i1[fixed] * float(PEER_NKEYS) + im[1][start:start + 8]
                pos = fixed * k + start + sub
            else:
                v = vm[0][start:start + 8] + v2[fixed]
                e = im[0][start:start + 8] * float(PEER_NKEYS) + i2[fixed]
                pos = (start + sub) * k + fixed
            if not all(owned):
                keep = functools.reduce(jnp.logical_or, [sub == r for r in range(8) if owned[r]])
                v = jnp.where(keep, v, -jnp.inf)
                pos = jnp.where(keep, pos, k * k)
            vals.append(v)
            eids.append(e)
            poss.append(pos.astype(F32))
        cand = jnp.concatenate(vals, axis=0)
        cidx = jnp.concatenate(eids, axis=0)
        rid = jnp.concatenate(poss, axis=0)
        top_s, top_e = [], []
        for _ in range(k):
            m = jnp.max(cand, axis=0, keepdims=True)
            first = jnp.min(jnp.where(cand == m, rid, float(k * k)), axis=0, keepdims=True)
            hit = rid == first
            top_e.append(jnp.sum(jnp.where(hit, cidx, 0.0), axis=0, keepdims=True))
            cand = jnp.where(hit, -jnp.inf, cand)
            top_s.append(m)
        ts = jnp.concatenate(top_s, axis=0)
        e = jnp.exp(ts - ts[0:1])
        gate_ref[0, h * k:(h + 1) * k, :] = e / jnp.sum(e, axis=0, keepdims=True)
        eid_ref[h * k:(h + 1) * k, :] = jnp.concatenate(top_e, axis=0)
    idx_ref[...] = (eid_ref[...].T * float(ROW_SUB)).astype(jnp.int32)


def _retrieve(qp, keys):
    n = qp.shape[0]
    nt = n // TC
    out = pl.BlockSpec((1, PEER_PAIRS, TC), lambda i: (i, 0, 0))
    return pl.pallas_call(
        _retrieve_body,
        grid=(nt,),
        in_specs=[pl.BlockSpec((TC, qp.shape[1]), lambda i: (i, 0)),
                  pl.BlockSpec(keys.shape, lambda i: (0, 0, 0, 0))],
        out_specs=[pl.BlockSpec((TC, PEER_PAIRS), lambda i: (i, 0)), out],
        out_shape=[jax.ShapeDtypeStruct((n, PEER_PAIRS), jnp.int32),
                   jax.ShapeDtypeStruct((nt, PEER_PAIRS, TC), F32)],
        scratch_shapes=[pltpu.VMEM((PEER_PAIRS, TC), F32)],
        compiler_params=_cparams(1, 48),
    )(qp, keys)


ROW_SUB = D_MODEL // 2 // LANES
ACT_UNROLL = 8
OFF_REGS = 8


def _load_row(tab_ref, scaled_idx):
    row = tab_ref[pl.ds(pl.multiple_of(scaled_idx, ROW_SUB), ROW_SUB), :]
    lo = lax.bitcast_convert_type(lax.shift_left(row, jnp.uint32(16)), F32)
    hi = lax.bitcast_convert_type(row & jnp.uint32(0xFFFF0000), F32)
    return lo, hi


def _expert_act_body(off_ref, idx_ref, h_ref, gate_ref, tab_ref, w_ref, slot_ref, part_ref):
    off = [off_ref[j] for j in range(OFF_REGS)]

    def token(t, carry):
        x = h_ref[t]
        xlo, xhi = x[:ROW_SUB], x[ROW_SUB:]
        base = pl.multiple_of(t * PEER_PAIRS, PEER_PAIRS)
        for p in range(PEER_PAIRS):
            if p % OFF_REGS == 0:
                idx_sub = idx_ref.at[pl.ds(base + p, OFF_REGS)]
            lo, hi = _load_row(tab_ref, idx_sub[off[p % OFF_REGS]])
            slot_ref[p * ROW_SUB:(p + 1) * ROW_SUB, :] = lo * xlo + hi * xhi
        part = slot_ref[pl.ds(0, PEER_PAIRS, stride=ROW_SUB), :]
        for s in range(1, ROW_SUB):
            part = part + slot_ref[pl.ds(s, PEER_PAIRS, stride=ROW_SUB), :]
        part_ref[pl.ds(pl.multiple_of(base, PEER_PAIRS), PEER_PAIRS), :] = part
        return carry

    lax.fori_loop(0, TC, token, 0)

    tok_lane = lax.broadcasted_iota(jnp.int32, (PEER_PAIRS, TC), 1)

    def lane_sums(i, act):
        for k in range(ACT_UNROLL):
            t = i * ACT_UNROLL + k
            part = part_ref[pl.ds(pl.multiple_of(t * PEER_PAIRS, PEER_PAIRS), PEER_PAIRS), :]
            act = jnp.where(tok_lane == t, jnp.sum(part, axis=1, keepdims=True), act)
        return act

    act = lax.fori_loop(0, TC // ACT_UNROLL, lane_sums, jnp.zeros((PEER_PAIRS, TC), F32))
    w_ref[...] = (gate_ref[0] * _gelu(act)).T


def _expert_act(idx_flat, h3, gate_t, tab):
    nt = gate_t.shape[0]
    tile = pl.BlockSpec((1, PEER_PAIRS, TC), lambda i: (i, 0, 0))
    return pl.pallas_call(
        _expert_act_body,
        grid=(nt,),
        in_specs=[pl.BlockSpec(memory_space=pltpu.SMEM),
                  pl.BlockSpec((TC * PEER_PAIRS,), lambda i: (i,), memory_space=pltpu.SMEM),
                  pl.BlockSpec((TC,) + h3.shape[1:], lambda i: (i, 0, 0)),
                  tile,
                  pl.BlockSpec(tab.shape, lambda i: (0, 0), pipeline_mode=pl.Buffered(1))],
        out_specs=pl.BlockSpec((TC, PEER_PAIRS), lambda i: (i, 0)),
        out_shape=jax.ShapeDtypeStruct((nt * TC, PEER_PAIRS), F32),
        scratch_shapes=[pltpu.VMEM((PEER_PAIRS * ROW_SUB, LANES), F32),
                        pltpu.VMEM((TC * PEER_PAIRS, LANES), F32)],
        compiler_params=_cparams(1, 52),
    )(jnp.arange(OFF_REGS, dtype=jnp.int32), idx_flat, h3, gate_t, tab)


def _expert_out_body(off_ref, idx_ref, w_ref, tab_ref, o_ref):
    n_acc = 4
    off = [off_ref[j] for j in range(OFF_REGS)]

    def token(t, carry):
        base = pl.multiple_of(t * PEER_PAIRS, PEER_PAIRS)
        acc_lo = [jnp.zeros((ROW_SUB, LANES), F32) for _ in range(n_acc)]
        acc_hi = [jnp.zeros((ROW_SUB, LANES), F32) for _ in range(n_acc)]
        for p in range(PEER_PAIRS):
            if p % OFF_REGS == 0:
                idx_sub = idx_ref.at[pl.ds(base + p, OFF_REGS)]
                w_sub = w_ref.at[pl.ds(base + p, OFF_REGS)]
            lo, hi = _load_row(tab_ref, idx_sub[off[p % OFF_REGS]])
            w = w_sub[off[p % OFF_REGS]]
            acc_lo[p % n_acc] = acc_lo[p % n_acc] + w * lo
            acc_hi[p % n_acc] = acc_hi[p % n_acc] + w * hi
        lo = (acc_lo[0] + acc_lo[1]) + (acc_lo[2] + acc_lo[3])
        hi = (acc_hi[0] + acc_hi[1]) + (acc_hi[2] + acc_hi[3])
        o_ref[t] = jnp.concatenate([lo, hi], axis=0)
        return carry

    lax.fori_loop(0, TB, token, 0)


def _expert_out(idx_flat, w_flat, tab, n):
    smem = pl.BlockSpec((TB * PEER_PAIRS,), lambda i: (i,), memory_space=pltpu.SMEM)
    rows = D_MODEL // LANES
    return pl.pallas_call(
        _expert_out_body,
        grid=(n // TB,),
        in_specs=[pl.BlockSpec(memory_space=pltpu.SMEM), smem, smem,
                  pl.BlockSpec(tab.shape, lambda i: (0, 0), pipeline_mode=pl.Buffered(1))],
        out_specs=pl.BlockSpec((TB, rows, LANES), lambda i: (i, 0, 0)),
        out_shape=jax.ShapeDtypeStruct((n, rows, LANES), F32),
        compiler_params=_cparams(1, 48),
    )(jnp.arange(OFF_REGS, dtype=jnp.int32), idx_flat, w_flat, tab)


def _final_body(x1_ref, f_ref, gt_ref, g_ref, b_ref, o_ref, *, alpha):
    o_ref[...] = _ln(alpha * x1_ref[...] + gt_ref[0] * f_ref[...]) * g_ref[...] + b_ref[...]


def _final(x1, ffn, gt2, g2, b2, seq, alpha):
    n, d = x1.shape
    tm = TM_LN
    nst = seq // tm
    row = pl.BlockSpec((tm, d), lambda i: (i, 0))
    const = pl.BlockSpec((1, d), lambda i: (0, 0))
    return pl.pallas_call(
        functools.partial(_final_body, alpha=alpha),
        grid=(n // tm,),
        in_specs=[row, row, pl.BlockSpec((1, 1, d), lambda i: (i // nst, 0, 0)), const, const],
        out_specs=row,
        out_shape=jax.ShapeDtypeStruct((n, d), F32),
        compiler_params=_cparams(1, 32),
    )(x1, ffn, gt2, g2, b2)


def _swap_halves(w, heads):
    d = w.shape[0]
    w4 = w.reshape(d, heads, 2, HEAD_DIM // 2)
    return jnp.flip(w4, axis=2).reshape(d, heads * HEAD_DIM)


def _fused_in_weight(w_in, w_merge):
    d = w_in.shape[0]
    o = 0
    parts = {}
    for name, width in (("q", NSA_WIDTH), ("kc", KV_WIDTH), ("vc", KV_WIDTH), ("ks", KV_WIDTH),
                        ("vs", KV_WIDTH), ("kw", KV_WIDTH), ("vw", KV_WIDTH), ("g", 3 * NSA_HEADS),
                        ("z", 2 * SGU_WIDTH)):
        parts[name] = w_in[:, o:o + width]
        o += width
    wq = parts["q"] * (HEAD_DIM ** -0.5)
    gcols = 3 * NSA_REP
    wg = jnp.zeros((d, 2 * LANES), w_in.dtype)
    wg = wg.at[:, :gcols].set(parts["g"][:, :gcols]).at[:, LANES:LANES + gcols].set(parts["g"][:, gcols:])
    g = NSA_KV_GROUPS
    cols = [wq, _swap_halves(wq, NSA_HEADS),
            parts["kc"], _swap_halves(parts["kc"], g), parts["ks"], _swap_halves(parts["ks"], g),
            parts["kw"], _swap_halves(parts["kw"], g), parts["vc"], parts["vs"], parts["vw"],
            wg, parts["z"], w_merge]
    return jnp.concatenate(cols, axis=1).astype(BF16)


def _rope_tables(seq):
    half = HEAD_DIM // 2
    pos = jnp.arange(seq, dtype=F32)
    inv_freq = ROPE_THETA ** (-jnp.arange(half, dtype=F32) / half)
    ang = pos[:, None] * inv_freq[None, :]
    cos, sin = jnp.cos(ang), jnp.sin(ang)
    reps = LANES // HEAD_DIM
    cosk = jnp.tile(jnp.concatenate([cos, cos], axis=1), (1, reps))
    sink = jnp.tile(jnp.concatenate([-sin, sin], axis=1), (1, reps))
    return cosk, sink


def _sel_aggregation(n_cmp_pad, n_sel):
    c0 = jnp.arange(n_cmp_pad)[:, None] * CMP_STRIDE
    s0 = jnp.arange(n_sel)[None, :] * SEL_BLOCK
    ov = jnp.clip(jnp.minimum(c0 + CMP_BLOCK, s0 + SEL_BLOCK) - jnp.maximum(c0, s0), 0, None)
    return (ov / CMP_BLOCK).astype(BF16)


def _pack_table(tab):
    half = tab.shape[1] // 2
    bits = lax.bitcast_convert_type(tab.astype(BF16), jnp.uint16).astype(jnp.uint32)
    packed = (bits[:, half:] << 16) | bits[:, :half]
    return packed.reshape(tab.shape[0] * ROW_SUB, LANES)


def _split_groups(a, bsz, seq):
    return a.reshape(bsz, seq, NSA_KV_GROUPS, HEAD_DIM).transpose(0, 2, 1, 3)


def kernel(x, c, w_ada, b_ada, w_in, cmp_pos, cmp_w1, cmp_b1, cmp_w2, cmp_b2, sgu_ln_g, sgu_ln_b, sgu_w, sgu_b, w_branch, w_merge, b_merge, w_out, ln1_g, ln1_b, peer_wq, peer_keys, peer_u, peer_v, ln2_g, ln2_b):
    bsz, seq, d = x.shape
    n = bsz * seq
    depth = w_ada.shape[0]
    alpha = (2.0 * depth) ** 0.25
    n_cmp = (seq - CMP_BLOCK) // CMP_STRIDE + 1
    n_half = seq // CMP_STRIDE
    n_sel = seq // SEL_BLOCK
    cosk, sink = _rope_tables(seq)
    agg = _sel_aggregation(n_half, n_sel)
    x2 = x.reshape(n, d)
    for l in range(depth):
        mod = _ada(c, w_ada[l], b_ada[l])
        sh1, sc1, gt1, sh2, sc2, gt2 = [m.reshape(bsz, 1, d) for m in jnp.split(mod, 6, axis=-1)]

        wall = _fused_in_weight(w_in[l], w_merge[l])
        (q_t, kc_r, vc_r, ks, vs, kw, vw, gn, u, v, gate) = _proj(
            x2, sc1, sh1, wall, b_merge[l].reshape(1, -1), cosk, sink,
            sgu_ln_g[l].reshape(1, -1), sgu_ln_b[l].reshape(1, -1), seq)

        t16 = jnp.stack([_split_groups(kc_r, bsz, seq), _split_groups(vc_r, bsz, seq)])
        t16 = t16.reshape(2, bsz, NSA_KV_GROUPS, n_half, CMP_STRIDE * HEAD_DIM)
        cmp_kv = _compress(t16, cmp_pos[l].reshape(2, 1, CMP_BLOCK * HEAD_DIM), cmp_w1[l],
                           cmp_b1[l].reshape(2, 1, CMP_HIDDEN), cmp_w2[l], cmp_b2[l].reshape(2, 1, HEAD_DIM))

        front = ((0, 0), (0, 0), (WINDOW, 0), (0, 0))
        key_tiles = lambda a, tk: a.reshape(a.shape[:2] + (a.shape[2] // tk, tk, HEAD_DIM)).transpose(0, 1, 2, 4, 3)
        gn_t = gn.reshape(bsz, seq, NSA_KV_GROUPS, LANES)[..., :16].transpose(0, 2, 3, 1)
        ks4 = _split_groups(ks, bsz, seq)
        block_onehot = ((jnp.arange(seq)[:, None] // SEL_BLOCK) % GROUP_BLOCKS
                        == jnp.arange(GROUP_BLOCKS)[None, :]).astype(BF16)
        ks_ext = jnp.concatenate([ks4, jnp.broadcast_to(block_onehot, ks4.shape[:3] + (GROUP_BLOCKS,))], axis=-1)
        o_nsa_t = _nsa(q_t, cmp_kv[0], cmp_kv[1].transpose(0, 1, 3, 2),
                       ks_ext, key_tiles(_split_groups(vs, bsz, seq), TK),
                       jnp.pad(_split_groups(kw, bsz, seq), front),
                       key_tiles(jnp.pad(_split_groups(vw, bsz, seq), front), TQ),
                       gn_t, agg.T, n_cmp)

        bs = jnp.repeat(sgu_b[l].T, SGU_GROUP_DIM, axis=1)
        x1, h2, qp = _merge(o_nsa_t, u, v, gate, x2, gt1, sc2, sh2,
                            sgu_w[l], bs, w_branch[l].astype(BF16), w_out[l].astype(BF16),
                            ln1_g[l].reshape(1, d), ln1_b[l].reshape(1, d), peer_wq[l].astype(BF16), seq, alpha)

        idx_tok, gate_t = _retrieve(qp, peer_keys[l])
        idx_flat = idx_tok.reshape(n * PEER_PAIRS)
        w_tok = _expert_act(idx_flat, h2.reshape(n, d // LANES, LANES), gate_t, _pack_table(peer_u[l]))
        w_flat = w_tok.reshape(n * PEER_PAIRS)
        ffn = _expert_out(idx_flat, w_flat, _pack_table(peer_v[l]), n)
        x2 = _final(x1, ffn.reshape(n, d), gt2, ln2_g[l].reshape(1, d), ln2_b[l].reshape(1, d), seq, alpha)
    return x2.reshape(bsz, seq, d)
```

```python
import functools

import jax
import jax.numpy as jnp
from jax import lax
from jax.experimental import pallas as pl
from jax.experimental.pallas import tpu as pltpu

D_MODEL = 1024
NSA_HEADS = 8
NSA_KV_GROUPS = 2
NSA_REP = NSA_HEADS // NSA_KV_GROUPS
HEAD_DIM = 64
NSA_WIDTH = NSA_HEADS * HEAD_DIM
KV_WIDTH = NSA_KV_GROUPS * HEAD_DIM
CMP_BLOCK = 32
CMP_STRIDE = 16
CMP_HIDDEN = 2 * HEAD_DIM
SEL_BLOCK = 64
SEL_TOPK = 16
WINDOW = 512
ROPE_THETA = 10000.0
SGU_GROUPS = 8
SGU_WIDTH = D_MODEL // 2
SGU_GROUP_DIM = SGU_WIDTH // SGU_GROUPS
SGU_CHUNK = 128
PEER_HEADS = 8
PEER_NKEYS = 128
PEER_EXPERTS = PEER_NKEYS * PEER_NKEYS
PEER_QDIM = 256
PEER_HALF = PEER_QDIM // 2
PEER_TOPK = 16
PEER_PAIRS = PEER_HEADS * PEER_TOPK
LN_EPS = 1e-5
NEG_INF = -1e30
FORCE_SCORE = 1e9

LANES = 128
MIB = 1024 * 1024
BF16 = jnp.bfloat16
F32 = jnp.float32

TM_PROJ = 256
TM_MERGE = 256
TM_LN = 512
TQ = 128
TK = 256
KV_GROUP = 4
GROUP_BLOCKS = TK * KV_GROUP // SEL_BLOCK
TC = 128
TB = 64

_OQ, _OQS = 0, 512
_OKC, _OKCS, _OKS, _OKSS, _OKW, _OKWS = 1024, 1152, 1280, 1408, 1536, 1664
_OVC, _OVS, _OVW = 1792, 1920, 2048
_OG = 2176
_OZ = 2432
_OM = 3456
_WCOLS = 5504


def _cparams(n_axes, vmem_mib):
    return pltpu.CompilerParams(
        dimension_semantics=("parallel",) * n_axes,
        vmem_limit_bytes=vmem_mib * MIB)


def _ln(x):
    mu = jnp.mean(x, axis=-1, keepdims=True)
    xc = x - mu
    var = jnp.mean(xc * xc, axis=-1, keepdims=True)
    return xc * lax.rsqrt(var + LN_EPS)


def _gelu(x):
    return 0.5 * x * (1.0 + lax.erf(x * (2.0 ** -0.5)))


def _dot(a, b):
    return jnp.dot(a, b, preferred_element_type=F32)


def _dot_nt(a, b):
    return lax.dot_general(a, b, (((1,), (1,)), ((), ())), preferred_element_type=F32)


def _ada_body(c_ref, w_ref, b_ref, o_ref):
    cv = c_ref[...]
    a = cv * jax.nn.sigmoid(cv)
    o_ref[...] = _dot(a.astype(BF16), w_ref[...].astype(BF16)) + b_ref[...]


def _ada(c, w, b):
    bsz, d = c.shape
    n = w.shape[1]
    tn = 1024
    return pl.pallas_call(
        _ada_body,
        grid=(n // tn,),
        in_specs=[pl.BlockSpec((bsz, d), lambda j: (0, 0)),
                  pl.BlockSpec((d, tn), lambda j: (0, j)),
                  pl.BlockSpec((1, tn), lambda j: (0, j))],
        out_specs=pl.BlockSpec((bsz, tn), lambda j: (0, j)),
        out_shape=jax.ShapeDtypeStruct((bsz, n), F32),
        compiler_params=_cparams(1, 32),
    )(c, w, b.reshape(1, n))


def _proj_body(x_ref, sc_ref, sh_ref, w_ref, bm_ref, cos_ref, sin_ref, sg_ref, sb_ref,
               q_ref, kc_ref, vc_ref, ks_ref, vs_ref, kw_ref, vw_ref, gn_ref, u_ref, v_ref, gate_ref):
    h = _ln(x_ref[...]) * (1.0 + sc_ref[0]) + sh_ref[0]
    hb = h.astype(BF16)

    def proj(off, width):
        return _dot(hb, w_ref[:, off:off + width])

    cosk = cos_ref[...]
    sink = sin_ref[...]
    cosq = jnp.concatenate([cosk] * (NSA_WIDTH // LANES), axis=1)
    sinq = jnp.concatenate([sink] * (NSA_WIDTH // LANES), axis=1)
    q_ref[0] = (proj(_OQ, NSA_WIDTH) * cosq + proj(_OQS, NSA_WIDTH) * sinq).T
    kc_ref[...] = proj(_OKC, KV_WIDTH) * cosk + proj(_OKCS, KV_WIDTH) * sink
    ks_ref[...] = (proj(_OKS, KV_WIDTH) * cosk + proj(_OKSS, KV_WIDTH) * sink).astype(BF16)
    kw_ref[...] = (proj(_OKW, KV_WIDTH) * cosk + proj(_OKWS, KV_WIDTH) * sink).astype(BF16)
    vc_ref[...] = proj(_OVC, KV_WIDTH)
    vs_ref[...] = proj(_OVS, KV_WIDTH).astype(BF16)
    vw_ref[...] = proj(_OVW, KV_WIDTH).astype(BF16)
    gn_ref[...] = jax.nn.sigmoid(proj(_OG, 2 * LANES))
    u_ref[...] = _gelu(proj(_OZ, SGU_WIDTH))
    v_ref[...] = _ln(_gelu(proj(_OZ + SGU_WIDTH, SGU_WIDTH))) * sg_ref[...] + sb_ref[...]
    gate_ref[...] = jax.nn.sigmoid(proj(_OM, 2 * D_MODEL) + bm_ref[...])


def _proj(x2, sc, sh, wall, bm, cosk, sink, sg, sb, seq):
    n, d = x2.shape
    tm = TM_PROJ
    nst = seq // tm
    row = lambda w: pl.BlockSpec((tm, w), lambda i: (i, 0))
    per_batch = pl.BlockSpec((1, 1, d), lambda i: (i // nst, 0, 0))
    const = lambda shp: pl.BlockSpec(shp, lambda i: (0,) * len(shp))
    pos = pl.BlockSpec((tm, LANES), lambda i: (i % nst, 0))
    outs = [(KV_WIDTH, F32), (KV_WIDTH, F32), (KV_WIDTH, BF16), (KV_WIDTH, BF16),
            (KV_WIDTH, BF16), (KV_WIDTH, BF16), (2 * LANES, F32), (SGU_WIDTH, F32), (SGU_WIDTH, F32),
            (2 * D_MODEL, F32)]
    q_spec = pl.BlockSpec((1, NSA_WIDTH, tm), lambda i: (i // nst, 0, i % nst))
    q_shape = jax.ShapeDtypeStruct((n // seq, NSA_WIDTH, seq), F32)
    return pl.pallas_call(
        _proj_body,
        grid=(n // tm,),
        in_specs=[row(d), per_batch, per_batch, const((d, _WCOLS)), const((1, 2 * D_MODEL)),
                  pos, pos, const((1, SGU_WIDTH)), const((1, SGU_WIDTH))],
        out_specs=[q_spec] + [row(w) for w, _ in outs],
        out_shape=[q_shape] + [jax.ShapeDtypeStruct((n, w), dt) for w, dt in outs],
        compiler_params=_cparams(1, 56),
    )(x2, sc, sh, wall, bm, cosk, sink, sg, sb)


def _cmp_body(t_ref, pos_ref, w1_ref, b1_ref, w2_ref, b2_ref, o_ref):
    half = (CMP_BLOCK // 2) * HEAD_DIM
    t = t_ref[0, 0, 0]
    pos = pos_ref[0]
    ta = (t + pos[:, :half]).astype(BF16)
    tb = (t + pos[:, half:]).astype(BF16)
    a = _dot(ta, w1_ref[0, :half, :].astype(BF16))
    b = _dot(tb, w1_ref[0, half:, :].astype(BF16))
    nrow = t.shape[0]
    b_next = pltpu.roll(b, nrow - 1, axis=0)
    hid = _gelu(a + b_next + b1_ref[0])
    out = _dot(hid.astype(BF16), w2_ref[0].astype(BF16)) + b2_ref[0]
    o_ref[0, 0, 0] = out.astype(o_ref.dtype)


def _compress(t16, pos, w1, b1, w2, b2):
    _, bsz, g, nr, dd = t16.shape
    per_kind = lambda shp: pl.BlockSpec((1,) + shp, lambda k, b, gg: (k,) + (0,) * len(shp))
    blk = lambda w: pl.BlockSpec((1, 1, 1, nr, w), lambda k, b, gg: (k, b, gg, 0, 0))
    return pl.pallas_call(
        _cmp_body,
        grid=(2, bsz, g),
        in_specs=[blk(dd), per_kind((1, CMP_BLOCK * HEAD_DIM)), per_kind((CMP_BLOCK * HEAD_DIM, CMP_HIDDEN)),
                  per_kind((1, CMP_HIDDEN)), per_kind((CMP_HIDDEN, HEAD_DIM)), per_kind((1, HEAD_DIM))],
        out_specs=blk(HEAD_DIM),
        out_shape=jax.ShapeDtypeStruct((2, bsz, g, nr, HEAD_DIM), BF16),
        compiler_params=_cparams(3, 48),
    )(t16, pos, w1, b1, w2, b2)


def _scores(q_all, kt, bias):
    s = _dot(kt, q_all)
    return s if bias is None else s + bias


def _row_max(m, q_all, kt, bias=None):
    return jnp.maximum(m, jnp.max(_scores(q_all, kt, bias), axis=0, keepdims=True))


def _accumulate(state, m, q_all, kt, vt_t, bias=None):
    l, acc = state
    p = jnp.exp(_scores(q_all, kt, bias) - m)
    return l + jnp.sum(p, axis=0, keepdims=True), acc + _dot(vt_t, p.astype(BF16))


def _max_init():
    return jnp.full((1, NSA_REP * TQ), NEG_INF, F32)


def _acc_init():
    w = NSA_REP * TQ
    return jnp.zeros((1, w), F32), jnp.zeros((HEAD_DIM, w), F32)


def _lane_tile(a):
    return jnp.concatenate([a] * NSA_REP, axis=1)


def _nsa_body(q_ref, kc_ref, vc_ref, ks_ref, vs_ref, kw_ref, vw_ref, gn_ref, agg_ref, o_ref, selneg_ref, score_ref, *, n_cmp):
    qi = pl.program_id(2)
    q0 = qi * TQ
    t1 = q0 + lax.broadcasted_iota(jnp.int32, (1, TQ), 1)
    t = _lane_tile(t1)
    q_all = jnp.concatenate([q_ref[0, r * HEAD_DIM:(r + 1) * HEAD_DIM, :] for r in range(NSA_REP)],
                            axis=1).astype(BF16)

    kcm = kc_ref[0, 0]
    ncp = kcm.shape[0]
    nidx = lax.broadcasted_iota(jnp.int32, (ncp, 1), 0)
    mask_c = (nidx * CMP_STRIDE + (CMP_BLOCK - 1) <= t) & (nidx < n_cmp)
    sm = jnp.where(mask_c, _dot(kcm, q_all), NEG_INF)
    e = jnp.where(mask_c, jnp.exp(sm - jnp.max(sm, axis=0, keepdims=True)), 0.0)
    l = jnp.sum(e, axis=0, keepdims=True)
    p = e * (1.0 / jnp.where(l > 0.0, l, 1.0))
    o_c = _dot(vc_ref[0, 0], p.astype(BF16))
    psum = p[:, :TQ]
    for r in range(1, NSA_REP):
        psum = psum + p[:, r * TQ:(r + 1) * TQ]

    p_hi = psum.astype(BF16)
    p_lo = (psum - p_hi.astype(F32)).astype(BF16)
    agg_t = agg_ref[...]
    imp = _dot(agg_t, p_hi) + _dot(agg_t, p_lo)
    n_sel = imp.shape[0]
    blk = lax.broadcasted_iota(jnp.int32, (n_sel, 1), 0)
    blk_f = blk.astype(F32)
    cur = t1 // SEL_BLOCK
    forced = (blk == 0) | (blk == cur) | (blk == cur - 1)
    score = jnp.where(forced, FORCE_SCORE, jnp.where(blk <= cur, imp, NEG_INF))
    sel = jnp.zeros((n_sel, TQ), jnp.bool_)
    for _ in range(min(SEL_TOPK, n_sel)):
        m = jnp.max(score, axis=0, keepdims=True)
        first = jnp.min(jnp.where(score == m, blk_f, float(n_sel)), axis=0, keepdims=True)
        hit = blk_f == first
        sel = sel | hit
        score = jnp.where(hit, -jnp.inf, score)
    selneg_ref[...] = _lane_tile(jnp.where(sel, 0.0, NEG_INF).astype(BF16))

    def q_ext(jg):
        rows = selneg_ref[pl.ds(pl.multiple_of(jg * GROUP_BLOCKS, GROUP_BLOCKS), GROUP_BLOCKS), :]
        return jnp.concatenate([q_all, rows], axis=0)

    krow_w = lax.broadcasted_iota(jnp.int32, (TQ, 1), 0)

    def win_chunk(c):
        kpos = q0 - WINDOW + c * TQ + krow_w
        dlt = t - kpos
        bias = jnp.where((dlt >= 0) & (dlt < WINDOW) & (kpos >= 0), 0.0, NEG_INF)
        return kw_ref[0, 0, pl.ds(pl.multiple_of(q0 + c * TQ, TQ), TQ), :], bias

    n_win = WINDOW // TQ + 1
    m_w = _max_init()
    for c in range(n_win):
        kt, bias = win_chunk(c)
        m_w = _row_max(m_w, q_all, kt, bias)
    state = _acc_init()
    for c in range(n_win):
        kt, bias = win_chunk(c)
        state = _accumulate(state, m_w, q_all, kt, vw_ref[0, 0, qi + c], bias)
    o_w = state[1] / state[0]

    krow = lax.broadcasted_iota(jnp.int32, (TK, 1), 0)
    group_rows = TK * KV_GROUP

    def max_group(jg, m):
        k0 = pl.multiple_of(jg * group_rows, group_rows)
        s = _scores(q_ext(jg), ks_ref[0, 0, pl.ds(k0, group_rows), :], None)
        score_ref[pl.ds(k0, group_rows), :] = s
        return jnp.maximum(m, jnp.max(s, axis=0, keepdims=True))

    def max_tile(j, m):
        k0 = pl.multiple_of(j * TK, TK)
        bias = jnp.where(k0 + krow <= t, 0.0, NEG_INF)
        s = _scores(q_ext(j // KV_GROUP), ks_ref[0, 0, pl.ds(k0, TK), :], bias)
        score_ref[pl.ds(k0, TK), :] = s
        return jnp.maximum(m, jnp.max(s, axis=0, keepdims=True))

    def sum_rows(state, k0, rows, vt_t):
        l, acc = state
        p = jnp.exp(score_ref[pl.ds(k0, rows), :] - m_s)
        return l + jnp.sum(p, axis=0, keepdims=True), acc + _dot(vt_t, p.astype(BF16))

    def sum_group(jg, state):
        vt_t = jnp.concatenate([vs_ref[0, 0, jg * KV_GROUP + sub] for sub in range(KV_GROUP)], axis=1)
        return sum_rows(state, pl.multiple_of(jg * group_rows, group_rows), group_rows, vt_t)

    def sum_tile(j, state):
        return sum_rows(state, pl.multiple_of(j * TK, TK), TK, vs_ref[0, 0, j])

    n_full = q0 // group_rows
    n_kv = (q0 + TQ - 1) // TK + 1
    m_s = lax.fori_loop(0, n_full, max_group, _max_init())
    m_s = lax.fori_loop(n_full * KV_GROUP, n_kv, max_tile, m_s)
    state = lax.fori_loop(0, n_full, sum_group, _acc_init())
    state = lax.fori_loop(n_full * KV_GROUP, n_kv, sum_tile, state)
    o_s = state[1] / state[0]

    gn = gn_ref[0, 0]
    gate = [jnp.concatenate([gn[3 * r + c:3 * r + c + 1, :] for r in range(NSA_REP)], axis=1) for c in range(3)]
    o = gate[0] * o_c + gate[1] * o_s + gate[2] * o_w
    for r in range(NSA_REP):
        o_ref[0, r * HEAD_DIM:(r + 1) * HEAD_DIM, :] = o[:, r * TQ:(r + 1) * TQ]


def _nsa(q_t, kc, vc_t, ks, vs_t, kwp, vwp_t, gn_t, agg_t, n_cmp):
    bsz, _, seq = q_t.shape
    g = NSA_KV_GROUPS
    gw = NSA_REP * HEAD_DIM
    per_bg = lambda a: pl.BlockSpec((1, 1) + a.shape[2:], lambda b, gg, i: (b, gg) + (0,) * (a.ndim - 2))
    return pl.pallas_call(
        functools.partial(_nsa_body, n_cmp=n_cmp),
        grid=(bsz, g, seq // TQ),
        in_specs=[pl.BlockSpec((1, gw, TQ), lambda b, gg, i: (b, gg, i)),
                  per_bg(kc), per_bg(vc_t), per_bg(ks), per_bg(vs_t), per_bg(kwp), per_bg(vwp_t),
                  pl.BlockSpec((1, 1, gn_t.shape[2], TQ), lambda b, gg, i: (b, gg, 0, i)),
                  pl.BlockSpec(agg_t.shape, lambda b, gg, i: (0, 0))],
        out_specs=pl.BlockSpec((1, gw, TQ), lambda b, gg, i: (b, gg, i)),
        out_shape=jax.ShapeDtypeStruct((bsz, NSA_WIDTH, seq), F32),
        scratch_shapes=[pltpu.VMEM((agg_t.shape[0], NSA_REP * TQ), BF16),
                        pltpu.VMEM((seq, NSA_REP * TQ), F32)],
        compiler_params=_cparams(3, 56),
    )(q_t, kc, vc_t, ks, vs_t, kwp, vwp_t, gn_t, agg_t)


def _merge_body(on_ref, u_ref, v_ref, gate_ref, x_ref, gt_ref, sc_ref, sh_ref, ws_ref, bs_ref,
                wb_ref, wo_ref, g1_ref, b1_ref, wq_ref, x1_ref, h2_ref, qp_ref, *, alpha):
    tm = x_ref.shape[0]
    row = lax.broadcasted_iota(jnp.int32, (SGU_CHUNK, SGU_CHUNK), 0)
    col = lax.broadcasted_iota(jnp.int32, (SGU_CHUNK, SGU_CHUNK), 1)
    lane_group = lax.broadcasted_iota(jnp.int32, (1, SGU_WIDTH), 1) // SGU_GROUP_DIM
    mixed = []
    for c in range(tm // SGU_CHUNK):
        v = v_ref[c * SGU_CHUNK:(c + 1) * SGU_CHUNK, :]
        acc = jnp.zeros((SGU_CHUNK, SGU_WIDTH), F32)
        for g in range(SGU_GROUPS):
            wg = jnp.where(col <= row, ws_ref[g], 0.0).astype(BF16)
            vg = jnp.where(lane_group == g, v, 0.0).astype(BF16)
            acc = acc + _dot(wg, vg)
        mixed.append(acc + bs_ref[...])
    o_sgu = u_ref[...] * jnp.concatenate(mixed, axis=0)
    gate = gate_ref[...]
    merged = (gate[:, :D_MODEL] * _dot(on_ref[0].T.astype(BF16), wb_ref[0])
              + gate[:, D_MODEL:] * _dot(o_sgu.astype(BF16), wb_ref[1]))
    mix = _dot(merged.astype(BF16), wo_ref[...])
    x1 = _ln(alpha * x_ref[...] + gt_ref[0] * mix) * g1_ref[...] + b1_ref[...]
    x1_ref[...] = x1
    h2 = _ln(x1) * (1.0 + sc_ref[0]) + sh_ref[0]
    h2_ref[...] = h2
    qp_ref[...] = _dot(h2.astype(BF16), wq_ref[...])


def _merge(o_nsa, u, v, gate, x2, gt1, sc2, sh2, ws, bs, wb, wo, g1, b1, wq, seq, alpha):
    n, d = x2.shape
    tm = TM_MERGE
    nst = seq // tm
    row = lambda w: pl.BlockSpec((tm, w), lambda i: (i, 0))
    per_batch = pl.BlockSpec((1, 1, d), lambda i: (i // nst, 0, 0))
    const = lambda a: pl.BlockSpec(a.shape, lambda i: (0,) * a.ndim)
    nq = wq.shape[1]
    return pl.pallas_call(
        functools.partial(_merge_body, alpha=alpha),
        grid=(n // tm,),
        in_specs=[pl.BlockSpec((1, NSA_WIDTH, tm), lambda i: (i // nst, 0, i % nst)),
                  row(SGU_WIDTH), row(SGU_WIDTH), row(2 * d), row(d),
                  per_batch, per_batch, per_batch, const(ws), const(bs), const(wb), const(wo),
                  const(g1), const(b1), const(wq)],
        out_specs=[row(d), row(d), row(nq)],
        out_shape=[jax.ShapeDtypeStruct((n, d), F32), jax.ShapeDtypeStruct((n, d), F32),
                   jax.ShapeDtypeStruct((n, nq), F32)],
        compiler_params=_cparams(1, 56),
    )(o_nsa, u, v, gate, x2, gt1, sc2, sh2, ws, bs, wb, wo, g1, b1, wq)


def _topk_rows(s, k):
    nrow = s.shape[0]
    rid = lax.broadcasted_iota(jnp.int32, s.shape, 0).astype(F32)
    vals, rows = [], []
    for _ in range(k):
        m = jnp.max(s, axis=0, keepdims=True)
        first = jnp.min(jnp.where(s == m, rid, float(nrow)), axis=0, keepdims=True)
        s = jnp.where(rid == first, -jnp.inf, s)
        vals.append(m)
        rows.append(first)
    return vals, rows


def _candidate_blocks(k):
    todo = {(a, b) for a in range(k) for b in range(k) if (a + 1) * (b + 1) <= k}
    options = [(axis, fixed, start) for axis in (0, 1) for fixed in range(k) for start in range(0, k, 8)]
    cells = lambda axis, fixed, start: [((fixed, start + r) if axis == 0 else (start + r, fixed)) for r in range(8)]
    blocks = []
    while todo:
        best = max(options, key=lambda o: sum(c in todo for c in cells(*o)))
        owned = tuple(c in todo for c in cells(*best))
        todo -= set(cells(*best))
        blocks.append(best + (owned,))
    return tuple(blocks)


_CAND_BLOCKS = _candidate_blocks(PEER_TOPK)


def _retrieve_body(qp_ref, keys_ref, idx_ref, gate_ref, eid_ref):
    k = PEER_TOPK
    for h in range(PEER_HEADS):
        tops = []
        for c in range(2):
            off = (h * 2 + c) * PEER_HALF
            qh = qp_ref[:, off:off + PEER_HALF].astype(BF16)
            s = _dot_nt(keys_ref[h, c].astype(BF16), qh)
            tops.append(_topk_rows(s, k))
        (v1, i1), (v2, i2) = tops
        vm = (jnp.concatenate(v1, axis=0), jnp.concatenate(v2, axis=0))
        im = (jnp.concatenate(i1, axis=0), jnp.concatenate(i2, axis=0))
        sub = lax.broadcasted_iota(jnp.int32, (8, 1), 0)
        vals, eids, poss = [], [], []
        for axis, fixed, start, owned in _CAND_BLOCKS:
            if axis == 0:
                v = v1[fixed] + vm[1][start:start + 8]
                e = i1[fixed] * float(PEER_NKEYS) + im[1][start:start + 8]
                pos = fixed * k + start + sub
            else:
                v = vm[0][start:start + 8] + v2[fixed]
                e = im[0][start:start + 8] * float(PEER_NKEYS) + i2[fixed]
                pos = (start + sub) * k + fixed
            if not all(owned):
                keep = functools.reduce(jnp.logical_or, [sub == r for r in range(8) if owned[r]])
                v = jnp.where(keep, v, -jnp.inf)
                pos = jnp.where(keep, pos, k * k)
            vals.append(v)
            eids.append(e)
            poss.append(pos.astype(F32))
        cand = jnp.concatenate(vals, axis=0)
        cidx = jnp.concatenate(eids, axis=0)
        rid = jnp.concatenate(poss, axis=0)
        top_s, top_e = [], []
        for _ in range(k):
            m = jnp.max(cand, axis=0, keepdims=True)
            first = jnp.min(jnp.where(cand == m, rid, float(k * k)), axis=0, keepdims=True)
            hit = rid == first
            top_e.append(jnp.sum(jnp.where(hit, cidx, 0.0), axis=0, keepdims=True))
            cand = jnp.where(hit, -jnp.inf, cand)
            top_s.append(m)
        ts = jnp.concatenate(top_s, axis=0)
        e = jnp.exp(ts - ts[0:1])
        gate_ref[0, h * k:(h + 1) * k, :] = e / jnp.sum(e, axis=0, keepdims=True)
        eid_ref[h * k:(h + 1) * k, :] = jnp.concatenate(top_e, axis=0)
    idx_ref[...] = (eid_ref[...].T * float(ROW_SUB)).astype(jnp.int32)


def _retrieve(qp, keys):
    n = qp.shape[0]
    nt = n // TC
    out = pl.BlockSpec((1, PEER_PAIRS, TC), lambda i: (i, 0, 0))
    return pl.pallas_call(
        _retrieve_body,
        grid=(nt,),
        in_specs=[pl.BlockSpec((TC, qp.shape[1]), lambda i: (i, 0)),
                  pl.BlockSpec(keys.shape, lambda i: (0, 0, 0, 0))],
        out_specs=[pl.BlockSpec((TC, PEER_PAIRS), lambda i: (i, 0)), out],
        out_shape=[jax.ShapeDtypeStruct((n, PEER_PAIRS), jnp.int32),
                   jax.ShapeDtypeStruct((nt, PEER_PAIRS, TC), F32)],
        scratch_shapes=[pltpu.VMEM((PEER_PAIRS, TC), F32)],
        compiler_params=_cparams(1, 48),
    )(qp, keys)


ROW_SUB = D_MODEL // 2 // LANES
ACT_UNROLL = 8
OFF_REGS = 8


def _load_row(tab_ref, scaled_idx):
    row = tab_ref[pl.ds(pl.multiple_of(scaled_idx, ROW_SUB), ROW_SUB), :]
    lo = lax.bitcast_convert_type(lax.shift_left(row, jnp.uint32(16)), F32)
    hi = lax.bitcast_convert_type(row & jnp.uint32(0xFFFF0000), F32)
    return lo, hi


def _expert_act_body(off_ref, idx_ref, h_ref, gate_ref, tab_ref, w_ref, slot_ref, part_ref):
    off = [off_ref[j] for j in range(OFF_REGS)]

    def token(t, carry):
        x = h_ref[pl.ds(t, 1), :]
        chunk = lambda s: x[:, s * LANES:(s + 1) * LANES]
        xlo = jnp.concatenate([chunk(s) for s in range(ROW_SUB)], axis=0)
        xhi = jnp.concatenate([chunk(ROW_SUB + s) for s in range(ROW_SUB)], axis=0)
        base = pl.multiple_of(t * PEER_PAIRS, PEER_PAIRS)
        for p in range(PEER_PAIRS):
            if p % OFF_REGS == 0:
                idx_sub = idx_ref.at[t, pl.ds(p, OFF_REGS)]
            lo, hi = _load_row(tab_ref, idx_sub[off[p % OFF_REGS]])
            slot_ref[p * ROW_SUB:(p + 1) * ROW_SUB, :] = lo * xlo + hi * xhi
        part = slot_ref[pl.ds(0, PEER_PAIRS, stride=ROW_SUB), :]
        for s in range(1, ROW_SUB):
            part = part + slot_ref[pl.ds(s, PEER_PAIRS, stride=ROW_SUB), :]
        part_ref[pl.ds(pl.multiple_of(base, PEER_PAIRS), PEER_PAIRS), :] = part
        return carry

    lax.fori_loop(0, TC, token, 0)

    tok_lane = lax.broadcasted_iota(jnp.int32, (PEER_PAIRS, TC), 1)

    def lane_sums(i, act):
        for k in range(ACT_UNROLL):
            t = i * ACT_UNROLL + k
            part = part_ref[pl.ds(pl.multiple_of(t * PEER_PAIRS, PEER_PAIRS), PEER_PAIRS), :]
            act = jnp.where(tok_lane == t, jnp.sum(part, axis=1, keepdims=True), act)
        return act

    act = lax.fori_loop(0, TC // ACT_UNROLL, lane_sums, jnp.zeros((PEER_PAIRS, TC), F32))
    w_ref[...] = (gate_ref[0] * _gelu(act)).T


def _expert_act(idx_tok, h2, gate_t, tab):
    nt = gate_t.shape[0]
    tile = pl.BlockSpec((1, PEER_PAIRS, TC), lambda i: (i, 0, 0))
    return pl.pallas_call(
        _expert_act_body,
        grid=(nt,),
        in_specs=[pl.BlockSpec(memory_space=pltpu.SMEM),
                  pl.BlockSpec((TC, PEER_PAIRS), lambda i: (i, 0), memory_space=pltpu.SMEM),
                  pl.BlockSpec((TC, h2.shape[1]), lambda i: (i, 0)),
                  tile,
                  pl.BlockSpec(tab.shape, lambda i: (0, 0), pipeline_mode=pl.Buffered(1))],
        out_specs=pl.BlockSpec((TC, PEER_PAIRS), lambda i: (i, 0)),
        out_shape=jax.ShapeDtypeStruct((nt * TC, PEER_PAIRS), F32),
        scratch_shapes=[pltpu.VMEM((PEER_PAIRS * ROW_SUB, LANES), F32),
                        pltpu.VMEM((TC * PEER_PAIRS, LANES), F32)],
        compiler_params=_cparams(1, 52),
    )(jnp.arange(OFF_REGS, dtype=jnp.int32), idx_tok, h2, gate_t, tab)


def _expert_out_body(off_ref, idx_ref, w_ref, tab_ref, o_ref):
    n_acc = 4
    off = [off_ref[j] for j in range(OFF_REGS)]

    def token(t, carry):
        acc_lo = [jnp.zeros((ROW_SUB, LANES), F32) for _ in range(n_acc)]
        acc_hi = [jnp.zeros((ROW_SUB, LANES), F32) for _ in range(n_acc)]
        for p in range(PEER_PAIRS):
            if p % OFF_REGS == 0:
                idx_sub = idx_ref.at[t, pl.ds(p, OFF_REGS)]
                w_sub = w_ref.at[t, pl.ds(p, OFF_REGS)]
            lo, hi = _load_row(tab_ref, idx_sub[off[p % OFF_REGS]])
            w = w_sub[off[p % OFF_REGS]]
            acc_lo[p % n_acc] = acc_lo[p % n_acc] + w * lo
            acc_hi[p % n_acc] = acc_hi[p % n_acc] + w * hi
        lo = (acc_lo[0] + acc_lo[1]) + (acc_lo[2] + acc_lo[3])
        hi = (acc_hi[0] + acc_hi[1]) + (acc_hi[2] + acc_hi[3])
        o_ref[t] = jnp.concatenate([lo, hi], axis=0)
        return carry

    lax.fori_loop(0, TB, token, 0)


def _expert_out(idx_tok, w_tok, tab):
    n = idx_tok.shape[0]
    smem = pl.BlockSpec((TB, PEER_PAIRS), lambda i: (i, 0), memory_space=pltpu.SMEM)
    return pl.pallas_call(
        _expert_out_body,
        grid=(n // TB,),
        in_specs=[pl.BlockSpec(memory_space=pltpu.SMEM), smem, smem,
                  pl.BlockSpec(tab.shape, lambda i: (0, 0), pipeline_mode=pl.Buffered(1))],
        out_specs=pl.BlockSpec((TB, D_MODEL // LANES, LANES), lambda i: (i, 0, 0)),
        out_shape=jax.ShapeDtypeStruct((n, D_MODEL // LANES, LANES), F32),
        compiler_params=_cparams(1, 48),
    )(jnp.arange(OFF_REGS, dtype=jnp.int32), idx_tok, w_tok, tab)


def _final_body(x1_ref, f_ref, gt_ref, g_ref, b_ref, o_ref, *, alpha):
    o_ref[...] = _ln(alpha * x1_ref[...] + gt_ref[0] * f_ref[...]) * g_ref[...] + b_ref[...]


def _final(x1, ffn, gt2, g2, b2, seq, alpha):
    n, d = x1.shape
    tm = TM_LN
    nst = seq // tm
    row = pl.BlockSpec((tm, d), lambda i: (i, 0))
    const = pl.BlockSpec((1, d), lambda i: (0, 0))
    return pl.pallas_call(
        functools.partial(_final_body, alpha=alpha),
        grid=(n // tm,),
        in_specs=[row, row, pl.BlockSpec((1, 1, d), lambda i: (i // nst, 0, 0)), const, const],
        out_specs=row,
        out_shape=jax.ShapeDtypeStruct((n, d), F32),
        compiler_params=_cparams(1, 32),
    )(x1, ffn, gt2, g2, b2)


def _swap_halves(w, heads):
    d = w.shape[0]
    w4 = w.reshape(d, heads, 2, HEAD_DIM // 2)
    return jnp.flip(w4, axis=2).reshape(d, heads * HEAD_DIM)


def _fused_in_weight(w_in, w_merge):
    d = w_in.shape[0]
    o = 0
    parts = {}
    for name, width in (("q", NSA_WIDTH), ("kc", KV_WIDTH), ("vc", KV_WIDTH), ("ks", KV_WIDTH),
                        ("vs", KV_WIDTH), ("kw", KV_WIDTH), ("vw", KV_WIDTH), ("g", 3 * NSA_HEADS),
                        ("z", 2 * SGU_WIDTH)):
        parts[name] = w_in[:, o:o + width]
        o += width
    wq = parts["q"] * (HEAD_DIM ** -0.5)
    gcols = 3 * NSA_REP
    wg = jnp.zeros((d, 2 * LANES), w_in.dtype)
    wg = wg.at[:, :gcols].set(parts["g"][:, :gcols]).at[:, LANES:LANES + gcols].set(parts["g"][:, gcols:])
    g = NSA_KV_GROUPS
    cols = [wq, _swap_halves(wq, NSA_HEADS),
            parts["kc"], _swap_halves(parts["kc"], g), parts["ks"], _swap_halves(parts["ks"], g),
            parts["kw"], _swap_halves(parts["kw"], g), parts["vc"], parts["vs"], parts["vw"],
            wg, parts["z"], w_merge]
    return jnp.concatenate(cols, axis=1).astype(BF16)


def _rope_tables(seq):
    half = HEAD_DIM // 2
    pos = jnp.arange(seq, dtype=F32)
    inv_freq = ROPE_THETA ** (-jnp.arange(half, dtype=F32) / half)
    ang = pos[:, None] * inv_freq[None, :]
    cos, sin = jnp.cos(ang), jnp.sin(ang)
    reps = LANES // HEAD_DIM
    cosk = jnp.tile(jnp.concatenate([cos, cos], axis=1), (1, reps))
    sink = jnp.tile(jnp.concatenate([-sin, sin], axis=1), (1, reps))
    return cosk, sink


def _sel_aggregation(n_cmp_pad, n_sel):
    c0 = jnp.arange(n_cmp_pad)[:, None] * CMP_STRIDE
    s0 = jnp.arange(n_sel)[None, :] * SEL_BLOCK
    ov = jnp.clip(jnp.minimum(c0 + CMP_BLOCK, s0 + SEL_BLOCK) - jnp.maximum(c0, s0), 0, None)
    return (ov / CMP_BLOCK).astype(BF16)


def _pack_table(tab):
    half = tab.shape[1] // 2
    bits = lax.bitcast_convert_type(tab.astype(BF16), jnp.uint16).astype(jnp.uint32)
    packed = (bits[:, half:] << 16) | bits[:, :half]
    return packed.reshape(tab.shape[0] * ROW_SUB, LANES)


def _split_groups(a, bsz, seq):
    return a.reshape(bsz, seq, NSA_KV_GROUPS, HEAD_DIM).transpose(0, 2, 1, 3)


def kernel(x, c, w_ada, b_ada, w_in, cmp_pos, cmp_w1, cmp_b1, cmp_w2, cmp_b2, sgu_ln_g, sgu_ln_b, sgu_w, sgu_b, w_branch, w_merge, b_merge, w_out, ln1_g, ln1_b, peer_wq, peer_keys, peer_u, peer_v, ln2_g, ln2_b):
    bsz, seq, d = x.shape
    n = bsz * seq
    depth = w_ada.shape[0]
    alpha = (2.0 * depth) ** 0.25
    n_cmp = (seq - CMP_BLOCK) // CMP_STRIDE + 1
    n_half = seq // CMP_STRIDE
    n_sel = seq // SEL_BLOCK
    cosk, sink = _rope_tables(seq)
    agg = _sel_aggregation(n_half, n_sel)
    x2 = x.reshape(n, d)
    for l in range(depth):
        mod = _ada(c, w_ada[l], b_ada[l])
        sh1, sc1, gt1, sh2, sc2, gt2 = [m.reshape(bsz, 1, d) for m in jnp.split(mod, 6, axis=-1)]

        wall = _fused_in_weight(w_in[l], w_merge[l])
        (q_t, kc_r, vc_r, ks, vs, kw, vw, gn, u, v, gate) = _proj(
            x2, sc1, sh1, wall, b_merge[l].reshape(1, -1), cosk, sink,
            sgu_ln_g[l].reshape(1, -1), sgu_ln_b[l].reshape(1, -1), seq)

        t16 = jnp.stack([_split_groups(kc_r, bsz, seq), _split_groups(vc_r, bsz, seq)])
        t16 = t16.reshape(2, bsz, NSA_KV_GROUPS, n_half, CMP_STRIDE * HEAD_DIM)
        cmp_kv = _compress(t16, cmp_pos[l].reshape(2, 1, CMP_BLOCK * HEAD_DIM), cmp_w1[l],
                           cmp_b1[l].reshape(2, 1, CMP_HIDDEN), cmp_w2[l], cmp_b2[l].reshape(2, 1, HEAD_DIM))

        front = ((0, 0), (0, 0), (WINDOW, 0), (0, 0))
        key_tiles = lambda a, tk: a.reshape(a.shape[:2] + (a.shape[2] // tk, tk, HEAD_DIM)).transpose(0, 1, 2, 4, 3)
        gn_t = gn.reshape(bsz, seq, NSA_KV_GROUPS, LANES)[..., :16].transpose(0, 2, 3, 1)
        ks4 = _split_groups(ks, bsz, seq)
        block_onehot = ((jnp.arange(seq)[:, None] // SEL_BLOCK) % GROUP_BLOCKS
                        == jnp.arange(GROUP_BLOCKS)[None, :]).astype(BF16)
        ks_ext = jnp.concatenate([ks4, jnp.broadcast_to(block_onehot, ks4.shape[:3] + (GROUP_BLOCKS,))], axis=-1)
        o_nsa_t = _nsa(q_t, cmp_kv[0], cmp_kv[1].transpose(0, 1, 3, 2),
                       ks_ext, key_tiles(_split_groups(vs, bsz, seq), TK),
                       jnp.pad(_split_groups(kw, bsz, seq), front),
                       key_tiles(jnp.pad(_split_groups(vw, bsz, seq), front), TQ),
                       gn_t, agg.T, n_cmp)

        bs = jnp.repeat(sgu_b[l].T, SGU_GROUP_DIM, axis=1)
        x1, h2, qp = _merge(o_nsa_t, u, v, gate, x2, gt1, sc2, sh2,
                            sgu_w[l], bs, w_branch[l].astype(BF16), w_out[l].astype(BF16),
                            ln1_g[l].reshape(1, d), ln1_b[l].reshape(1, d), peer_wq[l].astype(BF16), seq, alpha)

        idx_tok, gate_t = _retrieve(qp, peer_keys[l])
        w_tok = _expert_act(idx_tok, h2, gate_t, _pack_table(peer_u[l]))
        ffn = _expert_out(idx_tok, w_tok, _pack_table(peer_v[l]))
        x2 = _final(x1, ffn.reshape(n, d), gt2, ln2_g[l].reshape(1, d), ln2_b[l].reshape(1, d), seq, alpha)
    return x2.reshape(bsz, seq, d)
```

```python
import functools

import jax
import jax.numpy as jnp
from jax import lax
from jax.experimental import pallas as pl
from jax.experimental.pallas import tpu as pltpu

D_MODEL = 1024
NSA_HEADS = 8
NSA_KV_GROUPS = 2
NSA_REP = NSA_HEADS // NSA_KV_GROUPS
HEAD_DIM = 64
NSA_WIDTH = NSA_HEADS * HEAD_DIM
KV_WIDTH = NSA_KV_GROUPS * HEAD_DIM
CMP_BLOCK = 32
CMP_STRIDE = 16
CMP_HIDDEN = 2 * HEAD_DIM
SEL_BLOCK = 64
SEL_TOPK = 16
WINDOW = 512
ROPE_THETA = 10000.0
SGU_GROUPS = 8
SGU_WIDTH = D_MODEL // 2
SGU_GROUP_DIM = SGU_WIDTH // SGU_GROUPS
SGU_CHUNK = 128
PEER_HEADS = 8
PEER_NKEYS = 128
PEER_EXPERTS = PEER_NKEYS * PEER_NKEYS
PEER_QDIM = 256
PEER_HALF = PEER_QDIM // 2
PEER_TOPK = 16
PEER_PAIRS = PEER_HEADS * PEER_TOPK
LN_EPS = 1e-5
NEG_INF = -1e30
FORCE_SCORE = 1e9

LANES = 128
MIB = 1024 * 1024
BF16 = jnp.bfloat16
F32 = jnp.float32

TM_PROJ = 256
TM_MERGE = 256
TM_LN = 512
TQ = 128
TK = 256
KV_GROUP = 4
GROUP_BLOCKS = TK * KV_GROUP // SEL_BLOCK
GATE_ROWS = 16
TC = 128
TB = 64

_OQ, _OQS = 0, 512
_OKC, _OKCS, _OKS, _OKSS, _OKW, _OKWS = 1024, 1152, 1280, 1408, 1536, 1664
_OVC, _OVS, _OVW = 1792, 1920, 2048
_OG = 2176
_OZ = 2432
_OM = 3456
_WCOLS = 5504


def _cparams(n_axes, vmem_mib):
    return pltpu.CompilerParams(
        dimension_semantics=("parallel",) * n_axes,
        vmem_limit_bytes=vmem_mib * MIB)


def _ln(x):
    mu = jnp.mean(x, axis=-1, keepdims=True)
    xc = x - mu
    var = jnp.mean(xc * xc, axis=-1, keepdims=True)
    return xc * lax.rsqrt(var + LN_EPS)


def _gelu(x):
    return 0.5 * x * (1.0 + lax.erf(x * (2.0 ** -0.5)))


def _dot(a, b):
    return jnp.dot(a, b, preferred_element_type=F32)


def _dot_nt(a, b):
    return lax.dot_general(a, b, (((1,), (1,)), ((), ())), preferred_element_type=F32)


def _ada_body(c_ref, w_ref, b_ref, o_ref):
    cv = c_ref[...]
    a = cv * jax.nn.sigmoid(cv)
    o_ref[...] = _dot(a.astype(BF16), w_ref[...].astype(BF16)) + b_ref[...]


def _ada(c, w, b):
    bsz, d = c.shape
    n = w.shape[1]
    tn = 1024
    return pl.pallas_call(
        _ada_body,
        grid=(n // tn,),
        in_specs=[pl.BlockSpec((bsz, d), lambda j: (0, 0)),
                  pl.BlockSpec((d, tn), lambda j: (0, j)),
                  pl.BlockSpec((1, tn), lambda j: (0, j))],
        out_specs=pl.BlockSpec((bsz, tn), lambda j: (0, j)),
        out_shape=jax.ShapeDtypeStruct((bsz, n), F32),
        compiler_params=_cparams(1, 32),
    )(c, w, b.reshape(1, n))


def _proj_body(x_ref, sc_ref, sh_ref, w_ref, bm_ref, cos_ref, sin_ref, sg_ref, sb_ref,
               q_ref, ks_ref, vs_ref, kw_ref, vw_ref, gn_ref, kc_ref, vc_ref, u_ref, v_ref, gate_ref,
               *, n_seq_tiles):
    h = _ln(x_ref[...]) * (1.0 + sc_ref[0]) + sh_ref[0]
    hb = h.astype(BF16)

    def proj(off, width):
        return _dot(hb, w_ref[:, off:off + width])

    cosk = cos_ref[...]
    sink = sin_ref[...]
    cosq = jnp.concatenate([cosk] * (NSA_WIDTH // LANES), axis=1)
    sinq = jnp.concatenate([sink] * (NSA_WIDTH // LANES), axis=1)
    q_ref[0] = (proj(_OQ, NSA_WIDTH) * cosq + proj(_OQS, NSA_WIDTH) * sinq).T
    kc_ref[...] = proj(_OKC, KV_WIDTH) * cosk + proj(_OKCS, KV_WIDTH) * sink
    vc_ref[...] = proj(_OVC, KV_WIDTH)
    tm = x_ref.shape[0]
    ks = proj(_OKS, KV_WIDTH) * cosk + proj(_OKSS, KV_WIDTH) * sink
    kw = proj(_OKW, KV_WIDTH) * cosk + proj(_OKWS, KV_WIDTH) * sink
    vs_t = proj(_OVS, KV_WIDTH).T
    vw_t = proj(_OVW, KV_WIDTH).T
    gn_t = jax.nn.sigmoid(proj(_OG, 2 * LANES)).T
    pos = (pl.program_id(0) % n_seq_tiles) * tm + lax.broadcasted_iota(jnp.int32, (tm, 1), 0)
    block_onehot = jnp.where((pos // SEL_BLOCK) % GROUP_BLOCKS
                             == lax.broadcasted_iota(jnp.int32, (1, GROUP_BLOCKS), 1), 1.0, 0.0)
    for g in range(NSA_KV_GROUPS):
        cols = slice(g * HEAD_DIM, (g + 1) * HEAD_DIM)
        ks_ref[0, g] = jnp.concatenate([ks[:, cols], block_onehot], axis=1).astype(BF16)
        kw_ref[0, g] = kw[:, cols].astype(BF16)
        vs_ref[0, g, 0] = vs_t[cols, :].astype(BF16)
        for c in range(tm // TQ):
            vw_ref[0, g, c] = vw_t[cols, c * TQ:(c + 1) * TQ].astype(BF16)
        gn_ref[0, g] = gn_t[g * LANES:g * LANES + GATE_ROWS, :]
    u_ref[...] = _gelu(proj(_OZ, SGU_WIDTH))
    v_ref[...] = _ln(_gelu(proj(_OZ + SGU_WIDTH, SGU_WIDTH))) * sg_ref[...] + sb_ref[...]
    gate_ref[...] = jax.nn.sigmoid(proj(_OM, 2 * D_MODEL) + bm_ref[...])


def _proj(x2, sc, sh, wall, bm, cosk, sink, sg, sb, seq):
    n, d = x2.shape
    tm = TM_PROJ
    nst = seq // tm
    row = lambda w: pl.BlockSpec((tm, w), lambda i: (i, 0))
    per_batch = pl.BlockSpec((1, 1, d), lambda i: (i // nst, 0, 0))
    const = lambda shp: pl.BlockSpec(shp, lambda i: (0,) * len(shp))
    pos = pl.BlockSpec((tm, LANES), lambda i: (i % nst, 0))
    outs = [(KV_WIDTH, F32), (KV_WIDTH, F32), (SGU_WIDTH, F32), (SGU_WIDTH, F32), (2 * D_MODEL, F32)]
    assert tm == TK and tm % TQ == 0
    bsz, g = n // seq, NSA_KV_GROUPS
    att = [((bsz, NSA_WIDTH, seq), (1, NSA_WIDTH, tm), lambda i: (i // nst, 0, i % nst), F32),
           ((bsz, g, seq, HEAD_DIM + GROUP_BLOCKS), (1, g, tm, HEAD_DIM + GROUP_BLOCKS),
            lambda i: (i // nst, 0, i % nst, 0), BF16),
           ((bsz, g, seq // TK, HEAD_DIM, TK), (1, g, 1, HEAD_DIM, TK), lambda i: (i // nst, 0, i % nst, 0, 0), BF16),
           ((bsz, g, seq, HEAD_DIM), (1, g, tm, HEAD_DIM), lambda i: (i // nst, 0, i % nst, 0), BF16),
           ((bsz, g, seq // TQ, HEAD_DIM, TQ), (1, g, tm // TQ, HEAD_DIM, TQ),
            lambda i: (i // nst, 0, i % nst, 0, 0), BF16),
           ((bsz, g, GATE_ROWS, seq), (1, g, GATE_ROWS, tm), lambda i: (i // nst, 0, 0, i % nst), F32)]
    return pl.pallas_call(
        functools.partial(_proj_body, n_seq_tiles=nst),
        grid=(n // tm,),
        in_specs=[row(d), per_batch, per_batch, const((d, _WCOLS)), const((1, 2 * D_MODEL)),
                  pos, pos, const((1, SGU_WIDTH)), const((1, SGU_WIDTH))],
        out_specs=[pl.BlockSpec(blk, imap) for _, blk, imap, _ in att] + [row(w) for w, _ in outs],
        out_shape=([jax.ShapeDtypeStruct(shp, dt) for shp, _, _, dt in att]
                   + [jax.ShapeDtypeStruct((n, w), dt) for w, dt in outs]),
        compiler_params=_cparams(1, 56),
    )(x2, sc, sh, wall, bm, cosk, sink, sg, sb)


def _cmp_body(t_ref, pos_ref, w1_ref, b1_ref, w2_ref, b2_ref, o_ref):
    half = (CMP_BLOCK // 2) * HEAD_DIM
    t = t_ref[0, 0, 0]
    pos = pos_ref[0]
    ta = (t + pos[:, :half]).astype(BF16)
    tb = (t + pos[:, half:]).astype(BF16)
    a = _dot(ta, w1_ref[0, :half, :].astype(BF16))
    b = _dot(tb, w1_ref[0, half:, :].astype(BF16))
    nrow = t.shape[0]
    b_next = pltpu.roll(b, nrow - 1, axis=0)
    hid = _gelu(a + b_next + b1_ref[0])
    out = _dot(hid.astype(BF16), w2_ref[0].astype(BF16)) + b2_ref[0]
    o_ref[0, 0, 0] = out.astype(o_ref.dtype)


def _compress(t16, pos, w1, b1, w2, b2):
    _, bsz, g, nr, dd = t16.shape
    per_kind = lambda shp: pl.BlockSpec((1,) + shp, lambda k, b, gg: (k,) + (0,) * len(shp))
    blk = lambda w: pl.BlockSpec((1, 1, 1, nr, w), lambda k, b, gg: (k, b, gg, 0, 0))
    return pl.pallas_call(
        _cmp_body,
        grid=(2, bsz, g),
        in_specs=[blk(dd), per_kind((1, CMP_BLOCK * HEAD_DIM)), per_kind((CMP_BLOCK * HEAD_DIM, CMP_HIDDEN)),
                  per_kind((1, CMP_HIDDEN)), per_kind((CMP_HIDDEN, HEAD_DIM)), per_kind((1, HEAD_DIM))],
        out_specs=blk(HEAD_DIM),
        out_shape=jax.ShapeDtypeStruct((2, bsz, g, nr, HEAD_DIM), BF16),
        compiler_params=_cparams(3, 48),
    )(t16, pos, w1, b1, w2, b2)


def _scores(q_all, kt, bias):
    s = _dot(kt, q_all)
    return s if bias is None else s + bias


def _row_max(m, q_all, kt, bias=None):
    return jnp.maximum(m, jnp.max(_scores(q_all, kt, bias), axis=0, keepdims=True))


def _accumulate(state, m, q_all, kt, vt_t, bias=None):
    l, acc = state
    p = jnp.exp(_scores(q_all, kt, bias) - m)
    return l + jnp.sum(p, axis=0, keepdims=True), acc + _dot(vt_t, p.astype(BF16))


def _max_init():
    return jnp.full((1, NSA_REP * TQ), NEG_INF, F32)


def _acc_init():
    w = NSA_REP * TQ
    return jnp.zeros((1, w), F32), jnp.zeros((HEAD_DIM, w), F32)


def _lane_tile(a):
    return jnp.concatenate([a] * NSA_REP, axis=1)


def _nsa_body(q_ref, kc_ref, vc_ref, ks_ref, vs_ref, kw_ref, vw_ref, gn_ref, agg_ref, o_ref, selneg_ref, score_ref, *, n_cmp):
    qi = pl.program_id(2)
    q0 = qi * TQ
    t1 = q0 + lax.broadcasted_iota(jnp.int32, (1, TQ), 1)
    t = _lane_tile(t1)
    q_all = jnp.concatenate([q_ref[0, r * HEAD_DIM:(r + 1) * HEAD_DIM, :] for r in range(NSA_REP)],
                            axis=1).astype(BF16)

    kcm = kc_ref[0, 0]
    ncp = kcm.shape[0]
    nidx = lax.broadcasted_iota(jnp.int32, (ncp, 1), 0)
    mask_c = (nidx * CMP_STRIDE + (CMP_BLOCK - 1) <= t) & (nidx < n_cmp)
    sm = jnp.where(mask_c, _dot(kcm, q_all), NEG_INF)
    e = jnp.where(mask_c, jnp.exp(sm - jnp.max(sm, axis=0, keepdims=True)), 0.0)
    l = jnp.sum(e, axis=0, keepdims=True)
    p = e * (1.0 / jnp.where(l > 0.0, l, 1.0))
    o_c = _dot(vc_ref[0, 0], p.astype(BF16))
    psum = p[:, :TQ]
    for r in range(1, NSA_REP):
        psum = psum + p[:, r * TQ:(r + 1) * TQ]

    p_hi = psum.astype(BF16)
    p_lo = (psum - p_hi.astype(F32)).astype(BF16)
    agg_t = agg_ref[...]
    imp = _dot(agg_t, p_hi) + _dot(agg_t, p_lo)
    n_sel = imp.shape[0]
    blk = lax.broadcasted_iota(jnp.int32, (n_sel, 1), 0)
    blk_f = blk.astype(F32)
    cur = t1 // SEL_BLOCK
    forced = (blk == 0) | (blk == cur) | (blk == cur - 1)
    score = jnp.where(forced, FORCE_SCORE, jnp.where(blk <= cur, imp, NEG_INF))
    sel = jnp.zeros((n_sel, TQ), jnp.bool_)
    for _ in range(min(SEL_TOPK, n_sel)):
        m = jnp.max(score, axis=0, keepdims=True)
        first = jnp.min(jnp.where(score == m, blk_f, float(n_sel)), axis=0, keepdims=True)
        hit = blk_f == first
        sel = sel | hit
        score = jnp.where(hit, -jnp.inf, score)
    selneg_ref[...] = _lane_tile(jnp.where(sel, 0.0, NEG_INF).astype(BF16))

    def q_ext(jg):
        rows = selneg_ref[pl.ds(pl.multiple_of(jg * GROUP_BLOCKS, GROUP_BLOCKS), GROUP_BLOCKS), :]
        return jnp.concatenate([q_all, rows], axis=0)

    krow_w = lax.broadcasted_iota(jnp.int32, (TQ, 1), 0)

    def win_chunk(c):
        kpos = q0 - WINDOW + c * TQ + krow_w
        dlt = t - kpos
        bias = jnp.where((dlt >= 0) & (dlt < WINDOW) & (kpos >= 0), 0.0, NEG_INF)
        start = jnp.maximum(q0 - WINDOW + c * TQ, 0)
        return kw_ref[0, 0, pl.ds(pl.multiple_of(start, TQ), TQ), :], bias

    n_win = WINDOW // TQ + 1
    m_w = _max_init()
    for c in range(n_win):
        kt, bias = win_chunk(c)
        m_w = _row_max(m_w, q_all, kt, bias)
    state = _acc_init()
    for c in range(n_win):
        kt, bias = win_chunk(c)
        state = _accumulate(state, m_w, q_all, kt, vw_ref[0, 0, jnp.maximum(qi + c - WINDOW // TQ, 0)], bias)
    o_w = state[1] / state[0]

    krow = lax.broadcasted_iota(jnp.int32, (TK, 1), 0)
    group_rows = TK * KV_GROUP

    def max_group(jg, m):
        k0 = pl.multiple_of(jg * group_rows, group_rows)
        s = _scores(q_ext(jg), ks_ref[0, 0, pl.ds(k0, group_rows), :], None)
        score_ref[pl.ds(k0, group_rows), :] = s
        return jnp.maximum(m, jnp.max(s, axis=0, keepdims=True))

    def max_tile(j, m):
        k0 = pl.multiple_of(j * TK, TK)
        bias = jnp.where(k0 + krow <= t, 0.0, NEG_INF)
        s = _scores(q_ext(j // KV_GROUP), ks_ref[0, 0, pl.ds(k0, TK), :], bias)
        score_ref[pl.ds(k0, TK), :] = s
        return jnp.maximum(m, jnp.max(s, axis=0, keepdims=True))

    def sum_rows(state, k0, rows, vt_t):
        l, acc = state
        p = jnp.exp(score_ref[pl.ds(k0, rows), :] - m_s)
        return l + jnp.sum(p, axis=0, keepdims=True), acc + _dot(vt_t, p.astype(BF16))

    def sum_group(jg, state):
        vt_t = jnp.concatenate([vs_ref[0, 0, jg * KV_GROUP + sub] for sub in range(KV_GROUP)], axis=1)
        return sum_rows(state, pl.multiple_of(jg * group_rows, group_rows), group_rows, vt_t)

    def sum_tile(j, state):
        return sum_rows(state, pl.multiple_of(j * TK, TK), TK, vs_ref[0, 0, j])

    n_full = q0 // group_rows
    n_kv = (q0 + TQ - 1) // TK + 1
    m_s = lax.fori_loop(0, n_full, max_group, _max_init())
    m_s = lax.fori_loop(n_full * KV_GROUP, n_kv, max_tile, m_s)
    state = lax.fori_loop(0, n_full, sum_group, _acc_init())
    state = lax.fori_loop(n_full * KV_GROUP, n_kv, sum_tile, state)
    o_s = state[1] / state[0]

    gn = gn_ref[0, 0]
    gate = [jnp.concatenate([gn[3 * r + c:3 * r + c + 1, :] for r in range(NSA_REP)], axis=1) for c in range(3)]
    o = gate[0] * o_c + gate[1] * o_s + gate[2] * o_w
    for r in range(NSA_REP):
        o_ref[0, r * HEAD_DIM:(r + 1) * HEAD_DIM, :] = o[:, r * TQ:(r + 1) * TQ]


def _nsa(q_t, kc, vc_t, ks, vs_t, kwp, vwp_t, gn_t, agg_t, n_cmp):
    bsz, _, seq = q_t.shape
    g = NSA_KV_GROUPS
    gw = NSA_REP * HEAD_DIM
    per_bg = lambda a: pl.BlockSpec((1, 1) + a.shape[2:], lambda b, gg, i: (b, gg) + (0,) * (a.ndim - 2))
    return pl.pallas_call(
        functools.partial(_nsa_body, n_cmp=n_cmp),
        grid=(bsz, g, seq // TQ),
        in_specs=[pl.BlockSpec((1, gw, TQ), lambda b, gg, i: (b, gg, i)),
                  per_bg(kc), per_bg(vc_t), per_bg(ks), per_bg(vs_t), per_bg(kwp), per_bg(vwp_t),
                  pl.BlockSpec((1, 1, gn_t.shape[2], TQ), lambda b, gg, i: (b, gg, 0, i)),
                  pl.BlockSpec(agg_t.shape, lambda b, gg, i: (0, 0))],
        out_specs=pl.BlockSpec((1, gw, TQ), lambda b, gg, i: (b, gg, i)),
        out_shape=jax.ShapeDtypeStruct((bsz, NSA_WIDTH, seq), F32),
        scratch_shapes=[pltpu.VMEM((agg_t.shape[0], NSA_REP * TQ), BF16),
                        pltpu.VMEM((seq, NSA_REP * TQ), F32)],
        compiler_params=_cparams(3, 56),
    )(q_t, kc, vc_t, ks, vs_t, kwp, vwp_t, gn_t, agg_t)


def _merge_body(on_ref, u_ref, v_ref, gate_ref, x_ref, gt_ref, sc_ref, sh_ref, ws_ref, bs_ref,
                wb_ref, wo_ref, g1_ref, b1_ref, wq_ref, x1_ref, h2_ref, qp_ref, *, alpha):
    tm = x_ref.shape[0]
    row = lax.broadcasted_iota(jnp.int32, (SGU_CHUNK, SGU_CHUNK), 0)
    col = lax.broadcasted_iota(jnp.int32, (SGU_CHUNK, SGU_CHUNK), 1)
    lane_group = lax.broadcasted_iota(jnp.int32, (1, SGU_WIDTH), 1) // SGU_GROUP_DIM
    mixed = []
    for c in range(tm // SGU_CHUNK):
        v = v_ref[c * SGU_CHUNK:(c + 1) * SGU_CHUNK, :]
        acc = jnp.zeros((SGU_CHUNK, SGU_WIDTH), F32)
        for g in range(SGU_GROUPS):
            wg = jnp.where(col <= row, ws_ref[g], 0.0).astype(BF16)
            vg = jnp.where(lane_group == g, v, 0.0).astype(BF16)
            acc = acc + _dot(wg, vg)
        mixed.append(acc + bs_ref[...])
    o_sgu = u_ref[...] * jnp.concatenate(mixed, axis=0)
    gate = gate_ref[...]
    merged = (gate[:, :D_MODEL] * _dot(on_ref[0].T.astype(BF16), wb_ref[0])
              + gate[:, D_MODEL:] * _dot(o_sgu.astype(BF16), wb_ref[1]))
    mix = _dot(merged.astype(BF16), wo_ref[...])
    x1 = _ln(alpha * x_ref[...] + gt_ref[0] * mix) * g1_ref[...] + b1_ref[...]
    x1_ref[...] = x1
    h2 = _ln(x1) * (1.0 + sc_ref[0]) + sh_ref[0]
    h2_ref[...] = h2
    qp_ref[...] = _dot(h2.astype(BF16), wq_ref[...])


def _merge(o_nsa, u, v, gate, x2, gt1, sc2, sh2, ws, bs, wb, wo, g1, b1, wq, seq, alpha):
    n, d = x2.shape
    tm = TM_MERGE
    nst = seq // tm
    row = lambda w: pl.BlockSpec((tm, w), lambda i: (i, 0))
    per_batch = pl.BlockSpec((1, 1, d), lambda i: (i // nst, 0, 0))
    const = lambda a: pl.BlockSpec(a.shape, lambda i: (0,) * a.ndim)
    nq = wq.shape[1]
    return pl.pallas_call(
        functools.partial(_merge_body, alpha=alpha),
        grid=(n // tm,),
        in_specs=[pl.BlockSpec((1, NSA_WIDTH, tm), lambda i: (i // nst, 0, i % nst)),
                  row(SGU_WIDTH), row(SGU_WIDTH), row(2 * d), row(d),
                  per_batch, per_batch, per_batch, const(ws), const(bs), const(wb), const(wo),
                  const(g1), const(b1), const(wq)],
        out_specs=[row(d), row(d), row(nq)],
        out_shape=[jax.ShapeDtypeStruct((n, d), F32), jax.ShapeDtypeStruct((n, d), F32),
                   jax.ShapeDtypeStruct((n, nq), F32)],
        compiler_params=_cparams(1, 56),
    )(o_nsa, u, v, gate, x2, gt1, sc2, sh2, ws, bs, wb, wo, g1, b1, wq)


def _topk_rows(s, k):
    nrow = s.shape[0]
    rid = lax.broadcasted_iota(jnp.int32, s.shape, 0).astype(F32)
    vals, rows = [], []
    for _ in range(k):
        m = jnp.max(s, axis=0, keepdims=True)
        first = jnp.min(jnp.where(s == m, rid, float(nrow)), axis=0, keepdims=True)
        s = jnp.where(rid == first, -jnp.inf, s)
        vals.append(m)
        rows.append(first)
    return vals, rows


def _candidate_blocks(k):
    todo = {(a, b) for a in range(k) for b in range(k) if (a + 1) * (b + 1) <= k}
    options = [(axis, fixed, start) for axis in (0, 1) for fixed in range(k) for start in range(0, k, 8)]
    cells = lambda axis, fixed, start: [((fixed, start + r) if axis == 0 else (start + r, fixed)) for r in range(8)]
    blocks = []
    while todo:
        best = max(options, key=lambda o: sum(c in todo for c in cells(*o)))
        owned = tuple(c in todo for c in cells(*best))
        todo -= set(cells(*best))
        blocks.append(best + (owned,))
    return tuple(blocks)


_CAND_BLOCKS = _candidate_blocks(PEER_TOPK)


def _retrieve_body(qp_ref, keys_ref, idx_ref, gate_ref, eid_ref):
    k = PEER_TOPK
    for h in range(PEER_HEADS):
        tops = []
        for c in range(2):
            off = (h * 2 + c) * PEER_HALF
            qh = qp_ref[:, off:off + PEER_HALF].astype(BF16)
            s = _dot_nt(keys_ref[h, c].astype(BF16), qh)
            tops.append(_topk_rows(s, k))
        (v1, i1), (v2, i2) = tops
        vm = (jnp.concatenate(v1, axis=0), jnp.concatenate(v2, axis=0))
        im = (jnp.concatenate(i1, axis=0), jnp.concatenate(i2, axis=0))
        sub = lax.broadcasted_iota(jnp.int32, (8, 1), 0)
        vals, eids, poss = [], [], []
        for axis, fixed, start, owned in _CAND_BLOCKS:
            if axis == 0:
                v = v1[fixed] + vm[1][start:start + 8]
                e = i1[fixed] * float(PEER_NKEYS) + im[1][start:start + 8]
                pos = fixed * k + start + sub
            else:
                v = vm[0][start:start + 8] + v2[fixed]
                e = im[0][start:start + 8] * float(PEER_NKEYS) + i2[fixed]
                pos = (start + sub) * k + fixed
            if not all(owned):
                keep = functools.reduce(jnp.logical_or, [sub == r for r in range(8) if owned[r]])
                v = jnp.where(keep, v, -jnp.inf)
                pos = jnp.where(keep, pos, k * k)
            vals.append(v)
            eids.append(e)
            poss.append(pos.astype(F32))
        cand = jnp.concatenate(vals, axis=0)
        cidx = jnp.concatenate(eids, axis=0)
        rid = jnp.concatenate(poss, axis=0)
        top_s, top_e = [], []
        for _ in range(k):
            m = jnp.max(cand, axis=0, keepdims=True)
            first = jnp.min(jnp.where(cand == m, rid, float(k * k)), axis=0, keepdims=True)
            hit = rid == first
            top_e.append(jnp.sum(jnp.where(hit, cidx, 0.0), axis=0, keepdims=True))
            cand = jnp.where(hit, -jnp.inf, cand)
            top_s.append(m)
        ts = jnp.concatenate(top_s, axis=0)
        e = jnp.exp(ts - ts[0:1])
        gate_ref[0, h * k:(h + 1) * k, :] = e / jnp.sum(e, axis=0, keepdims=True)
        eid_ref[h * k:(h + 1) * k, :] = jnp.concatenate(top_e, axis=0)
    idx_ref[...] = (eid_ref[...].T * float(ROW_SUB)).astype(jnp.int32)


def _retrieve(qp, keys):
    n = qp.shape[0]
    nt = n // TC
    out = pl.BlockSpec((1, PEER_PAIRS, TC), lambda i: (i, 0, 0))
    return pl.pallas_call(
        _retrieve_body,
        grid=(nt,),
        in_specs=[pl.BlockSpec((TC, qp.shape[1]), lambda i: (i, 0)),
                  pl.BlockSpec(keys.shape, lambda i: (0, 0, 0, 0))],
        out_specs=[pl.BlockSpec((TC, PEER_PAIRS), lambda i: (i, 0)), out],
        out_shape=[jax.ShapeDtypeStruct((n, PEER_PAIRS), jnp.int32),
                   jax.ShapeDtypeStruct((nt, PEER_PAIRS, TC), F32)],
        scratch_shapes=[pltpu.VMEM((PEER_PAIRS, TC), F32)],
        compiler_params=_cparams(1, 48),
    )(qp, keys)


ROW_SUB = D_MODEL // 2 // LANES
ACT_UNROLL = 8
OFF_REGS = 8


def _load_row(tab_ref, scaled_idx):
    row = tab_ref[pl.ds(pl.multiple_of(scaled_idx, ROW_SUB), ROW_SUB), :]
    lo = lax.bitcast_convert_type(lax.shift_left(row, jnp.uint32(16)), F32)
    hi = lax.bitcast_convert_type(row & jnp.uint32(0xFFFF0000), F32)
    return lo, hi


def _expert_act_body(off_ref, idx_ref, h_ref, gate_ref, tab_ref, w_ref, slot_ref, part_ref):
    off = [off_ref[j] for j in range(OFF_REGS)]

    def token(t, carry):
        x = h_ref[pl.ds(t, 1), :]
        chunk = lambda s: x[:, s * LANES:(s + 1) * LANES]
        xlo = jnp.concatenate([chunk(s) for s in range(ROW_SUB)], axis=0)
        xhi = jnp.concatenate([chunk(ROW_SUB + s) for s in range(ROW_SUB)], axis=0)
        base = pl.multiple_of(t * PEER_PAIRS, PEER_PAIRS)
        for p in range(PEER_PAIRS):
            if p % OFF_REGS == 0:
                idx_sub = idx_ref.at[t, pl.ds(p, OFF_REGS)]
            lo, hi = _load_row(tab_ref, idx_sub[off[p % OFF_REGS]])
            slot_ref[p * ROW_SUB:(p + 1) * ROW_SUB, :] = lo * xlo + hi * xhi
        part = slot_ref[pl.ds(0, PEER_PAIRS, stride=ROW_SUB), :]
        for s in range(1, ROW_SUB):
            part = part + slot_ref[pl.ds(s, PEER_PAIRS, stride=ROW_SUB), :]
        part_ref[pl.ds(pl.multiple_of(base, PEER_PAIRS), PEER_PAIRS), :] = part
        return carry

    lax.fori_loop(0, TC, token, 0)

    tok_lane = lax.broadcasted_iota(jnp.int32, (PEER_PAIRS, TC), 1)

    def lane_sums(i, act):
        for k in range(ACT_UNROLL):
            t = i * ACT_UNROLL + k
            part = part_ref[pl.ds(pl.multiple_of(t * PEER_PAIRS, PEER_PAIRS), PEER_PAIRS), :]
            act = jnp.where(tok_lane == t, jnp.sum(part, axis=1, keepdims=True), act)
        return act

    act = lax.fori_loop(0, TC // ACT_UNROLL, lane_sums, jnp.zeros((PEER_PAIRS, TC), F32))
    w_ref[...] = (gate_ref[0] * _gelu(act)).T


def _expert_act(idx_tok, h2, gate_t, tab):
    nt = gate_t.shape[0]
    tile = pl.BlockSpec((1, PEER_PAIRS, TC), lambda i: (i, 0, 0))
    return pl.pallas_call(
        _expert_act_body,
        grid=(nt,),
        in_specs=[pl.BlockSpec(memory_space=pltpu.SMEM),
                  pl.BlockSpec((TC, PEER_PAIRS), lambda i: (i, 0), memory_space=pltpu.SMEM),
                  pl.BlockSpec((TC, h2.shape[1]), lambda i: (i, 0)),
                  tile,
                  pl.BlockSpec(tab.shape, lambda i: (0, 0), pipeline_mode=pl.Buffered(1))],
        out_specs=pl.BlockSpec((TC, PEER_PAIRS), lambda i: (i, 0)),
        out_shape=jax.ShapeDtypeStruct((nt * TC, PEER_PAIRS), F32),
        scratch_shapes=[pltpu.VMEM((PEER_PAIRS * ROW_SUB, LANES), F32),
                        pltpu.VMEM((TC * PEER_PAIRS, LANES), F32)],
        compiler_params=_cparams(1, 52),
    )(jnp.arange(OFF_REGS, dtype=jnp.int32), idx_tok, h2, gate_t, tab)


def _expert_out_body(off_ref, idx_ref, w_ref, tab_ref, o_ref):
    n_acc = 4
    off = [off_ref[j] for j in range(OFF_REGS)]

    def token(t, carry):
        acc_lo = [jnp.zeros((ROW_SUB, LANES), F32) for _ in range(n_acc)]
        acc_hi = [jnp.zeros((ROW_SUB, LANES), F32) for _ in range(n_acc)]
        for p in range(PEER_PAIRS):
            if p % OFF_REGS == 0:
                idx_sub = idx_ref.at[t, pl.ds(p, OFF_REGS)]
                w_sub = w_ref.at[t, pl.ds(p, OFF_REGS)]
            lo, hi = _load_row(tab_ref, idx_sub[off[p % OFF_REGS]])
            w = w_sub[off[p % OFF_REGS]]
            acc_lo[p % n_acc] = acc_lo[p % n_acc] + w * lo
            acc_hi[p % n_acc] = acc_hi[p % n_acc] + w * hi
        lo = (acc_lo[0] + acc_lo[1]) + (acc_lo[2] + acc_lo[3])
        hi = (acc_hi[0] + acc_hi[1]) + (acc_hi[2] + acc_hi[3])
        o_ref[t] = jnp.concatenate([lo, hi], axis=0)
        return carry

    lax.fori_loop(0, TB, token, 0)


def _expert_out(idx_tok, w_tok, tab):
    n = idx_tok.shape[0]
    smem = pl.BlockSpec((TB, PEER_PAIRS), lambda i: (i, 0), memory_space=pltpu.SMEM)
    return pl.pallas_call(
        _expert_out_body,
        grid=(n // TB,),
        in_specs=[pl.BlockSpec(memory_space=pltpu.SMEM), smem, smem,
                  pl.BlockSpec(tab.shape, lambda i: (0, 0), pipeline_mode=pl.Buffered(1))],
        out_specs=pl.BlockSpec((TB, D_MODEL // LANES, LANES), lambda i: (i, 0, 0)),
        out_shape=jax.ShapeDtypeStruct((n, D_MODEL // LANES, LANES), F32),
        compiler_params=_cparams(1, 48),
    )(jnp.arange(OFF_REGS, dtype=jnp.int32), idx_tok, w_tok, tab)


def _final_body(x1_ref, f_ref, gt_ref, g_ref, b_ref, o_ref, *, alpha):
    o_ref[...] = _ln(alpha * x1_ref[...] + gt_ref[0] * f_ref[...]) * g_ref[...] + b_ref[...]


def _final(x1, ffn, gt2, g2, b2, seq, alpha):
    n, d = x1.shape
    tm = TM_LN
    nst = seq // tm
    row = pl.BlockSpec((tm, d), lambda i: (i, 0))
    const = pl.BlockSpec((1, d), lambda i: (0, 0))
    return pl.pallas_call(
        functools.partial(_final_body, alpha=alpha),
        grid=(n // tm,),
        in_specs=[row, row, pl.BlockSpec((1, 1, d), lambda i: (i // nst, 0, 0)), const, const],
        out_specs=row,
        out_shape=jax.ShapeDtypeStruct((n, d), F32),
        compiler_params=_cparams(1, 32),
    )(x1, ffn, gt2, g2, b2)


def _swap_halves(w, heads):
    d = w.shape[0]
    w4 = w.reshape(d, heads, 2, HEAD_DIM // 2)
    return jnp.flip(w4, axis=2).reshape(d, heads * HEAD_DIM)


def _fused_in_weight(w_in, w_merge):
    d = w_in.shape[0]
    o = 0
    parts = {}
    for name, width in (("q", NSA_WIDTH), ("kc", KV_WIDTH), ("vc", KV_WIDTH), ("ks", KV_WIDTH),
                        ("vs", KV_WIDTH), ("kw", KV_WIDTH), ("vw", KV_WIDTH), ("g", 3 * NSA_HEADS),
                        ("z", 2 * SGU_WIDTH)):
        parts[name] = w_in[:, o:o + width]
        o += width
    wq = parts["q"] * (HEAD_DIM ** -0.5)
    gcols = 3 * NSA_REP
    wg = jnp.zeros((d, 2 * LANES), w_in.dtype)
    wg = wg.at[:, :gcols].set(parts["g"][:, :gcols]).at[:, LANES:LANES + gcols].set(parts["g"][:, gcols:])
    g = NSA_KV_GROUPS
    cols = [wq, _swap_halves(wq, NSA_HEADS),
            parts["kc"], _swap_halves(parts["kc"], g), parts["ks"], _swap_halves(parts["ks"], g),
            parts["kw"], _swap_halves(parts["kw"], g), parts["vc"], parts["vs"], parts["vw"],
            wg, parts["z"], w_merge]
    return jnp.concatenate(cols, axis=1).astype(BF16)


def _rope_tables(seq):
    half = HEAD_DIM // 2
    pos = jnp.arange(seq, dtype=F32)
    inv_freq = ROPE_THETA ** (-jnp.arange(half, dtype=F32) / half)
    ang = pos[:, None] * inv_freq[None, :]
    cos, sin = jnp.cos(ang), jnp.sin(ang)
    reps = LANES // HEAD_DIM
    cosk = jnp.tile(jnp.concatenate([cos, cos], axis=1), (1, reps))
    sink = jnp.tile(jnp.concatenate([-sin, sin], axis=1), (1, reps))
    return cosk, sink


def _sel_aggregation(n_cmp_pad, n_sel):
    c0 = jnp.arange(n_cmp_pad)[:, None] * CMP_STRIDE
    s0 = jnp.arange(n_sel)[None, :] * SEL_BLOCK
    ov = jnp.clip(jnp.minimum(c0 + CMP_BLOCK, s0 + SEL_BLOCK) - jnp.maximum(c0, s0), 0, None)
    return (ov / CMP_BLOCK).astype(BF16)


def _pack_table(tab):
    half = tab.shape[1] // 2
    bits = lax.bitcast_convert_type(tab.astype(BF16), jnp.uint16).astype(jnp.uint32)
    packed = (bits[:, half:] << 16) | bits[:, :half]
    return packed.reshape(tab.shape[0] * ROW_SUB, LANES)


def _split_groups(a, bsz, seq):
    return a.reshape(bsz, seq, NSA_KV_GROUPS, HEAD_DIM).transpose(0, 2, 1, 3)


def kernel(x, c, w_ada, b_ada, w_in, cmp_pos, cmp_w1, cmp_b1, cmp_w2, cmp_b2, sgu_ln_g, sgu_ln_b, sgu_w, sgu_b, w_branch, w_merge, b_merge, w_out, ln1_g, ln1_b, peer_wq, peer_keys, peer_u, peer_v, ln2_g, ln2_b):
    bsz, seq, d = x.shape
    n = bsz * seq
    depth = w_ada.shape[0]
    alpha = (2.0 * depth) ** 0.25
    n_cmp = (seq - CMP_BLOCK) // CMP_STRIDE + 1
    n_half = seq // CMP_STRIDE
    n_sel = seq // SEL_BLOCK
    cosk, sink = _rope_tables(seq)
    agg = _sel_aggregation(n_half, n_sel)
    x2 = x.reshape(n, d)
    for l in range(depth):
        mod = _ada(c, w_ada[l], b_ada[l])
        sh1, sc1, gt1, sh2, sc2, gt2 = [m.reshape(bsz, 1, d) for m in jnp.split(mod, 6, axis=-1)]

        wall = _fused_in_weight(w_in[l], w_merge[l])
        (q_t, ks_ext, vs_t, kw_g, vw_t, gn_t, kc_r, vc_r, u, v, gate) = _proj(
            x2, sc1, sh1, wall, b_merge[l].reshape(1, -1), cosk, sink,
            sgu_ln_g[l].reshape(1, -1), sgu_ln_b[l].reshape(1, -1), seq)

        t16 = jnp.stack([_split_groups(kc_r, bsz, seq), _split_groups(vc_r, bsz, seq)])
        t16 = t16.reshape(2, bsz, NSA_KV_GROUPS, n_half, CMP_STRIDE * HEAD_DIM)
        cmp_kv = _compress(t16, cmp_pos[l].reshape(2, 1, CMP_BLOCK * HEAD_DIM), cmp_w1[l],
                           cmp_b1[l].reshape(2, 1, CMP_HIDDEN), cmp_w2[l], cmp_b2[l].reshape(2, 1, HEAD_DIM))

        o_nsa_t = _nsa(q_t, cmp_kv[0], cmp_kv[1].transpose(0, 1, 3, 2), ks_ext, vs_t, kw_g, vw_t, gn_t, agg.T, n_cmp)

        bs = jnp.repeat(sgu_b[l].T, SGU_GROUP_DIM, axis=1)
        x1, h2, qp = _merge(o_nsa_t, u, v, gate, x2, gt1, sc2, sh2,
                            sgu_w[l], bs, w_branch[l].astype(BF16), w_out[l].astype(BF16),
                            ln1_g[l].reshape(1, d), ln1_b[l].reshape(1, d), peer_wq[l].astype(BF16), seq, alpha)

        idx_tok, gate_t = _retrieve(qp, peer_keys[l])
        w_tok = _expert_act(idx_tok, h2, gate_t, _pack_table(peer_u[l]))
        ffn = _expert_out(idx_tok, w_tok, _pack_table(peer_v[l]))
        x2 = _final(x1, ffn.reshape(n, d), gt2, ln2_g[l].reshape(1, d), ln2_b[l].reshape(1, d), seq, alpha)
    return x2.reshape(bsz, seq, d)
```

```python
import functools

import jax
import jax.numpy as jnp
from jax import lax
from jax.experimental import pallas as pl
from jax.experimental.pallas import tpu as pltpu

D_MODEL = 1024
NSA_HEADS = 8
NSA_KV_GROUPS = 2
NSA_REP = NSA_HEADS // NSA_KV_GROUPS
HEAD_DIM = 64
NSA_WIDTH = NSA_HEADS * HEAD_DIM
KV_WIDTH = NSA_KV_GROUPS * HEAD_DIM
CMP_BLOCK = 32
CMP_STRIDE = 16
CMP_HIDDEN = 2 * HEAD_DIM
SEL_BLOCK = 64
SEL_TOPK = 16
WINDOW = 512
ROPE_THETA = 10000.0
SGU_GROUPS = 8
SGU_WIDTH = D_MODEL // 2
SGU_GROUP_DIM = SGU_WIDTH // SGU_GROUPS
SGU_CHUNK = 128
PEER_HEADS = 8
PEER_NKEYS = 128
PEER_EXPERTS = PEER_NKEYS * PEER_NKEYS
PEER_QDIM = 256
PEER_HALF = PEER_QDIM // 2
PEER_TOPK = 16
PEER_PAIRS = PEER_HEADS * PEER_TOPK
LN_EPS = 1e-5
NEG_INF = -1e30
FORCE_SCORE = 1e9

LANES = 128
MIB = 1024 * 1024
BF16 = jnp.bfloat16
F32 = jnp.float32

TM_PROJ = 256
TM_MERGE = 256
TM_LN = 512
TQ = 128
TK = 256
KV_GROUP = 4
GROUP_BLOCKS = TK * KV_GROUP // SEL_BLOCK
GATE_ROWS = 16
TC = 128
TB = 128

_OQ, _OQS = 0, 512
_OKC, _OKCS, _OKS, _OKSS, _OKW, _OKWS = 1024, 1152, 1280, 1408, 1536, 1664
_OVC, _OVS, _OVW = 1792, 1920, 2048
_OG = 2176
_OZ = 2432
_OM = 3456
_WCOLS = 5504


def _cparams(n_axes, vmem_mib):
    return pltpu.CompilerParams(
        dimension_semantics=("parallel",) * n_axes,
        vmem_limit_bytes=vmem_mib * MIB)


def _ln(x):
    mu = jnp.mean(x, axis=-1, keepdims=True)
    xc = x - mu
    var = jnp.mean(xc * xc, axis=-1, keepdims=True)
    return xc * lax.rsqrt(var + LN_EPS)


def _gelu(x):
    return 0.5 * x * (1.0 + lax.erf(x * (2.0 ** -0.5)))


def _dot(a, b):
    return jnp.dot(a, b, preferred_element_type=F32)


def _dot_nt(a, b):
    return lax.dot_general(a, b, (((1,), (1,)), ((), ())), preferred_element_type=F32)


def _ada_body(c_ref, w_ref, b_ref, o_ref):
    cv = c_ref[...]
    a = cv * jax.nn.sigmoid(cv)
    o_ref[...] = _dot(a.astype(BF16), w_ref[...].astype(BF16)) + b_ref[...]


def _ada(c, w, b):
    bsz, d = c.shape
    n = w.shape[1]
    tn = 1024
    return pl.pallas_call(
        _ada_body,
        grid=(n // tn,),
        in_specs=[pl.BlockSpec((bsz, d), lambda j: (0, 0)),
                  pl.BlockSpec((d, tn), lambda j: (0, j)),
                  pl.BlockSpec((1, tn), lambda j: (0, j))],
        out_specs=pl.BlockSpec((bsz, tn), lambda j: (0, j)),
        out_shape=jax.ShapeDtypeStruct((bsz, n), F32),
        compiler_params=_cparams(1, 32),
    )(c, w, b.reshape(1, n))


def _proj_body(x_ref, sc_ref, sh_ref, w_ref, bm_ref, cos_ref, sin_ref, sg_ref, sb_ref,
               q_ref, ks_ref, vs_ref, kw_ref, vw_ref, gn_ref, kc_ref, vc_ref, u_ref, v_ref, gate_ref,
               *, n_seq_tiles):
    h = _ln(x_ref[...]) * (1.0 + sc_ref[0]) + sh_ref[0]
    hb = h.astype(BF16)

    def proj(off, width):
        return _dot(hb, w_ref[:, off:off + width])

    cosk = cos_ref[...]
    sink = sin_ref[...]
    cosq = jnp.concatenate([cosk] * (NSA_WIDTH // LANES), axis=1)
    sinq = jnp.concatenate([sink] * (NSA_WIDTH // LANES), axis=1)
    q_ref[0] = (proj(_OQ, NSA_WIDTH) * cosq + proj(_OQS, NSA_WIDTH) * sinq).T
    kc_ref[...] = proj(_OKC, KV_WIDTH) * cosk + proj(_OKCS, KV_WIDTH) * sink
    vc_ref[...] = proj(_OVC, KV_WIDTH)
    tm = x_ref.shape[0]
    ks = proj(_OKS, KV_WIDTH) * cosk + proj(_OKSS, KV_WIDTH) * sink
    kw = proj(_OKW, KV_WIDTH) * cosk + proj(_OKWS, KV_WIDTH) * sink
    vs_t = proj(_OVS, KV_WIDTH).T
    vw_t = proj(_OVW, KV_WIDTH).T
    gn_t = jax.nn.sigmoid(proj(_OG, 2 * LANES)).T
    pos = (pl.program_id(0) % n_seq_tiles) * tm + lax.broadcasted_iota(jnp.int32, (tm, 1), 0)
    block_onehot = jnp.where((pos // SEL_BLOCK) % GROUP_BLOCKS
                             == lax.broadcasted_iota(jnp.int32, (1, GROUP_BLOCKS), 1), 1.0, 0.0)
    for g in range(NSA_KV_GROUPS):
        cols = slice(g * HEAD_DIM, (g + 1) * HEAD_DIM)
        ks_ref[0, g] = jnp.concatenate([ks[:, cols], block_onehot], axis=1).astype(BF16)
        kw_ref[0, g] = kw[:, cols].astype(BF16)
        vs_ref[0, g, 0] = vs_t[cols, :].astype(BF16)
        for c in range(tm // TQ):
            vw_ref[0, g, c] = vw_t[cols, c * TQ:(c + 1) * TQ].astype(BF16)
        gn_ref[0, g] = gn_t[g * LANES:g * LANES + GATE_ROWS, :]
    u_ref[...] = _gelu(proj(_OZ, SGU_WIDTH))
    v_ref[...] = _ln(_gelu(proj(_OZ + SGU_WIDTH, SGU_WIDTH))) * sg_ref[...] + sb_ref[...]
    gate_ref[...] = jax.nn.sigmoid(proj(_OM, 2 * D_MODEL) + bm_ref[...])


def _proj(x2, sc, sh, wall, bm, cosk, sink, sg, sb, seq):
    n, d = x2.shape
    tm = TM_PROJ
    nst = seq // tm
    row = lambda w: pl.BlockSpec((tm, w), lambda i: (i, 0))
    per_batch = pl.BlockSpec((1, 1, d), lambda i: (i // nst, 0, 0))
    const = lambda shp: pl.BlockSpec(shp, lambda i: (0,) * len(shp))
    pos = pl.BlockSpec((tm, LANES), lambda i: (i % nst, 0))
    outs = [(KV_WIDTH, F32), (KV_WIDTH, F32), (SGU_WIDTH, F32), (SGU_WIDTH, F32), (2 * D_MODEL, F32)]
    assert tm == TK and tm % TQ == 0
    bsz, g = n // seq, NSA_KV_GROUPS
    att = [((bsz, NSA_WIDTH, seq), (1, NSA_WIDTH, tm), lambda i: (i // nst, 0, i % nst), F32),
           ((bsz, g, seq, HEAD_DIM + GROUP_BLOCKS), (1, g, tm, HEAD_DIM + GROUP_BLOCKS),
            lambda i: (i // nst, 0, i % nst, 0), BF16),
           ((bsz, g, seq // TK, HEAD_DIM, TK), (1, g, 1, HEAD_DIM, TK), lambda i: (i // nst, 0, i % nst, 0, 0), BF16),
           ((bsz, g, seq, HEAD_DIM), (1, g, tm, HEAD_DIM), lambda i: (i // nst, 0, i % nst, 0), BF16),
           ((bsz, g, seq // TQ, HEAD_DIM, TQ), (1, g, tm // TQ, HEAD_DIM, TQ),
            lambda i: (i // nst, 0, i % nst, 0, 0), BF16),
           ((bsz, g, GATE_ROWS, seq), (1, g, GATE_ROWS, tm), lambda i: (i // nst, 0, 0, i % nst), F32)]
    return pl.pallas_call(
        functools.partial(_proj_body, n_seq_tiles=nst),
        grid=(n // tm,),
        in_specs=[row(d), per_batch, per_batch, const((d, _WCOLS)), const((1, 2 * D_MODEL)),
                  pos, pos, const((1, SGU_WIDTH)), const((1, SGU_WIDTH))],
        out_specs=[pl.BlockSpec(blk, imap) for _, blk, imap, _ in att] + [row(w) for w, _ in outs],
        out_shape=([jax.ShapeDtypeStruct(shp, dt) for shp, _, _, dt in att]
                   + [jax.ShapeDtypeStruct((n, w), dt) for w, dt in outs]),
        compiler_params=_cparams(1, 56),
    )(x2, sc, sh, wall, bm, cosk, sink, sg, sb)


def _cmp_body(t_ref, pos_ref, w1_ref, b1_ref, w2_ref, b2_ref, o_ref):
    half = (CMP_BLOCK // 2) * HEAD_DIM
    t = t_ref[0, 0, 0]
    pos = pos_ref[0]
    ta = (t + pos[:, :half]).astype(BF16)
    tb = (t + pos[:, half:]).astype(BF16)
    a = _dot(ta, w1_ref[0, :half, :].astype(BF16))
    b = _dot(tb, w1_ref[0, half:, :].astype(BF16))
    nrow = t.shape[0]
    b_next = pltpu.roll(b, nrow - 1, axis=0)
    hid = _gelu(a + b_next + b1_ref[0])
    out = _dot(hid.astype(BF16), w2_ref[0].astype(BF16)) + b2_ref[0]
    o_ref[0, 0, 0] = out.astype(o_ref.dtype)


def _compress(t16, pos, w1, b1, w2, b2):
    _, bsz, g, nr, dd = t16.shape
    per_kind = lambda shp: pl.BlockSpec((1,) + shp, lambda k, b, gg: (k,) + (0,) * len(shp))
    blk = lambda w: pl.BlockSpec((1, 1, 1, nr, w), lambda k, b, gg: (k, b, gg, 0, 0))
    return pl.pallas_call(
        _cmp_body,
        grid=(2, bsz, g),
        in_specs=[blk(dd), per_kind((1, CMP_BLOCK * HEAD_DIM)), per_kind((CMP_BLOCK * HEAD_DIM, CMP_HIDDEN)),
                  per_kind((1, CMP_HIDDEN)), per_kind((CMP_HIDDEN, HEAD_DIM)), per_kind((1, HEAD_DIM))],
        out_specs=blk(HEAD_DIM),
        out_shape=jax.ShapeDtypeStruct((2, bsz, g, nr, HEAD_DIM), BF16),
        compiler_params=_cparams(3, 48),
    )(t16, pos, w1, b1, w2, b2)


def _scores(q_all, kt, bias):
    s = _dot(kt, q_all)
    return s if bias is None else s + bias


def _max_init():
    return jnp.full((1, NSA_REP * TQ), NEG_INF, F32)


def _acc_init():
    w = NSA_REP * TQ
    return jnp.zeros((1, w), F32), jnp.zeros((HEAD_DIM, w), F32)


def _lane_tile(a):
    return jnp.concatenate([a] * NSA_REP, axis=1)


def _nsa_body(q_ref, kc_ref, vc_ref, ks_ref, vs_ref, kw_ref, vw_ref, gn_ref, agg_ref, o_ref, selneg_ref, score_ref, *, n_cmp):
    qi = pl.program_id(2)
    q0 = qi * TQ
    t1 = q0 + lax.broadcasted_iota(jnp.int32, (1, TQ), 1)
    t = _lane_tile(t1)
    q_all = jnp.concatenate([q_ref[0, r * HEAD_DIM:(r + 1) * HEAD_DIM, :] for r in range(NSA_REP)],
                            axis=1).astype(BF16)

    kcm = kc_ref[0, 0]
    ncp = kcm.shape[0]
    nidx = lax.broadcasted_iota(jnp.int32, (ncp, 1), 0)
    mask_c = (nidx * CMP_STRIDE + (CMP_BLOCK - 1) <= t) & (nidx < n_cmp)
    sm = jnp.where(mask_c, _dot(kcm, q_all), NEG_INF)
    e = jnp.where(mask_c, jnp.exp(sm - jnp.max(sm, axis=0, keepdims=True)), 0.0)
    l = jnp.sum(e, axis=0, keepdims=True)
    p = e * (1.0 / jnp.where(l > 0.0, l, 1.0))
    o_c = _dot(vc_ref[0, 0], p.astype(BF16))
    psum = p[:, :TQ]
    for r in range(1, NSA_REP):
        psum = psum + p[:, r * TQ:(r + 1) * TQ]

    p_hi = psum.astype(BF16)
    p_lo = (psum - p_hi.astype(F32)).astype(BF16)
    agg_t = agg_ref[...]
    imp = _dot(agg_t, p_hi) + _dot(agg_t, p_lo)
    n_sel = imp.shape[0]
    blk = lax.broadcasted_iota(jnp.int32, (n_sel, 1), 0)
    blk_f = blk.astype(F32)
    cur = t1 // SEL_BLOCK
    forced = (blk == 0) | (blk == cur) | (blk == cur - 1)
    score = jnp.where(forced, FORCE_SCORE, jnp.where(blk <= cur, imp, NEG_INF))
    sel = jnp.zeros((n_sel, TQ), jnp.bool_)
    for _ in range(min(SEL_TOPK, n_sel)):
        m = jnp.max(score, axis=0, keepdims=True)
        first = jnp.min(jnp.where(score == m, blk_f, float(n_sel)), axis=0, keepdims=True)
        hit = blk_f == first
        sel = sel | hit
        score = jnp.where(hit, -jnp.inf, score)
    selneg_ref[...] = _lane_tile(jnp.where(sel, 0.0, NEG_INF).astype(BF16))

    def q_ext(jg):
        rows = selneg_ref[pl.ds(pl.multiple_of(jg * GROUP_BLOCKS, GROUP_BLOCKS), GROUP_BLOCKS), :]
        return jnp.concatenate([q_all, rows], axis=0)

    krow_w = lax.broadcasted_iota(jnp.int32, (TQ, 1), 0)

    def win_chunk(c):
        kpos = q0 - WINDOW + c * TQ + krow_w
        dlt = t - kpos
        bias = jnp.where((dlt >= 0) & (dlt < WINDOW) & (kpos >= 0), 0.0, NEG_INF)
        start = jnp.maximum(q0 - WINDOW + c * TQ, 0)
        return kw_ref[0, 0, pl.ds(pl.multiple_of(start, TQ), TQ), :], bias

    n_win = WINDOW // TQ + 1
    s_w = [_scores(q_all, *win_chunk(c)) for c in range(n_win)]
    m_w = functools.reduce(jnp.maximum, [jnp.max(s, axis=0, keepdims=True) for s in s_w])
    l_w, acc_w = _acc_init()
    for c in range(n_win):
        p = jnp.exp(s_w[c] - m_w)
        l_w = l_w + jnp.sum(p, axis=0, keepdims=True)
        acc_w = acc_w + _dot(vw_ref[0, 0, jnp.maximum(qi + c - WINDOW // TQ, 0)], p.astype(BF16))
    o_w = acc_w / l_w

    krow = lax.broadcasted_iota(jnp.int32, (TK, 1), 0)
    group_rows = TK * KV_GROUP

    def max_group(jg, m):
        k0 = pl.multiple_of(jg * group_rows, group_rows)
        s = _scores(q_ext(jg), ks_ref[0, 0, pl.ds(k0, group_rows), :], None)
        score_ref[pl.ds(k0, group_rows), :] = s
        return jnp.maximum(m, jnp.max(s, axis=0, keepdims=True))

    def max_tile(j, m):
        k0 = pl.multiple_of(j * TK, TK)
        bias = jnp.where(k0 + krow <= t, 0.0, NEG_INF)
        s = _scores(q_ext(j // KV_GROUP), ks_ref[0, 0, pl.ds(k0, TK), :], bias)
        score_ref[pl.ds(k0, TK), :] = s
        return jnp.maximum(m, jnp.max(s, axis=0, keepdims=True))

    def sum_rows(state, k0, rows, vt_t):
        l, acc = state
        p = jnp.exp(score_ref[pl.ds(k0, rows), :] - m_s)
        return l + jnp.sum(p, axis=0, keepdims=True), acc + _dot(vt_t, p.astype(BF16))

    def sum_group(jg, state):
        vt_t = jnp.concatenate([vs_ref[0, 0, jg * KV_GROUP + sub] for sub in range(KV_GROUP)], axis=1)
        return sum_rows(state, pl.multiple_of(jg * group_rows, group_rows), group_rows, vt_t)

    def sum_tile(j, state):
        return sum_rows(state, pl.multiple_of(j * TK, TK), TK, vs_ref[0, 0, j])

    n_full = q0 // group_rows
    n_kv = (q0 + TQ - 1) // TK + 1
    m_s = lax.fori_loop(0, n_full, max_group, _max_init())
    m_s = lax.fori_loop(n_full * KV_GROUP, n_kv, max_tile, m_s)
    state = lax.fori_loop(0, n_full, sum_group, _acc_init())
    state = lax.fori_loop(n_full * KV_GROUP, n_kv, sum_tile, state)
    o_s = state[1] / state[0]

    gn = gn_ref[0, 0]
    gate = [jnp.concatenate([gn[3 * r + c:3 * r + c + 1, :] for r in range(NSA_REP)], axis=1) for c in range(3)]
    o = gate[0] * o_c + gate[1] * o_s + gate[2] * o_w
    for r in range(NSA_REP):
        o_ref[0, r * HEAD_DIM:(r + 1) * HEAD_DIM, :] = o[:, r * TQ:(r + 1) * TQ]


def _nsa(q_t, kc, vc_t, ks, vs_t, kwp, vwp_t, gn_t, agg_t, n_cmp):
    bsz, _, seq = q_t.shape
    g = NSA_KV_GROUPS
    gw = NSA_REP * HEAD_DIM
    per_bg = lambda a: pl.BlockSpec((1, 1) + a.shape[2:], lambda b, gg, i: (b, gg) + (0,) * (a.ndim - 2))
    return pl.pallas_call(
        functools.partial(_nsa_body, n_cmp=n_cmp),
        grid=(bsz, g, seq // TQ),
        in_specs=[pl.BlockSpec((1, gw, TQ), lambda b, gg, i: (b, gg, i)),
                  per_bg(kc), per_bg(vc_t), per_bg(ks), per_bg(vs_t), per_bg(kwp), per_bg(vwp_t),
                  pl.BlockSpec((1, 1, gn_t.shape[2], TQ), lambda b, gg, i: (b, gg, 0, i)),
                  pl.BlockSpec(agg_t.shape, lambda b, gg, i: (0, 0))],
        out_specs=pl.BlockSpec((1, gw, TQ), lambda b, gg, i: (b, gg, i)),
        out_shape=jax.ShapeDtypeStruct((bsz, NSA_WIDTH, seq), F32),
        scratch_shapes=[pltpu.VMEM((agg_t.shape[0], NSA_REP * TQ), BF16),
                        pltpu.VMEM((seq, NSA_REP * TQ), F32)],
        compiler_params=_cparams(3, 56),
    )(q_t, kc, vc_t, ks, vs_t, kwp, vwp_t, gn_t, agg_t)


def _merge_body(on_ref, u_ref, v_ref, gate_ref, x_ref, gt_ref, sc_ref, sh_ref, ws_ref, bs_ref,
                wb_ref, wo_ref, g1_ref, b1_ref, wq_ref, x1_ref, h2_ref, qp_ref, *, alpha):
    tm = x_ref.shape[0]
    row = lax.broadcasted_iota(jnp.int32, (SGU_CHUNK, SGU_CHUNK), 0)
    col = lax.broadcasted_iota(jnp.int32, (SGU_CHUNK, SGU_CHUNK), 1)
    lane_group = lax.broadcasted_iota(jnp.int32, (1, SGU_WIDTH), 1) // SGU_GROUP_DIM
    mixed = []
    for c in range(tm // SGU_CHUNK):
        v = v_ref[c * SGU_CHUNK:(c + 1) * SGU_CHUNK, :]
        acc = jnp.zeros((SGU_CHUNK, SGU_WIDTH), F32)
        for g in range(SGU_GROUPS):
            wg = jnp.where(col <= row, ws_ref[g], 0.0).astype(BF16)
            vg = jnp.where(lane_group == g, v, 0.0).astype(BF16)
            acc = acc + _dot(wg, vg)
        mixed.append(acc + bs_ref[...])
    o_sgu = u_ref[...] * jnp.concatenate(mixed, axis=0)
    gate = gate_ref[...]
    merged = (gate[:, :D_MODEL] * _dot(on_ref[0].T.astype(BF16), wb_ref[0])
              + gate[:, D_MODEL:] * _dot(o_sgu.astype(BF16), wb_ref[1]))
    mix = _dot(merged.astype(BF16), wo_ref[...])
    x1 = _ln(alpha * x_ref[...] + gt_ref[0] * mix) * g1_ref[...] + b1_ref[...]
    x1_ref[...] = x1
    h2 = _ln(x1) * (1.0 + sc_ref[0]) + sh_ref[0]
    h2_ref[...] = h2
    qp_ref[...] = _dot(h2.astype(BF16), wq_ref[...])


def _merge(o_nsa, u, v, gate, x2, gt1, sc2, sh2, ws, bs, wb, wo, g1, b1, wq, seq, alpha):
    n, d = x2.shape
    tm = TM_MERGE
    nst = seq // tm
    row = lambda w: pl.BlockSpec((tm, w), lambda i: (i, 0))
    per_batch = pl.BlockSpec((1, 1, d), lambda i: (i // nst, 0, 0))
    const = lambda a: pl.BlockSpec(a.shape, lambda i: (0,) * a.ndim)
    nq = wq.shape[1]
    return pl.pallas_call(
        functools.partial(_merge_body, alpha=alpha),
        grid=(n // tm,),
        in_specs=[pl.BlockSpec((1, NSA_WIDTH, tm), lambda i: (i // nst, 0, i % nst)),
                  row(SGU_WIDTH), row(SGU_WIDTH), row(2 * d), row(d),
                  per_batch, per_batch, per_batch, const(ws), const(bs), const(wb), const(wo),
                  const(g1), const(b1), const(wq)],
        out_specs=[row(d), row(d), row(nq)],
        out_shape=[jax.ShapeDtypeStruct((n, d), F32), jax.ShapeDtypeStruct((n, d), F32),
                   jax.ShapeDtypeStruct((n, nq), F32)],
        compiler_params=_cparams(1, 56),
    )(o_nsa, u, v, gate, x2, gt1, sc2, sh2, ws, bs, wb, wo, g1, b1, wq)


def _topk_rows(s, k):
    nrow = s.shape[0]
    rid = lax.broadcasted_iota(jnp.int32, s.shape, 0).astype(F32)
    vals, rows = [], []
    for _ in range(k):
        m = jnp.max(s, axis=0, keepdims=True)
        first = jnp.min(jnp.where(s == m, rid, float(nrow)), axis=0, keepdims=True)
        s = jnp.where(rid == first, -jnp.inf, s)
        vals.append(m)
        rows.append(first)
    return vals, rows


def _candidate_blocks(k):
    todo = {(a, b) for a in range(k) for b in range(k) if (a + 1) * (b + 1) <= k}
    options = [(axis, fixed, start) for axis in (0, 1) for fixed in range(k) for start in range(0, k, 8)]
    cells = lambda axis, fixed, start: [((fixed, start + r) if axis == 0 else (start + r, fixed)) for r in range(8)]
    blocks = []
    while todo:
        best = max(options, key=lambda o: sum(c in todo for c in cells(*o)))
        owned = tuple(c in todo for c in cells(*best))
        todo -= set(cells(*best))
        blocks.append(best + (owned,))
    return tuple(blocks)


_CAND_BLOCKS = _candidate_blocks(PEER_TOPK)


def _retrieve_body(qp_ref, keys_ref, idx_ref, gate_ref, eid_ref):
    k = PEER_TOPK
    for h in range(PEER_HEADS):
        tops = []
        for c in range(2):
            off = (h * 2 + c) * PEER_HALF
            qh = qp_ref[:, off:off + PEER_HALF].astype(BF16)
            s = _dot_nt(keys_ref[h, c].astype(BF16), qh)
            tops.append(_topk_rows(s, k))
        (v1, i1), (v2, i2) = tops
        vm = (jnp.concatenate(v1, axis=0), jnp.concatenate(v2, axis=0))
        im = (jnp.concatenate(i1, axis=0), jnp.concatenate(i2, axis=0))
        sub = lax.broadcasted_iota(jnp.int32, (8, 1), 0)
        vals, eids, poss = [], [], []
        for axis, fixed, start, owned in _CAND_BLOCKS:
            if axis == 0:
                v = v1[fixed] + vm[1][start:start + 8]
                e = i1[fixed] * float(PEER_NKEYS) + im[1][start:start + 8]
                pos = fixed * k + start + sub
            else:
                v = vm[0][start:start + 8] + v2[fixed]
                e = im[0][start:start + 8] * float(PEER_NKEYS) + i2[fixed]
                pos = (start + sub) * k + fixed
            if not all(owned):
                keep = functools.reduce(jnp.logical_or, [sub == r for r in range(8) if owned[r]])
                v = jnp.where(keep, v, -jnp.inf)
                pos = jnp.where(keep, pos, k * k)
            vals.append(v)
            eids.append(e)
            poss.append(pos.astype(F32))
        cand = jnp.concatenate(vals, axis=0)
        cidx = jnp.concatenate(eids, axis=0)
        rid = jnp.concatenate(poss, axis=0)
        top_s, top_e = [], []
        for _ in range(k):
            m = jnp.max(cand, axis=0, keepdims=True)
            first = jnp.min(jnp.where(cand == m, rid, float(k * k)), axis=0, keepdims=True)
            hit = rid == first
            top_e.append(jnp.sum(jnp.where(hit, cidx, 0.0), axis=0, keepdims=True))
            cand = jnp.where(hit, -jnp.inf, cand)
            top_s.append(m)
        ts = jnp.concatenate(top_s, axis=0)
        e = jnp.exp(ts - ts[0:1])
        gate_ref[0, h * k:(h + 1) * k, :] = e / jnp.sum(e, axis=0, keepdims=True)
        eid_ref[h * k:(h + 1) * k, :] = jnp.concatenate(top_e, axis=0)
    idx_ref[...] = (eid_ref[...].T * float(ROW_SUB)).astype(jnp.int32)


def _retrieve(qp, keys):
    n = qp.shape[0]
    nt = n // TC
    out = pl.BlockSpec((1, PEER_PAIRS, TC), lambda i: (i, 0, 0))
    return pl.pallas_call(
        _retrieve_body,
        grid=(nt,),
        in_specs=[pl.BlockSpec((TC, qp.shape[1]), lambda i: (i, 0)),
                  pl.BlockSpec(keys.shape, lambda i: (0, 0, 0, 0))],
        out_specs=[pl.BlockSpec((TC, PEER_PAIRS), lambda i: (i, 0)), out],
        out_shape=[jax.ShapeDtypeStruct((n, PEER_PAIRS), jnp.int32),
                   jax.ShapeDtypeStruct((nt, PEER_PAIRS, TC), F32)],
        scratch_shapes=[pltpu.VMEM((PEER_PAIRS, TC), F32)],
        compiler_params=_cparams(1, 48),
    )(qp, keys)


ROW_SUB = D_MODEL // 2 // LANES
ACT_UNROLL = 8
OFF_REGS = 8


def _load_row(tab_ref, scaled_idx):
    row = tab_ref[pl.ds(pl.multiple_of(scaled_idx, ROW_SUB), ROW_SUB), :]
    lo = lax.bitcast_convert_type(lax.shift_left(row, jnp.uint32(16)), F32)
    hi = lax.bitcast_convert_type(row & jnp.uint32(0xFFFF0000), F32)
    return lo, hi


def _expert_act_body(off_ref, idx_ref, h_ref, gate_ref, tab_ref, w_ref, slot_ref, part_ref):
    off = [off_ref[j] for j in range(OFF_REGS)]

    def token(t, carry):
        x = h_ref[pl.ds(t, 1), :]
        chunk = lambda s: x[:, s * LANES:(s + 1) * LANES]
        xlo = jnp.concatenate([chunk(s) for s in range(ROW_SUB)], axis=0)
        xhi = jnp.concatenate([chunk(ROW_SUB + s) for s in range(ROW_SUB)], axis=0)
        base = pl.multiple_of(t * PEER_PAIRS, PEER_PAIRS)
        for p in range(PEER_PAIRS):
            if p % OFF_REGS == 0:
                idx_sub = idx_ref.at[t, pl.ds(p, OFF_REGS)]
            lo, hi = _load_row(tab_ref, idx_sub[off[p % OFF_REGS]])
            slot_ref[p * ROW_SUB:(p + 1) * ROW_SUB, :] = lo * xlo + hi * xhi
        part = slot_ref[pl.ds(0, PEER_PAIRS, stride=ROW_SUB), :]
        for s in range(1, ROW_SUB):
            part = part + slot_ref[pl.ds(s, PEER_PAIRS, stride=ROW_SUB), :]
        part_ref[pl.ds(pl.multiple_of(base, PEER_PAIRS), PEER_PAIRS), :] = part
        return carry

    lax.fori_loop(0, TC, token, 0)

    tok_lane = lax.broadcasted_iota(jnp.int32, (PEER_PAIRS, TC), 1)

    def lane_sums(i, act):
        for k in range(ACT_UNROLL):
            t = i * ACT_UNROLL + k
            part = part_ref[pl.ds(pl.multiple_of(t * PEER_PAIRS, PEER_PAIRS), PEER_PAIRS), :]
            act = jnp.where(tok_lane == t, jnp.sum(part, axis=1, keepdims=True), act)
        return act

    act = lax.fori_loop(0, TC // ACT_UNROLL, lane_sums, jnp.zeros((PEER_PAIRS, TC), F32))
    w_ref[...] = (gate_ref[0] * _gelu(act)).T


def _expert_act(idx_tok, h2, gate_t, tab):
    nt = gate_t.shape[0]
    tile = pl.BlockSpec((1, PEER_PAIRS, TC), lambda i: (i, 0, 0))
    return pl.pallas_call(
        _expert_act_body,
        grid=(nt,),
        in_specs=[pl.BlockSpec(memory_space=pltpu.SMEM),
                  pl.BlockSpec((TC, PEER_PAIRS), lambda i: (i, 0), memory_space=pltpu.SMEM),
                  pl.BlockSpec((TC, h2.shape[1]), lambda i: (i, 0)),
                  tile,
                  pl.BlockSpec(tab.shape, lambda i: (0, 0), pipeline_mode=pl.Buffered(1))],
        out_specs=pl.BlockSpec((TC, PEER_PAIRS), lambda i: (i, 0)),
        out_shape=jax.ShapeDtypeStruct((nt * TC, PEER_PAIRS), F32),
        scratch_shapes=[pltpu.VMEM((PEER_PAIRS * ROW_SUB, LANES), F32),
                        pltpu.VMEM((TC * PEER_PAIRS, LANES), F32)],
        compiler_params=_cparams(1, 52),
    )(jnp.arange(OFF_REGS, dtype=jnp.int32), idx_tok, h2, gate_t, tab)


def _expert_out_body(off_ref, idx_ref, w_ref, tab_ref, o_ref):
    n_acc = 4
    off = [off_ref[j] for j in range(OFF_REGS)]

    def token(t, carry):
        acc_lo = [jnp.zeros((ROW_SUB, LANES), F32) for _ in range(n_acc)]
        acc_hi = [jnp.zeros((ROW_SUB, LANES), F32) for _ in range(n_acc)]
        for p in range(PEER_PAIRS):
            if p % OFF_REGS == 0:
                idx_sub = idx_ref.at[t, pl.ds(p, OFF_REGS)]
                w_sub = w_ref.at[t, pl.ds(p, OFF_REGS)]
            lo, hi = _load_row(tab_ref, idx_sub[off[p % OFF_REGS]])
            w = w_sub[off[p % OFF_REGS]]
            acc_lo[p % n_acc] = acc_lo[p % n_acc] + w * lo
            acc_hi[p % n_acc] = acc_hi[p % n_acc] + w * hi
        lo = (acc_lo[0] + acc_lo[1]) + (acc_lo[2] + acc_lo[3])
        hi = (acc_hi[0] + acc_hi[1]) + (acc_hi[2] + acc_hi[3])
        o_ref[t] = jnp.concatenate([lo, hi], axis=0)
        return carry

    lax.fori_loop(0, TB, token, 0)


def _expert_out(idx_tok, w_tok, tab):
    n = idx_tok.shape[0]
    smem = pl.BlockSpec((TB, PEER_PAIRS), lambda i: (i, 0), memory_space=pltpu.SMEM)
    return pl.pallas_call(
        _expert_out_body,
        grid=(n // TB,),
        in_specs=[pl.BlockSpec(memory_space=pltpu.SMEM), smem, smem,
                  pl.BlockSpec(tab.shape, lambda i: (0, 0), pipeline_mode=pl.Buffered(1))],
        out_specs=pl.BlockSpec((TB, D_MODEL // LANES, LANES), lambda i: (i, 0, 0)),
        out_shape=jax.ShapeDtypeStruct((n, D_MODEL // LANES, LANES), F32),
        compiler_params=_cparams(1, 48),
    )(jnp.arange(OFF_REGS, dtype=jnp.int32), idx_tok, w_tok, tab)


def _final_body(x1_ref, f_ref, gt_ref, g_ref, b_ref, o_ref, *, alpha):
    o_ref[...] = _ln(alpha * x1_ref[...] + gt_ref[0] * f_ref[...]) * g_ref[...] + b_ref[...]


def _final(x1, ffn, gt2, g2, b2, seq, alpha):
    n, d = x1.shape
    tm = TM_LN
    nst = seq // tm
    row = pl.BlockSpec((tm, d), lambda i: (i, 0))
    const = pl.BlockSpec((1, d), lambda i: (0, 0))
    return pl.pallas_call(
        functools.partial(_final_body, alpha=alpha),
        grid=(n // tm,),
        in_specs=[row, row, pl.BlockSpec((1, 1, d), lambda i: (i // nst, 0, 0)), const, const],
        out_specs=row,
        out_shape=jax.ShapeDtypeStruct((n, d), F32),
        compiler_params=_cparams(1, 32),
    )(x1, ffn, gt2, g2, b2)


def _swap_halves(w, heads):
    d = w.shape[0]
    w4 = w.reshape(d, heads, 2, HEAD_DIM // 2)
    return jnp.flip(w4, axis=2).reshape(d, heads * HEAD_DIM)


def _fused_in_weight(w_in, w_merge):
    d = w_in.shape[0]
    o = 0
    parts = {}
    for name, width in (("q", NSA_WIDTH), ("kc", KV_WIDTH), ("vc", KV_WIDTH), ("ks", KV_WIDTH),
                        ("vs", KV_WIDTH), ("kw", KV_WIDTH), ("vw", KV_WIDTH), ("g", 3 * NSA_HEADS),
                        ("z", 2 * SGU_WIDTH)):
        parts[name] = w_in[:, o:o + width]
        o += width
    wq = parts["q"] * (HEAD_DIM ** -0.5)
    gcols = 3 * NSA_REP
    wg = jnp.zeros((d, 2 * LANES), w_in.dtype)
    wg = wg.at[:, :gcols].set(parts["g"][:, :gcols]).at[:, LANES:LANES + gcols].set(parts["g"][:, gcols:])
    g = NSA_KV_GROUPS
    cols = [wq, _swap_halves(wq, NSA_HEADS),
            parts["kc"], _swap_halves(parts["kc"], g), parts["ks"], _swap_halves(parts["ks"], g),
            parts["kw"], _swap_halves(parts["kw"], g), parts["vc"], parts["vs"], parts["vw"],
            wg, parts["z"], w_merge]
    return jnp.concatenate(cols, axis=1).astype(BF16)


def _rope_tables(seq):
    half = HEAD_DIM // 2
    pos = jnp.arange(seq, dtype=F32)
    inv_freq = ROPE_THETA ** (-jnp.arange(half, dtype=F32) / half)
    ang = pos[:, None] * inv_freq[None, :]
    cos, sin = jnp.cos(ang), jnp.sin(ang)
    reps = LANES // HEAD_DIM
    cosk = jnp.tile(jnp.concatenate([cos, cos], axis=1), (1, reps))
    sink = jnp.tile(jnp.concatenate([-sin, sin], axis=1), (1, reps))
    return cosk, sink


def _sel_aggregation(n_cmp_pad, n_sel):
    c0 = jnp.arange(n_cmp_pad)[:, None] * CMP_STRIDE
    s0 = jnp.arange(n_sel)[None, :] * SEL_BLOCK
    ov = jnp.clip(jnp.minimum(c0 + CMP_BLOCK, s0 + SEL_BLOCK) - jnp.maximum(c0, s0), 0, None)
    return (ov / CMP_BLOCK).astype(BF16)


def _pack_table(tab):
    half = tab.shape[1] // 2
    bits = lax.bitcast_convert_type(tab.astype(BF16), jnp.uint16).astype(jnp.uint32)
    packed = (bits[:, half:] << 16) | bits[:, :half]
    return packed.reshape(tab.shape[0] * ROW_SUB, LANES)


def _split_groups(a, bsz, seq):
    return a.reshape(bsz, seq, NSA_KV_GROUPS, HEAD_DIM).transpose(0, 2, 1, 3)


def kernel(x, c, w_ada, b_ada, w_in, cmp_pos, cmp_w1, cmp_b1, cmp_w2, cmp_b2, sgu_ln_g, sgu_ln_b, sgu_w, sgu_b, w_branch, w_merge, b_merge, w_out, ln1_g, ln1_b, peer_wq, peer_keys, peer_u, peer_v, ln2_g, ln2_b):
    bsz, seq, d = x.shape
    n = bsz * seq
    depth = w_ada.shape[0]
    alpha = (2.0 * depth) ** 0.25
    n_cmp = (seq - CMP_BLOCK) // CMP_STRIDE + 1
    n_half = seq // CMP_STRIDE
    n_sel = seq // SEL_BLOCK
    cosk, sink = _rope_tables(seq)
    agg = _sel_aggregation(n_half, n_sel)
    x2 = x.reshape(n, d)
    for l in range(depth):
        mod = _ada(c, w_ada[l], b_ada[l])
        sh1, sc1, gt1, sh2, sc2, gt2 = [m.reshape(bsz, 1, d) for m in jnp.split(mod, 6, axis=-1)]

        wall = _fused_in_weight(w_in[l], w_merge[l])
        (q_t, ks_ext, vs_t, kw_g, vw_t, gn_t, kc_r, vc_r, u, v, gate) = _proj(
            x2, sc1, sh1, wall, b_merge[l].reshape(1, -1), cosk, sink,
            sgu_ln_g[l].reshape(1, -1), sgu_ln_b[l].reshape(1, -1), seq)

        t16 = jnp.stack([_split_groups(kc_r, bsz, seq), _split_groups(vc_r, bsz, seq)])
        t16 = t16.reshape(2, bsz, NSA_KV_GROUPS, n_half, CMP_STRIDE * HEAD_DIM)
        cmp_kv = _compress(t16, cmp_pos[l].reshape(2, 1, CMP_BLOCK * HEAD_DIM), cmp_w1[l],
                           cmp_b1[l].reshape(2, 1, CMP_HIDDEN), cmp_w2[l], cmp_b2[l].reshape(2, 1, HEAD_DIM))

        o_nsa_t = _nsa(q_t, cmp_kv[0], cmp_kv[1].transpose(0, 1, 3, 2), ks_ext, vs_t, kw_g, vw_t, gn_t, agg.T, n_cmp)

        bs = jnp.repeat(sgu_b[l].T, SGU_GROUP_DIM, axis=1)
        x1, h2, qp = _merge(o_nsa_t, u, v, gate, x2, gt1, sc2, sh2,
                            sgu_w[l], bs, w_branch[l].astype(BF16), w_out[l].astype(BF16),
                            ln1_g[l].reshape(1, d), ln1_b[l].reshape(1, d), peer_wq[l].astype(BF16), seq, alpha)

        idx_tok, gate_t = _retrieve(qp, peer_keys[l])
        w_tok = _expert_act(idx_tok, h2, gate_t, _pack_table(peer_u[l]))
        ffn = _expert_out(idx_tok, w_tok, _pack_table(peer_v[l]))
        x2 = _final(x1, ffn.reshape(n, d), gt2, ln2_g[l].reshape(1, d), ln2_b[l].reshape(1, d), seq, alpha)
    return x2.reshape(bsz, seq, d)
```

```python
import functools

import jax
import jax.numpy as jnp
from jax import lax
from jax.experimental import pallas as pl
from jax.experimental.pallas import tpu as pltpu

D_MODEL = 1024
NSA_HEADS = 8
NSA_KV_GROUPS = 2
NSA_REP = NSA_HEADS // NSA_KV_GROUPS
HEAD_DIM = 64
NSA_WIDTH = NSA_HEADS * HEAD_DIM
KV_WIDTH = NSA_KV_GROUPS * HEAD_DIM
CMP_BLOCK = 32
CMP_STRIDE = 16
CMP_HIDDEN = 2 * HEAD_DIM
SEL_BLOCK = 64
SEL_TOPK = 16
WINDOW = 512
ROPE_THETA = 10000.0
SGU_GROUPS = 8
SGU_WIDTH = D_MODEL // 2
SGU_GROUP_DIM = SGU_WIDTH // SGU_GROUPS
SGU_CHUNK = 128
PEER_HEADS = 8
PEER_NKEYS = 128
PEER_EXPERTS = PEER_NKEYS * PEER_NKEYS
PEER_QDIM = 256
PEER_HALF = PEER_QDIM // 2
PEER_TOPK = 16
PEER_PAIRS = PEER_HEADS * PEER_TOPK
LN_EPS = 1e-5
NEG_INF = -1e30
FORCE_SCORE = 1e9

LANES = 128
MIB = 1024 * 1024
BF16 = jnp.bfloat16
F32 = jnp.float32

TM_PROJ = 256
TM_MERGE = 256
TM_LN = 512
TQ = 128
TK = 256
KV_GROUP = 4
GROUP_BLOCKS = TK * KV_GROUP // SEL_BLOCK
GATE_ROWS = 16
TC = 128
TB = 128

_OQ, _OQS = 0, 512
_OKC, _OKCS, _OKS, _OKSS, _OKW, _OKWS = 1024, 1152, 1280, 1408, 1536, 1664
_OVC, _OVS, _OVW = 1792, 1920, 2048
_OG = 2176
_OZ = 2432
_OM = 3456
_WCOLS = 5504


def _cparams(n_axes, vmem_mib):
    return pltpu.CompilerParams(
        dimension_semantics=("parallel",) * n_axes,
        vmem_limit_bytes=vmem_mib * MIB)


def _ln(x):
    mu = jnp.mean(x, axis=-1, keepdims=True)
    xc = x - mu
    var = jnp.mean(xc * xc, axis=-1, keepdims=True)
    return xc * lax.rsqrt(var + LN_EPS)


def _gelu(x):
    return 0.5 * x * (1.0 + lax.erf(x * (2.0 ** -0.5)))


def _dot(a, b):
    return jnp.dot(a, b, preferred_element_type=F32)


def _dot_nt(a, b):
    return lax.dot_general(a, b, (((1,), (1,)), ((), ())), preferred_element_type=F32)


def _ada_body(c_ref, w_ref, b_ref, o_ref):
    cv = c_ref[...]
    a = cv * jax.nn.sigmoid(cv)
    o_ref[...] = _dot(a.astype(BF16), w_ref[...].astype(BF16)) + b_ref[...]


def _ada(c, w, b):
    bsz, d = c.shape
    n = w.shape[1]
    tn = 1024
    return pl.pallas_call(
        _ada_body,
        grid=(n // tn,),
        in_specs=[pl.BlockSpec((bsz, d), lambda j: (0, 0)),
                  pl.BlockSpec((d, tn), lambda j: (0, j)),
                  pl.BlockSpec((1, tn), lambda j: (0, j))],
        out_specs=pl.BlockSpec((bsz, tn), lambda j: (0, j)),
        out_shape=jax.ShapeDtypeStruct((bsz, n), F32),
        compiler_params=_cparams(1, 32),
    )(c, w, b.reshape(1, n))


def _proj_body(x_ref, sc_ref, sh_ref, w_ref, bm_ref, cos_ref, sin_ref, sg_ref, sb_ref,
               q_ref, ks_ref, vs_ref, kw_ref, vw_ref, gn_ref, kc_ref, vc_ref, u_ref, v_ref, gate_ref,
               *, n_seq_tiles):
    h = _ln(x_ref[...]) * (1.0 + sc_ref[0]) + sh_ref[0]
    hb = h.astype(BF16)

    def proj(off, width):
        return _dot(hb, w_ref[:, off:off + width])

    cosk = cos_ref[...]
    sink = sin_ref[...]
    cosq = jnp.concatenate([cosk] * (NSA_WIDTH // LANES), axis=1)
    sinq = jnp.concatenate([sink] * (NSA_WIDTH // LANES), axis=1)
    q_ref[0] = (proj(_OQ, NSA_WIDTH) * cosq + proj(_OQS, NSA_WIDTH) * sinq).T
    kc_ref[...] = proj(_OKC, KV_WIDTH) * cosk + proj(_OKCS, KV_WIDTH) * sink
    vc_ref[...] = proj(_OVC, KV_WIDTH)
    tm = x_ref.shape[0]
    ks = proj(_OKS, KV_WIDTH) * cosk + proj(_OKSS, KV_WIDTH) * sink
    kw = proj(_OKW, KV_WIDTH) * cosk + proj(_OKWS, KV_WIDTH) * sink
    vs_t = proj(_OVS, KV_WIDTH).T
    vw_t = proj(_OVW, KV_WIDTH).T
    gn_t = jax.nn.sigmoid(proj(_OG, 2 * LANES)).T
    pos = (pl.program_id(0) % n_seq_tiles) * tm + lax.broadcasted_iota(jnp.int32, (tm, 1), 0)
    block_onehot = jnp.where((pos // SEL_BLOCK) % GROUP_BLOCKS
                             == lax.broadcasted_iota(jnp.int32, (1, GROUP_BLOCKS), 1), 1.0, 0.0)
    for g in range(NSA_KV_GROUPS):
        cols = slice(g * HEAD_DIM, (g + 1) * HEAD_DIM)
        ks_ref[0, g] = jnp.concatenate([ks[:, cols], block_onehot], axis=1).astype(BF16)
        kw_ref[0, g] = kw[:, cols].astype(BF16)
        vs_ref[0, g, 0] = vs_t[cols, :].astype(BF16)
        for c in range(tm // TQ):
            vw_ref[0, g, c] = vw_t[cols, c * TQ:(c + 1) * TQ].astype(BF16)
        gn_ref[0, g] = gn_t[g * LANES:g * LANES + GATE_ROWS, :]
    u_ref[...] = _gelu(proj(_OZ, SGU_WIDTH)).astype(BF16)
    v_ref[...] = (_ln(_gelu(proj(_OZ + SGU_WIDTH, SGU_WIDTH))) * sg_ref[...] + sb_ref[...]).astype(BF16)
    gate_ref[...] = jax.nn.sigmoid(proj(_OM, 2 * D_MODEL) + bm_ref[...]).astype(BF16)


def _proj(x2, sc, sh, wall, bm, cosk, sink, sg, sb, seq):
    n, d = x2.shape
    tm = TM_PROJ
    nst = seq // tm
    row = lambda w: pl.BlockSpec((tm, w), lambda i: (i, 0))
    per_batch = pl.BlockSpec((1, 1, d), lambda i: (i // nst, 0, 0))
    const = lambda shp: pl.BlockSpec(shp, lambda i: (0,) * len(shp))
    pos = pl.BlockSpec((tm, LANES), lambda i: (i % nst, 0))
    outs = [(KV_WIDTH, F32), (KV_WIDTH, F32), (SGU_WIDTH, BF16), (SGU_WIDTH, BF16), (2 * D_MODEL, BF16)]
    assert tm == TK and tm % TQ == 0
    bsz, g = n // seq, NSA_KV_GROUPS
    att = [((bsz, NSA_WIDTH, seq), (1, NSA_WIDTH, tm), lambda i: (i // nst, 0, i % nst), F32),
           ((bsz, g, seq, HEAD_DIM + GROUP_BLOCKS), (1, g, tm, HEAD_DIM + GROUP_BLOCKS),
            lambda i: (i // nst, 0, i % nst, 0), BF16),
           ((bsz, g, seq // TK, HEAD_DIM, TK), (1, g, 1, HEAD_DIM, TK), lambda i: (i // nst, 0, i % nst, 0, 0), BF16),
           ((bsz, g, seq, HEAD_DIM), (1, g, tm, HEAD_DIM), lambda i: (i // nst, 0, i % nst, 0), BF16),
           ((bsz, g, seq // TQ, HEAD_DIM, TQ), (1, g, tm // TQ, HEAD_DIM, TQ),
            lambda i: (i // nst, 0, i % nst, 0, 0), BF16),
           ((bsz, g, GATE_ROWS, seq), (1, g, GATE_ROWS, tm), lambda i: (i // nst, 0, 0, i % nst), F32)]
    return pl.pallas_call(
        functools.partial(_proj_body, n_seq_tiles=nst),
        grid=(n // tm,),
        in_specs=[row(d), per_batch, per_batch, const((d, _WCOLS)), const((1, 2 * D_MODEL)),
                  pos, pos, const((1, SGU_WIDTH)), const((1, SGU_WIDTH))],
        out_specs=[pl.BlockSpec(blk, imap) for _, blk, imap, _ in att] + [row(w) for w, _ in outs],
        out_shape=([jax.ShapeDtypeStruct(shp, dt) for shp, _, _, dt in att]
                   + [jax.ShapeDtypeStruct((n, w), dt) for w, dt in outs]),
        compiler_params=_cparams(1, 56),
    )(x2, sc, sh, wall, bm, cosk, sink, sg, sb)


def _cmp_body(t_ref, pos_ref, w1_ref, b1_ref, w2_ref, b2_ref, o_ref):
    half = (CMP_BLOCK // 2) * HEAD_DIM
    t = t_ref[0, 0, 0]
    pos = pos_ref[0]
    ta = (t + pos[:, :half]).astype(BF16)
    tb = (t + pos[:, half:]).astype(BF16)
    a = _dot(ta, w1_ref[0, :half, :].astype(BF16))
    b = _dot(tb, w1_ref[0, half:, :].astype(BF16))
    nrow = t.shape[0]
    b_next = pltpu.roll(b, nrow - 1, axis=0)
    hid = _gelu(a + b_next + b1_ref[0])
    out = _dot(hid.astype(BF16), w2_ref[0].astype(BF16)) + b2_ref[0]
    o_ref[0, 0, 0] = out.astype(o_ref.dtype)


def _compress(t16, pos, w1, b1, w2, b2):
    _, bsz, g, nr, dd = t16.shape
    per_kind = lambda shp: pl.BlockSpec((1,) + shp, lambda k, b, gg: (k,) + (0,) * len(shp))
    blk = lambda w: pl.BlockSpec((1, 1, 1, nr, w), lambda k, b, gg: (k, b, gg, 0, 0))
    return pl.pallas_call(
        _cmp_body,
        grid=(2, bsz, g),
        in_specs=[blk(dd), per_kind((1, CMP_BLOCK * HEAD_DIM)), per_kind((CMP_BLOCK * HEAD_DIM, CMP_HIDDEN)),
                  per_kind((1, CMP_HIDDEN)), per_kind((CMP_HIDDEN, HEAD_DIM)), per_kind((1, HEAD_DIM))],
        out_specs=blk(HEAD_DIM),
        out_shape=jax.ShapeDtypeStruct((2, bsz, g, nr, HEAD_DIM), BF16),
        compiler_params=_cparams(3, 48),
    )(t16, pos, w1, b1, w2, b2)


def _scores(q_all, kt, bias):
    s = _dot(kt, q_all)
    return s if bias is None else s + bias


def _max_init():
    return jnp.full((1, NSA_REP * TQ), NEG_INF, F32)


def _acc_init():
    w = NSA_REP * TQ
    return jnp.zeros((1, w), F32), jnp.zeros((HEAD_DIM, w), F32)


def _lane_tile(a):
    return jnp.concatenate([a] * NSA_REP, axis=1)


def _nsa_body(q_ref, kc_ref, vc_ref, ks_ref, vs_ref, kw_ref, vw_ref, gn_ref, agg_ref, o_ref, selneg_ref, score_ref, *, n_cmp):
    qi = pl.program_id(2)
    q0 = qi * TQ
    t1 = q0 + lax.broadcasted_iota(jnp.int32, (1, TQ), 1)
    t = _lane_tile(t1)
    q_all = jnp.concatenate([q_ref[0, r * HEAD_DIM:(r + 1) * HEAD_DIM, :] for r in range(NSA_REP)],
                            axis=1).astype(BF16)

    kcm = kc_ref[0, 0]
    ncp = kcm.shape[0]
    nidx = lax.broadcasted_iota(jnp.int32, (ncp, 1), 0)
    mask_c = (nidx * CMP_STRIDE + (CMP_BLOCK - 1) <= t) & (nidx < n_cmp)
    sm = jnp.where(mask_c, _dot(kcm, q_all), NEG_INF)
    e = jnp.where(mask_c, jnp.exp(sm - jnp.max(sm, axis=0, keepdims=True)), 0.0)
    l = jnp.sum(e, axis=0, keepdims=True)
    p = e * (1.0 / jnp.where(l > 0.0, l, 1.0))
    o_c = _dot(vc_ref[0, 0], p.astype(BF16))
    psum = p[:, :TQ]
    for r in range(1, NSA_REP):
        psum = psum + p[:, r * TQ:(r + 1) * TQ]

    p_hi = psum.astype(BF16)
    p_lo = (psum - p_hi.astype(F32)).astype(BF16)
    agg_t = agg_ref[...]
    imp = _dot(agg_t, p_hi) + _dot(agg_t, p_lo)
    n_sel = imp.shape[0]
    blk = lax.broadcasted_iota(jnp.int32, (n_sel, 1), 0)
    blk_f = blk.astype(F32)
    cur = t1 // SEL_BLOCK
    forced = (blk == 0) | (blk == cur) | (blk == cur - 1)
    score = jnp.where(forced, FORCE_SCORE, jnp.where(blk <= cur, imp, NEG_INF))
    sel = jnp.zeros((n_sel, TQ), jnp.bool_)
    for _ in range(min(SEL_TOPK, n_sel)):
        m = jnp.max(score, axis=0, keepdims=True)
        first = jnp.min(jnp.where(score == m, blk_f, float(n_sel)), axis=0, keepdims=True)
        hit = blk_f == first
        sel = sel | hit
        score = jnp.where(hit, -jnp.inf, score)
    selneg_ref[...] = _lane_tile(jnp.where(sel, 0.0, NEG_INF).astype(BF16))

    def q_ext(jg):
        rows = selneg_ref[pl.ds(pl.multiple_of(jg * GROUP_BLOCKS, GROUP_BLOCKS), GROUP_BLOCKS), :]
        return jnp.concatenate([q_all, rows], axis=0)

    krow_w = lax.broadcasted_iota(jnp.int32, (TQ, 1), 0)

    def win_chunk(c):
        kpos = q0 - WINDOW + c * TQ + krow_w
        dlt = t - kpos
        bias = jnp.where((dlt >= 0) & (dlt < WINDOW) & (kpos >= 0), 0.0, NEG_INF)
        start = jnp.maximum(q0 - WINDOW + c * TQ, 0)
        return kw_ref[0, 0, pl.ds(pl.multiple_of(start, TQ), TQ), :], bias

    n_win = WINDOW // TQ + 1
    s_w = [_scores(q_all, *win_chunk(c)) for c in range(n_win)]
    m_w = functools.reduce(jnp.maximum, [jnp.max(s, axis=0, keepdims=True) for s in s_w])
    l_w, acc_w = _acc_init()
    for c in range(n_win):
        p = jnp.exp(s_w[c] - m_w)
        l_w = l_w + jnp.sum(p, axis=0, keepdims=True)
        acc_w = acc_w + _dot(vw_ref[0, 0, jnp.maximum(qi + c - WINDOW // TQ, 0)], p.astype(BF16))
    o_w = acc_w / l_w

    krow = lax.broadcasted_iota(jnp.int32, (TK, 1), 0)
    group_rows = TK * KV_GROUP

    def max_group(jg, m):
        k0 = pl.multiple_of(jg * group_rows, group_rows)
        s = _scores(q_ext(jg), ks_ref[0, 0, pl.ds(k0, group_rows), :], None)
        score_ref[pl.ds(k0, group_rows), :] = s
        return jnp.maximum(m, jnp.max(s, axis=0, keepdims=True))

    def max_tile(j, m):
        k0 = pl.multiple_of(j * TK, TK)
        bias = jnp.where(k0 + krow <= t, 0.0, NEG_INF)
        s = _scores(q_ext(j // KV_GROUP), ks_ref[0, 0, pl.ds(k0, TK), :], bias)
        score_ref[pl.ds(k0, TK), :] = s
        return jnp.maximum(m, jnp.max(s, axis=0, keepdims=True))

    def sum_rows(state, k0, rows, vt_t):
        l, acc = state
        p = jnp.exp(score_ref[pl.ds(k0, rows), :] - m_s)
        return l + jnp.sum(p, axis=0, keepdims=True), acc + _dot(vt_t, p.astype(BF16))

    def sum_group(jg, state):
        vt_t = jnp.concatenate([vs_ref[0, 0, jg * KV_GROUP + sub] for sub in range(KV_GROUP)], axis=1)
        return sum_rows(state, pl.multiple_of(jg * group_rows, group_rows), group_rows, vt_t)

    def sum_tile(j, state):
        return sum_rows(state, pl.multiple_of(j * TK, TK), TK, vs_ref[0, 0, j])

    n_full = q0 // group_rows
    n_kv = (q0 + TQ - 1) // TK + 1
    m_s = lax.fori_loop(0, n_full, max_group, _max_init())
    m_s = lax.fori_loop(n_full * KV_GROUP, n_kv, max_tile, m_s)
    state = lax.fori_loop(0, n_full, sum_group, _acc_init())
    state = lax.fori_loop(n_full * KV_GROUP, n_kv, sum_tile, state)
    o_s = state[1] / state[0]

    gn = gn_ref[0, 0]
    gate = [jnp.concatenate([gn[3 * r + c:3 * r + c + 1, :] for r in range(NSA_REP)], axis=1) for c in range(3)]
    o = gate[0] * o_c + gate[1] * o_s + gate[2] * o_w
    for r in range(NSA_REP):
        o_ref[0, r * HEAD_DIM:(r + 1) * HEAD_DIM, :] = o[:, r * TQ:(r + 1) * TQ]


def _nsa(q_t, kc, vc_t, ks, vs_t, kwp, vwp_t, gn_t, agg_t, n_cmp):
    bsz, _, seq = q_t.shape
    g = NSA_KV_GROUPS
    gw = NSA_REP * HEAD_DIM
    per_bg = lambda a: pl.BlockSpec((1, 1) + a.shape[2:], lambda b, gg, i: (b, gg) + (0,) * (a.ndim - 2))
    return pl.pallas_call(
        functools.partial(_nsa_body, n_cmp=n_cmp),
        grid=(bsz, g, seq // TQ),
        in_specs=[pl.BlockSpec((1, gw, TQ), lambda b, gg, i: (b, gg, i)),
                  per_bg(kc), per_bg(vc_t), per_bg(ks), per_bg(vs_t), per_bg(kwp), per_bg(vwp_t),
                  pl.BlockSpec((1, 1, gn_t.shape[2], TQ), lambda b, gg, i: (b, gg, 0, i)),
                  pl.BlockSpec(agg_t.shape, lambda b, gg, i: (0, 0))],
        out_specs=pl.BlockSpec((1, gw, TQ), lambda b, gg, i: (b, gg, i)),
        out_shape=jax.ShapeDtypeStruct((bsz, NSA_WIDTH, seq), F32),
        scratch_shapes=[pltpu.VMEM((agg_t.shape[0], NSA_REP * TQ), BF16),
                        pltpu.VMEM((seq, NSA_REP * TQ), F32)],
        compiler_params=_cparams(3, 56),
    )(q_t, kc, vc_t, ks, vs_t, kwp, vwp_t, gn_t, agg_t)


def _merge_body(on_ref, u_ref, v_ref, gate_ref, x_ref, gt_ref, sc_ref, sh_ref, ws_ref, bs_ref,
                wb_ref, wo_ref, g1_ref, b1_ref, wq_ref, x1_ref, h2_ref, qp_ref, *, alpha):
    tm = x_ref.shape[0]
    row = lax.broadcasted_iota(jnp.int32, (SGU_CHUNK, SGU_CHUNK), 0)
    col = lax.broadcasted_iota(jnp.int32, (SGU_CHUNK, SGU_CHUNK), 1)
    lane_group = lax.broadcasted_iota(jnp.int32, (1, SGU_WIDTH), 1) // SGU_GROUP_DIM
    mixed = []
    for c in range(tm // SGU_CHUNK):
        v = v_ref[c * SGU_CHUNK:(c + 1) * SGU_CHUNK, :]
        acc = jnp.zeros((SGU_CHUNK, SGU_WIDTH), F32)
        for g in range(SGU_GROUPS):
            wg = jnp.where(col <= row, ws_ref[g], 0.0).astype(BF16)
            vg = jnp.where(lane_group == g, v, jnp.zeros_like(v))
            acc = acc + _dot(wg, vg)
        mixed.append(acc + bs_ref[...])
    o_sgu = u_ref[...].astype(F32) * jnp.concatenate(mixed, axis=0)
    gate = gate_ref[...].astype(F32)
    merged = (gate[:, :D_MODEL] * _dot(on_ref[0].T.astype(BF16), wb_ref[0])
              + gate[:, D_MODEL:] * _dot(o_sgu.astype(BF16), wb_ref[1]))
    mix = _dot(merged.astype(BF16), wo_ref[...])
    x1 = _ln(alpha * x_ref[...] + gt_ref[0] * mix) * g1_ref[...] + b1_ref[...]
    x1_ref[...] = x1
    h2 = _ln(x1) * (1.0 + sc_ref[0]) + sh_ref[0]
    h2_ref[...] = h2
    qp_ref[...] = _dot(h2.astype(BF16), wq_ref[...]).astype(BF16)


def _merge(o_nsa, u, v, gate, x2, gt1, sc2, sh2, ws, bs, wb, wo, g1, b1, wq, seq, alpha):
    n, d = x2.shape
    tm = TM_MERGE
    nst = seq // tm
    row = lambda w: pl.BlockSpec((tm, w), lambda i: (i, 0))
    per_batch = pl.BlockSpec((1, 1, d), lambda i: (i // nst, 0, 0))
    const = lambda a: pl.BlockSpec(a.shape, lambda i: (0,) * a.ndim)
    nq = wq.shape[1]
    return pl.pallas_call(
        functools.partial(_merge_body, alpha=alpha),
        grid=(n // tm,),
        in_specs=[pl.BlockSpec((1, NSA_WIDTH, tm), lambda i: (i // nst, 0, i % nst)),
                  row(SGU_WIDTH), row(SGU_WIDTH), row(2 * d), row(d),
                  per_batch, per_batch, per_batch, const(ws), const(bs), const(wb), const(wo),
                  const(g1), const(b1), const(wq)],
        out_specs=[row(d), row(d), row(nq)],
        out_shape=[jax.ShapeDtypeStruct((n, d), F32), jax.ShapeDtypeStruct((n, d), F32),
                   jax.ShapeDtypeStruct((n, nq), BF16)],
        compiler_params=_cparams(1, 56),
    )(o_nsa, u, v, gate, x2, gt1, sc2, sh2, ws, bs, wb, wo, g1, b1, wq)


def _topk_rows(s, k):
    nrow = s.shape[0]
    rid = lax.broadcasted_iota(jnp.int32, s.shape, 0).astype(F32)
    vals, rows = [], []
    for _ in range(k):
        m = jnp.max(s, axis=0, keepdims=True)
        first = jnp.min(jnp.where(s == m, rid, float(nrow)), axis=0, keepdims=True)
        s = jnp.where(rid == first, -jnp.inf, s)
        vals.append(m)
        rows.append(first)
    return vals, rows


def _candidate_blocks(k):
    todo = {(a, b) for a in range(k) for b in range(k) if (a + 1) * (b + 1) <= k}
    options = [(axis, fixed, start) for axis in (0, 1) for fixed in range(k) for start in range(0, k, 8)]
    cells = lambda axis, fixed, start: [((fixed, start + r) if axis == 0 else (start + r, fixed)) for r in range(8)]
    blocks = []
    while todo:
        best = max(options, key=lambda o: sum(c in todo for c in cells(*o)))
        owned = tuple(c in todo for c in cells(*best))
        todo -= set(cells(*best))
        blocks.append(best + (owned,))
    return tuple(blocks)


_CAND_BLOCKS = _candidate_blocks(PEER_TOPK)


def _retrieve_body(qp_ref, keys_ref, idx_ref, gate_ref, eid_ref):
    k = PEER_TOPK
    for h in range(PEER_HEADS):
        tops = []
        for c in range(2):
            off = (h * 2 + c) * PEER_HALF
            qh = qp_ref[:, off:off + PEER_HALF].astype(BF16)
            s = _dot_nt(keys_ref[h, c].astype(BF16), qh)
            tops.append(_topk_rows(s, k))
        (v1, i1), (v2, i2) = tops
        vm = (jnp.concatenate(v1, axis=0), jnp.concatenate(v2, axis=0))
        im = (jnp.concatenate(i1, axis=0), jnp.concatenate(i2, axis=0))
        sub = lax.broadcasted_iota(jnp.int32, (8, 1), 0)
        vals, eids, poss = [], [], []
        for axis, fixed, start, owned in _CAND_BLOCKS:
            if axis == 0:
                v = v1[fixed] + vm[1][start:start + 8]
                e = i1[fixed] * float(PEER_NKEYS) + im[1][start:start + 8]
                pos = fixed * k + start + sub
            else:
                v = vm[0][start:start + 8] + v2[fixed]
                e = im[0][start:start + 8] * float(PEER_NKEYS) + i2[fixed]
                pos = (start + sub) * k + fixed
            if not all(owned):
                keep = functools.reduce(jnp.logical_or, [sub == r for r in range(8) if owned[r]])
                v = jnp.where(keep, v, -jnp.inf)
                pos = jnp.where(keep, pos, k * k)
            vals.append(v)
            eids.append(e)
            poss.append(pos.astype(F32))
        cand = jnp.concatenate(vals, axis=0)
        cidx = jnp.concatenate(eids, axis=0)
        rid = jnp.concatenate(poss, axis=0)
        top_s, top_e = [], []
        for _ in range(k):
            m = jnp.max(cand, axis=0, keepdims=True)
            first = jnp.min(jnp.where(cand == m, rid, float(k * k)), axis=0, keepdims=True)
            hit = rid == first
            top_e.append(jnp.sum(jnp.where(hit, cidx, 0.0), axis=0, keepdims=True))
            cand = jnp.where(hit, -jnp.inf, cand)
            top_s.append(m)
        ts = jnp.concatenate(top_s, axis=0)
        e = jnp.exp(ts - ts[0:1])
        gate_ref[0, h * k:(h + 1) * k, :] = e / jnp.sum(e, axis=0, keepdims=True)
        eid_ref[h * k:(h + 1) * k, :] = jnp.concatenate(top_e, axis=0)
    idx_ref[...] = (eid_ref[...].T * float(ROW_SUB)).astype(jnp.int32)


def _retrieve(qp, keys):
    n = qp.shape[0]
    nt = n // TC
    out = pl.BlockSpec((1, PEER_PAIRS, TC), lambda i: (i, 0, 0))
    return pl.pallas_call(
        _retrieve_body,
        grid=(nt,),
        in_specs=[pl.BlockSpec((TC, qp.shape[1]), lambda i: (i, 0)),
                  pl.BlockSpec(keys.shape, lambda i: (0, 0, 0, 0))],
        out_specs=[pl.BlockSpec((TC, PEER_PAIRS), lambda i: (i, 0)), out],
        out_shape=[jax.ShapeDtypeStruct((n, PEER_PAIRS), jnp.int32),
                   jax.ShapeDtypeStruct((nt, PEER_PAIRS, TC), F32)],
        scratch_shapes=[pltpu.VMEM((PEER_PAIRS, TC), F32)],
        compiler_params=_cparams(1, 48),
    )(qp, keys)


ROW_SUB = D_MODEL // 2 // LANES
ACT_UNROLL = 8
OFF_REGS = 8


def _load_row(tab_ref, scaled_idx):
    row = tab_ref[pl.ds(pl.multiple_of(scaled_idx, ROW_SUB), ROW_SUB), :]
    lo = lax.bitcast_convert_type(lax.shift_left(row, jnp.uint32(16)), F32)
    hi = lax.bitcast_convert_type(row & jnp.uint32(0xFFFF0000), F32)
    return lo, hi


def _expert_act_body(off_ref, idx_ref, h_ref, gate_ref, tab_ref, w_ref, slot_ref, part_ref):
    off = [off_ref[j] for j in range(OFF_REGS)]

    def token(t, carry):
        x = h_ref[pl.ds(t, 1), :]
        chunk = lambda s: x[:, s * LANES:(s + 1) * LANES]
        xlo = jnp.concatenate([chunk(s) for s in range(ROW_SUB)], axis=0)
        xhi = jnp.concatenate([chunk(ROW_SUB + s) for s in range(ROW_SUB)], axis=0)
        base = pl.multiple_of(t * PEER_PAIRS, PEER_PAIRS)
        for p in range(PEER_PAIRS):
            if p % OFF_REGS == 0:
                idx_sub = idx_ref.at[t, pl.ds(p, OFF_REGS)]
            lo, hi = _load_row(tab_ref, idx_sub[off[p % OFF_REGS]])
            slot_ref[p * ROW_SUB:(p + 1) * ROW_SUB, :] = lo * xlo + hi * xhi
        part = slot_ref[pl.ds(0, PEER_PAIRS, stride=ROW_SUB), :]
        for s in range(1, ROW_SUB):
            part = part + slot_ref[pl.ds(s, PEER_PAIRS, stride=ROW_SUB), :]
        part_ref[pl.ds(pl.multiple_of(base, PEER_PAIRS), PEER_PAIRS), :] = part
        return carry

    lax.fori_loop(0, TC, token, 0)

    tok_lane = lax.broadcasted_iota(jnp.int32, (PEER_PAIRS, TC), 1)

    def lane_sums(i, act):
        for k in range(ACT_UNROLL):
            t = i * ACT_UNROLL + k
            part = part_ref[pl.ds(pl.multiple_of(t * PEER_PAIRS, PEER_PAIRS), PEER_PAIRS), :]
            act = jnp.where(tok_lane == t, jnp.sum(part, axis=1, keepdims=True), act)
        return act

    act = lax.fori_loop(0, TC // ACT_UNROLL, lane_sums, jnp.zeros((PEER_PAIRS, TC), F32))
    w_ref[...] = (gate_ref[0] * _gelu(act)).T


def _expert_act(idx_tok, h2, gate_t, tab):
    nt = gate_t.shape[0]
    tile = pl.BlockSpec((1, PEER_PAIRS, TC), lambda i: (i, 0, 0))
    return pl.pallas_call(
        _expert_act_body,
        grid=(nt,),
        in_specs=[pl.BlockSpec(memory_space=pltpu.SMEM),
                  pl.BlockSpec((TC, PEER_PAIRS), lambda i: (i, 0), memory_space=pltpu.SMEM),
                  pl.BlockSpec((TC, h2.shape[1]), lambda i: (i, 0)),
                  tile,
                  pl.BlockSpec(tab.shape, lambda i: (0, 0), pipeline_mode=pl.Buffered(1))],
        out_specs=pl.BlockSpec((TC, PEER_PAIRS), lambda i: (i, 0)),
        out_shape=jax.ShapeDtypeStruct((nt * TC, PEER_PAIRS), F32),
        scratch_shapes=[pltpu.VMEM((PEER_PAIRS * ROW_SUB, LANES), F32),
                        pltpu.VMEM((TC * PEER_PAIRS, LANES), F32)],
        compiler_params=_cparams(1, 52),
    )(jnp.arange(OFF_REGS, dtype=jnp.int32), idx_tok, h2, gate_t, tab)


def _expert_out_body(off_ref, idx_ref, w_ref, tab_ref, o_ref):
    n_acc = 4
    off = [off_ref[j] for j in range(OFF_REGS)]

    def token(t, carry):
        acc_lo = [jnp.zeros((ROW_SUB, LANES), F32) for _ in range(n_acc)]
        acc_hi = [jnp.zeros((ROW_SUB, LANES), F32) for _ in range(n_acc)]
        for p in range(PEER_PAIRS):
            if p % OFF_REGS == 0:
                idx_sub = idx_ref.at[t, pl.ds(p, OFF_REGS)]
                w_sub = w_ref.at[t, pl.ds(p, OFF_REGS)]
            lo, hi = _load_row(tab_ref, idx_sub[off[p % OFF_REGS]])
            w = w_sub[off[p % OFF_REGS]]
            acc_lo[p % n_acc] = acc_lo[p % n_acc] + w * lo
            acc_hi[p % n_acc] = acc_hi[p % n_acc] + w * hi
        lo = (acc_lo[0] + acc_lo[1]) + (acc_lo[2] + acc_lo[3])
        hi = (acc_hi[0] + acc_hi[1]) + (acc_hi[2] + acc_hi[3])
        o_ref[t] = jnp.concatenate([lo, hi], axis=0)
        return carry

    lax.fori_loop(0, TB, token, 0)


def _expert_out(idx_tok, w_tok, tab):
    n = idx_tok.shape[0]
    smem = pl.BlockSpec((TB, PEER_PAIRS), lambda i: (i, 0), memory_space=pltpu.SMEM)
    return pl.pallas_call(
        _expert_out_body,
        grid=(n // TB,),
        in_specs=[pl.BlockSpec(memory_space=pltpu.SMEM), smem, smem,
                  pl.BlockSpec(tab.shape, lambda i: (0, 0), pipeline_mode=pl.Buffered(1))],
        out_specs=pl.BlockSpec((TB, D_MODEL // LANES, LANES), lambda i: (i, 0, 0)),
        out_shape=jax.ShapeDtypeStruct((n, D_MODEL // LANES, LANES), F32),
        compiler_params=_cparams(1, 48),
    )(jnp.arange(OFF_REGS, dtype=jnp.int32), idx_tok, w_tok, tab)


def _final_body(x1_ref, f_ref, gt_ref, g_ref, b_ref, o_ref, *, alpha):
    o_ref[...] = _ln(alpha * x1_ref[...] + gt_ref[0] * f_ref[...]) * g_ref[...] + b_ref[...]


def _final(x1, ffn, gt2, g2, b2, seq, alpha):
    n, d = x1.shape
    tm = TM_LN
    nst = seq // tm
    row = pl.BlockSpec((tm, d), lambda i: (i, 0))
    const = pl.BlockSpec((1, d), lambda i: (0, 0))
    return pl.pallas_call(
        functools.partial(_final_body, alpha=alpha),
        grid=(n // tm,),
        in_specs=[row, row, pl.BlockSpec((1, 1, d), lambda i: (i // nst, 0, 0)), const, const],
        out_specs=row,
        out_shape=jax.ShapeDtypeStruct((n, d), F32),
        compiler_params=_cparams(1, 32),
    )(x1, ffn, gt2, g2, b2)


def _swap_halves(w, heads):
    d = w.shape[0]
    w4 = w.reshape(d, heads, 2, HEAD_DIM // 2)
    return jnp.flip(w4, axis=2).reshape(d, heads * HEAD_DIM)


def _fused_in_weight(w_in, w_merge):
    d = w_in.shape[0]
    o = 0
    parts = {}
    for name, width in (("q", NSA_WIDTH), ("kc", KV_WIDTH), ("vc", KV_WIDTH), ("ks", KV_WIDTH),
                        ("vs", KV_WIDTH), ("kw", KV_WIDTH), ("vw", KV_WIDTH), ("g", 3 * NSA_HEADS),
                        ("z", 2 * SGU_WIDTH)):
        parts[name] = w_in[:, o:o + width]
        o += width
    wq = parts["q"] * (HEAD_DIM ** -0.5)
    gcols = 3 * NSA_REP
    wg = jnp.zeros((d, 2 * LANES), w_in.dtype)
    wg = wg.at[:, :gcols].set(parts["g"][:, :gcols]).at[:, LANES:LANES + gcols].set(parts["g"][:, gcols:])
    g = NSA_KV_GROUPS
    cols = [wq, _swap_halves(wq, NSA_HEADS),
            parts["kc"], _swap_halves(parts["kc"], g), parts["ks"], _swap_halves(parts["ks"], g),
            parts["kw"], _swap_halves(parts["kw"], g), parts["vc"], parts["vs"], parts["vw"],
            wg, parts["z"], w_merge]
    return jnp.concatenate(cols, axis=1).astype(BF16)


def _rope_tables(seq):
    half = HEAD_DIM // 2
    pos = jnp.arange(seq, dtype=F32)
    inv_freq = ROPE_THETA ** (-jnp.arange(half, dtype=F32) / half)
    ang = pos[:, None] * inv_freq[None, :]
    cos, sin = jnp.cos(ang), jnp.sin(ang)
    reps = LANES // HEAD_DIM
    cosk = jnp.tile(jnp.concatenate([cos, cos], axis=1), (1, reps))
    sink = jnp.tile(jnp.concatenate([-sin, sin], axis=1), (1, reps))
    return cosk, sink


def _sel_aggregation(n_cmp_pad, n_sel):
    c0 = jnp.arange(n_cmp_pad)[:, None] * CMP_STRIDE
    s0 = jnp.arange(n_sel)[None, :] * SEL_BLOCK
    ov = jnp.clip(jnp.minimum(c0 + CMP_BLOCK, s0 + SEL_BLOCK) - jnp.maximum(c0, s0), 0, None)
    return (ov / CMP_BLOCK).astype(BF16)


def _pack_table(tab):
    half = tab.shape[1] // 2
    bits = lax.bitcast_convert_type(tab.astype(BF16), jnp.uint16).astype(jnp.uint32)
    packed = (bits[:, half:] << 16) | bits[:, :half]
    return packed.reshape(tab.shape[0] * ROW_SUB, LANES)


def _split_groups(a, bsz, seq):
    return a.reshape(bsz, seq, NSA_KV_GROUPS, HEAD_DIM).transpose(0, 2, 1, 3)


def kernel(x, c, w_ada, b_ada, w_in, cmp_pos, cmp_w1, cmp_b1, cmp_w2, cmp_b2, sgu_ln_g, sgu_ln_b, sgu_w, sgu_b, w_branch, w_merge, b_merge, w_out, ln1_g, ln1_b, peer_wq, peer_keys, peer_u, peer_v, ln2_g, ln2_b):
    bsz, seq, d = x.shape
    n = bsz * seq
    depth = w_ada.shape[0]
    alpha = (2.0 * depth) ** 0.25
    n_cmp = (seq - CMP_BLOCK) // CMP_STRIDE + 1
    n_half = seq // CMP_STRIDE
    n_sel = seq // SEL_BLOCK
    cosk, sink = _rope_tables(seq)
    agg = _sel_aggregation(n_half, n_sel)
    x2 = x.reshape(n, d)
    for l in range(depth):
        mod = _ada(c, w_ada[l], b_ada[l])
        sh1, sc1, gt1, sh2, sc2, gt2 = [m.reshape(bsz, 1, d) for m in jnp.split(mod, 6, axis=-1)]

        wall = _fused_in_weight(w_in[l], w_merge[l])
        (q_t, ks_ext, vs_t, kw_g, vw_t, gn_t, kc_r, vc_r, u, v, gate) = _proj(
            x2, sc1, sh1, wall, b_merge[l].reshape(1, -1), cosk, sink,
            sgu_ln_g[l].reshape(1, -1), sgu_ln_b[l].reshape(1, -1), seq)

        t16 = jnp.stack([_split_groups(kc_r, bsz, seq), _split_groups(vc_r, bsz, seq)])
        t16 = t16.reshape(2, bsz, NSA_KV_GROUPS, n_half, CMP_STRIDE * HEAD_DIM)
        cmp_kv = _compress(t16, cmp_pos[l].reshape(2, 1, CMP_BLOCK * HEAD_DIM), cmp_w1[l],
                           cmp_b1[l].reshape(2, 1, CMP_HIDDEN), cmp_w2[l], cmp_b2[l].reshape(2, 1, HEAD_DIM))

        o_nsa_t = _nsa(q_t, cmp_kv[0], cmp_kv[1].transpose(0, 1, 3, 2), ks_ext, vs_t, kw_g, vw_t, gn_t, agg.T, n_cmp)

        bs = jnp.repeat(sgu_b[l].T, SGU_GROUP_DIM, axis=1)
        x1, h2, qp = _merge(o_nsa_t, u, v, gate, x2, gt1, sc2, sh2,
                            sgu_w[l], bs, w_branch[l].astype(BF16), w_out[l].astype(BF16),
                            ln1_g[l].reshape(1, d), ln1_b[l].reshape(1, d), peer_wq[l].astype(BF16), seq, alpha)

        idx_tok, gate_t = _retrieve(qp, peer_keys[l])
        w_tok = _expert_act(idx_tok, h2, gate_t, _pack_table(peer_u[l]))
        ffn = _expert_out(idx_tok, w_tok, _pack_table(peer_v[l]))
        x2 = _final(x1, ffn.reshape(n, d), gt2, ln2_g[l].reshape(1, d), ln2_b[l].reshape(1, d), seq, alpha)
    return x2.reshape(bsz, seq, d)
```

```python
import functools

import jax
import jax.numpy as jnp
from jax import lax
from jax.experimental import pallas as pl
from jax.experimental.pallas import tpu as pltpu

D_MODEL = 1024
NSA_HEADS = 8
NSA_KV_GROUPS = 2
NSA_REP = NSA_HEADS // NSA_KV_GROUPS
HEAD_DIM = 64
NSA_WIDTH = NSA_HEADS * HEAD_DIM
KV_WIDTH = NSA_KV_GROUPS * HEAD_DIM
CMP_BLOCK = 32
CMP_STRIDE = 16
CMP_HIDDEN = 2 * HEAD_DIM
SEL_BLOCK = 64
SEL_TOPK = 16
WINDOW = 512
ROPE_THETA = 10000.0
SGU_GROUPS = 8
SGU_WIDTH = D_MODEL // 2
SGU_GROUP_DIM = SGU_WIDTH // SGU_GROUPS
SGU_CHUNK = 128
PEER_HEADS = 8
PEER_NKEYS = 128
PEER_EXPERTS = PEER_NKEYS * PEER_NKEYS
PEER_QDIM = 256
PEER_HALF = PEER_QDIM // 2
PEER_TOPK = 16
PEER_PAIRS = PEER_HEADS * PEER_TOPK
LN_EPS = 1e-5
NEG_INF = -1e30
FORCE_SCORE = 1e9

LANES = 128
MIB = 1024 * 1024
BF16 = jnp.bfloat16
F32 = jnp.float32

TM_PROJ = 256
TM_MERGE = 256
TM_LN = 512
TQ = 128
TK = 256
KV_GROUP = 4
GROUP_BLOCKS = TK * KV_GROUP // SEL_BLOCK
GATE_ROWS = 16
TC = 128
TB = 128

_SEGMENTS = (("q", NSA_WIDTH), ("q_swap", NSA_WIDTH), ("kc", KV_WIDTH), ("kc_swap", KV_WIDTH),
             ("ks", KV_WIDTH), ("ks_swap", KV_WIDTH), ("kw", KV_WIDTH), ("kw_swap", KV_WIDTH),
             ("vc", KV_WIDTH), ("vs", KV_WIDTH), ("vw", KV_WIDTH), ("gates", 2 * LANES),
             ("z", 2 * SGU_WIDTH), ("merge", 2 * D_MODEL))


def _segment_offsets():
    offs, o = {}, 0
    for name, width in _SEGMENTS:
        offs[name] = o
        o += width
    return offs, o


_OFF, _WCOLS = _segment_offsets()
_OQ, _OQS, _OKC, _OKCS = _OFF["q"], _OFF["q_swap"], _OFF["kc"], _OFF["kc_swap"]
_OKS, _OKSS, _OKW, _OKWS = _OFF["ks"], _OFF["ks_swap"], _OFF["kw"], _OFF["kw_swap"]
_OVC, _OVS, _OVW, _OG, _OZ, _OM = _OFF["vc"], _OFF["vs"], _OFF["vw"], _OFF["gates"], _OFF["z"], _OFF["merge"]


VMEM_MIB_V7X = 64


def _cparams(n_axes, vmem_mib):
    assert vmem_mib < VMEM_MIB_V7X
    return pltpu.CompilerParams(
        dimension_semantics=("parallel",) * n_axes,
        vmem_limit_bytes=vmem_mib * MIB)


def _ln(x):
    mu = jnp.mean(x, axis=-1, keepdims=True)
    xc = x - mu
    var = jnp.mean(xc * xc, axis=-1, keepdims=True)
    return xc * lax.rsqrt(var + LN_EPS)


def _gelu(x):
    return 0.5 * x * (1.0 + lax.erf(x * (2.0 ** -0.5)))


def _dot(a, b):
    return jnp.dot(a, b, preferred_element_type=F32)


def _dot_nt(a, b):
    return lax.dot_general(a, b, (((1,), (1,)), ((), ())), preferred_element_type=F32)


def _ada_body(c_ref, w_ref, b_ref, o_ref):
    cv = c_ref[...]
    a = cv * jax.nn.sigmoid(cv)
    o_ref[...] = _dot(a.astype(BF16), w_ref[...].astype(BF16)) + b_ref[...]


def _ada(c, w, b):
    bsz, d = c.shape
    n = w.shape[1]
    tn = 1024
    return pl.pallas_call(
        _ada_body,
        grid=(n // tn,),
        in_specs=[pl.BlockSpec((bsz, d), lambda j: (0, 0)),
                  pl.BlockSpec((d, tn), lambda j: (0, j)),
                  pl.BlockSpec((1, tn), lambda j: (0, j))],
        out_specs=pl.BlockSpec((bsz, tn), lambda j: (0, j)),
        out_shape=jax.ShapeDtypeStruct((bsz, n), F32),
        compiler_params=_cparams(1, 32),
    )(c, w, b.reshape(1, n))


def _proj_body(x_ref, sc_ref, sh_ref, w_ref, bm_ref, cos_ref, sin_ref, sg_ref, sb_ref,
               q_ref, ks_ref, vs_ref, kw_ref, vw_ref, gn_ref, kc_ref, vc_ref, u_ref, v_ref, gate_ref,
               *, n_seq_tiles):
    h = _ln(x_ref[...]) * (1.0 + sc_ref[0]) + sh_ref[0]
    hb = h.astype(BF16)

    def proj(off, width):
        return _dot(hb, w_ref[:, off:off + width])

    cosk = cos_ref[...]
    sink = sin_ref[...]
    cosq = jnp.concatenate([cosk] * (NSA_WIDTH // LANES), axis=1)
    sinq = jnp.concatenate([sink] * (NSA_WIDTH // LANES), axis=1)
    q_ref[0] = (proj(_OQ, NSA_WIDTH) * cosq + proj(_OQS, NSA_WIDTH) * sinq).T
    kc_ref[...] = proj(_OKC, KV_WIDTH) * cosk + proj(_OKCS, KV_WIDTH) * sink
    vc_ref[...] = proj(_OVC, KV_WIDTH)
    tm = x_ref.shape[0]
    ks = proj(_OKS, KV_WIDTH) * cosk + proj(_OKSS, KV_WIDTH) * sink
    kw = proj(_OKW, KV_WIDTH) * cosk + proj(_OKWS, KV_WIDTH) * sink
    vs_t = proj(_OVS, KV_WIDTH).T
    vw_t = proj(_OVW, KV_WIDTH).T
    gn_t = jax.nn.sigmoid(proj(_OG, 2 * LANES)).T
    pos = (pl.program_id(0) % n_seq_tiles) * tm + lax.broadcasted_iota(jnp.int32, (tm, 1), 0)
    block_onehot = jnp.where((pos // SEL_BLOCK) % GROUP_BLOCKS
                             == lax.broadcasted_iota(jnp.int32, (1, GROUP_BLOCKS), 1), 1.0, 0.0)
    for g in range(NSA_KV_GROUPS):
        cols = slice(g * HEAD_DIM, (g + 1) * HEAD_DIM)
        ks_ref[0, g] = jnp.concatenate([ks[:, cols], block_onehot], axis=1).astype(BF16)
        kw_ref[0, g] = kw[:, cols].astype(BF16)
        vs_ref[0, g, 0] = vs_t[cols, :].astype(BF16)
        for c in range(tm // TQ):
            vw_ref[0, g, c] = vw_t[cols, c * TQ:(c + 1) * TQ].astype(BF16)
        gn_ref[0, g] = gn_t[g * LANES:g * LANES + GATE_ROWS, :]
    u_ref[...] = _gelu(proj(_OZ, SGU_WIDTH))
    v_ref[...] = _ln(_gelu(proj(_OZ + SGU_WIDTH, SGU_WIDTH))) * sg_ref[...] + sb_ref[...]
    gate_ref[...] = jax.nn.sigmoid(proj(_OM, 2 * D_MODEL) + bm_ref[...])


def _proj(x2, sc, sh, wall, bm, cosk, sink, sg, sb, seq):
    n, d = x2.shape
    tm = TM_PROJ
    nst = seq // tm
    row = lambda w: pl.BlockSpec((tm, w), lambda i: (i, 0))
    per_batch = pl.BlockSpec((1, 1, d), lambda i: (i // nst, 0, 0))
    const = lambda shp: pl.BlockSpec(shp, lambda i: (0,) * len(shp))
    pos = pl.BlockSpec((tm, LANES), lambda i: (i % nst, 0))
    outs = [(KV_WIDTH, F32), (KV_WIDTH, F32), (SGU_WIDTH, F32), (SGU_WIDTH, F32), (2 * D_MODEL, F32)]
    assert tm == TK and tm % TQ == 0
    bsz, g = n // seq, NSA_KV_GROUPS
    att = [((bsz, NSA_WIDTH, seq), (1, NSA_WIDTH, tm), lambda i: (i // nst, 0, i % nst), F32),
           ((bsz, g, seq, HEAD_DIM + GROUP_BLOCKS), (1, g, tm, HEAD_DIM + GROUP_BLOCKS),
            lambda i: (i // nst, 0, i % nst, 0), BF16),
           ((bsz, g, seq // TK, HEAD_DIM, TK), (1, g, 1, HEAD_DIM, TK), lambda i: (i // nst, 0, i % nst, 0, 0), BF16),
           ((bsz, g, seq, HEAD_DIM), (1, g, tm, HEAD_DIM), lambda i: (i // nst, 0, i % nst, 0), BF16),
           ((bsz, g, seq // TQ, HEAD_DIM, TQ), (1, g, tm // TQ, HEAD_DIM, TQ),
            lambda i: (i // nst, 0, i % nst, 0, 0), BF16),
           ((bsz, g, GATE_ROWS, seq), (1, g, GATE_ROWS, tm), lambda i: (i // nst, 0, 0, i % nst), F32)]
    return pl.pallas_call(
        functools.partial(_proj_body, n_seq_tiles=nst),
        grid=(n // tm,),
        in_specs=[row(d), per_batch, per_batch, const((d, _WCOLS)), const((1, 2 * D_MODEL)),
                  pos, pos, const((1, SGU_WIDTH)), const((1, SGU_WIDTH))],
        out_specs=[pl.BlockSpec(blk, imap) for _, blk, imap, _ in att] + [row(w) for w, _ in outs],
        out_shape=([jax.ShapeDtypeStruct(shp, dt) for shp, _, _, dt in att]
                   + [jax.ShapeDtypeStruct((n, w), dt) for w, dt in outs]),
        compiler_params=_cparams(1, 56),
    )(x2, sc, sh, wall, bm, cosk, sink, sg, sb)


def _cmp_body(t_ref, pos_ref, w1_ref, b1_ref, w2_ref, b2_ref, o_ref):
    half = (CMP_BLOCK // 2) * HEAD_DIM
    t = t_ref[0, 0, 0]
    pos = pos_ref[0]
    ta = (t + pos[:, :half]).astype(BF16)
    tb = (t + pos[:, half:]).astype(BF16)
    a = _dot(ta, w1_ref[0, :half, :].astype(BF16))
    b = _dot(tb, w1_ref[0, half:, :].astype(BF16))
    nrow = t.shape[0]
    b_next = pltpu.roll(b, nrow - 1, axis=0)
    hid = _gelu(a + b_next + b1_ref[0])
    out = _dot(hid.astype(BF16), w2_ref[0].astype(BF16)) + b2_ref[0]
    o_ref[0, 0, 0] = out.astype(o_ref.dtype)


def _compress(t16, pos, w1, b1, w2, b2):
    _, bsz, g, nr, dd = t16.shape
    per_kind = lambda shp: pl.BlockSpec((1,) + shp, lambda k, b, gg: (k,) + (0,) * len(shp))
    blk = lambda w: pl.BlockSpec((1, 1, 1, nr, w), lambda k, b, gg: (k, b, gg, 0, 0))
    return pl.pallas_call(
        _cmp_body,
        grid=(2, bsz, g),
        in_specs=[blk(dd), per_kind((1, CMP_BLOCK * HEAD_DIM)), per_kind((CMP_BLOCK * HEAD_DIM, CMP_HIDDEN)),
                  per_kind((1, CMP_HIDDEN)), per_kind((CMP_HIDDEN, HEAD_DIM)), per_kind((1, HEAD_DIM))],
        out_specs=blk(HEAD_DIM),
        out_shape=jax.ShapeDtypeStruct((2, bsz, g, nr, HEAD_DIM), BF16),
        compiler_params=_cparams(3, 48),
    )(t16, pos, w1, b1, w2, b2)


def _scores(q_all, kt, bias):
    s = _dot(kt, q_all)
    return s if bias is None else s + bias


def _max_init():
    return jnp.full((1, NSA_REP * TQ), NEG_INF, F32)


def _acc_init():
    w = NSA_REP * TQ
    return jnp.zeros((1, w), F32), jnp.zeros((HEAD_DIM, w), F32)


def _lane_tile(a):
    return jnp.concatenate([a] * NSA_REP, axis=1)


def _nsa_body(q_ref, kc_ref, vc_ref, ks_ref, vs_ref, kw_ref, vw_ref, gn_ref, agg_ref, o_ref, selneg_ref, score_ref, *, n_cmp):
    qi = pl.program_id(2)
    q0 = qi * TQ
    t1 = q0 + lax.broadcasted_iota(jnp.int32, (1, TQ), 1)
    t = _lane_tile(t1)
    q_all = jnp.concatenate([q_ref[0, r * HEAD_DIM:(r + 1) * HEAD_DIM, :] for r in range(NSA_REP)],
                            axis=1).astype(BF16)

    kcm = kc_ref[0, 0]
    ncp = kcm.shape[0]
    nidx = lax.broadcasted_iota(jnp.int32, (ncp, 1), 0)
    mask_c = (nidx * CMP_STRIDE + (CMP_BLOCK - 1) <= t) & (nidx < n_cmp)
    sm = jnp.where(mask_c, _dot(kcm, q_all), NEG_INF)
    e = jnp.where(mask_c, jnp.exp(sm - jnp.max(sm, axis=0, keepdims=True)), 0.0)
    l = jnp.sum(e, axis=0, keepdims=True)
    p = e * (1.0 / jnp.where(l > 0.0, l, 1.0))
    o_c = _dot(vc_ref[0, 0], p.astype(BF16))
    psum = p[:, :TQ]
    for r in range(1, NSA_REP):
        psum = psum + p[:, r * TQ:(r + 1) * TQ]

    p_hi = psum.astype(BF16)
    p_lo = (psum - p_hi.astype(F32)).astype(BF16)
    agg_t = agg_ref[...]
    imp = _dot(agg_t, p_hi) + _dot(agg_t, p_lo)
    n_sel = imp.shape[0]
    blk = lax.broadcasted_iota(jnp.int32, (n_sel, 1), 0)
    blk_f = blk.astype(F32)
    cur = t1 // SEL_BLOCK
    forced = (blk == 0) | (blk == cur) | (blk == cur - 1)
    score = jnp.where(forced, FORCE_SCORE, jnp.where(blk <= cur, imp, NEG_INF))
    sel = jnp.zeros((n_sel, TQ), jnp.bool_)
    for _ in range(min(SEL_TOPK, n_sel)):
        m = jnp.max(score, axis=0, keepdims=True)
        first = jnp.min(jnp.where(score == m, blk_f, float(n_sel)), axis=0, keepdims=True)
        hit = blk_f == first
        sel = sel | hit
        score = jnp.where(hit, -jnp.inf, score)
    selneg_ref[...] = _lane_tile(jnp.where(sel, 0.0, NEG_INF).astype(BF16))

    def q_ext(jg):
        rows = selneg_ref[pl.ds(pl.multiple_of(jg * GROUP_BLOCKS, GROUP_BLOCKS), GROUP_BLOCKS), :]
        return jnp.concatenate([q_all, rows], axis=0)

    krow_w = lax.broadcasted_iota(jnp.int32, (TQ, 1), 0)

    def win_chunk(c):
        kpos = q0 - WINDOW + c * TQ + krow_w
        dlt = t - kpos
        bias = jnp.where((dlt >= 0) & (dlt < WINDOW) & (kpos >= 0), 0.0, NEG_INF)
        start = jnp.maximum(q0 - WINDOW + c * TQ, 0)
        return kw_ref[0, 0, pl.ds(pl.multiple_of(start, TQ), TQ), :], bias

    n_win = WINDOW // TQ + 1
    s_w = [_scores(q_all, *win_chunk(c)) for c in range(n_win)]
    m_w = functools.reduce(jnp.maximum, [jnp.max(s, axis=0, keepdims=True) for s in s_w])
    l_w, acc_w = _acc_init()
    for c in range(n_win):
        p = jnp.exp(s_w[c] - m_w)
        l_w = l_w + jnp.sum(p, axis=0, keepdims=True)
        acc_w = acc_w + _dot(vw_ref[0, 0, jnp.maximum(qi + c - WINDOW // TQ, 0)], p.astype(BF16))
    o_w = acc_w / l_w

    krow = lax.broadcasted_iota(jnp.int32, (TK, 1), 0)
    group_rows = TK * KV_GROUP

    def max_group(jg, m):
        k0 = pl.multiple_of(jg * group_rows, group_rows)
        s = _scores(q_ext(jg), ks_ref[0, 0, pl.ds(k0, group_rows), :], None)
        score_ref[pl.ds(k0, group_rows), :] = s
        return jnp.maximum(m, jnp.max(s, axis=0, keepdims=True))

    def max_tile(j, m):
        k0 = pl.multiple_of(j * TK, TK)
        bias = jnp.where(k0 + krow <= t, 0.0, NEG_INF)
        s = _scores(q_ext(j // KV_GROUP), ks_ref[0, 0, pl.ds(k0, TK), :], bias)
        score_ref[pl.ds(k0, TK), :] = s
        return jnp.maximum(m, jnp.max(s, axis=0, keepdims=True))

    def sum_rows(state, k0, rows, vt_t):
        l, acc = state
        p = jnp.exp(score_ref[pl.ds(k0, rows), :] - m_s)
        return l + jnp.sum(p, axis=0, keepdims=True), acc + _dot(vt_t, p.astype(BF16))

    def sum_group(jg, state):
        vt_t = jnp.concatenate([vs_ref[0, 0, jg * KV_GROUP + sub] for sub in range(KV_GROUP)], axis=1)
        return sum_rows(state, pl.multiple_of(jg * group_rows, group_rows), group_rows, vt_t)

    def sum_tile(j, state):
        return sum_rows(state, pl.multiple_of(j * TK, TK), TK, vs_ref[0, 0, j])

    n_full = q0 // group_rows
    n_kv = (q0 + TQ - 1) // TK + 1
    m_s = lax.fori_loop(0, n_full, max_group, _max_init())
    m_s = lax.fori_loop(n_full * KV_GROUP, n_kv, max_tile, m_s)
    state = lax.fori_loop(0, n_full, sum_group, _acc_init())
    state = lax.fori_loop(n_full * KV_GROUP, n_kv, sum_tile, state)
    o_s = state[1] / state[0]

    gn = gn_ref[0, 0]
    gate = [jnp.concatenate([gn[3 * r + c:3 * r + c + 1, :] for r in range(NSA_REP)], axis=1) for c in range(3)]
    o = gate[0] * o_c + gate[1] * o_s + gate[2] * o_w
    for r in range(NSA_REP):
        o_ref[0, r * HEAD_DIM:(r + 1) * HEAD_DIM, :] = o[:, r * TQ:(r + 1) * TQ]


def _nsa(q_t, kc, vc_t, ks, vs_t, kwp, vwp_t, gn_t, agg_t, n_cmp):
    bsz, _, seq = q_t.shape
    g = NSA_KV_GROUPS
    gw = NSA_REP * HEAD_DIM
    per_bg = lambda a: pl.BlockSpec((1, 1) + a.shape[2:], lambda b, gg, i: (b, gg) + (0,) * (a.ndim - 2))
    return pl.pallas_call(
        functools.partial(_nsa_body, n_cmp=n_cmp),
        grid=(bsz, g, seq // TQ),
        in_specs=[pl.BlockSpec((1, gw, TQ), lambda b, gg, i: (b, gg, i)),
                  per_bg(kc), per_bg(vc_t), per_bg(ks), per_bg(vs_t), per_bg(kwp), per_bg(vwp_t),
                  pl.BlockSpec((1, 1, gn_t.shape[2], TQ), lambda b, gg, i: (b, gg, 0, i)),
                  pl.BlockSpec(agg_t.shape, lambda b, gg, i: (0, 0))],
        out_specs=pl.BlockSpec((1, gw, TQ), lambda b, gg, i: (b, gg, i)),
        out_shape=jax.ShapeDtypeStruct((bsz, NSA_WIDTH, seq), F32),
        scratch_shapes=[pltpu.VMEM((agg_t.shape[0], NSA_REP * TQ), BF16),
                        pltpu.VMEM((seq, NSA_REP * TQ), F32)],
        compiler_params=_cparams(3, 56),
    )(q_t, kc, vc_t, ks, vs_t, kwp, vwp_t, gn_t, agg_t)


def _merge_body(on_ref, u_ref, v_ref, gate_ref, x_ref, gt_ref, sc_ref, sh_ref, ws_ref, bs_ref,
                wb_ref, wo_ref, g1_ref, b1_ref, wq_ref, x1_ref, h2_ref, qp_ref, *, alpha):
    tm = x_ref.shape[0]
    row = lax.broadcasted_iota(jnp.int32, (SGU_CHUNK, SGU_CHUNK), 0)
    col = lax.broadcasted_iota(jnp.int32, (SGU_CHUNK, SGU_CHUNK), 1)
    lane_group = lax.broadcasted_iota(jnp.int32, (1, SGU_WIDTH), 1) // SGU_GROUP_DIM
    mixed = []
    for c in range(tm // SGU_CHUNK):
        v = v_ref[c * SGU_CHUNK:(c + 1) * SGU_CHUNK, :]
        acc = jnp.zeros((SGU_CHUNK, SGU_WIDTH), F32)
        for g in range(SGU_GROUPS):
            wg = jnp.where(col <= row, ws_ref[g], 0.0).astype(BF16)
            vg = jnp.where(lane_group == g, v, 0.0).astype(BF16)
            acc = acc + _dot(wg, vg)
        mixed.append(acc + bs_ref[...])
    o_sgu = u_ref[...] * jnp.concatenate(mixed, axis=0)
    gate = gate_ref[...]
    merged = (gate[:, :D_MODEL] * _dot(on_ref[0].T.astype(BF16), wb_ref[0])
              + gate[:, D_MODEL:] * _dot(o_sgu.astype(BF16), wb_ref[1]))
    mix = _dot(merged.astype(BF16), wo_ref[...])
    x1 = _ln(alpha * x_ref[...] + gt_ref[0] * mix) * g1_ref[...] + b1_ref[...]
    x1_ref[...] = x1
    h2 = _ln(x1) * (1.0 + sc_ref[0]) + sh_ref[0]
    h2_ref[...] = h2
    qp_ref[...] = _dot(h2.astype(BF16), wq_ref[...])


def _merge(o_nsa, u, v, gate, x2, gt1, sc2, sh2, ws, bs, wb, wo, g1, b1, wq, seq, alpha):
    n, d = x2.shape
    tm = TM_MERGE
    nst = seq // tm
    row = lambda w: pl.BlockSpec((tm, w), lambda i: (i, 0))
    per_batch = pl.BlockSpec((1, 1, d), lambda i: (i // nst, 0, 0))
    const = lambda a: pl.BlockSpec(a.shape, lambda i: (0,) * a.ndim)
    nq = wq.shape[1]
    return pl.pallas_call(
        functools.partial(_merge_body, alpha=alpha),
        grid=(n // tm,),
        in_specs=[pl.BlockSpec((1, NSA_WIDTH, tm), lambda i: (i // nst, 0, i % nst)),
                  row(SGU_WIDTH), row(SGU_WIDTH), row(2 * d), row(d),
                  per_batch, per_batch, per_batch, const(ws), const(bs), const(wb), const(wo),
                  const(g1), const(b1), const(wq)],
        out_specs=[row(d), row(d), row(nq)],
        out_shape=[jax.ShapeDtypeStruct((n, d), F32), jax.ShapeDtypeStruct((n, d), F32),
                   jax.ShapeDtypeStruct((n, nq), F32)],
        compiler_params=_cparams(1, 56),
    )(o_nsa, u, v, gate, x2, gt1, sc2, sh2, ws, bs, wb, wo, g1, b1, wq)


def _topk_rows(s, k):
    nrow = s.shape[0]
    rid = lax.broadcasted_iota(jnp.int32, s.shape, 0).astype(F32)
    vals, rows = [], []
    for _ in range(k):
        m = jnp.max(s, axis=0, keepdims=True)
        first = jnp.min(jnp.where(s == m, rid, float(nrow)), axis=0, keepdims=True)
        s = jnp.where(rid == first, -jnp.inf, s)
        vals.append(m)
        rows.append(first)
    return vals, rows


def _candidate_blocks(k):
    todo = {(a, b) for a in range(k) for b in range(k) if (a + 1) * (b + 1) <= k}
    options = [(axis, fixed, start) for axis in (0, 1) for fixed in range(k) for start in range(0, k, 8)]
    cells = lambda axis, fixed, start: [((fixed, start + r) if axis == 0 else (start + r, fixed)) for r in range(8)]
    blocks = []
    while todo:
        best = max(options, key=lambda o: sum(c in todo for c in cells(*o)))
        owned = tuple(c in todo for c in cells(*best))
        todo -= set(cells(*best))
        blocks.append(best + (owned,))
    return tuple(blocks)


_CAND_BLOCKS = _candidate_blocks(PEER_TOPK)


def _retrieve_body(qp_ref, keys_ref, idx_ref, gate_ref, eid_ref):
    k = PEER_TOPK
    for h in range(PEER_HEADS):
        tops = []
        for c in range(2):
            off = (h * 2 + c) * PEER_HALF
            qh = qp_ref[:, off:off + PEER_HALF].astype(BF16)
            s = _dot_nt(keys_ref[h, c].astype(BF16), qh)
            tops.append(_topk_rows(s, k))
        (v1, i1), (v2, i2) = tops
        vm = (jnp.concatenate(v1, axis=0), jnp.concatenate(v2, axis=0))
        im = (jnp.concatenate(i1, axis=0), jnp.concatenate(i2, axis=0))
        sub = lax.broadcasted_iota(jnp.int32, (8, 1), 0)
        vals, eids, poss = [], [], []
        for axis, fixed, start, owned in _CAND_BLOCKS:
            if axis == 0:
                v = v1[fixed] + vm[1][start:start + 8]
                e = i1[fixed] * float(PEER_NKEYS) + im[1][start:start + 8]
                pos = fixed * k + start + sub
            else:
                v = vm[0][start:start + 8] + v2[fixed]
                e = im[0][start:start + 8] * float(PEER_NKEYS) + i2[fixed]
                pos = (start + sub) * k + fixed
            if not all(owned):
                keep = functools.reduce(jnp.logical_or, [sub == r for r in range(8) if owned[r]])
                v = jnp.where(keep, v, -jnp.inf)
                pos = jnp.where(keep, pos, k * k)
            vals.append(v)
            eids.append(e)
            poss.append(pos.astype(F32))
        cand = jnp.concatenate(vals, axis=0)
        cidx = jnp.concatenate(eids, axis=0)
        rid = jnp.concatenate(poss, axis=0)
        top_s, top_e = [], []
        for _ in range(k):
            m = jnp.max(cand, axis=0, keepdims=True)
            first = jnp.min(jnp.where(cand == m, rid, float(k * k)), axis=0, keepdims=True)
            hit = rid == first
            top_e.append(jnp.sum(jnp.where(hit, cidx, 0.0), axis=0, keepdims=True))
            cand = jnp.where(hit, -jnp.inf, cand)
            top_s.append(m)
        ts = jnp.concatenate(top_s, axis=0)
        e = jnp.exp(ts - ts[0:1])
        gate_ref[0, h * k:(h + 1) * k, :] = e / jnp.sum(e, axis=0, keepdims=True)
        eid_ref[h * k:(h + 1) * k, :] = jnp.concatenate(top_e, axis=0)
    idx_ref[...] = (eid_ref[...].T * float(ROW_SUB)).astype(jnp.int32)


def _retrieve(qp, keys):
    n = qp.shape[0]
    nt = n // TC
    out = pl.BlockSpec((1, PEER_PAIRS, TC), lambda i: (i, 0, 0))
    return pl.pallas_call(
        _retrieve_body,
        grid=(nt,),
        in_specs=[pl.BlockSpec((TC, qp.shape[1]), lambda i: (i, 0)),
                  pl.BlockSpec(keys.shape, lambda i: (0, 0, 0, 0))],
        out_specs=[pl.BlockSpec((TC, PEER_PAIRS), lambda i: (i, 0)), out],
        out_shape=[jax.ShapeDtypeStruct((n, PEER_PAIRS), jnp.int32),
                   jax.ShapeDtypeStruct((nt, PEER_PAIRS, TC), F32)],
        scratch_shapes=[pltpu.VMEM((PEER_PAIRS, TC), F32)],
        compiler_params=_cparams(1, 48),
    )(qp, keys)


ROW_SUB = D_MODEL // 2 // LANES
ACT_UNROLL = 16
OFF_REGS = 8


def _load_row(tab_ref, scaled_idx):
    row = tab_ref[pl.ds(pl.multiple_of(scaled_idx, ROW_SUB), ROW_SUB), :]
    lo = lax.bitcast_convert_type(lax.shift_left(row, jnp.uint32(16)), F32)
    hi = lax.bitcast_convert_type(row & jnp.uint32(0xFFFF0000), F32)
    return lo, hi


def _expert_act_body(off_ref, idx_ref, h_ref, gate_ref, tab_ref, w_ref, slot_ref, part_ref):
    off = [off_ref[j] for j in range(OFF_REGS)]

    def token(t, carry):
        x = h_ref[pl.ds(t, 1), :]
        chunk = lambda s: x[:, s * LANES:(s + 1) * LANES]
        xlo = jnp.concatenate([chunk(s) for s in range(ROW_SUB)], axis=0)
        xhi = jnp.concatenate([chunk(ROW_SUB + s) for s in range(ROW_SUB)], axis=0)
        base = pl.multiple_of(t * PEER_PAIRS, PEER_PAIRS)
        for p in range(PEER_PAIRS):
            if p % OFF_REGS == 0:
                idx_sub = idx_ref.at[t, pl.ds(p, OFF_REGS)]
            lo, hi = _load_row(tab_ref, idx_sub[off[p % OFF_REGS]])
            slot_ref[p * ROW_SUB:(p + 1) * ROW_SUB, :] = lo * xlo + hi * xhi
        part = slot_ref[pl.ds(0, PEER_PAIRS, stride=ROW_SUB), :]
        for s in range(1, ROW_SUB):
            part = part + slot_ref[pl.ds(s, PEER_PAIRS, stride=ROW_SUB), :]
        part_ref[pl.ds(pl.multiple_of(base, PEER_PAIRS), PEER_PAIRS), :] = part
        return carry

    lax.fori_loop(0, TC, token, 0)

    tok_lane = lax.broadcasted_iota(jnp.int32, (PEER_PAIRS, TC), 1)

    def lane_sums(i, act):
        for k in range(ACT_UNROLL):
            t = i * ACT_UNROLL + k
            part = part_ref[pl.ds(pl.multiple_of(t * PEER_PAIRS, PEER_PAIRS), PEER_PAIRS), :]
            act = jnp.where(tok_lane == t, jnp.sum(part, axis=1, keepdims=True), act)
        return act

    act = lax.fori_loop(0, TC // ACT_UNROLL, lane_sums, jnp.zeros((PEER_PAIRS, TC), F32))
    w_ref[...] = (gate_ref[0] * _gelu(act)).T


def _expert_act(idx_tok, h2, gate_t, tab):
    nt = gate_t.shape[0]
    tile = pl.BlockSpec((1, PEER_PAIRS, TC), lambda i: (i, 0, 0))
    return pl.pallas_call(
        _expert_act_body,
        grid=(nt,),
        in_specs=[pl.BlockSpec(memory_space=pltpu.SMEM),
                  pl.BlockSpec((TC, PEER_PAIRS), lambda i: (i, 0), memory_space=pltpu.SMEM),
                  pl.BlockSpec((TC, h2.shape[1]), lambda i: (i, 0)),
                  tile,
                  pl.BlockSpec(tab.shape, lambda i: (0, 0), pipeline_mode=pl.Buffered(1))],
        out_specs=pl.BlockSpec((TC, PEER_PAIRS), lambda i: (i, 0)),
        out_shape=jax.ShapeDtypeStruct((nt * TC, PEER_PAIRS), F32),
        scratch_shapes=[pltpu.VMEM((PEER_PAIRS * ROW_SUB, LANES), F32),
                        pltpu.VMEM((TC * PEER_PAIRS, LANES), F32)],
        compiler_params=_cparams(1, 52),
    )(jnp.arange(OFF_REGS, dtype=jnp.int32), idx_tok, h2, gate_t, tab)


def _expert_out_body(off_ref, idx_ref, w_ref, tab_ref, o_ref):
    n_acc = 4
    off = [off_ref[j] for j in range(OFF_REGS)]

    def token(t, carry):
        acc_lo = [jnp.zeros((ROW_SUB, LANES), F32) for _ in range(n_acc)]
        acc_hi = [jnp.zeros((ROW_SUB, LANES), F32) for _ in range(n_acc)]
        for p in range(PEER_PAIRS):
            if p % OFF_REGS == 0:
                idx_sub = idx_ref.at[t, pl.ds(p, OFF_REGS)]
                w_sub = w_ref.at[t, pl.ds(p, OFF_REGS)]
            lo, hi = _load_row(tab_ref, idx_sub[off[p % OFF_REGS]])
            w = w_sub[off[p % OFF_REGS]]
            acc_lo[p % n_acc] = acc_lo[p % n_acc] + w * lo
            acc_hi[p % n_acc] = acc_hi[p % n_acc] + w * hi
        lo = (acc_lo[0] + acc_lo[1]) + (acc_lo[2] + acc_lo[3])
        hi = (acc_hi[0] + acc_hi[1]) + (acc_hi[2] + acc_hi[3])
        o_ref[t] = jnp.concatenate([lo, hi], axis=0)
        return carry

    lax.fori_loop(0, TB, token, 0)


def _expert_out(idx_tok, w_tok, tab):
    n = idx_tok.shape[0]
    smem = pl.BlockSpec((TB, PEER_PAIRS), lambda i: (i, 0), memory_space=pltpu.SMEM)
    return pl.pallas_call(
        _expert_out_body,
        grid=(n // TB,),
        in_specs=[pl.BlockSpec(memory_space=pltpu.SMEM), smem, smem,
                  pl.BlockSpec(tab.shape, lambda i: (0, 0), pipeline_mode=pl.Buffered(1))],
        out_specs=pl.BlockSpec((TB, D_MODEL // LANES, LANES), lambda i: (i, 0, 0)),
        out_shape=jax.ShapeDtypeStruct((n, D_MODEL // LANES, LANES), F32),
        compiler_params=_cparams(1, 48),
    )(jnp.arange(OFF_REGS, dtype=jnp.int32), idx_tok, w_tok, tab)


def _final_body(x1_ref, f_ref, gt_ref, g_ref, b_ref, o_ref, *, alpha):
    o_ref[...] = _ln(alpha * x1_ref[...] + gt_ref[0] * f_ref[...]) * g_ref[...] + b_ref[...]


def _final(x1, ffn, gt2, g2, b2, seq, alpha):
    n, d = x1.shape
    tm = TM_LN
    nst = seq // tm
    row = pl.BlockSpec((tm, d), lambda i: (i, 0))
    const = pl.BlockSpec((1, d), lambda i: (0, 0))
    return pl.pallas_call(
        functools.partial(_final_body, alpha=alpha),
        grid=(n // tm,),
        in_specs=[row, row, pl.BlockSpec((1, 1, d), lambda i: (i // nst, 0, 0)), const, const],
        out_specs=row,
        out_shape=jax.ShapeDtypeStruct((n, d), F32),
        compiler_params=_cparams(1, 32),
    )(x1, ffn, gt2, g2, b2)


def _swap_halves(w, heads):
    d = w.shape[0]
    w4 = w.reshape(d, heads, 2, HEAD_DIM // 2)
    return jnp.flip(w4, axis=2).reshape(d, heads * HEAD_DIM)


def _fused_in_weight(w_in, w_merge):
    d = w_in.shape[0]
    o = 0
    parts = {}
    for name, width in (("q", NSA_WIDTH), ("kc", KV_WIDTH), ("vc", KV_WIDTH), ("ks", KV_WIDTH),
                        ("vs", KV_WIDTH), ("kw", KV_WIDTH), ("vw", KV_WIDTH), ("g", 3 * NSA_HEADS),
                        ("z", 2 * SGU_WIDTH)):
        parts[name] = w_in[:, o:o + width]
        o += width
    wq = parts["q"] * (HEAD_DIM ** -0.5)
    gcols = 3 * NSA_REP
    wg = jnp.zeros((d, 2 * LANES), w_in.dtype)
    wg = wg.at[:, :gcols].set(parts["g"][:, :gcols]).at[:, LANES:LANES + gcols].set(parts["g"][:, gcols:])
    g = NSA_KV_GROUPS
    cols = {"q": wq, "q_swap": _swap_halves(wq, NSA_HEADS),
            "kc": parts["kc"], "kc_swap": _swap_halves(parts["kc"], g),
            "ks": parts["ks"], "ks_swap": _swap_halves(parts["ks"], g),
            "kw": parts["kw"], "kw_swap": _swap_halves(parts["kw"], g),
            "vc": parts["vc"], "vs": parts["vs"], "vw": parts["vw"],
            "gates": wg, "z": parts["z"], "merge": w_merge}
    assert all(cols[name].shape[1] == width for name, width in _SEGMENTS)
    return jnp.concatenate([cols[name] for name, _ in _SEGMENTS], axis=1).astype(BF16)


def _rope_tables(seq):
    half = HEAD_DIM // 2
    pos = jnp.arange(seq, dtype=F32)
    inv_freq = ROPE_THETA ** (-jnp.arange(half, dtype=F32) / half)
    ang = pos[:, None] * inv_freq[None, :]
    cos, sin = jnp.cos(ang), jnp.sin(ang)
    reps = LANES // HEAD_DIM
    cosk = jnp.tile(jnp.concatenate([cos, cos], axis=1), (1, reps))
    sink = jnp.tile(jnp.concatenate([-sin, sin], axis=1), (1, reps))
    return cosk, sink


def _sel_aggregation(n_cmp_pad, n_sel):
    c0 = jnp.arange(n_cmp_pad)[:, None] * CMP_STRIDE
    s0 = jnp.arange(n_sel)[None, :] * SEL_BLOCK
    ov = jnp.clip(jnp.minimum(c0 + CMP_BLOCK, s0 + SEL_BLOCK) - jnp.maximum(c0, s0), 0, None)
    return (ov / CMP_BLOCK).astype(BF16)


def _pack_table(tab):
    half = tab.shape[1] // 2
    bits = lax.bitcast_convert_type(tab.astype(BF16), jnp.uint16).astype(jnp.uint32)
    packed = (bits[:, half:] << 16) | bits[:, :half]
    return packed.reshape(tab.shape[0] * ROW_SUB, LANES)


def _split_groups(a, bsz, seq):
    return a.reshape(bsz, seq, NSA_KV_GROUPS, HEAD_DIM).transpose(0, 2, 1, 3)


def kernel(x, c, w_ada, b_ada, w_in, cmp_pos, cmp_w1, cmp_b1, cmp_w2, cmp_b2, sgu_ln_g, sgu_ln_b, sgu_w, sgu_b, w_branch, w_merge, b_merge, w_out, ln1_g, ln1_b, peer_wq, peer_keys, peer_u, peer_v, ln2_g, ln2_b):
    bsz, seq, d = x.shape
    n = bsz * seq
    depth = w_ada.shape[0]
    alpha = (2.0 * depth) ** 0.25
    n_cmp = (seq - CMP_BLOCK) // CMP_STRIDE + 1
    n_half = seq // CMP_STRIDE
    n_sel = seq // SEL_BLOCK
    cosk, sink = _rope_tables(seq)
    agg = _sel_aggregation(n_half, n_sel)
    x2 = x.reshape(n, d)
    for l in range(depth):
        mod = _ada(c, w_ada[l], b_ada[l])
        sh1, sc1, gt1, sh2, sc2, gt2 = [m.reshape(bsz, 1, d) for m in jnp.split(mod, 6, axis=-1)]

        wall = _fused_in_weight(w_in[l], w_merge[l])
        (q_t, ks_ext, vs_t, kw_g, vw_t, gn_t, kc_r, vc_r, u, v, gate) = _proj(
            x2, sc1, sh1, wall, b_merge[l].reshape(1, -1), cosk, sink,
            sgu_ln_g[l].reshape(1, -1), sgu_ln_b[l].reshape(1, -1), seq)

        t16 = jnp.stack([_split_groups(kc_r, bsz, seq), _split_groups(vc_r, bsz, seq)])
        t16 = t16.reshape(2, bsz, NSA_KV_GROUPS, n_half, CMP_STRIDE * HEAD_DIM)
        cmp_kv = _compress(t16, cmp_pos[l].reshape(2, 1, CMP_BLOCK * HEAD_DIM), cmp_w1[l],
                           cmp_b1[l].reshape(2, 1, CMP_HIDDEN), cmp_w2[l], cmp_b2[l].reshape(2, 1, HEAD_DIM))

        o_nsa_t = _nsa(q_t, cmp_kv[0], cmp_kv[1].transpose(0, 1, 3, 2), ks_ext, vs_t, kw_g, vw_t, gn_t, agg.T, n_cmp)

        bs = jnp.repeat(sgu_b[l].T, SGU_GROUP_DIM, axis=1)
        x1, h2, qp = _merge(o_nsa_t, u, v, gate, x2, gt1, sc2, sh2,
                            sgu_w[l], bs, w_branch[l].astype(BF16), w_out[l].astype(BF16),
                            ln1_g[l].reshape(1, d), ln1_b[l].reshape(1, d), peer_wq[l].astype(BF16), seq, alpha)

        idx_tok, gate_t = _retrieve(qp, peer_keys[l])
        w_tok = _expert_act(idx_tok, h2, gate_t, _pack_table(peer_u[l]))
        ffn = _expert_out(idx_tok, w_tok, _pack_table(peer_v[l]))
        x2 = _final(x1, ffn.reshape(n, d), gt2, ln2_g[l].reshape(1, d), ln2_b[l].reshape(1, d), seq, alpha)
    return x2.reshape(bsz, seq, d)
```

```python
import functools

import jax
import jax.numpy as jnp
from jax import lax
from jax.experimental import pallas as pl
from jax.experimental.pallas import tpu as pltpu

D_MODEL = 1024
NSA_HEADS = 8
NSA_KV_GROUPS = 2
NSA_REP = NSA_HEADS // NSA_KV_GROUPS
HEAD_DIM = 64
NSA_WIDTH = NSA_HEADS * HEAD_DIM
KV_WIDTH = NSA_KV_GROUPS * HEAD_DIM
CMP_BLOCK = 32
CMP_STRIDE = 16
CMP_HIDDEN = 2 * HEAD_DIM
SEL_BLOCK = 64
SEL_TOPK = 16
WINDOW = 512
ROPE_THETA = 10000.0
SGU_GROUPS = 8
SGU_WIDTH = D_MODEL // 2
SGU_GROUP_DIM = SGU_WIDTH // SGU_GROUPS
SGU_CHUNK = 128
PEER_HEADS = 8
PEER_NKEYS = 128
PEER_EXPERTS = PEER_NKEYS * PEER_NKEYS
PEER_QDIM = 256
PEER_HALF = PEER_QDIM // 2
PEER_TOPK = 16
PEER_PAIRS = PEER_HEADS * PEER_TOPK
LN_EPS = 1e-5
NEG_INF = -1e30
FORCE_SCORE = 1e9

LANES = 128
MIB = 1024 * 1024
BF16 = jnp.bfloat16
F32 = jnp.float32

TM_PROJ = 256
TM_MERGE = 256
TM_LN = 512
TQ = 128
TK = 256
KV_GROUP = 4
GROUP_BLOCKS = TK * KV_GROUP // SEL_BLOCK
GATE_ROWS = 16
TC = 128
TB = 128

_SEGMENTS = (("q", NSA_WIDTH), ("q_swap", NSA_WIDTH), ("kc", KV_WIDTH), ("kc_swap", KV_WIDTH),
             ("ks", KV_WIDTH), ("ks_swap", KV_WIDTH), ("kw", KV_WIDTH), ("kw_swap", KV_WIDTH),
             ("vc", KV_WIDTH), ("vs", KV_WIDTH), ("vw", KV_WIDTH), ("gates", 2 * LANES),
             ("z", 2 * SGU_WIDTH), ("merge", 2 * D_MODEL))


def _segment_offsets():
    offs, o = {}, 0
    for name, width in _SEGMENTS:
        offs[name] = o
        o += width
    return offs, o


_OFF, _WCOLS = _segment_offsets()
_OQ, _OQS, _OKC, _OKCS = _OFF["q"], _OFF["q_swap"], _OFF["kc"], _OFF["kc_swap"]
_OKS, _OKSS, _OKW, _OKWS = _OFF["ks"], _OFF["ks_swap"], _OFF["kw"], _OFF["kw_swap"]
_OVC, _OVS, _OVW, _OG, _OZ, _OM = _OFF["vc"], _OFF["vs"], _OFF["vw"], _OFF["gates"], _OFF["z"], _OFF["merge"]


VMEM_MIB_V7X = 64


def _cparams(n_axes, vmem_mib):
    assert vmem_mib < VMEM_MIB_V7X
    return pltpu.CompilerParams(
        dimension_semantics=("parallel",) * n_axes,
        vmem_limit_bytes=vmem_mib * MIB)


def _ln(x):
    mu = jnp.mean(x, axis=-1, keepdims=True)
    xc = x - mu
    var = jnp.mean(xc * xc, axis=-1, keepdims=True)
    return xc * lax.rsqrt(var + LN_EPS)


def _gelu(x):
    return 0.5 * x * (1.0 + lax.erf(x * (2.0 ** -0.5)))


def _dot(a, b):
    return jnp.dot(a, b, preferred_element_type=F32)


def _dot_nt(a, b):
    return lax.dot_general(a, b, (((1,), (1,)), ((), ())), preferred_element_type=F32)


def _ada_body(c_ref, w_ref, b_ref, o_ref):
    cv = c_ref[...]
    a = cv * jax.nn.sigmoid(cv)
    o_ref[...] = _dot(a.astype(BF16), w_ref[...].astype(BF16)) + b_ref[...]


def _ada(c, w, b):
    bsz, d = c.shape
    n = w.shape[1]
    tn = 1024
    return pl.pallas_call(
        _ada_body,
        grid=(n // tn,),
        in_specs=[pl.BlockSpec((bsz, d), lambda j: (0, 0)),
                  pl.BlockSpec((d, tn), lambda j: (0, j)),
                  pl.BlockSpec((1, tn), lambda j: (0, j))],
        out_specs=pl.BlockSpec((bsz, tn), lambda j: (0, j)),
        out_shape=jax.ShapeDtypeStruct((bsz, n), F32),
        compiler_params=_cparams(1, 32),
    )(c, w, b.reshape(1, n))


def _proj_body(x_ref, sc_ref, sh_ref, w_ref, bm_ref, cos_ref, sin_ref, sg_ref, sb_ref,
               q_ref, ks_ref, vs_ref, kw_ref, vw_ref, gn_ref, kc_ref, vc_ref, u_ref, v_ref, gate_ref,
               *, n_seq_tiles):
    h = _ln(x_ref[...]) * (1.0 + sc_ref[0]) + sh_ref[0]
    hb = h.astype(BF16)

    def proj(off, width):
        return _dot(hb, w_ref[:, off:off + width])

    cosk = cos_ref[...]
    sink = sin_ref[...]
    cosq = jnp.concatenate([cosk] * (NSA_WIDTH // LANES), axis=1)
    sinq = jnp.concatenate([sink] * (NSA_WIDTH // LANES), axis=1)
    q_ref[0] = (proj(_OQ, NSA_WIDTH) * cosq + proj(_OQS, NSA_WIDTH) * sinq).T
    kc_ref[...] = proj(_OKC, KV_WIDTH) * cosk + proj(_OKCS, KV_WIDTH) * sink
    vc_ref[...] = proj(_OVC, KV_WIDTH)
    tm = x_ref.shape[0]
    ks = proj(_OKS, KV_WIDTH) * cosk + proj(_OKSS, KV_WIDTH) * sink
    kw = proj(_OKW, KV_WIDTH) * cosk + proj(_OKWS, KV_WIDTH) * sink
    vs_t = proj(_OVS, KV_WIDTH).T
    vw_t = proj(_OVW, KV_WIDTH).T
    gn_t = jax.nn.sigmoid(proj(_OG, 2 * LANES)).T
    pos = (pl.program_id(0) % n_seq_tiles) * tm + lax.broadcasted_iota(jnp.int32, (tm, 1), 0)
    block_onehot = jnp.where((pos // SEL_BLOCK) % GROUP_BLOCKS
                             == lax.broadcasted_iota(jnp.int32, (1, GROUP_BLOCKS), 1), 1.0, 0.0)
    for g in range(NSA_KV_GROUPS):
        cols = slice(g * HEAD_DIM, (g + 1) * HEAD_DIM)
        ks_ref[0, g] = jnp.concatenate([ks[:, cols], block_onehot], axis=1).astype(BF16)
        kw_ref[0, g] = kw[:, cols].astype(BF16)
        vs_ref[0, g, 0] = vs_t[cols, :].astype(BF16)
        for c in range(tm // TQ):
            vw_ref[0, g, c] = vw_t[cols, c * TQ:(c + 1) * TQ].astype(BF16)
        gn_ref[0, g] = gn_t[g * LANES:g * LANES + GATE_ROWS, :]
    u_ref[...] = _gelu(proj(_OZ, SGU_WIDTH))
    v_ref[...] = _ln(_gelu(proj(_OZ + SGU_WIDTH, SGU_WIDTH))) * sg_ref[...] + sb_ref[...]
    gate_ref[...] = jax.nn.sigmoid(proj(_OM, 2 * D_MODEL) + bm_ref[...])


def _proj(x2, sc, sh, wall, bm, cosk, sink, sg, sb, seq):
    n, d = x2.shape
    tm = TM_PROJ
    nst = seq // tm
    row = lambda w: pl.BlockSpec((tm, w), lambda i: (i, 0))
    per_batch = pl.BlockSpec((1, 1, d), lambda i: (i // nst, 0, 0))
    const = lambda shp: pl.BlockSpec(shp, lambda i: (0,) * len(shp))
    pos = pl.BlockSpec((tm, LANES), lambda i: (i % nst, 0))
    outs = [(KV_WIDTH, F32), (KV_WIDTH, F32), (SGU_WIDTH, F32), (SGU_WIDTH, F32), (2 * D_MODEL, F32)]
    assert tm == TK and tm % TQ == 0
    bsz, g = n // seq, NSA_KV_GROUPS
    att = [((bsz, NSA_WIDTH, seq), (1, NSA_WIDTH, tm), lambda i: (i // nst, 0, i % nst), F32),
           ((bsz, g, seq, HEAD_DIM + GROUP_BLOCKS), (1, g, tm, HEAD_DIM + GROUP_BLOCKS),
            lambda i: (i // nst, 0, i % nst, 0), BF16),
           ((bsz, g, seq // TK, HEAD_DIM, TK), (1, g, 1, HEAD_DIM, TK), lambda i: (i // nst, 0, i % nst, 0, 0), BF16),
           ((bsz, g, seq, HEAD_DIM), (1, g, tm, HEAD_DIM), lambda i: (i // nst, 0, i % nst, 0), BF16),
           ((bsz, g, seq // TQ, HEAD_DIM, TQ), (1, g, tm // TQ, HEAD_DIM, TQ),
            lambda i: (i // nst, 0, i % nst, 0, 0), BF16),
           ((bsz, g, GATE_ROWS, seq), (1, g, GATE_ROWS, tm), lambda i: (i // nst, 0, 0, i % nst), F32)]
    return pl.pallas_call(
        functools.partial(_proj_body, n_seq_tiles=nst),
        grid=(n // tm,),
        in_specs=[row(d), per_batch, per_batch, const((d, _WCOLS)), const((1, 2 * D_MODEL)),
                  pos, pos, const((1, SGU_WIDTH)), const((1, SGU_WIDTH))],
        out_specs=[pl.BlockSpec(blk, imap) for _, blk, imap, _ in att] + [row(w) for w, _ in outs],
        out_shape=([jax.ShapeDtypeStruct(shp, dt) for shp, _, _, dt in att]
                   + [jax.ShapeDtypeStruct((n, w), dt) for w, dt in outs]),
        compiler_params=_cparams(1, 56),
    )(x2, sc, sh, wall, bm, cosk, sink, sg, sb)


def _cmp_body(t_ref, pos_ref, w1_ref, b1_ref, w2_ref, b2_ref, o_ref):
    half = (CMP_BLOCK // 2) * HEAD_DIM
    t = t_ref[0, 0, 0]
    pos = pos_ref[0]
    ta = (t + pos[:, :half]).astype(BF16)
    tb = (t + pos[:, half:]).astype(BF16)
    a = _dot(ta, w1_ref[0, :half, :].astype(BF16))
    b = _dot(tb, w1_ref[0, half:, :].astype(BF16))
    nrow = t.shape[0]
    b_next = pltpu.roll(b, nrow - 1, axis=0)
    hid = _gelu(a + b_next + b1_ref[0])
    out = _dot(hid.astype(BF16), w2_ref[0].astype(BF16)) + b2_ref[0]
    o_ref[0, 0, 0] = out.astype(o_ref.dtype)


def _compress(t16, pos, w1, b1, w2, b2):
    _, bsz, g, nr, dd = t16.shape
    per_kind = lambda shp: pl.BlockSpec((1,) + shp, lambda k, b, gg: (k,) + (0,) * len(shp))
    blk = lambda w: pl.BlockSpec((1, 1, 1, nr, w), lambda k, b, gg: (k, b, gg, 0, 0))
    return pl.pallas_call(
        _cmp_body,
        grid=(2, bsz, g),
        in_specs=[blk(dd), per_kind((1, CMP_BLOCK * HEAD_DIM)), per_kind((CMP_BLOCK * HEAD_DIM, CMP_HIDDEN)),
                  per_kind((1, CMP_HIDDEN)), per_kind((CMP_HIDDEN, HEAD_DIM)), per_kind((1, HEAD_DIM))],
        out_specs=blk(HEAD_DIM),
        out_shape=jax.ShapeDtypeStruct((2, bsz, g, nr, HEAD_DIM), BF16),
        compiler_params=_cparams(3, 48),
    )(t16, pos, w1, b1, w2, b2)


def _scores(q_all, kt, bias):
    s = _dot(kt, q_all)
    return s if bias is None else s + bias


def _max_init():
    return jnp.full((1, NSA_REP * TQ), NEG_INF, F32)


def _acc_init():
    w = NSA_REP * TQ
    return jnp.zeros((1, w), F32), jnp.zeros((HEAD_DIM, w), F32)


def _lane_tile(a):
    return jnp.concatenate([a] * NSA_REP, axis=1)


def _nsa_body(q_ref, kc_ref, vc_ref, ks_ref, vs_ref, kw_ref, vw_ref, gn_ref, agg_ref, o_ref, selneg_ref, score_ref, *, n_cmp):
    qi = pl.program_id(2)
    q0 = qi * TQ
    t1 = q0 + lax.broadcasted_iota(jnp.int32, (1, TQ), 1)
    t = _lane_tile(t1)
    q_all = jnp.concatenate([q_ref[0, r * HEAD_DIM:(r + 1) * HEAD_DIM, :] for r in range(NSA_REP)],
                            axis=1).astype(BF16)

    kcm = kc_ref[0, 0]
    ncp = kcm.shape[0]
    nidx = lax.broadcasted_iota(jnp.int32, (ncp, 1), 0)
    mask_c = (nidx * CMP_STRIDE + (CMP_BLOCK - 1) <= t) & (nidx < n_cmp)
    sm = jnp.where(mask_c, _dot(kcm, q_all), NEG_INF)
    e = jnp.where(mask_c, jnp.exp(sm - jnp.max(sm, axis=0, keepdims=True)), 0.0)
    l = jnp.sum(e, axis=0, keepdims=True)
    p = e * (1.0 / jnp.where(l > 0.0, l, 1.0))
    o_c = _dot(vc_ref[0, 0], p.astype(BF16))
    psum = p[:, :TQ]
    for r in range(1, NSA_REP):
        psum = psum + p[:, r * TQ:(r + 1) * TQ]

    p_hi = psum.astype(BF16)
    p_lo = (psum - p_hi.astype(F32)).astype(BF16)
    agg_t = agg_ref[...]
    imp = _dot(agg_t, p_hi) + _dot(agg_t, p_lo)
    n_sel = imp.shape[0]
    blk = lax.broadcasted_iota(jnp.int32, (n_sel, 1), 0)
    blk_f = blk.astype(F32)
    cur = t1 // SEL_BLOCK
    forced = (blk == 0) | (blk == cur) | (blk == cur - 1)
    score = jnp.where(forced, FORCE_SCORE, jnp.where(blk <= cur, imp, NEG_INF))
    sel = jnp.zeros((n_sel, TQ), jnp.bool_)
    for _ in range(min(SEL_TOPK, n_sel)):
        m = jnp.max(score, axis=0, keepdims=True)
        first = jnp.min(jnp.where(score == m, blk_f, float(n_sel)), axis=0, keepdims=True)
        hit = blk_f == first
        sel = sel | hit
        score = jnp.where(hit, -jnp.inf, score)
    selneg_ref[...] = _lane_tile(jnp.where(sel, 0.0, NEG_INF).astype(BF16))

    def q_ext(jg):
        rows = selneg_ref[pl.ds(pl.multiple_of(jg * GROUP_BLOCKS, GROUP_BLOCKS), GROUP_BLOCKS), :]
        return jnp.concatenate([q_all, rows], axis=0)

    krow_w = lax.broadcasted_iota(jnp.int32, (TQ, 1), 0)

    def win_chunk(c):
        kpos = q0 - WINDOW + c * TQ + krow_w
        dlt = t - kpos
        bias = jnp.where((dlt >= 0) & (dlt < WINDOW) & (kpos >= 0), 0.0, NEG_INF)
        start = jnp.maximum(q0 - WINDOW + c * TQ, 0)
        return kw_ref[0, 0, pl.ds(pl.multiple_of(start, TQ), TQ), :], bias

    n_win = WINDOW // TQ + 1
    s_w = [_scores(q_all, *win_chunk(c)) for c in range(n_win)]
    m_w = functools.reduce(jnp.maximum, [jnp.max(s, axis=0, keepdims=True) for s in s_w])
    l_w, acc_w = _acc_init()
    for c in range(n_win):
        p = jnp.exp(s_w[c] - m_w)
        l_w = l_w + jnp.sum(p, axis=0, keepdims=True)
        acc_w = acc_w + _dot(vw_ref[0, 0, jnp.maximum(qi + c - WINDOW // TQ, 0)], p.astype(BF16))
    o_w = acc_w / l_w

    krow = lax.broadcasted_iota(jnp.int32, (TK, 1), 0)
    group_rows = TK * KV_GROUP

    def max_group(jg, m):
        k0 = pl.multiple_of(jg * group_rows, group_rows)
        s = _scores(q_ext(jg), ks_ref[0, 0, pl.ds(k0, group_rows), :], None)
        score_ref[pl.ds(k0, group_rows), :] = s
        return jnp.maximum(m, jnp.max(s, axis=0, keepdims=True))

    def max_tile(j, m):
        k0 = pl.multiple_of(j * TK, TK)
        bias = jnp.where(k0 + krow <= t, 0.0, NEG_INF)
        s = _scores(q_ext(j // KV_GROUP), ks_ref[0, 0, pl.ds(k0, TK), :], bias)
        score_ref[pl.ds(k0, TK), :] = s
        return jnp.maximum(m, jnp.max(s, axis=0, keepdims=True))

    def sum_rows(state, k0, rows, vt_t):
        l, acc = state
        p = jnp.exp(score_ref[pl.ds(k0, rows), :] - m_s)
        return l + jnp.sum(p, axis=0, keepdims=True), acc + _dot(vt_t, p.astype(BF16))

    def sum_group(jg, state):
        vt_t = jnp.concatenate([vs_ref[0, 0, jg * KV_GROUP + sub] for sub in range(KV_GROUP)], axis=1)
        return sum_rows(state, pl.multiple_of(jg * group_rows, group_rows), group_rows, vt_t)

    def sum_tile(j, state):
        return sum_rows(state, pl.multiple_of(j * TK, TK), TK, vs_ref[0, 0, j])

    n_full = q0 // group_rows
    n_kv = (q0 + TQ - 1) // TK + 1
    m_s = lax.fori_loop(0, n_full, max_group, _max_init())
    m_s = lax.fori_loop(n_full * KV_GROUP, n_kv, max_tile, m_s)
    state = lax.fori_loop(0, n_full, sum_group, _acc_init())
    state = lax.fori_loop(n_full * KV_GROUP, n_kv, sum_tile, state)
    o_s = state[1] / state[0]

    gn = gn_ref[0, 0]
    gate = [jnp.concatenate([gn[3 * r + c:3 * r + c + 1, :] for r in range(NSA_REP)], axis=1) for c in range(3)]
    o = gate[0] * o_c + gate[1] * o_s + gate[2] * o_w
    for r in range(NSA_REP):
        o_ref[0, r * HEAD_DIM:(r + 1) * HEAD_DIM, :] = o[:, r * TQ:(r + 1) * TQ]


def _nsa(q_t, kc, vc_t, ks, vs_t, kwp, vwp_t, gn_t, agg_t, n_cmp):
    bsz, _, seq = q_t.shape
    g = NSA_KV_GROUPS
    gw = NSA_REP * HEAD_DIM
    per_bg = lambda a: pl.BlockSpec((1, 1) + a.shape[2:], lambda b, gg, i: (b, gg) + (0,) * (a.ndim - 2))
    return pl.pallas_call(
        functools.partial(_nsa_body, n_cmp=n_cmp),
        grid=(bsz, g, seq // TQ),
        in_specs=[pl.BlockSpec((1, gw, TQ), lambda b, gg, i: (b, gg, i)),
                  per_bg(kc), per_bg(vc_t), per_bg(ks), per_bg(vs_t), per_bg(kwp), per_bg(vwp_t),
                  pl.BlockSpec((1, 1, gn_t.shape[2], TQ), lambda b, gg, i: (b, gg, 0, i)),
                  pl.BlockSpec(agg_t.shape, lambda b, gg, i: (0, 0))],
        out_specs=pl.BlockSpec((1, gw, TQ), lambda b, gg, i: (b, gg, i)),
        out_shape=jax.ShapeDtypeStruct((bsz, NSA_WIDTH, seq), F32),
        scratch_shapes=[pltpu.VMEM((agg_t.shape[0], NSA_REP * TQ), BF16),
                        pltpu.VMEM((seq, NSA_REP * TQ), F32)],
        compiler_params=_cparams(3, 56),
    )(q_t, kc, vc_t, ks, vs_t, kwp, vwp_t, gn_t, agg_t)


def _merge_body(on_ref, u_ref, v_ref, gate_ref, x_ref, gt_ref, sc_ref, sh_ref, ws_ref, bs_ref,
                wb_ref, wo_ref, g1_ref, b1_ref, wq_ref, x1_ref, h2_ref, qp_ref, *, alpha):
    tm = x_ref.shape[0]
    row = lax.broadcasted_iota(jnp.int32, (SGU_CHUNK, SGU_CHUNK), 0)
    col = lax.broadcasted_iota(jnp.int32, (SGU_CHUNK, SGU_CHUNK), 1)
    lane_group = lax.broadcasted_iota(jnp.int32, (1, SGU_WIDTH), 1) // SGU_GROUP_DIM
    mixed = []
    for c in range(tm // SGU_CHUNK):
        v = v_ref[c * SGU_CHUNK:(c + 1) * SGU_CHUNK, :]
        acc = jnp.zeros((SGU_CHUNK, SGU_WIDTH), F32)
        for g in range(SGU_GROUPS):
            wg = jnp.where(col <= row, ws_ref[g], 0.0).astype(BF16)
            vg = jnp.where(lane_group == g, v, 0.0).astype(BF16)
            acc = acc + _dot(wg, vg)
        mixed.append(acc + bs_ref[...])
    o_sgu = u_ref[...] * jnp.concatenate(mixed, axis=0)
    gate = gate_ref[...]
    merged = (gate[:, :D_MODEL] * _dot(on_ref[0].T.astype(BF16), wb_ref[0])
              + gate[:, D_MODEL:] * _dot(o_sgu.astype(BF16), wb_ref[1]))
    mix = _dot(merged.astype(BF16), wo_ref[...])
    x1 = _ln(alpha * x_ref[...] + gt_ref[0] * mix) * g1_ref[...] + b1_ref[...]
    x1_ref[...] = x1
    h2 = _ln(x1) * (1.0 + sc_ref[0]) + sh_ref[0]
    h2_ref[...] = h2
    qp_ref[...] = _dot(h2.astype(BF16), wq_ref[...])


def _merge(o_nsa, u, v, gate, x2, gt1, sc2, sh2, ws, bs, wb, wo, g1, b1, wq, seq, alpha):
    n, d = x2.shape
    tm = TM_MERGE
    nst = seq // tm
    row = lambda w: pl.BlockSpec((tm, w), lambda i: (i, 0))
    per_batch = pl.BlockSpec((1, 1, d), lambda i: (i // nst, 0, 0))
    const = lambda a: pl.BlockSpec(a.shape, lambda i: (0,) * a.ndim)
    nq = wq.shape[1]
    return pl.pallas_call(
        functools.partial(_merge_body, alpha=alpha),
        grid=(n // tm,),
        in_specs=[pl.BlockSpec((1, NSA_WIDTH, tm), lambda i: (i // nst, 0, i % nst)),
                  row(SGU_WIDTH), row(SGU_WIDTH), row(2 * d), row(d),
                  per_batch, per_batch, per_batch, const(ws), const(bs), const(wb), const(wo),
                  const(g1), const(b1), const(wq)],
        out_specs=[row(d), row(d), row(nq)],
        out_shape=[jax.ShapeDtypeStruct((n, d), F32), jax.ShapeDtypeStruct((n, d), F32),
                   jax.ShapeDtypeStruct((n, nq), F32)],
        compiler_params=_cparams(1, 56),
    )(o_nsa, u, v, gate, x2, gt1, sc2, sh2, ws, bs, wb, wo, g1, b1, wq)


def _topk_rows(s, k):
    nrow = s.shape[0]
    rid = lax.broadcasted_iota(jnp.int32, s.shape, 0).astype(F32)
    vals, rows = [], []
    for _ in range(k):
        m = jnp.max(s, axis=0, keepdims=True)
        first = jnp.min(jnp.where(s == m, rid, float(nrow)), axis=0, keepdims=True)
        s = jnp.where(rid == first, -jnp.inf, s)
        vals.append(m)
        rows.append(first)
    return vals, rows


def _candidate_blocks(k):
    todo = {(a, b) for a in range(k) for b in range(k) if (a + 1) * (b + 1) <= k}
    options = [(axis, fixed, start) for axis in (0, 1) for fixed in range(k) for start in range(0, k, 8)]
    cells = lambda axis, fixed, start: [((fixed, start + r) if axis == 0 else (start + r, fixed)) for r in range(8)]
    blocks = []
    while todo:
        best = max(options, key=lambda o: sum(c in todo for c in cells(*o)))
        owned = tuple(c in todo for c in cells(*best))
        todo -= set(cells(*best))
        blocks.append(best + (owned,))
    return tuple(blocks)


_CAND_BLOCKS = _candidate_blocks(PEER_TOPK)


def _retrieve_body(qp_ref, keys_ref, idx_ref, gate_ref, eid_ref):
    k = PEER_TOPK
    for h in range(PEER_HEADS):
        tops = []
        for c in range(2):
            off = (h * 2 + c) * PEER_HALF
            qh = qp_ref[:, off:off + PEER_HALF].astype(BF16)
            s = _dot_nt(keys_ref[h, c].astype(BF16), qh)
            tops.append(_topk_rows(s, k))
        (v1, i1), (v2, i2) = tops
        vm = (jnp.concatenate(v1, axis=0), jnp.concatenate(v2, axis=0))
        im = (jnp.concatenate(i1, axis=0), jnp.concatenate(i2, axis=0))
        sub = lax.broadcasted_iota(jnp.int32, (8, 1), 0)
        vals, eids, poss = [], [], []
        for axis, fixed, start, owned in _CAND_BLOCKS:
            if axis == 0:
                v = v1[fixed] + vm[1][start:start + 8]
                e = i1[fixed] * float(PEER_NKEYS) + im[1][start:start + 8]
                pos = fixed * k + start + sub
            else:
                v = vm[0][start:start + 8] + v2[fixed]
                e = im[0][start:start + 8] * float(PEER_NKEYS) + i2[fixed]
                pos = (start + sub) * k + fixed
            if not all(owned):
                keep = functools.reduce(jnp.logical_or, [sub == r for r in range(8) if owned[r]])
                v = jnp.where(keep, v, -jnp.inf)
                pos = jnp.where(keep, pos, k * k)
            vals.append(v)
            eids.append(e)
            poss.append(pos.astype(F32))
        cand = jnp.concatenate(vals, axis=0)
        cidx = jnp.concatenate(eids, axis=0)
        rid = jnp.concatenate(poss, axis=0)
        top_s, top_e = [], []
        for _ in range(k):
            m = jnp.max(cand, axis=0, keepdims=True)
            first = jnp.min(jnp.where(cand == m, rid, float(k * k)), axis=0, keepdims=True)
            hit = rid == first
            top_e.append(jnp.sum(jnp.where(hit, cidx, 0.0), axis=0, keepdims=True))
            cand = jnp.where(hit, -jnp.inf, cand)
            top_s.append(m)
        ts = jnp.concatenate(top_s, axis=0)
        e = jnp.exp(ts - ts[0:1])
        gate_ref[0, h * k:(h + 1) * k, :] = e / jnp.sum(e, axis=0, keepdims=True)
        eid_ref[h * k:(h + 1) * k, :] = jnp.concatenate(top_e, axis=0)
    idx_ref[...] = (eid_ref[...].T * float(ROW_SUB)).astype(jnp.int32)


def _retrieve(qp, keys):
    n = qp.shape[0]
    nt = n // TC
    out = pl.BlockSpec((1, PEER_PAIRS, TC), lambda i: (i, 0, 0))
    return pl.pallas_call(
        _retrieve_body,
        grid=(nt,),
        in_specs=[pl.BlockSpec((TC, qp.shape[1]), lambda i: (i, 0)),
                  pl.BlockSpec(keys.shape, lambda i: (0, 0, 0, 0))],
        out_specs=[pl.BlockSpec((TC, PEER_PAIRS), lambda i: (i, 0)), out],
        out_shape=[jax.ShapeDtypeStruct((n, PEER_PAIRS), jnp.int32),
                   jax.ShapeDtypeStruct((nt, PEER_PAIRS, TC), F32)],
        scratch_shapes=[pltpu.VMEM((PEER_PAIRS, TC), F32)],
        compiler_params=_cparams(1, 48),
    )(qp, keys)


ROW_SUB = D_MODEL // 2 // LANES
ACT_UNROLL = 32
OFF_REGS = 8


def _load_row(tab_ref, scaled_idx):
    row = tab_ref[pl.ds(pl.multiple_of(scaled_idx, ROW_SUB), ROW_SUB), :]
    lo = lax.bitcast_convert_type(lax.shift_left(row, jnp.uint32(16)), F32)
    hi = lax.bitcast_convert_type(row & jnp.uint32(0xFFFF0000), F32)
    return lo, hi


def _expert_act_body(off_ref, idx_ref, h_ref, gate_ref, tab_ref, w_ref, slot_ref, part_ref):
    off = [off_ref[j] for j in range(OFF_REGS)]

    def token(t, carry):
        x = h_ref[pl.ds(t, 1), :]
        chunk = lambda s: x[:, s * LANES:(s + 1) * LANES]
        xlo = jnp.concatenate([chunk(s) for s in range(ROW_SUB)], axis=0)
        xhi = jnp.concatenate([chunk(ROW_SUB + s) for s in range(ROW_SUB)], axis=0)
        base = pl.multiple_of(t * PEER_PAIRS, PEER_PAIRS)
        for p in range(PEER_PAIRS):
            if p % OFF_REGS == 0:
                idx_sub = idx_ref.at[t, pl.ds(p, OFF_REGS)]
            lo, hi = _load_row(tab_ref, idx_sub[off[p % OFF_REGS]])
            slot_ref[p * ROW_SUB:(p + 1) * ROW_SUB, :] = lo * xlo + hi * xhi
        part = slot_ref[pl.ds(0, PEER_PAIRS, stride=ROW_SUB), :]
        for s in range(1, ROW_SUB):
            part = part + slot_ref[pl.ds(s, PEER_PAIRS, stride=ROW_SUB), :]
        part_ref[pl.ds(pl.multiple_of(base, PEER_PAIRS), PEER_PAIRS), :] = part
        return carry

    lax.fori_loop(0, TC, token, 0)

    tok_lane = lax.broadcasted_iota(jnp.int32, (PEER_PAIRS, TC), 1)

    def lane_sums(i, act):
        for k in range(ACT_UNROLL):
            t = i * ACT_UNROLL + k
            part = part_ref[pl.ds(pl.multiple_of(t * PEER_PAIRS, PEER_PAIRS), PEER_PAIRS), :]
            act = jnp.where(tok_lane == t, jnp.sum(part, axis=1, keepdims=True), act)
        return act

    act = lax.fori_loop(0, TC // ACT_UNROLL, lane_sums, jnp.zeros((PEER_PAIRS, TC), F32))
    w_ref[...] = (gate_ref[0] * _gelu(act)).T


def _expert_act(idx_tok, h2, gate_t, tab):
    nt = gate_t.shape[0]
    tile = pl.BlockSpec((1, PEER_PAIRS, TC), lambda i: (i, 0, 0))
    return pl.pallas_call(
        _expert_act_body,
        grid=(nt,),
        in_specs=[pl.BlockSpec(memory_space=pltpu.SMEM),
                  pl.BlockSpec((TC, PEER_PAIRS), lambda i: (i, 0), memory_space=pltpu.SMEM),
                  pl.BlockSpec((TC, h2.shape[1]), lambda i: (i, 0)),
                  tile,
                  pl.BlockSpec(tab.shape, lambda i: (0, 0), pipeline_mode=pl.Buffered(1))],
        out_specs=pl.BlockSpec((TC, PEER_PAIRS), lambda i: (i, 0)),
        out_shape=jax.ShapeDtypeStruct((nt * TC, PEER_PAIRS), F32),
        scratch_shapes=[pltpu.VMEM((PEER_PAIRS * ROW_SUB, LANES), F32),
                        pltpu.VMEM((TC * PEER_PAIRS, LANES), F32)],
        compiler_params=_cparams(1, 52),
    )(jnp.arange(OFF_REGS, dtype=jnp.int32), idx_tok, h2, gate_t, tab)


def _expert_out_body(off_ref, idx_ref, w_ref, tab_ref, o_ref):
    n_acc = 4
    off = [off_ref[j] for j in range(OFF_REGS)]

    def token(t, carry):
        acc_lo = [jnp.zeros((ROW_SUB, LANES), F32) for _ in range(n_acc)]
        acc_hi = [jnp.zeros((ROW_SUB, LANES), F32) for _ in range(n_acc)]
        for p in range(PEER_PAIRS):
            if p % OFF_REGS == 0:
                idx_sub = idx_ref.at[t, pl.ds(p, OFF_REGS)]
                w_sub = w_ref.at[t, pl.ds(p, OFF_REGS)]
            lo, hi = _load_row(tab_ref, idx_sub[off[p % OFF_REGS]])
            w = w_sub[off[p % OFF_REGS]]
            acc_lo[p % n_acc] = acc_lo[p % n_acc] + w * lo
            acc_hi[p % n_acc] = acc_hi[p % n_acc] + w * hi
        lo = (acc_lo[0] + acc_lo[1]) + (acc_lo[2] + acc_lo[3])
        hi = (acc_hi[0] + acc_hi[1]) + (acc_hi[2] + acc_hi[3])
        o_ref[t] = jnp.concatenate([lo, hi], axis=0)
        return carry

    lax.fori_loop(0, TB, token, 0)


def _expert_out(idx_tok, w_tok, tab):
    n = idx_tok.shape[0]
    smem = pl.BlockSpec((TB, PEER_PAIRS), lambda i: (i, 0), memory_space=pltpu.SMEM)
    return pl.pallas_call(
        _expert_out_body,
        grid=(n // TB,),
        in_specs=[pl.BlockSpec(memory_space=pltpu.SMEM), smem, smem,
                  pl.BlockSpec(tab.shape, lambda i: (0, 0), pipeline_mode=pl.Buffered(1))],
        out_specs=pl.BlockSpec((TB, D_MODEL // LANES, LANES), lambda i: (i, 0, 0)),
        out_shape=jax.ShapeDtypeStruct((n, D_MODEL // LANES, LANES), F32),
        compiler_params=_cparams(1, 48),
    )(jnp.arange(OFF_REGS, dtype=jnp.int32), idx_tok, w_tok, tab)


def _final_body(x1_ref, f_ref, gt_ref, g_ref, b_ref, o_ref, *, alpha):
    o_ref[...] = _ln(alpha * x1_ref[...] + gt_ref[0] * f_ref[...]) * g_ref[...] + b_ref[...]


def _final(x1, ffn, gt2, g2, b2, seq, alpha):
    n, d = x1.shape
    tm = TM_LN
    nst = seq // tm
    row = pl.BlockSpec((tm, d), lambda i: (i, 0))
    const = pl.BlockSpec((1, d), lambda i: (0, 0))
    return pl.pallas_call(
        functools.partial(_final_body, alpha=alpha),
        grid=(n // tm,),
        in_specs=[row, row, pl.BlockSpec((1, 1, d), lambda i: (i // nst, 0, 0)), const, const],
        out_specs=row,
        out_shape=jax.ShapeDtypeStruct((n, d), F32),
        compiler_params=_cparams(1, 32),
    )(x1, ffn, gt2, g2, b2)


def _swap_halves(w, heads):
    d = w.shape[0]
    w4 = w.reshape(d, heads, 2, HEAD_DIM // 2)
    return jnp.flip(w4, axis=2).reshape(d, heads * HEAD_DIM)


def _fused_in_weight(w_in, w_merge):
    d = w_in.shape[0]
    o = 0
    parts = {}
    for name, width in (("q", NSA_WIDTH), ("kc", KV_WIDTH), ("vc", KV_WIDTH), ("ks", KV_WIDTH),
                        ("vs", KV_WIDTH), ("kw", KV_WIDTH), ("vw", KV_WIDTH), ("g", 3 * NSA_HEADS),
                        ("z", 2 * SGU_WIDTH)):
        parts[name] = w_in[:, o:o + width]
        o += width
    wq = parts["q"] * (HEAD_DIM ** -0.5)
    gcols = 3 * NSA_REP
    wg = jnp.zeros((d, 2 * LANES), w_in.dtype)
    wg = wg.at[:, :gcols].set(parts["g"][:, :gcols]).at[:, LANES:LANES + gcols].set(parts["g"][:, gcols:])
    g = NSA_KV_GROUPS
    cols = {"q": wq, "q_swap": _swap_halves(wq, NSA_HEADS),
            "kc": parts["kc"], "kc_swap": _swap_halves(parts["kc"], g),
            "ks": parts["ks"], "ks_swap": _swap_halves(parts["ks"], g),
            "kw": parts["kw"], "kw_swap": _swap_halves(parts["kw"], g),
            "vc": parts["vc"], "vs": parts["vs"], "vw": parts["vw"],
            "gates": wg, "z": parts["z"], "merge": w_merge}
    assert all(cols[name].shape[1] == width for name, width in _SEGMENTS)
    return jnp.concatenate([cols[name] for name, _ in _SEGMENTS], axis=1).astype(BF16)


def _rope_tables(seq):
    half = HEAD_DIM // 2
    pos = jnp.arange(seq, dtype=F32)
    inv_freq = ROPE_THETA ** (-jnp.arange(half, dtype=F32) / half)
    ang = pos[:, None] * inv_freq[None, :]
    cos, sin = jnp.cos(ang), jnp.sin(ang)
    reps = LANES // HEAD_DIM
    cosk = jnp.tile(jnp.concatenate([cos, cos], axis=1), (1, reps))
    sink = jnp.tile(jnp.concatenate([-sin, sin], axis=1), (1, reps))
    return cosk, sink


def _sel_aggregation(n_cmp_pad, n_sel):
    c0 = jnp.arange(n_cmp_pad)[:, None] * CMP_STRIDE
    s0 = jnp.arange(n_sel)[None, :] * SEL_BLOCK
    ov = jnp.clip(jnp.minimum(c0 + CMP_BLOCK, s0 + SEL_BLOCK) - jnp.maximum(c0, s0), 0, None)
    return (ov / CMP_BLOCK).astype(BF16)


def _pack_table(tab):
    half = tab.shape[1] // 2
    bits = lax.bitcast_convert_type(tab.astype(BF16), jnp.uint16).astype(jnp.uint32)
    packed = (bits[:, half:] << 16) | bits[:, :half]
    return packed.reshape(tab.shape[0] * ROW_SUB, LANES)


def _split_groups(a, bsz, seq):
    return a.reshape(bsz, seq, NSA_KV_GROUPS, HEAD_DIM).transpose(0, 2, 1, 3)


def kernel(x, c, w_ada, b_ada, w_in, cmp_pos, cmp_w1, cmp_b1, cmp_w2, cmp_b2, sgu_ln_g, sgu_ln_b, sgu_w, sgu_b, w_branch, w_merge, b_merge, w_out, ln1_g, ln1_b, peer_wq, peer_keys, peer_u, peer_v, ln2_g, ln2_b):
    bsz, seq, d = x.shape
    n = bsz * seq
    depth = w_ada.shape[0]
    alpha = (2.0 * depth) ** 0.25
    n_cmp = (seq - CMP_BLOCK) // CMP_STRIDE + 1
    n_half = seq // CMP_STRIDE
    n_sel = seq // SEL_BLOCK
    cosk, sink = _rope_tables(seq)
    agg = _sel_aggregation(n_half, n_sel)
    x2 = x.reshape(n, d)
    for l in range(depth):
        mod = _ada(c, w_ada[l], b_ada[l])
        sh1, sc1, gt1, sh2, sc2, gt2 = [m.reshape(bsz, 1, d) for m in jnp.split(mod, 6, axis=-1)]

        wall = _fused_in_weight(w_in[l], w_merge[l])
        (q_t, ks_ext, vs_t, kw_g, vw_t, gn_t, kc_r, vc_r, u, v, gate) = _proj(
            x2, sc1, sh1, wall, b_merge[l].reshape(1, -1), cosk, sink,
            sgu_ln_g[l].reshape(1, -1), sgu_ln_b[l].reshape(1, -1), seq)

        t16 = jnp.stack([_split_groups(kc_r, bsz, seq), _split_groups(vc_r, bsz, seq)])
        t16 = t16.reshape(2, bsz, NSA_KV_GROUPS, n_half, CMP_STRIDE * HEAD_DIM)
        cmp_kv = _compress(t16, cmp_pos[l].reshape(2, 1, CMP_BLOCK * HEAD_DIM), cmp_w1[l],
                           cmp_b1[l].reshape(2, 1, CMP_HIDDEN), cmp_w2[l], cmp_b2[l].reshape(2, 1, HEAD_DIM))

        o_nsa_t = _nsa(q_t, cmp_kv[0], cmp_kv[1].transpose(0, 1, 3, 2), ks_ext, vs_t, kw_g, vw_t, gn_t, agg.T, n_cmp)

        bs = jnp.repeat(sgu_b[l].T, SGU_GROUP_DIM, axis=1)
        x1, h2, qp = _merge(o_nsa_t, u, v, gate, x2, gt1, sc2, sh2,
                            sgu_w[l], bs, w_branch[l].astype(BF16), w_out[l].astype(BF16),
                            ln1_g[l].reshape(1, d), ln1_b[l].reshape(1, d), peer_wq[l].astype(BF16), seq, alpha)

        idx_tok, gate_t = _retrieve(qp, peer_keys[l])
        w_tok = _expert_act(idx_tok, h2, gate_t, _pack_table(peer_u[l]))
        ffn = _expert_out(idx_tok, w_tok, _pack_table(peer_v[l]))
        x2 = _final(x1, ffn.reshape(n, d), gt2, ln2_g[l].reshape(1, d), ln2_b[l].reshape(1, d), seq, alpha)
    return x2.reshape(bsz, seq, d)
```

```python
import functools

import jax
import jax.numpy as jnp
from jax import lax
from jax.experimental import pallas as pl
from jax.experimental.pallas import tpu as pltpu

D_MODEL = 1024
NSA_HEADS = 8
NSA_KV_GROUPS = 2
NSA_REP = NSA_HEADS // NSA_KV_GROUPS
HEAD_DIM = 64
NSA_WIDTH = NSA_HEADS * HEAD_DIM
KV_WIDTH = NSA_KV_GROUPS * HEAD_DIM
CMP_BLOCK = 32
CMP_STRIDE = 16
CMP_HIDDEN = 2 * HEAD_DIM
SEL_BLOCK = 64
SEL_TOPK = 16
WINDOW = 512
ROPE_THETA = 10000.0
SGU_GROUPS = 8
SGU_WIDTH = D_MODEL // 2
SGU_GROUP_DIM = SGU_WIDTH // SGU_GROUPS
SGU_CHUNK = 128
PEER_HEADS = 8
PEER_NKEYS = 128
PEER_EXPERTS = PEER_NKEYS * PEER_NKEYS
PEER_QDIM = 256
PEER_HALF = PEER_QDIM // 2
PEER_TOPK = 16
PEER_PAIRS = PEER_HEADS * PEER_TOPK
LN_EPS = 1e-5
NEG_INF = -1e30
FORCE_SCORE = 1e9

LANES = 128
MIB = 1024 * 1024
BF16 = jnp.bfloat16
F32 = jnp.float32

TM_PROJ = 256
TM_MERGE = 256
TM_LN = 512
TQ = 128
TK = 256
KV_GROUP = 4
GROUP_BLOCKS = TK * KV_GROUP // SEL_BLOCK
GATE_ROWS = 16
TC = 128
TB = 128

_SEGMENTS = (("q", NSA_WIDTH), ("q_swap", NSA_WIDTH), ("kc", KV_WIDTH), ("kc_swap", KV_WIDTH),
             ("ks", KV_WIDTH), ("ks_swap", KV_WIDTH), ("kw", KV_WIDTH), ("kw_swap", KV_WIDTH),
             ("vc", KV_WIDTH), ("vs", KV_WIDTH), ("vw", KV_WIDTH), ("gates", 2 * LANES),
             ("z", 2 * SGU_WIDTH), ("merge", 2 * D_MODEL))


def _segment_offsets():
    offs, o = {}, 0
    for name, width in _SEGMENTS:
        offs[name] = o
        o += width
    return offs, o


_OFF, _WCOLS = _segment_offsets()
_OQ, _OQS, _OKC, _OKCS = _OFF["q"], _OFF["q_swap"], _OFF["kc"], _OFF["kc_swap"]
_OKS, _OKSS, _OKW, _OKWS = _OFF["ks"], _OFF["ks_swap"], _OFF["kw"], _OFF["kw_swap"]
_OVC, _OVS, _OVW, _OG, _OZ, _OM = _OFF["vc"], _OFF["vs"], _OFF["vw"], _OFF["gates"], _OFF["z"], _OFF["merge"]


VMEM_MIB_V7X = 64


def _cparams(n_axes, vmem_mib):
    assert vmem_mib < VMEM_MIB_V7X
    return pltpu.CompilerParams(
        dimension_semantics=("parallel",) * n_axes,
        vmem_limit_bytes=vmem_mib * MIB)


def _ln(x):
    mu = jnp.mean(x, axis=-1, keepdims=True)
    xc = x - mu
    var = jnp.mean(xc * xc, axis=-1, keepdims=True)
    return xc * lax.rsqrt(var + LN_EPS)


def _gelu(x):
    return 0.5 * x * (1.0 + lax.erf(x * (2.0 ** -0.5)))


def _dot(a, b):
    return jnp.dot(a, b, preferred_element_type=F32)


def _dot_nt(a, b):
    return lax.dot_general(a, b, (((1,), (1,)), ((), ())), preferred_element_type=F32)


def _ada_body(c_ref, w_ref, b_ref, o_ref):
    cv = c_ref[...]
    a = cv * jax.nn.sigmoid(cv)
    o_ref[...] = _dot(a.astype(BF16), w_ref[...].astype(BF16)) + b_ref[...]


def _ada(c, w, b):
    bsz, d = c.shape
    n = w.shape[1]
    tn = 1024
    return pl.pallas_call(
        _ada_body,
        grid=(n // tn,),
        in_specs=[pl.BlockSpec((bsz, d), lambda j: (0, 0)),
                  pl.BlockSpec((d, tn), lambda j: (0, j)),
                  pl.BlockSpec((1, tn), lambda j: (0, j))],
        out_specs=pl.BlockSpec((bsz, tn), lambda j: (0, j)),
        out_shape=jax.ShapeDtypeStruct((bsz, n), F32),
        compiler_params=_cparams(1, 32),
    )(c, w, b.reshape(1, n))


def _proj_body(x_ref, sc_ref, sh_ref, w_ref, bm_ref, cos_ref, sin_ref, sg_ref, sb_ref,
               q_ref, ks_ref, vs_ref, kw_ref, vw_ref, gn_ref, kc_ref, vc_ref, u_ref, v_ref, gate_ref,
               *, n_seq_tiles):
    h = _ln(x_ref[...]) * (1.0 + sc_ref[0]) + sh_ref[0]
    hb = h.astype(BF16)

    def proj(off, width):
        return _dot(hb, w_ref[:, off:off + width])

    cosk = cos_ref[...]
    sink = sin_ref[...]
    cosq = jnp.concatenate([cosk] * (NSA_WIDTH // LANES), axis=1)
    sinq = jnp.concatenate([sink] * (NSA_WIDTH // LANES), axis=1)
    q_ref[0] = (proj(_OQ, NSA_WIDTH) * cosq + proj(_OQS, NSA_WIDTH) * sinq).T
    kc_ref[...] = proj(_OKC, KV_WIDTH) * cosk + proj(_OKCS, KV_WIDTH) * sink
    vc_ref[...] = proj(_OVC, KV_WIDTH)
    tm = x_ref.shape[0]
    ks = proj(_OKS, KV_WIDTH) * cosk + proj(_OKSS, KV_WIDTH) * sink
    kw = proj(_OKW, KV_WIDTH) * cosk + proj(_OKWS, KV_WIDTH) * sink
    vs_t = proj(_OVS, KV_WIDTH).T
    vw_t = proj(_OVW, KV_WIDTH).T
    gn_t = jax.nn.sigmoid(proj(_OG, 2 * LANES)).T
    pos = (pl.program_id(0) % n_seq_tiles) * tm + lax.broadcasted_iota(jnp.int32, (tm, 1), 0)
    block_onehot = jnp.where((pos // SEL_BLOCK) % GROUP_BLOCKS
                             == lax.broadcasted_iota(jnp.int32, (1, GROUP_BLOCKS), 1), 1.0, 0.0)
    for g in range(NSA_KV_GROUPS):
        cols = slice(g * HEAD_DIM, (g + 1) * HEAD_DIM)
        ks_ref[0, g] = jnp.concatenate([ks[:, cols], block_onehot], axis=1).astype(BF16)
        kw_ref[0, g] = kw[:, cols].astype(BF16)
        vs_ref[0, g, 0] = vs_t[cols, :].astype(BF16)
        for c in range(tm // TQ):
            vw_ref[0, g, c] = vw_t[cols, c * TQ:(c + 1) * TQ].astype(BF16)
        gn_ref[0, g] = gn_t[g * LANES:g * LANES + GATE_ROWS, :]
    u_ref[...] = _gelu(proj(_OZ, SGU_WIDTH))
    v_ref[...] = _ln(_gelu(proj(_OZ + SGU_WIDTH, SGU_WIDTH))) * sg_ref[...] + sb_ref[...]
    gate_ref[...] = jax.nn.sigmoid(proj(_OM, 2 * D_MODEL) + bm_ref[...])


def _proj(x2, sc, sh, wall, bm, cosk, sink, sg, sb, seq):
    n, d = x2.shape
    tm = TM_PROJ
    nst = seq // tm
    row = lambda w: pl.BlockSpec((tm, w), lambda i: (i, 0))
    per_batch = pl.BlockSpec((1, 1, d), lambda i: (i // nst, 0, 0))
    const = lambda shp: pl.BlockSpec(shp, lambda i: (0,) * len(shp))
    pos = pl.BlockSpec((tm, LANES), lambda i: (i % nst, 0))
    outs = [(KV_WIDTH, F32), (KV_WIDTH, F32), (SGU_WIDTH, F32), (SGU_WIDTH, F32), (2 * D_MODEL, F32)]
    assert tm == TK and tm % TQ == 0
    bsz, g = n // seq, NSA_KV_GROUPS
    att = [((bsz, NSA_WIDTH, seq), (1, NSA_WIDTH, tm), lambda i: (i // nst, 0, i % nst), F32),
           ((bsz, g, seq, HEAD_DIM + GROUP_BLOCKS), (1, g, tm, HEAD_DIM + GROUP_BLOCKS),
            lambda i: (i // nst, 0, i % nst, 0), BF16),
           ((bsz, g, seq // TK, HEAD_DIM, TK), (1, g, 1, HEAD_DIM, TK), lambda i: (i // nst, 0, i % nst, 0, 0), BF16),
           ((bsz, g, seq, HEAD_DIM), (1, g, tm, HEAD_DIM), lambda i: (i // nst, 0, i % nst, 0), BF16),
           ((bsz, g, seq // TQ, HEAD_DIM, TQ), (1, g, tm // TQ, HEAD_DIM, TQ),
            lambda i: (i // nst, 0, i % nst, 0, 0), BF16),
           ((bsz, g, GATE_ROWS, seq), (1, g, GATE_ROWS, tm), lambda i: (i // nst, 0, 0, i % nst), F32)]
    return pl.pallas_call(
        functools.partial(_proj_body, n_seq_tiles=nst),
        grid=(n // tm,),
        in_specs=[row(d), per_batch, per_batch, const((d, _WCOLS)), const((1, 2 * D_MODEL)),
                  pos, pos, const((1, SGU_WIDTH)), const((1, SGU_WIDTH))],
        out_specs=[pl.BlockSpec(blk, imap) for _, blk, imap, _ in att] + [row(w) for w, _ in outs],
        out_shape=([jax.ShapeDtypeStruct(shp, dt) for shp, _, _, dt in att]
                   + [jax.ShapeDtypeStruct((n, w), dt) for w, dt in outs]),
        compiler_params=_cparams(1, 56),
    )(x2, sc, sh, wall, bm, cosk, sink, sg, sb)


def _cmp_body(t_ref, pos_ref, w1_ref, b1_ref, w2_ref, b2_ref, o_ref):
    half = (CMP_BLOCK // 2) * HEAD_DIM
    t = t_ref[0, 0, 0]
    pos = pos_ref[0]
    ta = (t + pos[:, :half]).astype(BF16)
    tb = (t + pos[:, half:]).astype(BF16)
    a = _dot(ta, w1_ref[0, :half, :].astype(BF16))
    b = _dot(tb, w1_ref[0, half:, :].astype(BF16))
    nrow = t.shape[0]
    b_next = pltpu.roll(b, nrow - 1, axis=0)
    hid = _gelu(a + b_next + b1_ref[0])
    out = _dot(hid.astype(BF16), w2_ref[0].astype(BF16)) + b2_ref[0]
    o_ref[0, 0, 0] = out.astype(o_ref.dtype)


def _compress(t16, pos, w1, b1, w2, b2):
    _, bsz, g, nr, dd = t16.shape
    per_kind = lambda shp: pl.BlockSpec((1,) + shp, lambda k, b, gg: (k,) + (0,) * len(shp))
    blk = lambda w: pl.BlockSpec((1, 1, 1, nr, w), lambda k, b, gg: (k, b, gg, 0, 0))
    return pl.pallas_call(
        _cmp_body,
        grid=(2, bsz, g),
        in_specs=[blk(dd), per_kind((1, CMP_BLOCK * HEAD_DIM)), per_kind((CMP_BLOCK * HEAD_DIM, CMP_HIDDEN)),
                  per_kind((1, CMP_HIDDEN)), per_kind((CMP_HIDDEN, HEAD_DIM)), per_kind((1, HEAD_DIM))],
        out_specs=blk(HEAD_DIM),
        out_shape=jax.ShapeDtypeStruct((2, bsz, g, nr, HEAD_DIM), BF16),
        compiler_params=_cparams(3, 48),
    )(t16, pos, w1, b1, w2, b2)


def _scores(q_all, kt, bias):
    s = _dot(kt, q_all)
    return s if bias is None else s + bias


def _max_init():
    return jnp.full((1, NSA_REP * TQ), NEG_INF, F32)


def _acc_init():
    w = NSA_REP * TQ
    return jnp.zeros((1, w), F32), jnp.zeros((HEAD_DIM, w), F32)


def _lane_tile(a):
    return jnp.concatenate([a] * NSA_REP, axis=1)


def _nsa_body(q_ref, kc_ref, vc_ref, ks_ref, vs_ref, kw_ref, vw_ref, gn_ref, agg_ref, o_ref, selneg_ref, score_ref, *, n_cmp):
    qi = pl.program_id(2)
    q0 = qi * TQ
    t1 = q0 + lax.broadcasted_iota(jnp.int32, (1, TQ), 1)
    t = _lane_tile(t1)
    q_all = jnp.concatenate([q_ref[0, r * HEAD_DIM:(r + 1) * HEAD_DIM, :] for r in range(NSA_REP)],
                            axis=1).astype(BF16)

    kcm = kc_ref[0, 0]
    ncp = kcm.shape[0]
    nidx = lax.broadcasted_iota(jnp.int32, (ncp, 1), 0)
    mask_c = (nidx * CMP_STRIDE + (CMP_BLOCK - 1) <= t) & (nidx < n_cmp)
    sm = jnp.where(mask_c, _dot(kcm, q_all), NEG_INF)
    e = jnp.where(mask_c, jnp.exp(sm - jnp.max(sm, axis=0, keepdims=True)), 0.0)
    l = jnp.sum(e, axis=0, keepdims=True)
    p = e * (1.0 / jnp.where(l > 0.0, l, 1.0))
    o_c = _dot(vc_ref[0, 0], p.astype(BF16))
    psum = p[:, :TQ]
    for r in range(1, NSA_REP):
        psum = psum + p[:, r * TQ:(r + 1) * TQ]

    p_hi = psum.astype(BF16)
    p_lo = (psum - p_hi.astype(F32)).astype(BF16)
    agg_t = agg_ref[...]
    imp = _dot(agg_t, p_hi) + _dot(agg_t, p_lo)
    n_sel = imp.shape[0]
    blk = lax.broadcasted_iota(jnp.int32, (n_sel, 1), 0)
    blk_f = blk.astype(F32)
    cur = t1 // SEL_BLOCK
    forced = (blk == 0) | (blk == cur) | (blk == cur - 1)
    score = jnp.where(forced, FORCE_SCORE, jnp.where(blk <= cur, imp, NEG_INF))
    sel = jnp.zeros((n_sel, TQ), jnp.bool_)
    for _ in range(min(SEL_TOPK, n_sel)):
        m = jnp.max(score, axis=0, keepdims=True)
        first = jnp.min(jnp.where(score == m, blk_f, float(n_sel)), axis=0, keepdims=True)
        hit = blk_f == first
        sel = sel | hit
        score = jnp.where(hit, -jnp.inf, score)
    selneg_ref[...] = _lane_tile(jnp.where(sel, 0.0, NEG_INF).astype(BF16))

    def q_ext(jg):
        rows = selneg_ref[pl.ds(pl.multiple_of(jg * GROUP_BLOCKS, GROUP_BLOCKS), GROUP_BLOCKS), :]
        return jnp.concatenate([q_all, rows], axis=0)

    krow_w = lax.broadcasted_iota(jnp.int32, (TQ, 1), 0)

    def win_chunk(c):
        kpos = q0 - WINDOW + c * TQ + krow_w
        dlt = t - kpos
        bias = jnp.where((dlt >= 0) & (dlt < WINDOW) & (kpos >= 0), 0.0, NEG_INF)
        start = jnp.maximum(q0 - WINDOW + c * TQ, 0)
        return kw_ref[0, 0, pl.ds(pl.multiple_of(start, TQ), TQ), :], bias

    n_win = WINDOW // TQ + 1
    s_w = [_scores(q_all, *win_chunk(c)) for c in range(n_win)]
    m_w = functools.reduce(jnp.maximum, [jnp.max(s, axis=0, keepdims=True) for s in s_w])
    l_w, acc_w = _acc_init()
    for c in range(n_win):
        p = jnp.exp(s_w[c] - m_w)
        l_w = l_w + jnp.sum(p, axis=0, keepdims=True)
        acc_w = acc_w + _dot(vw_ref[0, 0, jnp.maximum(qi + c - WINDOW // TQ, 0)], p.astype(BF16))
    o_w = acc_w / l_w

    krow = lax.broadcasted_iota(jnp.int32, (TK, 1), 0)
    group_rows = TK * KV_GROUP

    def max_group(jg, m):
        k0 = pl.multiple_of(jg * group_rows, group_rows)
        s = _scores(q_ext(jg), ks_ref[0, 0, pl.ds(k0, group_rows), :], None)
        score_ref[pl.ds(k0, group_rows), :] = s
        return jnp.maximum(m, jnp.max(s, axis=0, keepdims=True))

    def max_tile(j, m):
        k0 = pl.multiple_of(j * TK, TK)
        bias = jnp.where(k0 + krow <= t, 0.0, NEG_INF)
        s = _scores(q_ext(j // KV_GROUP), ks_ref[0, 0, pl.ds(k0, TK), :], bias)
        score_ref[pl.ds(k0, TK), :] = s
        return jnp.maximum(m, jnp.max(s, axis=0, keepdims=True))

    def sum_rows(state, k0, rows, vt_t):
        l, acc = state
        p = jnp.exp(score_ref[pl.ds(k0, rows), :] - m_s)
        return l + jnp.sum(p, axis=0, keepdims=True), acc + _dot(vt_t, p.astype(BF16))

    def sum_group(jg, state):
        vt_t = jnp.concatenate([vs_ref[0, 0, jg * KV_GROUP + sub] for sub in range(KV_GROUP)], axis=1)
        return sum_rows(state, pl.multiple_of(jg * group_rows, group_rows), group_rows, vt_t)

    def sum_tile(j, state):
        return sum_rows(state, pl.multiple_of(j * TK, TK), TK, vs_ref[0, 0, j])

    n_full = q0 // group_rows
    n_kv = (q0 + TQ - 1) // TK + 1
    m_s = lax.fori_loop(0, n_full, max_group, _max_init())
    m_s = lax.fori_loop(n_full * KV_GROUP, n_kv, max_tile, m_s)
    state = lax.fori_loop(0, n_full, sum_group, _acc_init())
    state = lax.fori_loop(n_full * KV_GROUP, n_kv, sum_tile, state)
    o_s = state[1] / state[0]

    gn = gn_ref[0, 0]
    gate = [jnp.concatenate([gn[3 * r + c:3 * r + c + 1, :] for r in range(NSA_REP)], axis=1) for c in range(3)]
    o = gate[0] * o_c + gate[1] * o_s + gate[2] * o_w
    for r in range(NSA_REP):
        o_ref[0, r * HEAD_DIM:(r + 1) * HEAD_DIM, :] = o[:, r * TQ:(r + 1) * TQ]


def _nsa(q_t, kc, vc_t, ks, vs_t, kwp, vwp_t, gn_t, agg_t, n_cmp):
    bsz, _, seq = q_t.shape
    g = NSA_KV_GROUPS
    gw = NSA_REP * HEAD_DIM
    per_bg = lambda a: pl.BlockSpec((1, 1) + a.shape[2:], lambda b, gg, i: (b, gg) + (0,) * (a.ndim - 2))
    return pl.pallas_call(
        functools.partial(_nsa_body, n_cmp=n_cmp),
        grid=(bsz, g, seq // TQ),
        in_specs=[pl.BlockSpec((1, gw, TQ), lambda b, gg, i: (b, gg, i)),
                  per_bg(kc), per_bg(vc_t), per_bg(ks), per_bg(vs_t), per_bg(kwp), per_bg(vwp_t),
                  pl.BlockSpec((1, 1, gn_t.shape[2], TQ), lambda b, gg, i: (b, gg, 0, i)),
                  pl.BlockSpec(agg_t.shape, lambda b, gg, i: (0, 0))],
        out_specs=pl.BlockSpec((1, gw, TQ), lambda b, gg, i: (b, gg, i)),
        out_shape=jax.ShapeDtypeStruct((bsz, NSA_WIDTH, seq), F32),
        scratch_shapes=[pltpu.VMEM((agg_t.shape[0], NSA_REP * TQ), BF16),
                        pltpu.VMEM((seq, NSA_REP * TQ), F32)],
        compiler_params=_cparams(3, 56),
    )(q_t, kc, vc_t, ks, vs_t, kwp, vwp_t, gn_t, agg_t)


def _merge_body(on_ref, u_ref, v_ref, gate_ref, x_ref, gt_ref, sc_ref, sh_ref, ws_ref, bs_ref,
                wb_ref, wo_ref, g1_ref, b1_ref, wq_ref, x1_ref, h2_ref, qp_ref, *, alpha):
    tm = x_ref.shape[0]
    row = lax.broadcasted_iota(jnp.int32, (SGU_CHUNK, SGU_CHUNK), 0)
    col = lax.broadcasted_iota(jnp.int32, (SGU_CHUNK, SGU_CHUNK), 1)
    lane_group = lax.broadcasted_iota(jnp.int32, (1, SGU_WIDTH), 1) // SGU_GROUP_DIM
    mixed = []
    for c in range(tm // SGU_CHUNK):
        v = v_ref[c * SGU_CHUNK:(c + 1) * SGU_CHUNK, :]
        acc = jnp.zeros((SGU_CHUNK, SGU_WIDTH), F32)
        for g in range(SGU_GROUPS):
            wg = jnp.where(col <= row, ws_ref[g], 0.0).astype(BF16)
            vg = jnp.where(lane_group == g, v, 0.0).astype(BF16)
            acc = acc + _dot(wg, vg)
        mixed.append(acc + bs_ref[...])
    o_sgu = u_ref[...] * jnp.concatenate(mixed, axis=0)
    gate = gate_ref[...]
    merged = (gate[:, :D_MODEL] * _dot(on_ref[0].T.astype(BF16), wb_ref[0])
              + gate[:, D_MODEL:] * _dot(o_sgu.astype(BF16), wb_ref[1]))
    mix = _dot(merged.astype(BF16), wo_ref[...])
    x1 = _ln(alpha * x_ref[...] + gt_ref[0] * mix) * g1_ref[...] + b1_ref[...]
    x1_ref[...] = x1
    h2 = _ln(x1) * (1.0 + sc_ref[0]) + sh_ref[0]
    h2_ref[...] = h2
    qp_ref[...] = _dot(h2.astype(BF16), wq_ref[...])


def _merge(o_nsa, u, v, gate, x2, gt1, sc2, sh2, ws, bs, wb, wo, g1, b1, wq, seq, alpha):
    n, d = x2.shape
    tm = TM_MERGE
    nst = seq // tm
    row = lambda w: pl.BlockSpec((tm, w), lambda i: (i, 0))
    per_batch = pl.BlockSpec((1, 1, d), lambda i: (i // nst, 0, 0))
    const = lambda a: pl.BlockSpec(a.shape, lambda i: (0,) * a.ndim)
    nq = wq.shape[1]
    return pl.pallas_call(
        functools.partial(_merge_body, alpha=alpha),
        grid=(n // tm,),
        in_specs=[pl.BlockSpec((1, NSA_WIDTH, tm), lambda i: (i // nst, 0, i % nst)),
                  row(SGU_WIDTH), row(SGU_WIDTH), row(2 * d), row(d),
                  per_batch, per_batch, per_batch, const(ws), const(bs), const(wb), const(wo),
                  const(g1), const(b1), const(wq)],
        out_specs=[row(d), row(d), row(nq)],
        out_shape=[jax.ShapeDtypeStruct((n, d), F32), jax.ShapeDtypeStruct((n, d), F32),
                   jax.ShapeDtypeStruct((n, nq), F32)],
        compiler_params=_cparams(1, 56),
    )(o_nsa, u, v, gate, x2, gt1, sc2, sh2, ws, bs, wb, wo, g1, b1, wq)


def _topk_rows(s, k):
    nrow = s.shape[0]
    rid = lax.broadcasted_iota(jnp.int32, s.shape, 0).astype(F32)
    vals, rows = [], []
    for _ in range(k):
        m = jnp.max(s, axis=0, keepdims=True)
        first = jnp.min(jnp.where(s == m, rid, float(nrow)), axis=0, keepdims=True)
        s = jnp.where(rid == first, -jnp.inf, s)
        vals.append(m)
        rows.append(first)
    return vals, rows


def _candidate_blocks(k):
    todo = {(a, b) for a in range(k) for b in range(k) if (a + 1) * (b + 1) <= k}
    options = [(axis, fixed, start) for axis in (0, 1) for fixed in range(k) for start in range(0, k, 8)]
    cells = lambda axis, fixed, start: [((fixed, start + r) if axis == 0 else (start + r, fixed)) for r in range(8)]
    blocks = []
    while todo:
        best = max(options, key=lambda o: sum(c in todo for c in cells(*o)))
        owned = tuple(c in todo for c in cells(*best))
        todo -= set(cells(*best))
        blocks.append(best + (owned,))
    return tuple(blocks)


_CAND_BLOCKS = _candidate_blocks(PEER_TOPK)


def _retrieve_body(qp_ref, keys_ref, idx_ref, gate_ref, eid_ref):
    k = PEER_TOPK
    for h in range(PEER_HEADS):
        tops = []
        for c in range(2):
            off = (h * 2 + c) * PEER_HALF
            qh = qp_ref[:, off:off + PEER_HALF].astype(BF16)
            s = _dot_nt(keys_ref[h, c].astype(BF16), qh)
            tops.append(_topk_rows(s, k))
        (v1, i1), (v2, i2) = tops
        vm = (jnp.concatenate(v1, axis=0), jnp.concatenate(v2, axis=0))
        im = (jnp.concatenate(i1, axis=0), jnp.concatenate(i2, axis=0))
        sub = lax.broadcasted_iota(jnp.int32, (8, 1), 0)
        vals, eids, poss = [], [], []
        for axis, fixed, start, owned in _CAND_BLOCKS:
            if axis == 0:
                v = v1[fixed] + vm[1][start:start + 8]
                e = i1[fixed] * float(PEER_NKEYS) + im[1][start:start + 8]
                pos = fixed * k + start + sub
            else:
                v = vm[0][start:start + 8] + v2[fixed]
                e = im[0][start:start + 8] * float(PEER_NKEYS) + i2[fixed]
                pos = (start + sub) * k + fixed
            if not all(owned):
                keep = functools.reduce(jnp.logical_or, [sub == r for r in range(8) if owned[r]])
                v = jnp.where(keep, v, -jnp.inf)
                pos = jnp.where(keep, pos, k * k)
            vals.append(v)
            eids.append(e)
            poss.append(pos.astype(F32))
        cand = jnp.concatenate(vals, axis=0)
        cidx = jnp.concatenate(eids, axis=0)
        rid = jnp.concatenate(poss, axis=0)
        top_s, top_e = [], []
        for _ in range(k):
            m = jnp.max(cand, axis=0, keepdims=True)
            first = jnp.min(jnp.where(cand == m, rid, float(k * k)), axis=0, keepdims=True)
            hit = rid == first
            top_e.append(jnp.sum(jnp.where(hit, cidx, 0.0), axis=0, keepdims=True))
            cand = jnp.where(hit, -jnp.inf, cand)
            top_s.append(m)
        ts = jnp.concatenate(top_s, axis=0)
        e = jnp.exp(ts - ts[0:1])
        gate_ref[0, h * k:(h + 1) * k, :] = e / jnp.sum(e, axis=0, keepdims=True)
        eid_ref[h * k:(h + 1) * k, :] = jnp.concatenate(top_e, axis=0)
    idx_ref[...] = (eid_ref[...].T * float(ROW_SUB)).astype(jnp.int32)


def _retrieve(qp, keys):
    n = qp.shape[0]
    nt = n // TC
    out = pl.BlockSpec((1, PEER_PAIRS, TC), lambda i: (i, 0, 0))
    return pl.pallas_call(
        _retrieve_body,
        grid=(nt,),
        in_specs=[pl.BlockSpec((TC, qp.shape[1]), lambda i: (i, 0)),
                  pl.BlockSpec(keys.shape, lambda i: (0, 0, 0, 0))],
        out_specs=[pl.BlockSpec((TC, PEER_PAIRS), lambda i: (i, 0)), out],
        out_shape=[jax.ShapeDtypeStruct((n, PEER_PAIRS), jnp.int32),
                   jax.ShapeDtypeStruct((nt, PEER_PAIRS, TC), F32)],
        scratch_shapes=[pltpu.VMEM((PEER_PAIRS, TC), F32)],
        compiler_params=_cparams(1, 48),
    )(qp, keys)


ROW_SUB = D_MODEL // 2 // LANES
ACT_UNROLL = 32
OFF_REGS = 8


def _raw_row(tab_ref, scaled_idx):
    return tab_ref[pl.ds(pl.multiple_of(scaled_idx, ROW_SUB), ROW_SUB), :]


def _widen(row):
    lo = lax.bitcast_convert_type(lax.shift_left(row, jnp.uint32(16)), F32)
    hi = lax.bitcast_convert_type(row & jnp.uint32(0xFFFF0000), F32)
    return lo, hi


def _load_row(tab_ref, scaled_idx):
    return _widen(_raw_row(tab_ref, scaled_idx))


def _expert_act_body(off_ref, idx_ref, h_ref, gate_ref, tab_ref, w_ref, slot_ref, part_ref):
    off = [off_ref[j] for j in range(OFF_REGS)]

    def token(t, carry):
        x = h_ref[pl.ds(t, 1), :]
        chunk = lambda s: x[:, s * LANES:(s + 1) * LANES]
        xlo = jnp.concatenate([chunk(s) for s in range(ROW_SUB)], axis=0)
        xhi = jnp.concatenate([chunk(ROW_SUB + s) for s in range(ROW_SUB)], axis=0)
        base = pl.multiple_of(t * PEER_PAIRS, PEER_PAIRS)
        xlo2 = jnp.concatenate([xlo, xlo], axis=0)
        xhi2 = jnp.concatenate([xhi, xhi], axis=0)
        for p in range(0, PEER_PAIRS, 2):
            if p % OFF_REGS == 0:
                idx_sub = idx_ref.at[t, pl.ds(p, OFF_REGS)]
            rows = jnp.concatenate([_raw_row(tab_ref, idx_sub[off[(p + k) % OFF_REGS]]) for k in range(2)], axis=0)
            lo, hi = _widen(rows)
            slot_ref[p * ROW_SUB:(p + 2) * ROW_SUB, :] = lo * xlo2 + hi * xhi2
        part = slot_ref[pl.ds(0, PEER_PAIRS, stride=ROW_SUB), :]
        for s in range(1, ROW_SUB):
            part = part + slot_ref[pl.ds(s, PEER_PAIRS, stride=ROW_SUB), :]
        part_ref[pl.ds(pl.multiple_of(base, PEER_PAIRS), PEER_PAIRS), :] = part
        return carry

    lax.fori_loop(0, TC, token, 0)

    tok_lane = lax.broadcasted_iota(jnp.int32, (PEER_PAIRS, TC), 1)

    def lane_sums(i, act):
        for k in range(ACT_UNROLL):
            t = i * ACT_UNROLL + k
            part = part_ref[pl.ds(pl.multiple_of(t * PEER_PAIRS, PEER_PAIRS), PEER_PAIRS), :]
            act = jnp.where(tok_lane == t, jnp.sum(part, axis=1, keepdims=True), act)
        return act

    act = lax.fori_loop(0, TC // ACT_UNROLL, lane_sums, jnp.zeros((PEER_PAIRS, TC), F32))
    w_ref[...] = (gate_ref[0] * _gelu(act)).T


def _expert_act(idx_tok, h2, gate_t, tab):
    nt = gate_t.shape[0]
    tile = pl.BlockSpec((1, PEER_PAIRS, TC), lambda i: (i, 0, 0))
    return pl.pallas_call(
        _expert_act_body,
        grid=(nt,),
        in_specs=[pl.BlockSpec(memory_space=pltpu.SMEM),
                  pl.BlockSpec((TC, PEER_PAIRS), lambda i: (i, 0), memory_space=pltpu.SMEM),
                  pl.BlockSpec((TC, h2.shape[1]), lambda i: (i, 0)),
                  tile,
                  pl.BlockSpec(tab.shape, lambda i: (0, 0), pipeline_mode=pl.Buffered(1))],
        out_specs=pl.BlockSpec((TC, PEER_PAIRS), lambda i: (i, 0)),
        out_shape=jax.ShapeDtypeStruct((nt * TC, PEER_PAIRS), F32),
        scratch_shapes=[pltpu.VMEM((PEER_PAIRS * ROW_SUB, LANES), F32),
                        pltpu.VMEM((TC * PEER_PAIRS, LANES), F32)],
        compiler_params=_cparams(1, 52),
    )(jnp.arange(OFF_REGS, dtype=jnp.int32), idx_tok, h2, gate_t, tab)


def _expert_out_body(off_ref, idx_ref, w_ref, tab_ref, o_ref):
    n_acc = 4
    off = [off_ref[j] for j in range(OFF_REGS)]

    def token(t, carry):
        acc_lo = [jnp.zeros((ROW_SUB, LANES), F32) for _ in range(n_acc)]
        acc_hi = [jnp.zeros((ROW_SUB, LANES), F32) for _ in range(n_acc)]
        for p in range(PEER_PAIRS):
            if p % OFF_REGS == 0:
                idx_sub = idx_ref.at[t, pl.ds(p, OFF_REGS)]
                w_sub = w_ref.at[t, pl.ds(p, OFF_REGS)]
            lo, hi = _load_row(tab_ref, idx_sub[off[p % OFF_REGS]])
            w = w_sub[off[p % OFF_REGS]]
            acc_lo[p % n_acc] = acc_lo[p % n_acc] + w * lo
            acc_hi[p % n_acc] = acc_hi[p % n_acc] + w * hi
        lo = (acc_lo[0] + acc_lo[1]) + (acc_lo[2] + acc_lo[3])
        hi = (acc_hi[0] + acc_hi[1]) + (acc_hi[2] + acc_hi[3])
        o_ref[t] = jnp.concatenate([lo, hi], axis=0)
        return carry

    lax.fori_loop(0, TB, token, 0)


def _expert_out(idx_tok, w_tok, tab):
    n = idx_tok.shape[0]
    smem = pl.BlockSpec((TB, PEER_PAIRS), lambda i: (i, 0), memory_space=pltpu.SMEM)
    return pl.pallas_call(
        _expert_out_body,
        grid=(n // TB,),
        in_specs=[pl.BlockSpec(memory_space=pltpu.SMEM), smem, smem,
                  pl.BlockSpec(tab.shape, lambda i: (0, 0), pipeline_mode=pl.Buffered(1))],
        out_specs=pl.BlockSpec((TB, D_MODEL // LANES, LANES), lambda i: (i, 0, 0)),
        out_shape=jax.ShapeDtypeStruct((n, D_MODEL // LANES, LANES), F32),
        compiler_params=_cparams(1, 48),
    )(jnp.arange(OFF_REGS, dtype=jnp.int32), idx_tok, w_tok, tab)


def _final_body(x1_ref, f_ref, gt_ref, g_ref, b_ref, o_ref, *, alpha):
    o_ref[...] = _ln(alpha * x1_ref[...] + gt_ref[0] * f_ref[...]) * g_ref[...] + b_ref[...]


def _final(x1, ffn, gt2, g2, b2, seq, alpha):
    n, d = x1.shape
    tm = TM_LN
    nst = seq // tm
    row = pl.BlockSpec((tm, d), lambda i: (i, 0))
    const = pl.BlockSpec((1, d), lambda i: (0, 0))
    return pl.pallas_call(
        functools.partial(_final_body, alpha=alpha),
        grid=(n // tm,),
        in_specs=[row, row, pl.BlockSpec((1, 1, d), lambda i: (i // nst, 0, 0)), const, const],
        out_specs=row,
        out_shape=jax.ShapeDtypeStruct((n, d), F32),
        compiler_params=_cparams(1, 32),
    )(x1, ffn, gt2, g2, b2)


def _swap_halves(w, heads):
    d = w.shape[0]
    w4 = w.reshape(d, heads, 2, HEAD_DIM // 2)
    return jnp.flip(w4, axis=2).reshape(d, heads * HEAD_DIM)


def _fused_in_weight(w_in, w_merge):
    d = w_in.shape[0]
    o = 0
    parts = {}
    for name, width in (("q", NSA_WIDTH), ("kc", KV_WIDTH), ("vc", KV_WIDTH), ("ks", KV_WIDTH),
                        ("vs", KV_WIDTH), ("kw", KV_WIDTH), ("vw", KV_WIDTH), ("g", 3 * NSA_HEADS),
                        ("z", 2 * SGU_WIDTH)):
        parts[name] = w_in[:, o:o + width]
        o += width
    wq = parts["q"] * (HEAD_DIM ** -0.5)
    gcols = 3 * NSA_REP
    wg = jnp.zeros((d, 2 * LANES), w_in.dtype)
    wg = wg.at[:, :gcols].set(parts["g"][:, :gcols]).at[:, LANES:LANES + gcols].set(parts["g"][:, gcols:])
    g = NSA_KV_GROUPS
    cols = {"q": wq, "q_swap": _swap_halves(wq, NSA_HEADS),
            "kc": parts["kc"], "kc_swap": _swap_halves(parts["kc"], g),
            "ks": parts["ks"], "ks_swap": _swap_halves(parts["ks"], g),
            "kw": parts["kw"], "kw_swap": _swap_halves(parts["kw"], g),
            "vc": parts["vc"], "vs": parts["vs"], "vw": parts["vw"],
            "gates": wg, "z": parts["z"], "merge": w_merge}
    assert all(cols[name].shape[1] == width for name, width in _SEGMENTS)
    return jnp.concatenate([cols[name] for name, _ in _SEGMENTS], axis=1).astype(BF16)


def _rope_tables(seq):
    half = HEAD_DIM // 2
    pos = jnp.arange(seq, dtype=F32)
    inv_freq = ROPE_THETA ** (-jnp.arange(half, dtype=F32) / half)
    ang = pos[:, None] * inv_freq[None, :]
    cos, sin = jnp.cos(ang), jnp.sin(ang)
    reps = LANES // HEAD_DIM
    cosk = jnp.tile(jnp.concatenate([cos, cos], axis=1), (1, reps))
    sink = jnp.tile(jnp.concatenate([-sin, sin], axis=1), (1, reps))
    return cosk, sink


def _sel_aggregation(n_cmp_pad, n_sel):
    c0 = jnp.arange(n_cmp_pad)[:, None] * CMP_STRIDE
    s0 = jnp.arange(n_sel)[None, :] * SEL_BLOCK
    ov = jnp.clip(jnp.minimum(c0 + CMP_BLOCK, s0 + SEL_BLOCK) - jnp.maximum(c0, s0), 0, None)
    return (ov / CMP_BLOCK).astype(BF16)


def _pack_table(tab):
    half = tab.shape[1] // 2
    bits = lax.bitcast_convert_type(tab.astype(BF16), jnp.uint16).astype(jnp.uint32)
    packed = (bits[:, half:] << 16) | bits[:, :half]
    return packed.reshape(tab.shape[0] * ROW_SUB, LANES)


def _split_groups(a, bsz, seq):
    return a.reshape(bsz, seq, NSA_KV_GROUPS, HEAD_DIM).transpose(0, 2, 1, 3)


def kernel(x, c, w_ada, b_ada, w_in, cmp_pos, cmp_w1, cmp_b1, cmp_w2, cmp_b2, sgu_ln_g, sgu_ln_b, sgu_w, sgu_b, w_branch, w_merge, b_merge, w_out, ln1_g, ln1_b, peer_wq, peer_keys, peer_u, peer_v, ln2_g, ln2_b):
    bsz, seq, d = x.shape
    n = bsz * seq
    depth = w_ada.shape[0]
    alpha = (2.0 * depth) ** 0.25
    n_cmp = (seq - CMP_BLOCK) // CMP_STRIDE + 1
    n_half = seq // CMP_STRIDE
    n_sel = seq // SEL_BLOCK
    cosk, sink = _rope_tables(seq)
    agg = _sel_aggregation(n_half, n_sel)
    x2 = x.reshape(n, d)
    for l in range(depth):
        mod = _ada(c, w_ada[l], b_ada[l])
        sh1, sc1, gt1, sh2, sc2, gt2 = [m.reshape(bsz, 1, d) for m in jnp.split(mod, 6, axis=-1)]

        wall = _fused_in_weight(w_in[l], w_merge[l])
        (q_t, ks_ext, vs_t, kw_g, vw_t, gn_t, kc_r, vc_r, u, v, gate) = _proj(
            x2, sc1, sh1, wall, b_merge[l].reshape(1, -1), cosk, sink,
            sgu_ln_g[l].reshape(1, -1), sgu_ln_b[l].reshape(1, -1), seq)

        t16 = jnp.stack([_split_groups(kc_r, bsz, seq), _split_groups(vc_r, bsz, seq)])
        t16 = t16.reshape(2, bsz, NSA_KV_GROUPS, n_half, CMP_STRIDE * HEAD_DIM)
        cmp_kv = _compress(t16, cmp_pos[l].reshape(2, 1, CMP_BLOCK * HEAD_DIM), cmp_w1[l],
                           cmp_b1[l].reshape(2, 1, CMP_HIDDEN), cmp_w2[l], cmp_b2[l].reshape(2, 1, HEAD_DIM))

        o_nsa_t = _nsa(q_t, cmp_kv[0], cmp_kv[1].transpose(0, 1, 3, 2), ks_ext, vs_t, kw_g, vw_t, gn_t, agg.T, n_cmp)

        bs = jnp.repeat(sgu_b[l].T, SGU_GROUP_DIM, axis=1)
        x1, h2, qp = _merge(o_nsa_t, u, v, gate, x2, gt1, sc2, sh2,
                            sgu_w[l], bs, w_branch[l].astype(BF16), w_out[l].astype(BF16),
                            ln1_g[l].reshape(1, d), ln1_b[l].reshape(1, d), peer_wq[l].astype(BF16), seq, alpha)

        idx_tok, gate_t = _retrieve(qp, peer_keys[l])
        w_tok = _expert_act(idx_tok, h2, gate_t, _pack_table(peer_u[l]))
        ffn = _expert_out(idx_tok, w_tok, _pack_table(peer_v[l]))
        x2 = _final(x1, ffn.reshape(n, d), gt2, ln2_g[l].reshape(1, d), ln2_b[l].reshape(1, d), seq, alpha)
    return x2.reshape(bsz, seq, d)
```

```python
import functools

import jax
import jax.numpy as jnp
from jax import lax
from jax.experimental import pallas as pl
from jax.experimental.pallas import tpu as pltpu

D_MODEL = 1024
NSA_HEADS = 8
NSA_KV_GROUPS = 2
NSA_REP = NSA_HEADS // NSA_KV_GROUPS
HEAD_DIM = 64
NSA_WIDTH = NSA_HEADS * HEAD_DIM
KV_WIDTH = NSA_KV_GROUPS * HEAD_DIM
CMP_BLOCK = 32
CMP_STRIDE = 16
CMP_HIDDEN = 2 * HEAD_DIM
SEL_BLOCK = 64
SEL_TOPK = 16
WINDOW = 512
ROPE_THETA = 10000.0
SGU_GROUPS = 8
SGU_WIDTH = D_MODEL // 2
SGU_GROUP_DIM = SGU_WIDTH // SGU_GROUPS
SGU_CHUNK = 128
PEER_HEADS = 8
PEER_NKEYS = 128
PEER_EXPERTS = PEER_NKEYS * PEER_NKEYS
PEER_QDIM = 256
PEER_HALF = PEER_QDIM // 2
PEER_TOPK = 16
PEER_PAIRS = PEER_HEADS * PEER_TOPK
LN_EPS = 1e-5
NEG_INF = -1e30
FORCE_SCORE = 1e9

LANES = 128
MIB = 1024 * 1024
BF16 = jnp.bfloat16
F32 = jnp.float32

TM_PROJ = 256
TM_MERGE = 256
TM_LN = 512
TQ = 128
TK = 256
KV_GROUP = 4
GROUP_BLOCKS = TK * KV_GROUP // SEL_BLOCK
GATE_ROWS = 16
TC = 128
TB = 128

_SEGMENTS = (("q", NSA_WIDTH), ("q_swap", NSA_WIDTH), ("kc", KV_WIDTH), ("kc_swap", KV_WIDTH),
             ("ks", KV_WIDTH), ("ks_swap", KV_WIDTH), ("kw", KV_WIDTH), ("kw_swap", KV_WIDTH),
             ("vc", KV_WIDTH), ("vs", KV_WIDTH), ("vw", KV_WIDTH), ("gates", 2 * LANES),
             ("z", 2 * SGU_WIDTH), ("merge", 2 * D_MODEL))


def _segment_offsets():
    offs, o = {}, 0
    for name, width in _SEGMENTS:
        offs[name] = o
        o += width
    return offs, o


_OFF, _WCOLS = _segment_offsets()
_OQ, _OQS, _OKC, _OKCS = _OFF["q"], _OFF["q_swap"], _OFF["kc"], _OFF["kc_swap"]
_OKS, _OKSS, _OKW, _OKWS = _OFF["ks"], _OFF["ks_swap"], _OFF["kw"], _OFF["kw_swap"]
_OVC, _OVS, _OVW, _OG, _OZ, _OM = _OFF["vc"], _OFF["vs"], _OFF["vw"], _OFF["gates"], _OFF["z"], _OFF["merge"]


VMEM_MIB_V7X = 64


def _cparams(n_axes, vmem_mib):
    assert vmem_mib < VMEM_MIB_V7X
    return pltpu.CompilerParams(
        dimension_semantics=("parallel",) * n_axes,
        vmem_limit_bytes=vmem_mib * MIB)


def _ln(x):
    mu = jnp.mean(x, axis=-1, keepdims=True)
    xc = x - mu
    var = jnp.mean(xc * xc, axis=-1, keepdims=True)
    return xc * lax.rsqrt(var + LN_EPS)


def _gelu(x):
    return 0.5 * x * (1.0 + lax.erf(x * (2.0 ** -0.5)))


def _dot(a, b):
    return jnp.dot(a, b, preferred_element_type=F32)


def _dot_nt(a, b):
    return lax.dot_general(a, b, (((1,), (1,)), ((), ())), preferred_element_type=F32)


def _ada_body(c_ref, w_ref, b_ref, o_ref):
    cv = c_ref[...]
    a = cv * jax.nn.sigmoid(cv)
    o_ref[...] = _dot(a.astype(BF16), w_ref[...].astype(BF16)) + b_ref[...]


def _ada(c, w, b):
    bsz, d = c.shape
    n = w.shape[1]
    tn = 1024
    return pl.pallas_call(
        _ada_body,
        grid=(n // tn,),
        in_specs=[pl.BlockSpec((bsz, d), lambda j: (0, 0)),
                  pl.BlockSpec((d, tn), lambda j: (0, j)),
                  pl.BlockSpec((1, tn), lambda j: (0, j))],
        out_specs=pl.BlockSpec((bsz, tn), lambda j: (0, j)),
        out_shape=jax.ShapeDtypeStruct((bsz, n), F32),
        compiler_params=_cparams(1, 32),
    )(c, w, b.reshape(1, n))


def _proj_body(x_ref, sc_ref, sh_ref, w_ref, bm_ref, cos_ref, sin_ref, sg_ref, sb_ref,
               q_ref, ks_ref, vs_ref, kw_ref, vw_ref, gn_ref, kc_ref, vc_ref, u_ref, v_ref, gate_ref,
               *, n_seq_tiles):
    h = _ln(x_ref[...]) * (1.0 + sc_ref[0]) + sh_ref[0]
    hb = h.astype(BF16)

    def proj(off, width):
        return _dot(hb, w_ref[:, off:off + width])

    cosk = cos_ref[...]
    sink = sin_ref[...]
    cosq = jnp.concatenate([cosk] * (NSA_WIDTH // LANES), axis=1)
    sinq = jnp.concatenate([sink] * (NSA_WIDTH // LANES), axis=1)
    q_ref[0] = (proj(_OQ, NSA_WIDTH) * cosq + proj(_OQS, NSA_WIDTH) * sinq).T
    kc_ref[...] = proj(_OKC, KV_WIDTH) * cosk + proj(_OKCS, KV_WIDTH) * sink
    vc_ref[...] = proj(_OVC, KV_WIDTH)
    tm = x_ref.shape[0]
    ks = proj(_OKS, KV_WIDTH) * cosk + proj(_OKSS, KV_WIDTH) * sink
    kw = proj(_OKW, KV_WIDTH) * cosk + proj(_OKWS, KV_WIDTH) * sink
    vs_t = proj(_OVS, KV_WIDTH).T
    vw_t = proj(_OVW, KV_WIDTH).T
    gn_t = jax.nn.sigmoid(proj(_OG, 2 * LANES)).T
    pos = (pl.program_id(0) % n_seq_tiles) * tm + lax.broadcasted_iota(jnp.int32, (tm, 1), 0)
    block_onehot = jnp.where((pos // SEL_BLOCK) % GROUP_BLOCKS
                             == lax.broadcasted_iota(jnp.int32, (1, GROUP_BLOCKS), 1), 1.0, 0.0)
    for g in range(NSA_KV_GROUPS):
        cols = slice(g * HEAD_DIM, (g + 1) * HEAD_DIM)
        ks_ref[0, g] = jnp.concatenate([ks[:, cols], block_onehot], axis=1).astype(BF16)
        kw_ref[0, g] = kw[:, cols].astype(BF16)
        vs_ref[0, g, 0] = vs_t[cols, :].astype(BF16)
        for c in range(tm // TQ):
            vw_ref[0, g, c] = vw_t[cols, c * TQ:(c + 1) * TQ].astype(BF16)
        gn_ref[0, g] = gn_t[g * LANES:g * LANES + GATE_ROWS, :]
    u_ref[...] = _gelu(proj(_OZ, SGU_WIDTH))
    v_ref[...] = _ln(_gelu(proj(_OZ + SGU_WIDTH, SGU_WIDTH))) * sg_ref[...] + sb_ref[...]
    gate_ref[...] = jax.nn.sigmoid(proj(_OM, 2 * D_MODEL) + bm_ref[...])


def _proj(x2, sc, sh, wall, bm, cosk, sink, sg, sb, seq):
    n, d = x2.shape
    tm = TM_PROJ
    nst = seq // tm
    row = lambda w: pl.BlockSpec((tm, w), lambda i: (i, 0))
    per_batch = pl.BlockSpec((1, 1, d), lambda i: (i // nst, 0, 0))
    const = lambda shp: pl.BlockSpec(shp, lambda i: (0,) * len(shp))
    pos = pl.BlockSpec((tm, LANES), lambda i: (i % nst, 0))
    outs = [(KV_WIDTH, F32), (KV_WIDTH, F32), (SGU_WIDTH, F32), (SGU_WIDTH, F32), (2 * D_MODEL, F32)]
    assert tm == TK and tm % TQ == 0
    bsz, g = n // seq, NSA_KV_GROUPS
    att = [((bsz, NSA_WIDTH, seq), (1, NSA_WIDTH, tm), lambda i: (i // nst, 0, i % nst), F32),
           ((bsz, g, seq, HEAD_DIM + GROUP_BLOCKS), (1, g, tm, HEAD_DIM + GROUP_BLOCKS),
            lambda i: (i // nst, 0, i % nst, 0), BF16),
           ((bsz, g, seq // TK, HEAD_DIM, TK), (1, g, 1, HEAD_DIM, TK), lambda i: (i // nst, 0, i % nst, 0, 0), BF16),
           ((bsz, g, seq, HEAD_DIM), (1, g, tm, HEAD_DIM), lambda i: (i // nst, 0, i % nst, 0), BF16),
           ((bsz, g, seq // TQ, HEAD_DIM, TQ), (1, g, tm // TQ, HEAD_DIM, TQ),
            lambda i: (i // nst, 0, i % nst, 0, 0), BF16),
           ((bsz, g, GATE_ROWS, seq), (1, g, GATE_ROWS, tm), lambda i: (i // nst, 0, 0, i % nst), F32)]
    return pl.pallas_call(
        functools.partial(_proj_body, n_seq_tiles=nst),
        grid=(n // tm,),
        in_specs=[row(d), per_batch, per_batch, const((d, _WCOLS)), const((1, 2 * D_MODEL)),
                  pos, pos, const((1, SGU_WIDTH)), const((1, SGU_WIDTH))],
        out_specs=[pl.BlockSpec(blk, imap) for _, blk, imap, _ in att] + [row(w) for w, _ in outs],
        out_shape=([jax.ShapeDtypeStruct(shp, dt) for shp, _, _, dt in att]
                   + [jax.ShapeDtypeStruct((n, w), dt) for w, dt in outs]),
        compiler_params=_cparams(1, 56),
    )(x2, sc, sh, wall, bm, cosk, sink, sg, sb)


def _cmp_body(t_ref, pos_ref, w1_ref, b1_ref, w2_ref, b2_ref, o_ref):
    half = (CMP_BLOCK // 2) * HEAD_DIM
    t = t_ref[0, 0, 0]
    pos = pos_ref[0]
    ta = (t + pos[:, :half]).astype(BF16)
    tb = (t + pos[:, half:]).astype(BF16)
    a = _dot(ta, w1_ref[0, :half, :].astype(BF16))
    b = _dot(tb, w1_ref[0, half:, :].astype(BF16))
    nrow = t.shape[0]
    b_next = pltpu.roll(b, nrow - 1, axis=0)
    hid = _gelu(a + b_next + b1_ref[0])
    out = _dot(hid.astype(BF16), w2_ref[0].astype(BF16)) + b2_ref[0]
    o_ref[0, 0, 0] = out.astype(o_ref.dtype)


def _compress(t16, pos, w1, b1, w2, b2):
    _, bsz, g, nr, dd = t16.shape
    per_kind = lambda shp: pl.BlockSpec((1,) + shp, lambda k, b, gg: (k,) + (0,) * len(shp))
    blk = lambda w: pl.BlockSpec((1, 1, 1, nr, w), lambda k, b, gg: (k, b, gg, 0, 0))
    return pl.pallas_call(
        _cmp_body,
        grid=(2, bsz, g),
        in_specs=[blk(dd), per_kind((1, CMP_BLOCK * HEAD_DIM)), per_kind((CMP_BLOCK * HEAD_DIM, CMP_HIDDEN)),
                  per_kind((1, CMP_HIDDEN)), per_kind((CMP_HIDDEN, HEAD_DIM)), per_kind((1, HEAD_DIM))],
        out_specs=blk(HEAD_DIM),
        out_shape=jax.ShapeDtypeStruct((2, bsz, g, nr, HEAD_DIM), BF16),
        compiler_params=_cparams(3, 48),
    )(t16, pos, w1, b1, w2, b2)


def _scores(q_all, kt, bias):
    s = _dot(kt, q_all)
    return s if bias is None else s + bias


def _max_init():
    return jnp.full((1, NSA_REP * TQ), NEG_INF, F32)


def _acc_init():
    w = NSA_REP * TQ
    return jnp.zeros((1, w), F32), jnp.zeros((HEAD_DIM, w), F32)


def _lane_tile(a):
    return jnp.concatenate([a] * NSA_REP, axis=1)


def _nsa_body(q_ref, kc_ref, vc_ref, ks_ref, vs_ref, kw_ref, vw_ref, gn_ref, agg_ref, o_ref, selneg_ref, score_ref, *, n_cmp):
    qi = pl.program_id(2)
    q0 = qi * TQ
    t1 = q0 + lax.broadcasted_iota(jnp.int32, (1, TQ), 1)
    t = _lane_tile(t1)
    q_all = jnp.concatenate([q_ref[0, r * HEAD_DIM:(r + 1) * HEAD_DIM, :] for r in range(NSA_REP)],
                            axis=1).astype(BF16)

    kcm = kc_ref[0, 0]
    ncp = kcm.shape[0]
    nidx = lax.broadcasted_iota(jnp.int32, (ncp, 1), 0)
    mask_c = (nidx * CMP_STRIDE + (CMP_BLOCK - 1) <= t) & (nidx < n_cmp)
    sm = jnp.where(mask_c, _dot(kcm, q_all), NEG_INF)
    e = jnp.where(mask_c, jnp.exp(sm - jnp.max(sm, axis=0, keepdims=True)), 0.0)
    l = jnp.sum(e, axis=0, keepdims=True)
    p = e * (1.0 / jnp.where(l > 0.0, l, 1.0))
    o_c = _dot(vc_ref[0, 0], p.astype(BF16))
    psum = p[:, :TQ]
    for r in range(1, NSA_REP):
        psum = psum + p[:, r * TQ:(r + 1) * TQ]

    p_hi = psum.astype(BF16)
    p_lo = (psum - p_hi.astype(F32)).astype(BF16)
    agg_t = agg_ref[...]
    imp = _dot(agg_t, p_hi) + _dot(agg_t, p_lo)
    n_sel = imp.shape[0]
    blk = lax.broadcasted_iota(jnp.int32, (n_sel, 1), 0)
    blk_f = blk.astype(F32)
    cur = t1 // SEL_BLOCK
    forced = (blk == 0) | (blk == cur) | (blk == cur - 1)
    score = jnp.where(forced, FORCE_SCORE, jnp.where(blk <= cur, imp, NEG_INF))
    sel = jnp.zeros((n_sel, TQ), jnp.bool_)
    for _ in range(min(SEL_TOPK, n_sel)):
        m = jnp.max(score, axis=0, keepdims=True)
        first = jnp.min(jnp.where(score == m, blk_f, float(n_sel)), axis=0, keepdims=True)
        hit = blk_f == first
        sel = sel | hit
        score = jnp.where(hit, -jnp.inf, score)
    selneg_ref[...] = _lane_tile(jnp.where(sel, 0.0, NEG_INF).astype(BF16))

    def q_ext(jg):
        rows = selneg_ref[pl.ds(pl.multiple_of(jg * GROUP_BLOCKS, GROUP_BLOCKS), GROUP_BLOCKS), :]
        return jnp.concatenate([q_all, rows], axis=0)

    krow_w = lax.broadcasted_iota(jnp.int32, (TQ, 1), 0)

    def win_chunk(c):
        kpos = q0 - WINDOW + c * TQ + krow_w
        dlt = t - kpos
        bias = jnp.where((dlt >= 0) & (dlt < WINDOW) & (kpos >= 0), 0.0, NEG_INF)
        start = jnp.maximum(q0 - WINDOW + c * TQ, 0)
        return kw_ref[0, 0, pl.ds(pl.multiple_of(start, TQ), TQ), :], bias

    n_win = WINDOW // TQ + 1
    s_w = [_scores(q_all, *win_chunk(c)) for c in range(n_win)]
    m_w = functools.reduce(jnp.maximum, [jnp.max(s, axis=0, keepdims=True) for s in s_w])
    l_w, acc_w = _acc_init()
    for c in range(n_win):
        p = jnp.exp(s_w[c] - m_w)
        l_w = l_w + jnp.sum(p, axis=0, keepdims=True)
        acc_w = acc_w + _dot(vw_ref[0, 0, jnp.maximum(qi + c - WINDOW // TQ, 0)], p.astype(BF16))
    o_w = acc_w / l_w

    krow = lax.broadcasted_iota(jnp.int32, (TK, 1), 0)
    group_rows = TK * KV_GROUP

    def max_group(jg, m):
        k0 = pl.multiple_of(jg * group_rows, group_rows)
        s = _scores(q_ext(jg), ks_ref[0, 0, pl.ds(k0, group_rows), :], None)
        score_ref[pl.ds(k0, group_rows), :] = s
        return jnp.maximum(m, jnp.max(s, axis=0, keepdims=True))

    def max_tile(j, m):
        k0 = pl.multiple_of(j * TK, TK)
        bias = jnp.where(k0 + krow <= t, 0.0, NEG_INF)
        s = _scores(q_ext(j // KV_GROUP), ks_ref[0, 0, pl.ds(k0, TK), :], bias)
        score_ref[pl.ds(k0, TK), :] = s
        return jnp.maximum(m, jnp.max(s, axis=0, keepdims=True))

    def sum_rows(state, k0, rows, vt_t):
        l, acc = state
        p = jnp.exp(score_ref[pl.ds(k0, rows), :] - m_s)
        return l + jnp.sum(p, axis=0, keepdims=True), acc + _dot(vt_t, p.astype(BF16))

    def sum_group(jg, state):
        vt_t = jnp.concatenate([vs_ref[0, 0, jg * KV_GROUP + sub] for sub in range(KV_GROUP)], axis=1)
        return sum_rows(state, pl.multiple_of(jg * group_rows, group_rows), group_rows, vt_t)

    def sum_tile(j, state):
        return sum_rows(state, pl.multiple_of(j * TK, TK), TK, vs_ref[0, 0, j])

    n_full = q0 // group_rows
    n_kv = (q0 + TQ - 1) // TK + 1
    m_s = lax.fori_loop(0, n_full, max_group, _max_init())
    m_s = lax.fori_loop(n_full * KV_GROUP, n_kv, max_tile, m_s)
    state = lax.fori_loop(0, n_full, sum_group, _acc_init())
    state = lax.fori_loop(n_full * KV_GROUP, n_kv, sum_tile, state)
    o_s = state[1] / state[0]

    gn = gn_ref[0, 0]
    gate = [jnp.concatenate([gn[3 * r + c:3 * r + c + 1, :] for r in range(NSA_REP)], axis=1) for c in range(3)]
    o = gate[0] * o_c + gate[1] * o_s + gate[2] * o_w
    for r in range(NSA_REP):
        o_ref[0, r * HEAD_DIM:(r + 1) * HEAD_DIM, :] = o[:, r * TQ:(r + 1) * TQ]


def _nsa(q_t, kc, vc_t, ks, vs_t, kwp, vwp_t, gn_t, agg_t, n_cmp):
    bsz, _, seq = q_t.shape
    g = NSA_KV_GROUPS
    gw = NSA_REP * HEAD_DIM
    per_bg = lambda a: pl.BlockSpec((1, 1) + a.shape[2:], lambda b, gg, i: (b, gg) + (0,) * (a.ndim - 2))
    return pl.pallas_call(
        functools.partial(_nsa_body, n_cmp=n_cmp),
        grid=(bsz, g, seq // TQ),
        in_specs=[pl.BlockSpec((1, gw, TQ), lambda b, gg, i: (b, gg, i)),
                  per_bg(kc), per_bg(vc_t), per_bg(ks), per_bg(vs_t), per_bg(kwp), per_bg(vwp_t),
                  pl.BlockSpec((1, 1, gn_t.shape[2], TQ), lambda b, gg, i: (b, gg, 0, i)),
                  pl.BlockSpec(agg_t.shape, lambda b, gg, i: (0, 0))],
        out_specs=pl.BlockSpec((1, gw, TQ), lambda b, gg, i: (b, gg, i)),
        out_shape=jax.ShapeDtypeStruct((bsz, NSA_WIDTH, seq), F32),
        scratch_shapes=[pltpu.VMEM((agg_t.shape[0], NSA_REP * TQ), BF16),
                        pltpu.VMEM((seq, NSA_REP * TQ), F32)],
        compiler_params=_cparams(3, 56),
    )(q_t, kc, vc_t, ks, vs_t, kwp, vwp_t, gn_t, agg_t)


def _merge_body(on_ref, u_ref, v_ref, gate_ref, x_ref, gt_ref, sc_ref, sh_ref, ws_ref, bs_ref,
                wb_ref, wo_ref, g1_ref, b1_ref, wq_ref, keys_ref, x1_ref, h2_ref, idx_ref, pgate_ref,
                qp_ref, eid_ref, *, alpha):
    tm = x_ref.shape[0]
    row = lax.broadcasted_iota(jnp.int32, (SGU_CHUNK, SGU_CHUNK), 0)
    col = lax.broadcasted_iota(jnp.int32, (SGU_CHUNK, SGU_CHUNK), 1)
    lane_group = lax.broadcasted_iota(jnp.int32, (1, SGU_WIDTH), 1) // SGU_GROUP_DIM
    mixed = []
    for c in range(tm // SGU_CHUNK):
        v = v_ref[c * SGU_CHUNK:(c + 1) * SGU_CHUNK, :]
        acc = jnp.zeros((SGU_CHUNK, SGU_WIDTH), F32)
        for g in range(SGU_GROUPS):
            wg = jnp.where(col <= row, ws_ref[g], 0.0).astype(BF16)
            vg = jnp.where(lane_group == g, v, 0.0).astype(BF16)
            acc = acc + _dot(wg, vg)
        mixed.append(acc + bs_ref[...])
    o_sgu = u_ref[...] * jnp.concatenate(mixed, axis=0)
    gate = gate_ref[...]
    merged = (gate[:, :D_MODEL] * _dot(on_ref[0].T.astype(BF16), wb_ref[0])
              + gate[:, D_MODEL:] * _dot(o_sgu.astype(BF16), wb_ref[1]))
    mix = _dot(merged.astype(BF16), wo_ref[...])
    x1 = _ln(alpha * x_ref[...] + gt_ref[0] * mix) * g1_ref[...] + b1_ref[...]
    x1_ref[...] = x1
    h2 = _ln(x1) * (1.0 + sc_ref[0]) + sh_ref[0]
    h2_ref[...] = h2
    qp_ref[...] = _dot(h2.astype(BF16), wq_ref[...])
    for sub in range(tm // TC):
        _retrieve_tile(qp_ref, keys_ref, idx_ref, pgate_ref, eid_ref, sub)


def _merge(o_nsa, u, v, gate, x2, gt1, sc2, sh2, ws, bs, wb, wo, g1, b1, wq, keys, seq, alpha):
    n, d = x2.shape
    tm = TM_MERGE
    nst = seq // tm
    row = lambda w: pl.BlockSpec((tm, w), lambda i: (i, 0))
    per_batch = pl.BlockSpec((1, 1, d), lambda i: (i // nst, 0, 0))
    const = lambda a: pl.BlockSpec(a.shape, lambda i: (0,) * a.ndim)
    nq = wq.shape[1]
    return pl.pallas_call(
        functools.partial(_merge_body, alpha=alpha),
        grid=(n // tm,),
        in_specs=[pl.BlockSpec((1, NSA_WIDTH, tm), lambda i: (i // nst, 0, i % nst)),
                  row(SGU_WIDTH), row(SGU_WIDTH), row(2 * d), row(d),
                  per_batch, per_batch, per_batch, const(ws), const(bs), const(wb), const(wo),
                  const(g1), const(b1), const(wq), const(keys)],
        out_specs=[row(d), row(d), row(PEER_PAIRS),
                   pl.BlockSpec((tm // TC, PEER_PAIRS, TC), lambda i: (i, 0, 0))],
        out_shape=[jax.ShapeDtypeStruct((n, d), F32), jax.ShapeDtypeStruct((n, d), F32),
                   jax.ShapeDtypeStruct((n, PEER_PAIRS), jnp.int32),
                   jax.ShapeDtypeStruct((n // TC, PEER_PAIRS, TC), F32)],
        scratch_shapes=[pltpu.VMEM((tm, nq), F32), pltpu.VMEM((PEER_PAIRS, TC), F32)],
        compiler_params=_cparams(1, 56),
    )(o_nsa, u, v, gate, x2, gt1, sc2, sh2, ws, bs, wb, wo, g1, b1, wq, keys)


def _topk_rows(s, k):
    nrow = s.shape[0]
    rid = lax.broadcasted_iota(jnp.int32, s.shape, 0).astype(F32)
    vals, rows = [], []
    for _ in range(k):
        m = jnp.max(s, axis=0, keepdims=True)
        first = jnp.min(jnp.where(s == m, rid, float(nrow)), axis=0, keepdims=True)
        s = jnp.where(rid == first, -jnp.inf, s)
        vals.append(m)
        rows.append(first)
    return vals, rows


def _candidate_blocks(k):
    todo = {(a, b) for a in range(k) for b in range(k) if (a + 1) * (b + 1) <= k}
    options = [(axis, fixed, start) for axis in (0, 1) for fixed in range(k) for start in range(0, k, 8)]
    cells = lambda axis, fixed, start: [((fixed, start + r) if axis == 0 else (start + r, fixed)) for r in range(8)]
    blocks = []
    while todo:
        best = max(options, key=lambda o: sum(c in todo for c in cells(*o)))
        owned = tuple(c in todo for c in cells(*best))
        todo -= set(cells(*best))
        blocks.append(best + (owned,))
    return tuple(blocks)


_CAND_BLOCKS = _candidate_blocks(PEER_TOPK)


def _retrieve_tile(qp_ref, keys_ref, idx_ref, gate_ref, eid_ref, tile):
    k = PEER_TOPK
    rows = slice(tile * TC, (tile + 1) * TC)
    for h in range(PEER_HEADS):
        tops = []
        for c in range(2):
            off = (h * 2 + c) * PEER_HALF
            qh = qp_ref[rows, off:off + PEER_HALF].astype(BF16)
            s = _dot_nt(keys_ref[h, c].astype(BF16), qh)
            tops.append(_topk_rows(s, k))
        (v1, i1), (v2, i2) = tops
        vm = (jnp.concatenate(v1, axis=0), jnp.concatenate(v2, axis=0))
        im = (jnp.concatenate(i1, axis=0), jnp.concatenate(i2, axis=0))
        sub = lax.broadcasted_iota(jnp.int32, (8, 1), 0)
        vals, eids, poss = [], [], []
        for axis, fixed, start, owned in _CAND_BLOCKS:
            if axis == 0:
                v = v1[fixed] + vm[1][start:start + 8]
                e = i1[fixed] * float(PEER_NKEYS) + im[1][start:start + 8]
                pos = fixed * k + start + sub
            else:
                v = vm[0][start:start + 8] + v2[fixed]
                e = im[0][start:start + 8] * float(PEER_NKEYS) + i2[fixed]
                pos = (start + sub) * k + fixed
            if not all(owned):
                keep = functools.reduce(jnp.logical_or, [sub == r for r in range(8) if owned[r]])
                v = jnp.where(keep, v, -jnp.inf)
                pos = jnp.where(keep, pos, k * k)
            vals.append(v)
            eids.append(e)
            poss.append(pos.astype(F32))
        cand = jnp.concatenate(vals, axis=0)
        cidx = jnp.concatenate(eids, axis=0)
        rid = jnp.concatenate(poss, axis=0)
        top_s, top_e = [], []
        for _ in range(k):
            m = jnp.max(cand, axis=0, keepdims=True)
            first = jnp.min(jnp.where(cand == m, rid, float(k * k)), axis=0, keepdims=True)
            hit = rid == first
            top_e.append(jnp.sum(jnp.where(hit, cidx, 0.0), axis=0, keepdims=True))
            cand = jnp.where(hit, -jnp.inf, cand)
            top_s.append(m)
        ts = jnp.concatenate(top_s, axis=0)
        e = jnp.exp(ts - ts[0:1])
        gate_ref[tile, h * k:(h + 1) * k, :] = e / jnp.sum(e, axis=0, keepdims=True)
        eid_ref[h * k:(h + 1) * k, :] = jnp.concatenate(top_e, axis=0)
    idx_ref[rows, :] = (eid_ref[...].T * float(ROW_SUB)).astype(jnp.int32)


ROW_SUB = D_MODEL // 2 // LANES
ACT_UNROLL = 32
OFF_REGS = 8


def _raw_row(tab_ref, scaled_idx):
    return tab_ref[pl.ds(pl.multiple_of(scaled_idx, ROW_SUB), ROW_SUB), :]


def _widen(row):
    lo = lax.bitcast_convert_type(lax.shift_left(row, jnp.uint32(16)), F32)
    hi = lax.bitcast_convert_type(row & jnp.uint32(0xFFFF0000), F32)
    return lo, hi


def _load_row(tab_ref, scaled_idx):
    return _widen(_raw_row(tab_ref, scaled_idx))


def _expert_act_body(off_ref, idx_ref, h_ref, gate_ref, tab_ref, w_ref, slot_ref, part_ref):
    off = [off_ref[j] for j in range(OFF_REGS)]

    def token(t, carry):
        x = h_ref[pl.ds(t, 1), :]
        chunk = lambda s: x[:, s * LANES:(s + 1) * LANES]
        xlo = jnp.concatenate([chunk(s) for s in range(ROW_SUB)], axis=0)
        xhi = jnp.concatenate([chunk(ROW_SUB + s) for s in range(ROW_SUB)], axis=0)
        base = pl.multiple_of(t * PEER_PAIRS, PEER_PAIRS)
        xlo2 = jnp.concatenate([xlo, xlo], axis=0)
        xhi2 = jnp.concatenate([xhi, xhi], axis=0)
        for p in range(0, PEER_PAIRS, 2):
            if p % OFF_REGS == 0:
                idx_sub = idx_ref.at[t, pl.ds(p, OFF_REGS)]
            rows = jnp.concatenate([_raw_row(tab_ref, idx_sub[off[(p + k) % OFF_REGS]]) for k in range(2)], axis=0)
            lo, hi = _widen(rows)
            slot_ref[p * ROW_SUB:(p + 2) * ROW_SUB, :] = lo * xlo2 + hi * xhi2
        part = slot_ref[pl.ds(0, PEER_PAIRS, stride=ROW_SUB), :]
        for s in range(1, ROW_SUB):
            part = part + slot_ref[pl.ds(s, PEER_PAIRS, stride=ROW_SUB), :]
        part_ref[pl.ds(pl.multiple_of(base, PEER_PAIRS), PEER_PAIRS), :] = part
        return carry

    lax.fori_loop(0, TC, token, 0)

    tok_lane = lax.broadcasted_iota(jnp.int32, (PEER_PAIRS, TC), 1)

    def lane_sums(i, act):
        for k in range(ACT_UNROLL):
            t = i * ACT_UNROLL + k
            part = part_ref[pl.ds(pl.multiple_of(t * PEER_PAIRS, PEER_PAIRS), PEER_PAIRS), :]
            act = jnp.where(tok_lane == t, jnp.sum(part, axis=1, keepdims=True), act)
        return act

    act = lax.fori_loop(0, TC // ACT_UNROLL, lane_sums, jnp.zeros((PEER_PAIRS, TC), F32))
    w_ref[...] = (gate_ref[0] * _gelu(act)).T


def _expert_act(idx_tok, h2, gate_t, tab):
    nt = gate_t.shape[0]
    tile = pl.BlockSpec((1, PEER_PAIRS, TC), lambda i: (i, 0, 0))
    return pl.pallas_call(
        _expert_act_body,
        grid=(nt,),
        in_specs=[pl.BlockSpec(memory_space=pltpu.SMEM),
                  pl.BlockSpec((TC, PEER_PAIRS), lambda i: (i, 0), memory_space=pltpu.SMEM),
                  pl.BlockSpec((TC, h2.shape[1]), lambda i: (i, 0)),
                  tile,
                  pl.BlockSpec(tab.shape, lambda i: (0, 0), pipeline_mode=pl.Buffered(1))],
        out_specs=pl.BlockSpec((TC, PEER_PAIRS), lambda i: (i, 0)),
        out_shape=jax.ShapeDtypeStruct((nt * TC, PEER_PAIRS), F32),
        scratch_shapes=[pltpu.VMEM((PEER_PAIRS * ROW_SUB, LANES), F32),
                        pltpu.VMEM((TC * PEER_PAIRS, LANES), F32)],
        compiler_params=_cparams(1, 52),
    )(jnp.arange(OFF_REGS, dtype=jnp.int32), idx_tok, h2, gate_t, tab)


def _expert_out_body(off_ref, idx_ref, w_ref, tab_ref, o_ref):
    n_acc = 4
    off = [off_ref[j] for j in range(OFF_REGS)]

    def token(t, carry):
        acc_lo = [jnp.zeros((ROW_SUB, LANES), F32) for _ in range(n_acc)]
        acc_hi = [jnp.zeros((ROW_SUB, LANES), F32) for _ in range(n_acc)]
        for p in range(PEER_PAIRS):
            if p % OFF_REGS == 0:
                idx_sub = idx_ref.at[t, pl.ds(p, OFF_REGS)]
                w_sub = w_ref.at[t, pl.ds(p, OFF_REGS)]
            lo, hi = _load_row(tab_ref, idx_sub[off[p % OFF_REGS]])
            w = w_sub[off[p % OFF_REGS]]
            acc_lo[p % n_acc] = acc_lo[p % n_acc] + w * lo
            acc_hi[p % n_acc] = acc_hi[p % n_acc] + w * hi
        lo = (acc_lo[0] + acc_lo[1]) + (acc_lo[2] + acc_lo[3])
        hi = (acc_hi[0] + acc_hi[1]) + (acc_hi[2] + acc_hi[3])
        o_ref[t] = jnp.concatenate([lo, hi], axis=0)
        return carry

    lax.fori_loop(0, TB, token, 0)


def _expert_out(idx_tok, w_tok, tab):
    n = idx_tok.shape[0]
    smem = pl.BlockSpec((TB, PEER_PAIRS), lambda i: (i, 0), memory_space=pltpu.SMEM)
    return pl.pallas_call(
        _expert_out_body,
        grid=(n // TB,),
        in_specs=[pl.BlockSpec(memory_space=pltpu.SMEM), smem, smem,
                  pl.BlockSpec(tab.shape, lambda i: (0, 0), pipeline_mode=pl.Buffered(1))],
        out_specs=pl.BlockSpec((TB, D_MODEL // LANES, LANES), lambda i: (i, 0, 0)),
        out_shape=jax.ShapeDtypeStruct((n, D_MODEL // LANES, LANES), F32),
        compiler_params=_cparams(1, 48),
    )(jnp.arange(OFF_REGS, dtype=jnp.int32), idx_tok, w_tok, tab)


def _final_body(x1_ref, f_ref, gt_ref, g_ref, b_ref, o_ref, *, alpha):
    o_ref[...] = _ln(alpha * x1_ref[...] + gt_ref[0] * f_ref[...]) * g_ref[...] + b_ref[...]


def _final(x1, ffn, gt2, g2, b2, seq, alpha):
    n, d = x1.shape
    tm = TM_LN
    nst = seq // tm
    row = pl.BlockSpec((tm, d), lambda i: (i, 0))
    const = pl.BlockSpec((1, d), lambda i: (0, 0))
    return pl.pallas_call(
        functools.partial(_final_body, alpha=alpha),
        grid=(n // tm,),
        in_specs=[row, row, pl.BlockSpec((1, 1, d), lambda i: (i // nst, 0, 0)), const, const],
        out_specs=row,
        out_shape=jax.ShapeDtypeStruct((n, d), F32),
        compiler_params=_cparams(1, 32),
    )(x1, ffn, gt2, g2, b2)


def _swap_halves(w, heads):
    d = w.shape[0]
    w4 = w.reshape(d, heads, 2, HEAD_DIM // 2)
    return jnp.flip(w4, axis=2).reshape(d, heads * HEAD_DIM)


def _fused_in_weight(w_in, w_merge):
    d = w_in.shape[0]
    o = 0
    parts = {}
    for name, width in (("q", NSA_WIDTH), ("kc", KV_WIDTH), ("vc", KV_WIDTH), ("ks", KV_WIDTH),
                        ("vs", KV_WIDTH), ("kw", KV_WIDTH), ("vw", KV_WIDTH), ("g", 3 * NSA_HEADS),
                        ("z", 2 * SGU_WIDTH)):
        parts[name] = w_in[:, o:o + width]
        o += width
    wq = parts["q"] * (HEAD_DIM ** -0.5)
    gcols = 3 * NSA_REP
    wg = jnp.zeros((d, 2 * LANES), w_in.dtype)
    wg = wg.at[:, :gcols].set(parts["g"][:, :gcols]).at[:, LANES:LANES + gcols].set(parts["g"][:, gcols:])
    g = NSA_KV_GROUPS
    cols = {"q": wq, "q_swap": _swap_halves(wq, NSA_HEADS),
            "kc": parts["kc"], "kc_swap": _swap_halves(parts["kc"], g),
            "ks": parts["ks"], "ks_swap": _swap_halves(parts["ks"], g),
            "kw": parts["kw"], "kw_swap": _swap_halves(parts["kw"], g),
            "vc": parts["vc"], "vs": parts["vs"], "vw": parts["vw"],
            "gates": wg, "z": parts["z"], "merge": w_merge}
    assert all(cols[name].shape[1] == width for name, width in _SEGMENTS)
    return jnp.concatenate([cols[name] for name, _ in _SEGMENTS], axis=1).astype(BF16)


def _rope_tables(seq):
    half = HEAD_DIM // 2
    pos = jnp.arange(seq, dtype=F32)
    inv_freq = ROPE_THETA ** (-jnp.arange(half, dtype=F32) / half)
    ang = pos[:, None] * inv_freq[None, :]
    cos, sin = jnp.cos(ang), jnp.sin(ang)
    reps = LANES // HEAD_DIM
    cosk = jnp.tile(jnp.concatenate([cos, cos], axis=1), (1, reps))
    sink = jnp.tile(jnp.concatenate([-sin, sin], axis=1), (1, reps))
    return cosk, sink


def _sel_aggregation(n_cmp_pad, n_sel):
    c0 = jnp.arange(n_cmp_pad)[:, None] * CMP_STRIDE
    s0 = jnp.arange(n_sel)[None, :] * SEL_BLOCK
    ov = jnp.clip(jnp.minimum(c0 + CMP_BLOCK, s0 + SEL_BLOCK) - jnp.maximum(c0, s0), 0, None)
    return (ov / CMP_BLOCK).astype(BF16)


def _pack_table(tab):
    half = tab.shape[1] // 2
    bits = lax.bitcast_convert_type(tab.astype(BF16), jnp.uint16).astype(jnp.uint32)
    packed = (bits[:, half:] << 16) | bits[:, :half]
    return packed.reshape(tab.shape[0] * ROW_SUB, LANES)


def _split_groups(a, bsz, seq):
    return a.reshape(bsz, seq, NSA_KV_GROUPS, HEAD_DIM).transpose(0, 2, 1, 3)


def kernel(x, c, w_ada, b_ada, w_in, cmp_pos, cmp_w1, cmp_b1, cmp_w2, cmp_b2, sgu_ln_g, sgu_ln_b, sgu_w, sgu_b, w_branch, w_merge, b_merge, w_out, ln1_g, ln1_b, peer_wq, peer_keys, peer_u, peer_v, ln2_g, ln2_b):
    bsz, seq, d = x.shape
    n = bsz * seq
    depth = w_ada.shape[0]
    alpha = (2.0 * depth) ** 0.25
    n_cmp = (seq - CMP_BLOCK) // CMP_STRIDE + 1
    n_half = seq // CMP_STRIDE
    n_sel = seq // SEL_BLOCK
    cosk, sink = _rope_tables(seq)
    agg = _sel_aggregation(n_half, n_sel)
    x2 = x.reshape(n, d)
    for l in range(depth):
        mod = _ada(c, w_ada[l], b_ada[l])
        sh1, sc1, gt1, sh2, sc2, gt2 = [m.reshape(bsz, 1, d) for m in jnp.split(mod, 6, axis=-1)]

        wall = _fused_in_weight(w_in[l], w_merge[l])
        (q_t, ks_ext, vs_t, kw_g, vw_t, gn_t, kc_r, vc_r, u, v, gate) = _proj(
            x2, sc1, sh1, wall, b_merge[l].reshape(1, -1), cosk, sink,
            sgu_ln_g[l].reshape(1, -1), sgu_ln_b[l].reshape(1, -1), seq)

        t16 = jnp.stack([_split_groups(kc_r, bsz, seq), _split_groups(vc_r, bsz, seq)])
        t16 = t16.reshape(2, bsz, NSA_KV_GROUPS, n_half, CMP_STRIDE * HEAD_DIM)
        cmp_kv = _compress(t16, cmp_pos[l].reshape(2, 1, CMP_BLOCK * HEAD_DIM), cmp_w1[l],
                           cmp_b1[l].reshape(2, 1, CMP_HIDDEN), cmp_w2[l], cmp_b2[l].reshape(2, 1, HEAD_DIM))

        o_nsa_t = _nsa(q_t, cmp_kv[0], cmp_kv[1].transpose(0, 1, 3, 2), ks_ext, vs_t, kw_g, vw_t, gn_t, agg.T, n_cmp)

        bs = jnp.repeat(sgu_b[l].T, SGU_GROUP_DIM, axis=1)
        x1, h2, idx_tok, gate_t = _merge(o_nsa_t, u, v, gate, x2, gt1, sc2, sh2,
                            sgu_w[l], bs, w_branch[l].astype(BF16), w_out[l].astype(BF16),
                            ln1_g[l].reshape(1, d), ln1_b[l].reshape(1, d), peer_wq[l].astype(BF16), peer_keys[l], seq, alpha)

        w_tok = _expert_act(idx_tok, h2, gate_t, _pack_table(peer_u[l]))
        ffn = _expert_out(idx_tok, w_tok, _pack_table(peer_v[l]))
        x2 = _final(x1, ffn.reshape(n, d), gt2, ln2_g[l].reshape(1, d), ln2_b[l].reshape(1, d), seq, alpha)
    return x2.reshape(bsz, seq, d)
```

```python
import functools

import jax
import jax.numpy as jnp
from jax import lax
from jax.experimental import pallas as pl
from jax.experimental.pallas import tpu as pltpu

D_MODEL = 1024
NSA_HEADS = 8
NSA_KV_GROUPS = 2
NSA_REP = NSA_HEADS // NSA_KV_GROUPS
HEAD_DIM = 64
NSA_WIDTH = NSA_HEADS * HEAD_DIM
KV_WIDTH = NSA_KV_GROUPS * HEAD_DIM
CMP_BLOCK = 32
CMP_STRIDE = 16
CMP_HIDDEN = 2 * HEAD_DIM
SEL_BLOCK = 64
SEL_TOPK = 16
WINDOW = 512
ROPE_THETA = 10000.0
SGU_GROUPS = 8
SGU_WIDTH = D_MODEL // 2
SGU_GROUP_DIM = SGU_WIDTH // SGU_GROUPS
SGU_CHUNK = 128
PEER_HEADS = 8
PEER_NKEYS = 128
PEER_EXPERTS = PEER_NKEYS * PEER_NKEYS
PEER_QDIM = 256
PEER_HALF = PEER_QDIM // 2
PEER_TOPK = 16
PEER_PAIRS = PEER_HEADS * PEER_TOPK
LN_EPS = 1e-5
NEG_INF = -1e30
FORCE_SCORE = 1e9

LANES = 128
MIB = 1024 * 1024
BF16 = jnp.bfloat16
F32 = jnp.float32

TM_PROJ = 256
TM_MERGE = 256
TM_LN = 512
TQ = 128
TK = 256
KV_GROUP = 4
GROUP_BLOCKS = TK * KV_GROUP // SEL_BLOCK
GATE_ROWS = 16
TC = 128
TB = 128

_SEGMENTS = (("q", NSA_WIDTH), ("q_swap", NSA_WIDTH), ("kc", KV_WIDTH), ("kc_swap", KV_WIDTH),
             ("ks", KV_WIDTH), ("ks_swap", KV_WIDTH), ("kw", KV_WIDTH), ("kw_swap", KV_WIDTH),
             ("vc", KV_WIDTH), ("vs", KV_WIDTH), ("vw", KV_WIDTH), ("gates", 2 * LANES),
             ("z", 2 * SGU_WIDTH), ("merge", 2 * D_MODEL))


def _segment_offsets():
    offs, o = {}, 0
    for name, width in _SEGMENTS:
        offs[name] = o
        o += width
    return offs, o


_OFF, _WCOLS = _segment_offsets()
_OQ, _OQS, _OKC, _OKCS = _OFF["q"], _OFF["q_swap"], _OFF["kc"], _OFF["kc_swap"]
_OKS, _OKSS, _OKW, _OKWS = _OFF["ks"], _OFF["ks_swap"], _OFF["kw"], _OFF["kw_swap"]
_OVC, _OVS, _OVW, _OG, _OZ, _OM = _OFF["vc"], _OFF["vs"], _OFF["vw"], _OFF["gates"], _OFF["z"], _OFF["merge"]


VMEM_MIB_V7X = 64


def _cparams(n_axes, vmem_mib):
    assert vmem_mib < VMEM_MIB_V7X
    return pltpu.CompilerParams(
        dimension_semantics=("parallel",) * n_axes,
        vmem_limit_bytes=vmem_mib * MIB)


def _ln(x):
    mu = jnp.mean(x, axis=-1, keepdims=True)
    xc = x - mu
    var = jnp.mean(xc * xc, axis=-1, keepdims=True)
    return xc * lax.rsqrt(var + LN_EPS)


def _gelu(x):
    return 0.5 * x * (1.0 + lax.erf(x * (2.0 ** -0.5)))


def _dot(a, b):
    return jnp.dot(a, b, preferred_element_type=F32)


def _dot_nt(a, b):
    return lax.dot_general(a, b, (((1,), (1,)), ((), ())), preferred_element_type=F32)


def _ada_body(c_ref, w_ref, b_ref, o_ref):
    cv = c_ref[...]
    a = cv * jax.nn.sigmoid(cv)
    o_ref[...] = _dot(a.astype(BF16), w_ref[...].astype(BF16)) + b_ref[...]


def _ada(c, w, b):
    bsz, d = c.shape
    n = w.shape[1]
    tn = 1024
    return pl.pallas_call(
        _ada_body,
        grid=(n // tn,),
        in_specs=[pl.BlockSpec((bsz, d), lambda j: (0, 0)),
                  pl.BlockSpec((d, tn), lambda j: (0, j)),
                  pl.BlockSpec((1, tn), lambda j: (0, j))],
        out_specs=pl.BlockSpec((bsz, tn), lambda j: (0, j)),
        out_shape=jax.ShapeDtypeStruct((bsz, n), F32),
        compiler_params=_cparams(1, 32),
    )(c, w, b.reshape(1, n))


def _proj_body(x_ref, sc_ref, sh_ref, w_ref, bm_ref, cos_ref, sin_ref, sg_ref, sb_ref,
               q_ref, ks_ref, vs_ref, kw_ref, vw_ref, gn_ref, kc_ref, vc_ref, u_ref, v_ref, gate_ref,
               *, n_seq_tiles):
    h = _ln(x_ref[...]) * (1.0 + sc_ref[0]) + sh_ref[0]
    hb = h.astype(BF16)

    def proj(off, width):
        return _dot(hb, w_ref[:, off:off + width])

    cosk = cos_ref[...]
    sink = sin_ref[...]
    cosq = jnp.concatenate([cosk] * (NSA_WIDTH // LANES), axis=1)
    sinq = jnp.concatenate([sink] * (NSA_WIDTH // LANES), axis=1)
    q_ref[0] = (proj(_OQ, NSA_WIDTH) * cosq + proj(_OQS, NSA_WIDTH) * sinq).T
    kc_ref[...] = proj(_OKC, KV_WIDTH) * cosk + proj(_OKCS, KV_WIDTH) * sink
    vc_ref[...] = proj(_OVC, KV_WIDTH)
    tm = x_ref.shape[0]
    ks = proj(_OKS, KV_WIDTH) * cosk + proj(_OKSS, KV_WIDTH) * sink
    kw = proj(_OKW, KV_WIDTH) * cosk + proj(_OKWS, KV_WIDTH) * sink
    vs_t = proj(_OVS, KV_WIDTH).T
    vw_t = proj(_OVW, KV_WIDTH).T
    gn_t = jax.nn.sigmoid(proj(_OG, 2 * LANES)).T
    pos = (pl.program_id(0) % n_seq_tiles) * tm + lax.broadcasted_iota(jnp.int32, (tm, 1), 0)
    block_onehot = jnp.where((pos // SEL_BLOCK) % GROUP_BLOCKS
                             == lax.broadcasted_iota(jnp.int32, (1, GROUP_BLOCKS), 1), 1.0, 0.0)
    for g in range(NSA_KV_GROUPS):
        cols = slice(g * HEAD_DIM, (g + 1) * HEAD_DIM)
        ks_ref[0, g] = jnp.concatenate([ks[:, cols], block_onehot], axis=1).astype(BF16)
        kw_ref[0, g] = kw[:, cols].astype(BF16)
        vs_ref[0, g, 0] = vs_t[cols, :].astype(BF16)
        for c in range(tm // TQ):
            vw_ref[0, g, c] = vw_t[cols, c * TQ:(c + 1) * TQ].astype(BF16)
        gn_ref[0, g] = gn_t[g * LANES:g * LANES + GATE_ROWS, :]
    u_ref[...] = _gelu(proj(_OZ, SGU_WIDTH))
    v_ref[...] = _ln(_gelu(proj(_OZ + SGU_WIDTH, SGU_WIDTH))) * sg_ref[...] + sb_ref[...]
    gate_ref[...] = jax.nn.sigmoid(proj(_OM, 2 * D_MODEL) + bm_ref[...])


def _proj(x2, sc, sh, wall, bm, cosk, sink, sg, sb, seq):
    n, d = x2.shape
    tm = TM_PROJ
    nst = seq // tm
    row = lambda w: pl.BlockSpec((tm, w), lambda i: (i, 0))
    per_batch = pl.BlockSpec((1, 1, d), lambda i: (i // nst, 0, 0))
    const = lambda shp: pl.BlockSpec(shp, lambda i: (0,) * len(shp))
    pos = pl.BlockSpec((tm, LANES), lambda i: (i % nst, 0))
    outs = [(KV_WIDTH, F32), (KV_WIDTH, F32), (SGU_WIDTH, F32), (SGU_WIDTH, F32), (2 * D_MODEL, F32)]
    assert tm == TK and tm % TQ == 0
    bsz, g = n // seq, NSA_KV_GROUPS
    att = [((bsz, NSA_WIDTH, seq), (1, NSA_WIDTH, tm), lambda i: (i // nst, 0, i % nst), F32),
           ((bsz, g, seq, HEAD_DIM + GROUP_BLOCKS), (1, g, tm, HEAD_DIM + GROUP_BLOCKS),
            lambda i: (i // nst, 0, i % nst, 0), BF16),
           ((bsz, g, seq // TK, HEAD_DIM, TK), (1, g, 1, HEAD_DIM, TK), lambda i: (i // nst, 0, i % nst, 0, 0), BF16),
           ((bsz, g, seq, HEAD_DIM), (1, g, tm, HEAD_DIM), lambda i: (i // nst, 0, i % nst, 0), BF16),
           ((bsz, g, seq // TQ, HEAD_DIM, TQ), (1, g, tm // TQ, HEAD_DIM, TQ),
            lambda i: (i // nst, 0, i % nst, 0, 0), BF16),
           ((bsz, g, GATE_ROWS, seq), (1, g, GATE_ROWS, tm), lambda i: (i // nst, 0, 0, i % nst), F32)]
    return pl.pallas_call(
        functools.partial(_proj_body, n_seq_tiles=nst),
        grid=(n // tm,),
        in_specs=[row(d), per_batch, per_batch, const((d, _WCOLS)), const((1, 2 * D_MODEL)),
                  pos, pos, const((1, SGU_WIDTH)), const((1, SGU_WIDTH))],
        out_specs=[pl.BlockSpec(blk, imap) for _, blk, imap, _ in att] + [row(w) for w, _ in outs],
        out_shape=([jax.ShapeDtypeStruct(shp, dt) for shp, _, _, dt in att]
                   + [jax.ShapeDtypeStruct((n, w), dt) for w, dt in outs]),
        compiler_params=_cparams(1, 56),
    )(x2, sc, sh, wall, bm, cosk, sink, sg, sb)


def _cmp_body(t_ref, pos_ref, w1_ref, b1_ref, w2_ref, b2_ref, o_ref):
    half = (CMP_BLOCK // 2) * HEAD_DIM
    t = t_ref[0, 0, 0]
    pos = pos_ref[0]
    ta = (t + pos[:, :half]).astype(BF16)
    tb = (t + pos[:, half:]).astype(BF16)
    a = _dot(ta, w1_ref[0, :half, :].astype(BF16))
    b = _dot(tb, w1_ref[0, half:, :].astype(BF16))
    nrow = t.shape[0]
    b_next = pltpu.roll(b, nrow - 1, axis=0)
    hid = _gelu(a + b_next + b1_ref[0])
    out = _dot(hid.astype(BF16), w2_ref[0].astype(BF16)) + b2_ref[0]
    o_ref[0, 0, 0] = out.astype(o_ref.dtype)


def _compress(t16, pos, w1, b1, w2, b2):
    _, bsz, g, nr, dd = t16.shape
    per_kind = lambda shp: pl.BlockSpec((1,) + shp, lambda k, b, gg: (k,) + (0,) * len(shp))
    blk = lambda w: pl.BlockSpec((1, 1, 1, nr, w), lambda k, b, gg: (k, b, gg, 0, 0))
    return pl.pallas_call(
        _cmp_body,
        grid=(2, bsz, g),
        in_specs=[blk(dd), per_kind((1, CMP_BLOCK * HEAD_DIM)), per_kind((CMP_BLOCK * HEAD_DIM, CMP_HIDDEN)),
                  per_kind((1, CMP_HIDDEN)), per_kind((CMP_HIDDEN, HEAD_DIM)), per_kind((1, HEAD_DIM))],
        out_specs=blk(HEAD_DIM),
        out_shape=jax.ShapeDtypeStruct((2, bsz, g, nr, HEAD_DIM), BF16),
        compiler_params=_cparams(3, 48),
    )(t16, pos, w1, b1, w2, b2)


def _scores(q_all, kt, bias):
    s = _dot(kt, q_all)
    return s if bias is None else s + bias


def _max_init():
    return jnp.full((1, NSA_REP * TQ), NEG_INF, F32)


def _acc_init():
    w = NSA_REP * TQ
    return jnp.zeros((1, w), F32), jnp.zeros((HEAD_DIM, w), F32)


def _lane_tile(a):
    return jnp.concatenate([a] * NSA_REP, axis=1)


def _nsa_body(q_ref, kc_ref, vc_ref, ks_ref, vs_ref, kw_ref, vw_ref, gn_ref, agg_ref, o_ref, selneg_ref, score_ref, *, n_cmp):
    qi = pl.program_id(2)
    q0 = qi * TQ
    t1 = q0 + lax.broadcasted_iota(jnp.int32, (1, TQ), 1)
    t = _lane_tile(t1)
    q_all = jnp.concatenate([q_ref[0, r * HEAD_DIM:(r + 1) * HEAD_DIM, :] for r in range(NSA_REP)],
                            axis=1).astype(BF16)

    kcm = kc_ref[0, 0]
    ncp = kcm.shape[0]
    nidx = lax.broadcasted_iota(jnp.int32, (ncp, 1), 0)
    mask_c = (nidx * CMP_STRIDE + (CMP_BLOCK - 1) <= t) & (nidx < n_cmp)
    sm = jnp.where(mask_c, _dot(kcm, q_all), NEG_INF)
    e = jnp.where(mask_c, jnp.exp(sm - jnp.max(sm, axis=0, keepdims=True)), 0.0)
    l = jnp.sum(e, axis=0, keepdims=True)
    p = e * (1.0 / jnp.where(l > 0.0, l, 1.0))
    o_c = _dot(vc_ref[0, 0], p.astype(BF16))
    psum = p[:, :TQ]
    for r in range(1, NSA_REP):
        psum = psum + p[:, r * TQ:(r + 1) * TQ]

    p_hi = psum.astype(BF16)
    p_lo = (psum - p_hi.astype(F32)).astype(BF16)
    agg_t = agg_ref[...]
    imp = _dot(agg_t, p_hi) + _dot(agg_t, p_lo)
    n_sel = imp.shape[0]
    blk = lax.broadcasted_iota(jnp.int32, (n_sel, 1), 0)
    blk_f = blk.astype(F32)
    cur = t1 // SEL_BLOCK
    forced = (blk == 0) | (blk == cur) | (blk == cur - 1)
    score = jnp.where(forced, FORCE_SCORE, jnp.where(blk <= cur, imp, NEG_INF))
    sel = jnp.zeros((n_sel, TQ), jnp.bool_)
    for _ in range(min(SEL_TOPK, n_sel)):
        m = jnp.max(score, axis=0, keepdims=True)
        first = jnp.min(jnp.where(score == m, blk_f, float(n_sel)), axis=0, keepdims=True)
        hit = blk_f == first
        sel = sel | hit
        score = jnp.where(hit, -jnp.inf, score)
    selneg_ref[...] = _lane_tile(jnp.where(sel, 0.0, NEG_INF).astype(BF16))

    def q_ext(jg):
        rows = selneg_ref[pl.ds(pl.multiple_of(jg * GROUP_BLOCKS, GROUP_BLOCKS), GROUP_BLOCKS), :]
        return jnp.concatenate([q_all, rows], axis=0)

    krow_w = lax.broadcasted_iota(jnp.int32, (TQ, 1), 0)

    def win_chunk(c):
        kpos = q0 - WINDOW + c * TQ + krow_w
        dlt = t - kpos
        bias = jnp.where((dlt >= 0) & (dlt < WINDOW) & (kpos >= 0), 0.0, NEG_INF)
        start = jnp.maximum(q0 - WINDOW + c * TQ, 0)
        return kw_ref[0, 0, pl.ds(pl.multiple_of(start, TQ), TQ), :], bias

    n_win = WINDOW // TQ + 1
    s_w = [_scores(q_all, *win_chunk(c)) for c in range(n_win)]
    m_w = functools.reduce(jnp.maximum, [jnp.max(s, axis=0, keepdims=True) for s in s_w])
    l_w, acc_w = _acc_init()
    for c in range(n_win):
        p = jnp.exp(s_w[c] - m_w)
        l_w = l_w + jnp.sum(p, axis=0, keepdims=True)
        acc_w = acc_w + _dot(vw_ref[0, 0, jnp.maximum(qi + c - WINDOW // TQ, 0)], p.astype(BF16))
    o_w = acc_w / l_w

    krow = lax.broadcasted_iota(jnp.int32, (TK, 1), 0)
    group_rows = TK * KV_GROUP

    def max_group(jg, m):
        k0 = pl.multiple_of(jg * group_rows, group_rows)
        s = _scores(q_ext(jg), ks_ref[0, 0, pl.ds(k0, group_rows), :], None)
        score_ref[pl.ds(k0, group_rows), :] = s
        return jnp.maximum(m, jnp.max(s, axis=0, keepdims=True))

    def max_tile(j, m):
        k0 = pl.multiple_of(j * TK, TK)
        bias = jnp.where(k0 + krow <= t, 0.0, NEG_INF)
        s = _scores(q_ext(j // KV_GROUP), ks_ref[0, 0, pl.ds(k0, TK), :], bias)
        score_ref[pl.ds(k0, TK), :] = s
        return jnp.maximum(m, jnp.max(s, axis=0, keepdims=True))

    def sum_rows(state, k0, rows, vt_t):
        l, acc = state
        p = jnp.exp(score_ref[pl.ds(k0, rows), :] - m_s)
        return l + jnp.sum(p, axis=0, keepdims=True), acc + _dot(vt_t, p.astype(BF16))

    def sum_group(jg, state):
        vt_t = jnp.concatenate([vs_ref[0, 0, jg * KV_GROUP + sub] for sub in range(KV_GROUP)], axis=1)
        return sum_rows(state, pl.multiple_of(jg * group_rows, group_rows), group_rows, vt_t)

    def sum_tile(j, state):
        return sum_rows(state, pl.multiple_of(j * TK, TK), TK, vs_ref[0, 0, j])

    n_full = q0 // group_rows
    n_kv = (q0 + TQ - 1) // TK + 1
    m_s = lax.fori_loop(0, n_full, max_group, _max_init())
    m_s = lax.fori_loop(n_full * KV_GROUP, n_kv, max_tile, m_s)
    state = lax.fori_loop(0, n_full, sum_group, _acc_init())
    state = lax.fori_loop(n_full * KV_GROUP, n_kv, sum_tile, state)
    o_s = state[1] / state[0]

    gn = gn_ref[0, 0]
    gate = [jnp.concatenate([gn[3 * r + c:3 * r + c + 1, :] for r in range(NSA_REP)], axis=1) for c in range(3)]
    o = gate[0] * o_c + gate[1] * o_s + gate[2] * o_w
    for r in range(NSA_REP):
        o_ref[0, r * HEAD_DIM:(r + 1) * HEAD_DIM, :] = o[:, r * TQ:(r + 1) * TQ]


def _nsa(q_t, kc, vc_t, ks, vs_t, kwp, vwp_t, gn_t, agg_t, n_cmp):
    bsz, _, seq = q_t.shape
    g = NSA_KV_GROUPS
    gw = NSA_REP * HEAD_DIM
    per_bg = lambda a: pl.BlockSpec((1, 1) + a.shape[2:], lambda b, gg, i: (b, gg) + (0,) * (a.ndim - 2))
    return pl.pallas_call(
        functools.partial(_nsa_body, n_cmp=n_cmp),
        grid=(bsz, g, seq // TQ),
        in_specs=[pl.BlockSpec((1, gw, TQ), lambda b, gg, i: (b, gg, i)),
                  per_bg(kc), per_bg(vc_t), per_bg(ks), per_bg(vs_t), per_bg(kwp), per_bg(vwp_t),
                  pl.BlockSpec((1, 1, gn_t.shape[2], TQ), lambda b, gg, i: (b, gg, 0, i)),
                  pl.BlockSpec(agg_t.shape, lambda b, gg, i: (0, 0))],
        out_specs=pl.BlockSpec((1, gw, TQ), lambda b, gg, i: (b, gg, i)),
        out_shape=jax.ShapeDtypeStruct((bsz, NSA_WIDTH, seq), F32),
        scratch_shapes=[pltpu.VMEM((agg_t.shape[0], NSA_REP * TQ), BF16),
                        pltpu.VMEM((seq, NSA_REP * TQ), F32)],
        compiler_params=_cparams(3, 56),
    )(q_t, kc, vc_t, ks, vs_t, kwp, vwp_t, gn_t, agg_t)


def _merge_body(on_ref, u_ref, v_ref, gate_ref, x_ref, gt_ref, sc_ref, sh_ref, ws_ref, bs_ref,
                wb_ref, wo_ref, g1_ref, b1_ref, wq_ref, keys_ref, x1_ref, h2_ref, idx_ref, pgate_ref,
                qp_ref, eid_ref, *, alpha):
    tm = x_ref.shape[0]
    row = lax.broadcasted_iota(jnp.int32, (SGU_CHUNK, SGU_CHUNK), 0)
    col = lax.broadcasted_iota(jnp.int32, (SGU_CHUNK, SGU_CHUNK), 1)
    lane_group = lax.broadcasted_iota(jnp.int32, (1, SGU_WIDTH), 1) // SGU_GROUP_DIM
    mixed = []
    for c in range(tm // SGU_CHUNK):
        v = v_ref[c * SGU_CHUNK:(c + 1) * SGU_CHUNK, :]
        acc = jnp.zeros((SGU_CHUNK, SGU_WIDTH), F32)
        for g in range(SGU_GROUPS):
            wg = jnp.where(col <= row, ws_ref[g], 0.0).astype(BF16)
            vg = jnp.where(lane_group == g, v, 0.0).astype(BF16)
            acc = acc + _dot(wg, vg)
        mixed.append(acc + bs_ref[...])
    o_sgu = u_ref[...] * jnp.concatenate(mixed, axis=0)
    gate = gate_ref[...]
    merged = (gate[:, :D_MODEL] * _dot(on_ref[0].T.astype(BF16), wb_ref[0])
              + gate[:, D_MODEL:] * _dot(o_sgu.astype(BF16), wb_ref[1]))
    mix = _dot(merged.astype(BF16), wo_ref[...])
    x1 = _ln(alpha * x_ref[...] + gt_ref[0] * mix) * g1_ref[...] + b1_ref[...]
    x1_ref[...] = x1
    h2 = _ln(x1) * (1.0 + sc_ref[0]) + sh_ref[0]
    h2_ref[...] = h2
    qp_ref[...] = _dot(h2.astype(BF16), wq_ref[...])
    for sub in range(tm // TC):
        _retrieve_tile(qp_ref, keys_ref, idx_ref, pgate_ref, eid_ref, sub)


def _merge(o_nsa, u, v, gate, x2, gt1, sc2, sh2, ws, bs, wb, wo, g1, b1, wq, keys, seq, alpha):
    n, d = x2.shape
    tm = TM_MERGE
    nst = seq // tm
    row = lambda w: pl.BlockSpec((tm, w), lambda i: (i, 0))
    per_batch = pl.BlockSpec((1, 1, d), lambda i: (i // nst, 0, 0))
    const = lambda a: pl.BlockSpec(a.shape, lambda i: (0,) * a.ndim)
    nq = wq.shape[1]
    return pl.pallas_call(
        functools.partial(_merge_body, alpha=alpha),
        grid=(n // tm,),
        in_specs=[pl.BlockSpec((1, NSA_WIDTH, tm), lambda i: (i // nst, 0, i % nst)),
                  row(SGU_WIDTH), row(SGU_WIDTH), row(2 * d), row(d),
                  per_batch, per_batch, per_batch, const(ws), const(bs), const(wb), const(wo),
                  const(g1), const(b1), const(wq), const(keys)],
        out_specs=[row(d), row(d), row(PEER_PAIRS),
                   pl.BlockSpec((tm // TC, PEER_PAIRS, TC), lambda i: (i, 0, 0))],
        out_shape=[jax.ShapeDtypeStruct((n, d), F32), jax.ShapeDtypeStruct((n, d), F32),
                   jax.ShapeDtypeStruct((n, PEER_PAIRS), jnp.int32),
                   jax.ShapeDtypeStruct((n // TC, PEER_PAIRS, TC), F32)],
        scratch_shapes=[pltpu.VMEM((tm, nq), F32), pltpu.VMEM((PEER_PAIRS, TC), F32)],
        compiler_params=_cparams(1, 56),
    )(o_nsa, u, v, gate, x2, gt1, sc2, sh2, ws, bs, wb, wo, g1, b1, wq, keys)


def _topk_rows(s, k):
    nrow = s.shape[0]
    rid = lax.broadcasted_iota(jnp.int32, s.shape, 0).astype(F32)
    vals, rows = [], []
    for _ in range(k):
        m = jnp.max(s, axis=0, keepdims=True)
        first = jnp.min(jnp.where(s == m, rid, float(nrow)), axis=0, keepdims=True)
        s = jnp.where(rid == first, -jnp.inf, s)
        vals.append(m)
        rows.append(first)
    return vals, rows


def _candidate_blocks(k):
    todo = {(a, b) for a in range(k) for b in range(k) if (a + 1) * (b + 1) <= k}
    options = [(axis, fixed, start) for axis in (0, 1) for fixed in range(k) for start in range(0, k, 8)]
    cells = lambda axis, fixed, start: [((fixed, start + r) if axis == 0 else (start + r, fixed)) for r in range(8)]
    blocks = []
    while todo:
        best = max(options, key=lambda o: sum(c in todo for c in cells(*o)))
        owned = tuple(c in todo for c in cells(*best))
        todo -= set(cells(*best))
        blocks.append(best + (owned,))
    return tuple(blocks)


_CAND_BLOCKS = _candidate_blocks(PEER_TOPK)


def _retrieve_tile(qp_ref, keys_ref, idx_ref, gate_ref, eid_ref, tile):
    k = PEER_TOPK
    rows = slice(tile * TC, (tile + 1) * TC)
    for h in range(PEER_HEADS):
        tops = []
        for c in range(2):
            off = (h * 2 + c) * PEER_HALF
            qh = qp_ref[rows, off:off + PEER_HALF].astype(BF16)
            s = _dot_nt(keys_ref[h, c].astype(BF16), qh)
            tops.append(_topk_rows(s, k))
        (v1, i1), (v2, i2) = tops
        vm = (jnp.concatenate(v1, axis=0), jnp.concatenate(v2, axis=0))
        im = (jnp.concatenate(i1, axis=0), jnp.concatenate(i2, axis=0))
        sub = lax.broadcasted_iota(jnp.int32, (8, 1), 0)
        vals, eids, poss = [], [], []
        for axis, fixed, start, owned in _CAND_BLOCKS:
            if axis == 0:
                v = v1[fixed] + vm[1][start:start + 8]
                e = i1[fixed] * float(PEER_NKEYS) + im[1][start:start + 8]
                pos = fixed * k + start + sub
            else:
                v = vm[0][start:start + 8] + v2[fixed]
                e = im[0][start:start + 8] * float(PEER_NKEYS) + i2[fixed]
                pos = (start + sub) * k + fixed
            if not all(owned):
                keep = functools.reduce(jnp.logical_or, [sub == r for r in range(8) if owned[r]])
                v = jnp.where(keep, v, -jnp.inf)
                pos = jnp.where(keep, pos, k * k)
            vals.append(v)
            eids.append(e)
            poss.append(pos.astype(F32))
        cand = jnp.concatenate(vals, axis=0)
        cidx = jnp.concatenate(eids, axis=0)
        rid = jnp.concatenate(poss, axis=0)
        top_s, top_e = [], []
        for _ in range(k):
            m = jnp.max(cand, axis=0, keepdims=True)
            first = jnp.min(jnp.where(cand == m, rid, float(k * k)), axis=0, keepdims=True)
            hit = rid == first
            top_e.append(jnp.sum(jnp.where(hit, cidx, 0.0), axis=0, keepdims=True))
            cand = jnp.where(hit, -jnp.inf, cand)
            top_s.append(m)
        ts = jnp.concatenate(top_s, axis=0)
        e = jnp.exp(ts - ts[0:1])
        gate_ref[tile, h * k:(h + 1) * k, :] = e / jnp.sum(e, axis=0, keepdims=True)
        eid_ref[h * k:(h + 1) * k, :] = jnp.concatenate(top_e, axis=0)
    idx_ref[rows, :] = (eid_ref[...].T * float(ROW_SUB)).astype(jnp.int32)


ROW_SUB = D_MODEL // 2 // LANES
ACT_UNROLL = 32
OFF_REGS = 8


def _raw_row(tab_ref, scaled_idx):
    return tab_ref[pl.ds(pl.multiple_of(scaled_idx, ROW_SUB), ROW_SUB), :]


def _widen(row):
    lo = lax.bitcast_convert_type(lax.shift_left(row, jnp.uint32(16)), F32)
    hi = lax.bitcast_convert_type(row & jnp.uint32(0xFFFF0000), F32)
    return lo, hi


def _load_row(tab_ref, scaled_idx):
    return _widen(_raw_row(tab_ref, scaled_idx))


def _expert_act_body(off_ref, idx_ref, h_ref, gate_ref, tab_ref, w_ref, slot_ref, part_ref):
    off = [off_ref[j] for j in range(OFF_REGS)]

    def token(t, carry):
        x = h_ref[pl.ds(t, 1), :]
        chunk = lambda s: x[:, s * LANES:(s + 1) * LANES]
        xlo = jnp.concatenate([chunk(s) for s in range(ROW_SUB)], axis=0)
        xhi = jnp.concatenate([chunk(ROW_SUB + s) for s in range(ROW_SUB)], axis=0)
        base = pl.multiple_of(t * PEER_PAIRS, PEER_PAIRS)
        xlo2 = jnp.concatenate([xlo, xlo], axis=0)
        xhi2 = jnp.concatenate([xhi, xhi], axis=0)
        for p in range(0, PEER_PAIRS, 2):
            if p % OFF_REGS == 0:
                idx_sub = idx_ref.at[t, pl.ds(p, OFF_REGS)]
            rows = jnp.concatenate([_raw_row(tab_ref, idx_sub[off[(p + k) % OFF_REGS]]) for k in range(2)], axis=0)
            lo, hi = _widen(rows)
            slot_ref[p * ROW_SUB:(p + 2) * ROW_SUB, :] = lo * xlo2 + hi * xhi2
        part = slot_ref[pl.ds(0, PEER_PAIRS, stride=ROW_SUB), :]
        for s in range(1, ROW_SUB):
            part = part + slot_ref[pl.ds(s, PEER_PAIRS, stride=ROW_SUB), :]
        part_ref[pl.ds(pl.multiple_of(base, PEER_PAIRS), PEER_PAIRS), :] = part
        return carry

    lax.fori_loop(0, TC, token, 0)

    tok_lane = lax.broadcasted_iota(jnp.int32, (PEER_PAIRS, TC), 1)

    def lane_sums(i, act):
        for k in range(ACT_UNROLL):
            t = i * ACT_UNROLL + k
            part = part_ref[pl.ds(pl.multiple_of(t * PEER_PAIRS, PEER_PAIRS), PEER_PAIRS), :]
            act = jnp.where(tok_lane == t, jnp.sum(part, axis=1, keepdims=True), act)
        return act

    act = lax.fori_loop(0, TC // ACT_UNROLL, lane_sums, jnp.zeros((PEER_PAIRS, TC), F32))
    w_ref[...] = (gate_ref[0] * _gelu(act)).T


def _expert_act(idx_tok, h2, gate_t, tab):
    nt = gate_t.shape[0]
    tile = pl.BlockSpec((1, PEER_PAIRS, TC), lambda i: (i, 0, 0))
    return pl.pallas_call(
        _expert_act_body,
        grid=(nt,),
        in_specs=[pl.BlockSpec(memory_space=pltpu.SMEM),
                  pl.BlockSpec((TC, PEER_PAIRS), lambda i: (i, 0), memory_space=pltpu.SMEM),
                  pl.BlockSpec((TC, h2.shape[1]), lambda i: (i, 0)),
                  tile,
                  pl.BlockSpec(tab.shape, lambda i: (0, 0), pipeline_mode=pl.Buffered(1))],
        out_specs=pl.BlockSpec((TC, PEER_PAIRS), lambda i: (i, 0)),
        out_shape=jax.ShapeDtypeStruct((nt * TC, PEER_PAIRS), F32),
        scratch_shapes=[pltpu.VMEM((PEER_PAIRS * ROW_SUB, LANES), F32),
                        pltpu.VMEM((TC * PEER_PAIRS, LANES), F32)],
        compiler_params=_cparams(1, 52),
    )(jnp.arange(OFF_REGS, dtype=jnp.int32), idx_tok, h2, gate_t, tab)


def _expert_out_body(off_ref, idx_ref, w_ref, tab_ref, o_ref):
    n_acc = 4
    off = [off_ref[j] for j in range(OFF_REGS)]

    def token(t, carry):
        acc_lo = [jnp.zeros((ROW_SUB, LANES), F32) for _ in range(n_acc)]
        acc_hi = [jnp.zeros((ROW_SUB, LANES), F32) for _ in range(n_acc)]
        for p in range(PEER_PAIRS):
            if p % OFF_REGS == 0:
                idx_sub = idx_ref.at[t, pl.ds(p, OFF_REGS)]
                w_sub = w_ref.at[t, pl.ds(p, OFF_REGS)]
            lo, hi = _load_row(tab_ref, idx_sub[off[p % OFF_REGS]])
            w = w_sub[off[p % OFF_REGS]]
            acc_lo[p % n_acc] = acc_lo[p % n_acc] + w * lo
            acc_hi[p % n_acc] = acc_hi[p % n_acc] + w * hi
        lo = (acc_lo[0] + acc_lo[1]) + (acc_lo[2] + acc_lo[3])
        hi = (acc_hi[0] + acc_hi[1]) + (acc_hi[2] + acc_hi[3])
        o_ref[t] = jnp.concatenate([lo, hi], axis=0)
        return carry

    lax.fori_loop(0, TB, token, 0)


def _expert_out(idx_tok, w_tok, tab):
    n = idx_tok.shape[0]
    smem = pl.BlockSpec((TB, PEER_PAIRS), lambda i: (i, 0), memory_space=pltpu.SMEM)
    return pl.pallas_call(
        _expert_out_body,
        grid=(n // TB,),
        in_specs=[pl.BlockSpec(memory_space=pltpu.SMEM), smem, smem,
                  pl.BlockSpec(tab.shape, lambda i: (0, 0), pipeline_mode=pl.Buffered(1))],
        out_specs=pl.BlockSpec((TB, D_MODEL // LANES, LANES), lambda i: (i, 0, 0)),
        out_shape=jax.ShapeDtypeStruct((n, D_MODEL // LANES, LANES), F32),
        compiler_params=_cparams(1, 48),
    )(jnp.arange(OFF_REGS, dtype=jnp.int32), idx_tok, w_tok, tab)


def _final_body(x1_ref, f_ref, gt_ref, g_ref, b_ref, o_ref, *, alpha):
    ffn = f_ref[...].reshape(x1_ref.shape)
    o_ref[...] = _ln(alpha * x1_ref[...] + gt_ref[0] * ffn) * g_ref[...] + b_ref[...]


def _final(x1, ffn, gt2, g2, b2, seq, alpha):
    n, d = x1.shape
    tm = TM_LN
    nst = seq // tm
    row = pl.BlockSpec((tm, d), lambda i: (i, 0))
    const = pl.BlockSpec((1, d), lambda i: (0, 0))
    return pl.pallas_call(
        functools.partial(_final_body, alpha=alpha),
        grid=(n // tm,),
        in_specs=[row, pl.BlockSpec((tm,) + ffn.shape[1:], lambda i: (i, 0, 0)),
                  pl.BlockSpec((1, 1, d), lambda i: (i // nst, 0, 0)), const, const],
        out_specs=row,
        out_shape=jax.ShapeDtypeStruct((n, d), F32),
        compiler_params=_cparams(1, 32),
    )(x1, ffn, gt2, g2, b2)


def _swap_halves(w, heads):
    d = w.shape[0]
    w4 = w.reshape(d, heads, 2, HEAD_DIM // 2)
    return jnp.flip(w4, axis=2).reshape(d, heads * HEAD_DIM)


def _fused_in_weight(w_in, w_merge):
    d = w_in.shape[0]
    o = 0
    parts = {}
    for name, width in (("q", NSA_WIDTH), ("kc", KV_WIDTH), ("vc", KV_WIDTH), ("ks", KV_WIDTH),
                        ("vs", KV_WIDTH), ("kw", KV_WIDTH), ("vw", KV_WIDTH), ("g", 3 * NSA_HEADS),
                        ("z", 2 * SGU_WIDTH)):
        parts[name] = w_in[:, o:o + width]
        o += width
    wq = parts["q"] * (HEAD_DIM ** -0.5)
    gcols = 3 * NSA_REP
    wg = jnp.zeros((d, 2 * LANES), w_in.dtype)
    wg = wg.at[:, :gcols].set(parts["g"][:, :gcols]).at[:, LANES:LANES + gcols].set(parts["g"][:, gcols:])
    g = NSA_KV_GROUPS
    cols = {"q": wq, "q_swap": _swap_halves(wq, NSA_HEADS),
            "kc": parts["kc"], "kc_swap": _swap_halves(parts["kc"], g),
            "ks": parts["ks"], "ks_swap": _swap_halves(parts["ks"], g),
            "kw": parts["kw"], "kw_swap": _swap_halves(parts["kw"], g),
            "vc": parts["vc"], "vs": parts["vs"], "vw": parts["vw"],
            "gates": wg, "z": parts["z"], "merge": w_merge}
    assert all(cols[name].shape[1] == width for name, width in _SEGMENTS)
    return jnp.concatenate([cols[name] for name, _ in _SEGMENTS], axis=1).astype(BF16)


def _rope_tables(seq):
    half = HEAD_DIM // 2
    pos = jnp.arange(seq, dtype=F32)
    inv_freq = ROPE_THETA ** (-jnp.arange(half, dtype=F32) / half)
    ang = pos[:, None] * inv_freq[None, :]
    cos, sin = jnp.cos(ang), jnp.sin(ang)
    reps = LANES // HEAD_DIM
    cosk = jnp.tile(jnp.concatenate([cos, cos], axis=1), (1, reps))
    sink = jnp.tile(jnp.concatenate([-sin, sin], axis=1), (1, reps))
    return cosk, sink


def _sel_aggregation(n_cmp_pad, n_sel):
    c0 = jnp.arange(n_cmp_pad)[:, None] * CMP_STRIDE
    s0 = jnp.arange(n_sel)[None, :] * SEL_BLOCK
    ov = jnp.clip(jnp.minimum(c0 + CMP_BLOCK, s0 + SEL_BLOCK) - jnp.maximum(c0, s0), 0, None)
    return (ov / CMP_BLOCK).astype(BF16)


def _pack_table(tab):
    half = tab.shape[1] // 2
    bits = lax.bitcast_convert_type(tab.astype(BF16), jnp.uint16).astype(jnp.uint32)
    packed = (bits[:, half:] << 16) | bits[:, :half]
    return packed.reshape(tab.shape[0] * ROW_SUB, LANES)


def _split_groups(a, bsz, seq):
    return a.reshape(bsz, seq, NSA_KV_GROUPS, HEAD_DIM).transpose(0, 2, 1, 3)


def kernel(x, c, w_ada, b_ada, w_in, cmp_pos, cmp_w1, cmp_b1, cmp_w2, cmp_b2, sgu_ln_g, sgu_ln_b, sgu_w, sgu_b, w_branch, w_merge, b_merge, w_out, ln1_g, ln1_b, peer_wq, peer_keys, peer_u, peer_v, ln2_g, ln2_b):
    bsz, seq, d = x.shape
    n = bsz * seq
    depth = w_ada.shape[0]
    alpha = (2.0 * depth) ** 0.25
    n_cmp = (seq - CMP_BLOCK) // CMP_STRIDE + 1
    n_half = seq // CMP_STRIDE
    n_sel = seq // SEL_BLOCK
    cosk, sink = _rope_tables(seq)
    agg = _sel_aggregation(n_half, n_sel)
    x2 = x.reshape(n, d)
    for l in range(depth):
        mod = _ada(c, w_ada[l], b_ada[l])
        sh1, sc1, gt1, sh2, sc2, gt2 = [m.reshape(bsz, 1, d) for m in jnp.split(mod, 6, axis=-1)]

        wall = _fused_in_weight(w_in[l], w_merge[l])
        (q_t, ks_ext, vs_t, kw_g, vw_t, gn_t, kc_r, vc_r, u, v, gate) = _proj(
            x2, sc1, sh1, wall, b_merge[l].reshape(1, -1), cosk, sink,
            sgu_ln_g[l].reshape(1, -1), sgu_ln_b[l].reshape(1, -1), seq)

        t16 = jnp.stack([_split_groups(kc_r, bsz, seq), _split_groups(vc_r, bsz, seq)])
        t16 = t16.reshape(2, bsz, NSA_KV_GROUPS, n_half, CMP_STRIDE * HEAD_DIM)
        cmp_kv = _compress(t16, cmp_pos[l].reshape(2, 1, CMP_BLOCK * HEAD_DIM), cmp_w1[l],
                           cmp_b1[l].reshape(2, 1, CMP_HIDDEN), cmp_w2[l], cmp_b2[l].reshape(2, 1, HEAD_DIM))

        o_nsa_t = _nsa(q_t, cmp_kv[0], cmp_kv[1].transpose(0, 1, 3, 2), ks_ext, vs_t, kw_g, vw_t, gn_t, agg.T, n_cmp)

        bs = jnp.repeat(sgu_b[l].T, SGU_GROUP_DIM, axis=1)
        x1, h2, idx_tok, gate_t = _merge(o_nsa_t, u, v, gate, x2, gt1, sc2, sh2,
                            sgu_w[l], bs, w_branch[l].astype(BF16), w_out[l].astype(BF16),
                            ln1_g[l].reshape(1, d), ln1_b[l].reshape(1, d), peer_wq[l].astype(BF16), peer_keys[l], seq, alpha)

        w_tok = _expert_act(idx_tok, h2, gate_t, _pack_table(peer_u[l]))
        ffn = _expert_out(idx_tok, w_tok, _pack_table(peer_v[l]))
        x2 = _final(x1, ffn, gt2, ln2_g[l].reshape(1, d), ln2_b[l].reshape(1, d), seq, alpha)
    return x2.reshape(bsz, seq, d)
```
